```python
import jax, jax.numpy as jnp
from jax import lax
import numpy as np

D_MODEL = 4096
BATCH = 8
SEQ = 4096
DEPTH = 1

MIX_WIDTH = D_MODEL
POOL_WIDTH = MIX_WIDTH // 2
CONV_WIDTH = MIX_WIDTH - POOL_WIDTH
POOL_WINDOWS = (2, 4, 8, 16)
N_POOL_GROUPS = len(POOL_WINDOWS)
POOL_GROUP_DIM = POOL_WIDTH // N_POOL_GROUPS
CONV_HEAD_DIM = 128
CONV_HEADS = CONV_WIDTH // CONV_HEAD_DIM
CONV_WIDTH_K = 3
IN_PROJ_WIDTH = POOL_WIDTH + 3 * CONV_WIDTH
D_FF = 4 * D_MODEL
N_MOD = 6
EPS = 1e-6

kernel_name = "hybrid_pool_shortconv_adaln_block"


def rmsnorm(x, g):
    xf = x.astype(jnp.float32)
    xn = xf * lax.rsqrt(jnp.mean(xf * xf, axis=-1, keepdims=True) + EPS)
    return xn.astype(x.dtype) * g


def group_rmsnorm(x, g, n_groups):
    b, s, w = x.shape
    xg = x.reshape(b, s, n_groups, w // n_groups).astype(jnp.float32)
    xn = xg * lax.rsqrt(jnp.mean(xg * xg, axis=-1, keepdims=True) + EPS)
    return xn.reshape(b, s, w).astype(x.dtype) * g


def modulate(h, shift, scale):
    return h * (1 + scale[:, None, :]) + shift[:, None, :]


def multiscale_pool(v):
    b, s, _ = v.shape
    vg = v.reshape(b, s, N_POOL_GROUPS, POOL_GROUP_DIM)
    cs = jnp.cumsum(vg.astype(jnp.float32), axis=1)
    cs = jnp.pad(cs, ((0, 0), (1, 0), (0, 0), (0, 0)))
    half = jnp.array(POOL_WINDOWS, dtype=jnp.int32) // 2
    t = jnp.arange(s, dtype=jnp.int32)[:, None]
    lo = jnp.clip(t - half[None, :], 0, s)
    hi = jnp.clip(t + half[None, :], 0, s)
    gidx = jnp.arange(N_POOL_GROUPS, dtype=jnp.int32)[None, :]
    win_sum = cs[:, hi, gidx, :] - cs[:, lo, gidx, :]
    count = (hi - lo).astype(jnp.float32)[None, :, :, None]
    out = win_sum / count - vg.astype(jnp.float32)
    return out.astype(v.dtype)


def depthwise_conv3_centred(u, w, bias):
    up = jnp.pad(u, ((0, 0), (1, 1), (0, 0)))
    return w[0] * up[:, :-2] + w[1] * up[:, 1:-1] + w[2] * up[:, 2:] + bias


def _fwd_setup_inputs(seed: int = 0) -> dict:
    key = jax.random.key(seed)
    ks = jax.random.split(key, 20)
    f32 = jnp.float32
    L, D = DEPTH, D_MODEL

    def nrm(k, shape, fan_in):
        return jax.random.normal(k, shape, f32) * (fan_in ** -0.5)

    def gain(k, shape):
        return 1.0 + 0.1 * jax.random.normal(k, shape, f32)

    return {
        "x": jax.random.normal(ks[0], (BATCH, SEQ, D), f32),
        "c": jax.random.normal(ks[1], (BATCH, D), f32),
        "w_ada": nrm(ks[2], (L, D, N_MOD * D), D) * 0.5,
        "b_ada": 0.02 * jax.random.normal(ks[3], (L, N_MOD * D), f32),
        "norm1_g": gain(ks[4], (L, D)),
        "w_in": nrm(ks[5], (L, D, IN_PROJ_WIDTH), D),
        "pool_mix_w": nrm(ks[6], (L, N_POOL_GROUPS, POOL_GROUP_DIM, POOL_GROUP_DIM), POOL_GROUP_DIM),
        "pool_scale": gain(ks[7], (L, POOL_WIDTH)),
        "conv_w": nrm(ks[8], (L, CONV_WIDTH_K, CONV_WIDTH), CONV_WIDTH_K),
        "conv_b": 0.02 * jax.random.normal(ks[9], (L, CONV_WIDTH), f32),
        "gnorm_pool_g": gain(ks[10], (L, POOL_WIDTH)),
        "gnorm_conv_g": gain(ks[11], (L, CONV_WIDTH)),
        "w_out": nrm(ks[12], (L, MIX_WIDTH, D), MIX_WIDTH),
        "norm2_g": gain(ks[13], (L, D)),
        "w_mlp_in": nrm(ks[14], (L, D, D_FF), D),
        "w_mlp_out": nrm(ks[15], (L, D_FF, D), D_FF),
        "final_g": gain(ks[16], (D,)),
    }


def _fwd_reference(x, c, w_ada, b_ada, norm1_g, w_in, pool_mix_w, pool_scale, conv_w, conv_b,
              gnorm_pool_g, gnorm_conv_g, w_out, norm2_g, w_mlp_in, w_mlp_out, final_g):
    c_act = jax.nn.silu(c)
    for l in range(DEPTH):
        mod = c_act @ w_ada[l] + b_ada[l]
        shift1, scale1, gate1, shift2, scale2, gate2 = jnp.split(mod, N_MOD, axis=-1)

        h = modulate(rmsnorm(x, norm1_g[l]), shift1, scale1)
        proj = jnp.einsum("bsd,de->bse", h, w_in[l])
        v_pool = proj[..., :POOL_WIDTH]
        b_gate, c_gate, u = jnp.split(proj[..., POOL_WIDTH:], 3, axis=-1)

        pooled = multiscale_pool(v_pool)
        a_out = jnp.einsum("bsgd,gde->bsge", pooled, pool_mix_w[l])
        a_out = a_out.reshape(x.shape[0], x.shape[1], POOL_WIDTH) * pool_scale[l]

        b_out = b_gate * depthwise_conv3_centred(c_gate * u, conv_w[l], conv_b[l])

        mixed = jnp.concatenate(
            [group_rmsnorm(a_out, gnorm_pool_g[l], N_POOL_GROUPS),
             group_rmsnorm(b_out, gnorm_conv_g[l], CONV_HEADS)], axis=-1)
        x = x + gate1[:, None, :] * jnp.einsum("bse,ed->bsd", mixed, w_out[l])

        h = modulate(rmsnorm(x, norm2_g[l]), shift2, scale2)
        hid = jnp.square(jax.nn.relu(jnp.einsum("bsd,df->bsf", h, w_mlp_in[l])))
        x = x + gate2[:, None, :] * jnp.einsum("bsf,fd->bsd", hid, w_mlp_out[l])

    return rmsnorm(x, final_g)


import jax as _jax
import jax.numpy as _jnp

TWIN_FORMAT = 'train_step'
FWD_PARAMS = ['x', 'c', 'w_ada', 'b_ada', 'norm1_g', 'w_in', 'pool_mix_w', 'pool_scale', 'conv_w', 'conv_b', 'gnorm_pool_g', 'gnorm_conv_g', 'w_out', 'norm2_g', 'w_mlp_in', 'w_mlp_out', 'final_g']
TWIN_WEIGHTS = ['w_ada', 'b_ada', 'norm1_g', 'w_in', 'pool_mix_w', 'pool_scale', 'conv_w', 'conv_b', 'gnorm_pool_g', 'gnorm_conv_g', 'w_out', 'norm2_g', 'w_mlp_in', 'w_mlp_out', 'final_g']
TWIN_DIFF_INPUT = 'x'
TWIN_INPUTS = ['x', 'c', 'w_ada', 'b_ada', 'norm1_g', 'w_in', 'pool_mix_w', 'pool_scale', 'conv_w', 'conv_b', 'gnorm_pool_g', 'gnorm_conv_g', 'w_out', 'norm2_g', 'w_mlp_in', 'w_mlp_out', 'final_g', 'loss_target', 'm_w_ada', 'm_b_ada', 'm_norm1_g', 'm_w_in', 'm_pool_mix_w', 'm_pool_scale', 'm_conv_w', 'm_conv_b', 'm_gnorm_pool_g', 'm_gnorm_conv_g', 'm_w_out', 'm_norm2_g', 'm_w_mlp_in', 'm_w_mlp_out', 'm_final_g', 'v_w_ada', 'v_b_ada', 'v_norm1_g', 'v_w_in', 'v_pool_mix_w', 'v_pool_scale', 'v_conv_w', 'v_conv_b', 'v_gnorm_pool_g', 'v_gnorm_conv_g', 'v_w_out', 'v_norm2_g', 'v_w_mlp_in', 'v_w_mlp_out', 'v_final_g']
TWIN_OUTPUTS = ['loss', 'grad_x', 'grad_w_ada', 'grad_b_ada', 'grad_norm1_g', 'grad_w_in', 'grad_pool_mix_w', 'grad_pool_scale', 'grad_conv_w', 'grad_conv_b', 'grad_gnorm_pool_g', 'grad_gnorm_conv_g', 'grad_w_out', 'grad_norm2_g', 'grad_w_mlp_in', 'grad_w_mlp_out', 'grad_final_g', 'delta_w_ada', 'delta_b_ada', 'delta_norm1_g', 'delta_w_in', 'delta_pool_mix_w', 'delta_pool_scale', 'delta_conv_w', 'delta_conv_b', 'delta_gnorm_pool_g', 'delta_gnorm_conv_g', 'delta_w_out', 'delta_norm2_g', 'delta_w_mlp_in', 'delta_w_mlp_out', 'delta_final_g', 'new_m_w_ada', 'new_m_b_ada', 'new_m_norm1_g', 'new_m_w_in', 'new_m_pool_mix_w', 'new_m_pool_scale', 'new_m_conv_w', 'new_m_conv_b', 'new_m_gnorm_pool_g', 'new_m_gnorm_conv_g', 'new_m_w_out', 'new_m_norm2_g', 'new_m_w_mlp_in', 'new_m_w_mlp_out', 'new_m_final_g', 'new_v_w_ada', 'new_v_b_ada', 'new_v_norm1_g', 'new_v_w_in', 'new_v_pool_mix_w', 'new_v_pool_scale', 'new_v_conv_w', 'new_v_conv_b', 'new_v_gnorm_pool_g', 'new_v_gnorm_conv_g', 'new_v_w_out', 'new_v_norm2_g', 'new_v_w_mlp_in', 'new_v_w_mlp_out', 'new_v_final_g']
TWIN_LEAF_KINDS = {'loss': 'loss', 'grad_x': 'grad_x', 'grad_w_ada': 'grad_w', 'grad_b_ada': 'grad_w', 'grad_norm1_g': 'grad_w', 'grad_w_in': 'grad_w', 'grad_pool_mix_w': 'grad_w', 'grad_pool_scale': 'grad_w', 'grad_conv_w': 'grad_w', 'grad_conv_b': 'grad_w', 'grad_gnorm_pool_g': 'grad_w', 'grad_gnorm_conv_g': 'grad_w', 'grad_w_out': 'grad_w', 'grad_norm2_g': 'grad_w', 'grad_w_mlp_in': 'grad_w', 'grad_w_mlp_out': 'grad_w', 'grad_final_g': 'grad_w', 'delta_w_ada': 'delta_w', 'delta_b_ada': 'delta_w', 'delta_norm1_g': 'delta_w', 'delta_w_in': 'delta_w', 'delta_pool_mix_w': 'delta_w', 'delta_pool_scale': 'delta_w', 'delta_conv_w': 'delta_w', 'delta_conv_b': 'delta_w', 'delta_gnorm_pool_g': 'delta_w', 'delta_gnorm_conv_g': 'delta_w', 'delta_w_out': 'delta_w', 'delta_norm2_g': 'delta_w', 'delta_w_mlp_in': 'delta_w', 'delta_w_mlp_out': 'delta_w', 'delta_final_g': 'delta_w', 'new_m_w_ada': 'new_m', 'new_m_b_ada': 'new_m', 'new_m_norm1_g': 'new_m', 'new_m_w_in': 'new_m', 'new_m_pool_mix_w': 'new_m', 'new_m_pool_scale': 'new_m', 'new_m_conv_w': 'new_m', 'new_m_conv_b': 'new_m', 'new_m_gnorm_pool_g': 'new_m', 'new_m_gnorm_conv_g': 'new_m', 'new_m_w_out': 'new_m', 'new_m_norm2_g': 'new_m', 'new_m_w_mlp_in': 'new_m', 'new_m_w_mlp_out': 'new_m', 'new_m_final_g': 'new_m', 'new_v_w_ada': 'new_v', 'new_v_b_ada': 'new_v', 'new_v_norm1_g': 'new_v', 'new_v_w_in': 'new_v', 'new_v_pool_mix_w': 'new_v', 'new_v_pool_scale': 'new_v', 'new_v_conv_w': 'new_v', 'new_v_conv_b': 'new_v', 'new_v_gnorm_pool_g': 'new_v', 'new_v_gnorm_conv_g': 'new_v', 'new_v_w_out': 'new_v', 'new_v_norm2_g': 'new_v', 'new_v_w_mlp_in': 'new_v', 'new_v_w_mlp_out': 'new_v', 'new_v_final_g': 'new_v'}


def _forward(args):
    return _fwd_reference(*[args[k] for k in FWD_PARAMS])


def _output_shape():
    out = _jax.eval_shape(lambda: _forward(_fwd_setup_inputs(0)))
    return out.shape, out.dtype

N_MICROBATCH = 1
ADAM_LR = 0.001
ADAM_B1 = 0.9
ADAM_B2 = 0.999
ADAM_EPS = 1e-08
ADAM_WD = 0.01
ADAM_STEP = 10
PER_EXAMPLE_BATCH_AXIS = {'x': 0, 'c': 0, 'loss_target': 0}
SHARED_INPUTS = []
_WEIGHT_DTYPES = {'w_ada': _jnp.float32, 'b_ada': _jnp.float32, 'norm1_g': _jnp.float32, 'w_in': _jnp.float32, 'pool_mix_w': _jnp.float32, 'pool_scale': _jnp.float32, 'conv_w': _jnp.float32, 'conv_b': _jnp.float32, 'gnorm_pool_g': _jnp.float32, 'gnorm_conv_g': _jnp.float32, 'w_out': _jnp.float32, 'norm2_g': _jnp.float32, 'w_mlp_in': _jnp.float32, 'w_mlp_out': _jnp.float32, 'final_g': _jnp.float32}
MOMENT_SCALE = {'w_ada': 9.094243e-02, 'b_ada': 1.759078e-01, 'norm1_g': 1.908836e-02, 'w_in': 1.388861e-02, 'pool_mix_w': 1.369620e-02, 'pool_scale': 1.389999e-02, 'conv_w': 1.508898e-02, 'conv_b': 1.288899e-02, 'gnorm_pool_g': 1.397615e-02, 'gnorm_conv_g': 1.345252e-02, 'w_out': 1.365402e-02, 'norm2_g': 1.989210e-02, 'w_mlp_in': 1.131458e-02, 'w_mlp_out': 3.860760e-02, 'final_g': 8.088010e+00}


def _to_microbatches(a, axis):
    t = _jnp.moveaxis(a, axis, 0)
    t = t.reshape((N_MICROBATCH, t.shape[0] // N_MICROBATCH) + t.shape[1:])
    return _jnp.moveaxis(t, 1, axis + 1)


def setup_inputs(seed: int = 0) -> dict:
    inp = _fwd_setup_inputs(seed)
    key = _jax.random.fold_in(_jax.random.key(seed), 7919)
    shape, _ = _output_shape()
    out = dict(inp)
    out["loss_target"] = _jax.random.normal(_jax.random.fold_in(key, 0), shape, _jnp.float32)
    for i, name in enumerate(TWIN_WEIGHTS):
        w = inp[name].astype(_jnp.float32)
        if MOMENT_SCALE is None:
            s = _jnp.sqrt(_jnp.mean(_jnp.square(w)) + 1e-30)
        else:
            s = MOMENT_SCALE[name]
        km, kv = _jax.random.split(_jax.random.fold_in(key, i + 1))
        out[name] = w
        out["m_" + name] = s * _jax.random.normal(km, w.shape, _jnp.float32)
        out["v_" + name] = (s * s) * _jax.random.uniform(kv, w.shape, _jnp.float32, 0.5, 1.5)
    if N_MICROBATCH > 1:
        for name, axis in PER_EXAMPLE_BATCH_AXIS.items():
            out[name] = _to_microbatches(out[name], axis)
    return {'x': out['x'], 'c': out['c'], 'w_ada': out['w_ada'], 'b_ada': out['b_ada'], 'norm1_g': out['norm1_g'], 'w_in': out['w_in'], 'pool_mix_w': out['pool_mix_w'], 'pool_scale': out['pool_scale'], 'conv_w': out['conv_w'], 'conv_b': out['conv_b'], 'gnorm_pool_g': out['gnorm_pool_g'], 'gnorm_conv_g': out['gnorm_conv_g'], 'w_out': out['w_out'], 'norm2_g': out['norm2_g'], 'w_mlp_in': out['w_mlp_in'], 'w_mlp_out': out['w_mlp_out'], 'final_g': out['final_g'], 'loss_target': out['loss_target'], 'm_w_ada': out['m_w_ada'], 'm_b_ada': out['m_b_ada'], 'm_norm1_g': out['m_norm1_g'], 'm_w_in': out['m_w_in'], 'm_pool_mix_w': out['m_pool_mix_w'], 'm_pool_scale': out['m_pool_scale'], 'm_conv_w': out['m_conv_w'], 'm_conv_b': out['m_conv_b'], 'm_gnorm_pool_g': out['m_gnorm_pool_g'], 'm_gnorm_conv_g': out['m_gnorm_conv_g'], 'm_w_out': out['m_w_out'], 'm_norm2_g': out['m_norm2_g'], 'm_w_mlp_in': out['m_w_mlp_in'], 'm_w_mlp_out': out['m_w_mlp_out'], 'm_final_g': out['m_final_g'], 'v_w_ada': out['v_w_ada'], 'v_b_ada': out['v_b_ada'], 'v_norm1_g': out['v_norm1_g'], 'v_w_in': out['v_w_in'], 'v_pool_mix_w': out['v_pool_mix_w'], 'v_pool_scale': out['v_pool_scale'], 'v_conv_w': out['v_conv_w'], 'v_conv_b': out['v_conv_b'], 'v_gnorm_pool_g': out['v_gnorm_pool_g'], 'v_gnorm_conv_g': out['v_gnorm_conv_g'], 'v_w_out': out['v_w_out'], 'v_norm2_g': out['v_norm2_g'], 'v_w_mlp_in': out['v_w_mlp_in'], 'v_w_mlp_out': out['v_w_mlp_out'], 'v_final_g': out['v_final_g']}


def _loss(weights, diff, rest, loss_target):
    with _jax.named_scope("forward"):
        args = {**rest, TWIN_DIFF_INPUT: diff, **{k: w.astype(_WEIGHT_DTYPES[k]) for k, w in weights.items()}}
        y = _forward(args)
    with _jax.named_scope("loss_head"):
        err = _jnp.square(y.astype(_jnp.float32) - loss_target)
        return 0.5 * _jnp.sum(_jnp.mean(err, axis=-1)) if err.ndim else 0.5 * err


def _adamw(w, g, m, v):
    m = ADAM_B1 * m + (1.0 - ADAM_B1) * g
    v = ADAM_B2 * v + (1.0 - ADAM_B2) * _jnp.square(g)
    m_hat = m / (1.0 - ADAM_B1 ** ADAM_STEP)
    v_hat = v / (1.0 - ADAM_B2 ** ADAM_STEP)
    delta = -ADAM_LR * (m_hat / (_jnp.sqrt(v_hat) + ADAM_EPS) + ADAM_WD * w)
    return delta, m, v


def reference(x, c, w_ada, b_ada, norm1_g, w_in, pool_mix_w, pool_scale, conv_w, conv_b, gnorm_pool_g, gnorm_conv_g, w_out, norm2_g, w_mlp_in, w_mlp_out, final_g, loss_target, m_w_ada, m_b_ada, m_norm1_g, m_w_in, m_pool_mix_w, m_pool_scale, m_conv_w, m_conv_b, m_gnorm_pool_g, m_gnorm_conv_g, m_w_out, m_norm2_g, m_w_mlp_in, m_w_mlp_out, m_final_g, v_w_ada, v_b_ada, v_norm1_g, v_w_in, v_pool_mix_w, v_pool_scale, v_conv_w, v_conv_b, v_gnorm_pool_g, v_gnorm_conv_g, v_w_out, v_norm2_g, v_w_mlp_in, v_w_mlp_out, v_final_g):
    given = dict(x=x, c=c, w_ada=w_ada, b_ada=b_ada, norm1_g=norm1_g, w_in=w_in, pool_mix_w=pool_mix_w, pool_scale=pool_scale, conv_w=conv_w, conv_b=conv_b, gnorm_pool_g=gnorm_pool_g, gnorm_conv_g=gnorm_conv_g, w_out=w_out, norm2_g=norm2_g, w_mlp_in=w_mlp_in, w_mlp_out=w_mlp_out, final_g=final_g, loss_target=loss_target, m_w_ada=m_w_ada, m_b_ada=m_b_ada, m_norm1_g=m_norm1_g, m_w_in=m_w_in, m_pool_mix_w=m_pool_mix_w, m_pool_scale=m_pool_scale, m_conv_w=m_conv_w, m_conv_b=m_conv_b, m_gnorm_pool_g=m_gnorm_pool_g, m_gnorm_conv_g=m_gnorm_conv_g, m_w_out=m_w_out, m_norm2_g=m_norm2_g, m_w_mlp_in=m_w_mlp_in, m_w_mlp_out=m_w_mlp_out, m_final_g=m_final_g, v_w_ada=v_w_ada, v_b_ada=v_b_ada, v_norm1_g=v_norm1_g, v_w_in=v_w_in, v_pool_mix_w=v_pool_mix_w, v_pool_scale=v_pool_scale, v_conv_w=v_conv_w, v_conv_b=v_conv_b, v_gnorm_pool_g=v_gnorm_pool_g, v_gnorm_conv_g=v_gnorm_conv_g, v_w_out=v_w_out, v_norm2_g=v_norm2_g, v_w_mlp_in=v_w_mlp_in, v_w_mlp_out=v_w_mlp_out, v_final_g=v_final_g)
    weights = {n: given[n] for n in TWIN_WEIGHTS}
    shared = {n: given[n] for n in SHARED_INPUTS}
    per_example = {n: given[n] for n in ['x', 'c']}
    grad_fn = _jax.value_and_grad(_loss, argnums=(0, 1))

    def one_microbatch(ex, loss_target):
        ex = dict(ex)
        diff = ex.pop(TWIN_DIFF_INPUT)
        return grad_fn(weights, diff, {**shared, **ex}, loss_target)

    if N_MICROBATCH == 1:
        loss, (grad_w, grad_x) = one_microbatch(per_example, given["loss_target"])
    else:
        def body(carry, xs):
            loss_sum, grad_sum = carry
            l_k, (gw_k, gx_k) = one_microbatch(xs[0], xs[1])
            with _jax.named_scope("update"):
                return (loss_sum + l_k, _jax.tree.map(_jnp.add, grad_sum, gw_k)), gx_k

        init = (_jnp.zeros((), _jnp.float32), _jax.tree.map(_jnp.zeros_like, weights))
        (loss, grad_w), grad_x = _jax.lax.scan(body, init, (per_example, given["loss_target"]))
    with _jax.named_scope("update"):
        delta_w, new_m, new_v = {}, {}, {}
        for n in TWIN_WEIGHTS:
            delta_w[n], new_m[n], new_v[n] = _adamw(weights[n], grad_w[n], given["m_" + n], given["v_" + n])
    return (loss, grad_x, *[grad_w[n] for n in TWIN_WEIGHTS], *[delta_w[n] for n in TWIN_WEIGHTS],
            *[new_m[n] for n in TWIN_WEIGHTS], *[new_v[n] for n in TWIN_WEIGHTS])
```

```python
import jax
import jax.numpy as jnp
from jax import lax
from jax.experimental import pallas as pl
from jax.experimental.pallas import tpu as pltpu

F32 = jnp.float32
BF16 = jnp.bfloat16
MESH = pl.DeviceIdType.MESH

N_DEV = 8
N_MOD = 6
EPS = 1e-6
POOL_WINDOWS = (2, 4, 8, 16)
N_POOL_GROUPS = len(POOL_WINDOWS)
CONV_HEAD_DIM = 128
PAD_ROWS = 16

ADAM_LR = 0.001
ADAM_B1 = 0.9
ADAM_B2 = 0.999
ADAM_EPS = 1e-08
ADAM_WD = 0.01
ADAM_STEP = 10

VMEM_LIMIT_BYTES = 56 * 1024 * 1024


def _pallas(body, **kw):
    return pl.pallas_call(body, **kw)


def _params(*sem):
    return pltpu.CompilerParams(dimension_semantics=sem, vmem_limit_bytes=VMEM_LIMIT_BYTES)


def _tile(pref, dim):
    if dim <= pref:
        return dim
    for t in range(pref - pref % 128, 0, -128):
        if dim % t == 0:
            return t
    return dim


def _exchange(name, arrays, modes):
    n = len(arrays)
    out_shape = []
    for a, mode in zip(arrays, modes):
        piece = a.shape if mode == "gather" else a.shape[1:]
        out_shape.append(jax.ShapeDtypeStruct((N_DEV,) + tuple(piece), a.dtype))

    def body(*refs):
        srcs, dsts = refs[:n], refs[n:2 * n]
        send_sems, recv_sems, local_sems = refs[2 * n:]
        x, y, c = lax.axis_index("x"), lax.axis_index("y"), lax.axis_index("c")
        me = 4 * x + 2 * y + c
        copies = []
        for i in range(n):
            gather = modes[i] == "gather"
            local = pltpu.make_async_copy(srcs[i] if gather else srcs[i].at[me], dsts[i].at[me], local_sems.at[i])
            local.start()
            copies.append(local)
            for k in range(1, N_DEV):
                kx, ky, kc = (k >> 2) & 1, (k >> 1) & 1, k & 1
                peer = (1 - x if kx else x, 1 - y if ky else y, 1 - c if kc else c)
                peer_idx = 4 * peer[0] + 2 * peer[1] + peer[2]
                remote = pltpu.make_async_remote_copy(
                    src_ref=srcs[i] if gather else srcs[i].at[peer_idx],
                    dst_ref=dsts[i].at[me],
                    send_sem=send_sems.at[i * (N_DEV - 1) + k - 1],
                    recv_sem=recv_sems.at[i * (N_DEV - 1) + k - 1],
                    device_id=peer, device_id_type=MESH)
                remote.start()
                copies.append(remote)
        for cp in copies:
            cp.wait()

    any_spec = pl.BlockSpec(memory_space=pl.ANY)
    return _pallas(
        body, name=name, out_shape=out_shape,
        in_specs=[any_spec] * n, out_specs=[any_spec] * n,
        scratch_shapes=[pltpu.SemaphoreType.DMA((n * (N_DEV - 1),)),
                        pltpu.SemaphoreType.DMA((n * (N_DEV - 1),)),
                        pltpu.SemaphoreType.DMA((n,))],
    )(*arrays)


_DOT_DIMS = {"nn": (((1,), (0,)), ((), ())), "nt": (((1,), (1,)), ((), ())), "tn": (((0,), (0,)), ((), ()))}


def _matmul(name, mode, operands, in_specs, out_shape, out_specs, grid, acc_shape, epilogue):
    n_in, n_out, nk = len(operands), len(out_shape), grid[2]
    dims = _DOT_DIMS[mode]

    def body(*refs):
        a_ref, b_ref = refs[0], refs[1]
        extras, outs, acc = refs[2:n_in], refs[n_in:n_in + n_out], refs[-1]
        k = pl.program_id(2)
        part = lax.dot_general(a_ref[...], b_ref[...], dims, preferred_element_type=F32)

        @pl.when(k == 0)
        def _():
            acc[...] = part

        @pl.when(k > 0)
        def _():
            acc[...] += part

        @pl.when(k == nk - 1)
        def _():
            epilogue(acc[...], extras, outs)

    return _pallas(body, name=name, grid=grid, in_specs=in_specs, out_specs=out_specs, out_shape=out_shape,
                   scratch_shapes=[pltpu.VMEM(acc_shape, F32)],
                   compiler_params=_params("parallel", "parallel", "arbitrary"))(*operands)


def _store(dtype):
    def epilogue(acc, extras, outs):
        outs[0][...] = acc.astype(dtype)
    return epilogue


def _residual_epilogue(acc, extras, outs):
    x_ref, gate_ref = extras
    outs[0][...] = acc
    outs[1][...] = x_ref[...] + gate_ref[...] * acc


def _relu2_epilogue(acc, extras, outs):
    r = jnp.maximum(acc, 0.0)
    outs[0][...] = r.astype(outs[0].dtype)
    outs[1][...] = (r * r).astype(outs[1].dtype)


def _relu2_bwd_epilogue(acc, extras, outs):
    outs[0][...] = (acc * (2.0 * extras[0][...].astype(F32))).astype(outs[0].dtype)


def _no_extra_specs(tm, tn):
    return []


def _mm_nn(name, a, b, n_total, b_split, out_shape, epilogue, extras=(), extra_specs=_no_extra_specs, tm=1024, tn=1024, tk=1024):
    m, kdim = a.shape
    tm, tk = _tile(tm, m), _tile(tk, kdim)
    if b_split:
        piece = b.shape[2]
        tn = _tile(tn, piece)
        per = piece // tn
        b_spec = pl.BlockSpec((None, tk, tn), lambda i, j, k: (j // per, k, j % per))
    else:
        tn = _tile(tn, n_total)
        b_spec = pl.BlockSpec((tk, tn), lambda i, j, k: (k, j))
    in_specs = [pl.BlockSpec((tm, tk), lambda i, j, k: (i, k)), b_spec] + list(extra_specs(tm, tn))
    out_specs = [pl.BlockSpec((tm, tn), lambda i, j, k: (i, j)) for _ in out_shape]
    return _matmul(name, "nn", (a, b) + tuple(extras), in_specs, out_shape, out_specs,
                   (m // tm, n_total // tn, kdim // tk), (tm, tn), epilogue)


def _mm_nt(name, a, b, n_total, b_split, out_shape, epilogue, extras=(), extra_specs=_no_extra_specs, tm=1024, tn=1024, tk=1024):
    m, kdim = a.shape
    tm, tn = _tile(tm, m), _tile(tn, n_total)
    if b_split:
        piece = b.shape[2]
        tk = _tile(tk, piece)
        per = piece // tk
        b_spec = pl.BlockSpec((None, tn, tk), lambda i, j, k: (k // per, j, k % per))
    else:
        tk = _tile(tk, kdim)
        b_spec = pl.BlockSpec((tn, tk), lambda i, j, k: (j, k))
    in_specs = [pl.BlockSpec((tm, tk), lambda i, j, k: (i, k)), b_spec] + list(extra_specs(tm, tn))
    out_specs = [pl.BlockSpec((tm, tn), lambda i, j, k: (i, j)) for _ in out_shape]
    return _matmul(name, "nt", (a, b) + tuple(extras), in_specs, out_shape, out_specs,
                   (m // tm, n_total // tn, kdim // tk), (tm, tn), epilogue)


def _mm_tn(name, a, b, out_split, tm=1024, tn=1024, tk=1024):
    kdim, m = a.shape
    n_total = b.shape[1]
    tm, tk = _tile(tm, m), _tile(tk, kdim)
    if out_split:
        piece = n_total // N_DEV
        tn = _tile(tn, piece)
        per = piece // tn
        out_shape = [jax.ShapeDtypeStruct((N_DEV, m, piece), BF16)]
        out_specs = [pl.BlockSpec((None, tm, tn), lambda i, j, k: (j // per, i, j % per))]
    else:
        tn = _tile(tn, n_total)
        out_shape = [jax.ShapeDtypeStruct((m, n_total), BF16)]
        out_specs = [pl.BlockSpec((tm, tn), lambda i, j, k: (i, j))]
    in_specs = [pl.BlockSpec((tk, tm), lambda i, j, k: (k, i)), pl.BlockSpec((tk, tn), lambda i, j, k: (k, j))]
    return _matmul(name, "tn", (a, b), in_specs, out_shape, out_specs,
                   (m // tm, n_total // tn, kdim // tk), (tm, tn), _store(BF16))[0]


def _rms(xv):
    return lax.rsqrt(jnp.mean(xv * xv, axis=-1, keepdims=True) + EPS)


def _colsum(v):
    return jnp.sum(v, axis=0, keepdims=True)


def _norm_mod(name, x, g, scale, shift, tr=256):
    s, d = x.shape
    tr = _tile(tr, s)

    def body(x_ref, g_ref, sc_ref, sh_ref, h_ref):
        xv = x_ref[...]
        h = (xv * _rms(xv)) * g_ref[...]
        h_ref[...] = (h * (1.0 + sc_ref[...]) + sh_ref[...]).astype(h_ref.dtype)

    row = pl.BlockSpec((tr, d), lambda i: (i, 0))
    vec = pl.BlockSpec((1, d), lambda i: (0, 0))
    return _pallas(body, name=name, grid=(s // tr,), in_specs=[row, vec, vec, vec], out_specs=row,
                   out_shape=jax.ShapeDtypeStruct((s, d), BF16), compiler_params=_params("parallel"))(x, g, scale, shift)


def _loss_head(x3, target, gf, gate2, mlp, tr=128):
    s, d = x3.shape
    tr = _tile(tr, s)

    def body(x_ref, t_ref, gf_ref, gate_ref, mlp_ref, dx_ref, dbr_ref, dgf_ref, dgate_ref, loss_ref):
        @pl.when(pl.program_id(0) == 0)
        def _():
            dgf_ref[...] = jnp.zeros_like(dgf_ref)
            dgate_ref[...] = jnp.zeros_like(dgate_ref)
            loss_ref[...] = jnp.zeros_like(loss_ref)

        xv = x_ref[...]
        r = _rms(xv)
        xn = xv * r
        gfv = gf_ref[...]
        err = xn * gfv - t_ref[...]
        loss_ref[...] += 0.5 * _colsum(jnp.mean(err * err, axis=-1, keepdims=True))
        dy = err * (1.0 / d)
        dgf_ref[...] += _colsum(dy * xn)
        dxn = dy * gfv
        dx = r * (dxn - xn * jnp.mean(dxn * xn, axis=-1, keepdims=True))
        dx_ref[...] = dx
        dbr_ref[...] = (dx * gate_ref[...]).astype(dbr_ref.dtype)
        dgate_ref[...] += _colsum(dx * mlp_ref[...])

    row = pl.BlockSpec((tr, d), lambda i: (i, 0))
    vec = pl.BlockSpec((1, d), lambda i: (0, 0))
    return _pallas(
        body, name="loss_head", grid=(s // tr,), in_specs=[row, row, vec, vec, row],
        out_specs=[row, row, vec, vec, pl.BlockSpec((1, 128), lambda i: (0, 0))],
        out_shape=[jax.ShapeDtypeStruct((s, d), F32), jax.ShapeDtypeStruct((s, d), BF16),
                   jax.ShapeDtypeStruct((1, d), F32), jax.ShapeDtypeStruct((1, d), F32),
                   jax.ShapeDtypeStruct((1, 128), F32)],
        compiler_params=_params("arbitrary"))(x3, target, gf, gate2, mlp)


def _norm_mod_bwd(name, dh, xin, g, scale, dx_up, branch=None, gate=None, tr=128):
    s, d = xin.shape
    tr = _tile(tr, s)
    with_gate = branch is not None

    def body(*refs):
        dh_ref, x_ref, g_ref, sc_ref, up_ref = refs[:5]
        if with_gate:
            br_ref, gate_ref = refs[5:7]
            dx_ref, dsh_ref, dsc_ref, dg_ref, dbr_ref, dgate_ref = refs[7:]
            sums = (dsh_ref, dsc_ref, dg_ref, dgate_ref)
        else:
            dx_ref, dsh_ref, dsc_ref, dg_ref = refs[5:]
            sums = (dsh_ref, dsc_ref, dg_ref)

        @pl.when(pl.program_id(0) == 0)
        def _():
            for ref in sums:
                ref[...] = jnp.zeros_like(ref)

        xv, dhv, gv = x_ref[...], dh_ref[...], g_ref[...]
        r = _rms(xv)
        xn = xv * r
        one_sc = 1.0 + sc_ref[...]
        dsh_ref[...] += _colsum(dhv)
        dsc_ref[...] += _colsum(dhv * (xn * gv))
        dg_ref[...] += _colsum(dhv * one_sc * xn)
        dxn = dhv * one_sc * gv
        dx = up_ref[...] + r * (dxn - xn * jnp.mean(dxn * xn, axis=-1, keepdims=True))
        dx_ref[...] = dx
        if with_gate:
            dbr_ref[...] = (dx * gate_ref[...]).astype(dbr_ref.dtype)
            dgate_ref[...] += _colsum(dx * br_ref[...])

    row = pl.BlockSpec((tr, d), lambda i: (i, 0))
    vec = pl.BlockSpec((1, d), lambda i: (0, 0))
    vshape = jax.ShapeDtypeStruct((1, d), F32)
    operands = [dh, xin, g, scale, dx_up]
    in_specs = [row, row, vec, vec, row]
    out_shape = [jax.ShapeDtypeStruct((s, d), F32), vshape, vshape, vshape]
    out_specs = [row, vec, vec, vec]
    if with_gate:
        operands += [branch, gate]
        in_specs += [row, vec]
        out_shape += [jax.ShapeDtypeStruct((s, d), BF16), vshape]
        out_specs += [row, vec]
    return _pallas(body, name=name, grid=(s // tr,), in_specs=in_specs, out_specs=out_specs, out_shape=out_shape,
                   compiler_params=_params("arbitrary"))(*operands)


def _window_count(c0, rows, half, s):
    t = c0 + lax.broadcasted_iota(jnp.int32, (rows, 1), 0)
    return (jnp.minimum(t + half, s) - jnp.maximum(t - half, 0)).astype(F32)


def _zero_pads(pad, s):
    zeros = jnp.zeros((PAD_ROWS, pad.shape[1]), pad.dtype)
    pad[0:PAD_ROWS, :] = zeros
    pad[PAD_ROWS + s:PAD_ROWS + s + PAD_ROWS, :] = zeros


def _pool_fwd(proj, s, gd, cb, ch):
    nsub = gd // cb

    def body(v_ref, o_ref, pad):
        g = pl.program_id(0)
        _zero_pads(pad, s)
        pad[PAD_ROWS:PAD_ROWS + s, :] = v_ref[...]
        for gi, window in enumerate(POOL_WINDOWS):
            half = window // 2

            @pl.when(g == gi)
            def _(half=half):
                for c0 in range(0, s, ch):
                    base = PAD_ROWS + c0
                    acc = pad[base - half:base - half + ch, :]
                    for j in range(-half + 1, half):
                        acc = acc + pad[base + j:base + j + ch, :]
                    out = acc / _window_count(c0, ch, half, s) - v_ref[c0:c0 + ch, :]
                    o_ref[c0:c0 + ch, :] = out.astype(o_ref.dtype)

    spec = pl.BlockSpec((s, cb), lambda g, j: (0, g * nsub + j))
    return _pallas(body, name="pool_fwd", grid=(N_POOL_GROUPS, nsub), in_specs=[spec], out_specs=spec,
                   out_shape=jax.ShapeDtypeStruct((s, N_POOL_GROUPS * gd), BF16),
                   scratch_shapes=[pltpu.VMEM((s + 2 * PAD_ROWS, cb), F32)],
                   compiler_params=_params("parallel", "parallel"))(proj)


def _pool_bwd(dpooled, dproj, s, gd, cb, ch):
    nsub = gd // cb

    def body(dp_ref, dproj_in, o_ref, pad):
        del dproj_in
        g = pl.program_id(0)
        _zero_pads(pad, s)
        for gi, window in enumerate(POOL_WINDOWS):
            half = window // 2

            @pl.when(g == gi)
            def _(half=half):
                for c0 in range(0, s, ch):
                    pad[PAD_ROWS + c0:PAD_ROWS + c0 + ch, :] = dp_ref[c0:c0 + ch, :] / _window_count(c0, ch, half, s)
                for c0 in range(0, s, ch):
                    base = PAD_ROWS + c0
                    acc = pad[base - half + 1:base - half + 1 + ch, :]
                    for j in range(-half + 2, half + 1):
                        acc = acc + pad[base + j:base + j + ch, :]
                    o_ref[c0:c0 + ch, :] = (acc - dp_ref[c0:c0 + ch, :]).astype(o_ref.dtype)

    spec = pl.BlockSpec((s, cb), lambda g, j: (0, g * nsub + j))
    return _pallas(body, name="pool_bwd", grid=(N_POOL_GROUPS, nsub),
                   in_specs=[spec, pl.BlockSpec(memory_space=pl.ANY)], out_specs=spec,
                   out_shape=jax.ShapeDtypeStruct(dproj.shape, dproj.dtype), input_output_aliases={1: 0},
                   scratch_shapes=[pltpu.VMEM((s + 2 * PAD_ROWS, cb), F32)],
                   compiler_params=_params("parallel", "parallel"))(dpooled, dproj)


def _poolmix_fwd(pooled, wmix, pool_scale, gnorm_g, d_model, tm=512):
    s = pooled.shape[0]
    gd = wmix.shape[1]
    tm = _tile(tm, s)

    def body(p_ref, w_ref, ps_ref, g_ref, apre_ref, mixed_ref):
        a_pre = jnp.dot(p_ref[...], w_ref[...], preferred_element_type=F32)
        apre_ref[...] = a_pre
        a_out = a_pre * ps_ref[...]
        mixed_ref[...] = ((a_out * _rms(a_out)) * g_ref[...]).astype(mixed_ref.dtype)

    blk = pl.BlockSpec((tm, gd), lambda g, i: (i, g))
    vec = pl.BlockSpec((1, gd), lambda g, i: (0, g))
    return _pallas(body, name="poolmix_fwd", grid=(N_POOL_GROUPS, s // tm),
                   in_specs=[blk, pl.BlockSpec((None, gd, gd), lambda g, i: (g, 0, 0)), vec, vec],
                   out_specs=[blk, blk],
                   out_shape=[jax.ShapeDtypeStruct((s, N_POOL_GROUPS * gd), F32), jax.ShapeDtypeStruct((s, d_model), BF16)],
                   compiler_params=_params("parallel", "parallel"))(pooled, wmix, pool_scale, gnorm_g)


def _poolmix_bwd(dmixed, a_pre, wmix, pool_scale, gnorm_g, tm=512):
    s = a_pre.shape[0]
    gd = wmix.shape[1]
    tm = _tile(tm, s)

    def body(dm_ref, apre_ref, w_ref, ps_ref, g_ref, dapre_ref, dpooled_ref, dps_ref, dg_ref):
        @pl.when(pl.program_id(1) == 0)
        def _():
            dps_ref[...] = jnp.zeros_like(dps_ref)
            dg_ref[...] = jnp.zeros_like(dg_ref)

        a_pre, dm, ps = apre_ref[...], dm_ref[...], ps_ref[...]
        a_out = a_pre * ps
        r = _rms(a_out)
        n = a_out * r
        dg_ref[...] += _colsum(dm * n)
        dn = dm * g_ref[...]
        da_out = r * (dn - n * jnp.mean(dn * n, axis=-1, keepdims=True))
        dps_ref[...] += _colsum(da_out * a_pre)
        da_pre = (da_out * ps).astype(BF16)
        dapre_ref[...] = da_pre
        dpooled_ref[...] = lax.dot_general(da_pre, w_ref[...], _DOT_DIMS["nt"], preferred_element_type=F32)

    blk = pl.BlockSpec((tm, gd), lambda g, i: (i, g))
    vec = pl.BlockSpec((1, gd), lambda g, i: (0, g))
    width = N_POOL_GROUPS * gd
    return _pallas(body, name="poolmix_bwd", grid=(N_POOL_GROUPS, s // tm),
                   in_specs=[blk, blk, pl.BlockSpec((None, gd, gd), lambda g, i: (g, 0, 0)), vec, vec],
                   out_specs=[blk, blk, vec, vec],
                   out_shape=[jax.ShapeDtypeStruct((s, width), BF16), jax.ShapeDtypeStruct((s, width), F32),
                              jax.ShapeDtypeStruct((1, width), F32), jax.ShapeDtypeStruct((1, width), F32)],
                   compiler_params=_params("parallel", "arbitrary"))(dmixed, a_pre, wmix, pool_scale, gnorm_g)


def _poolmix_wgrad(pooled, da_pre, gd, tk=1024):
    s = pooled.shape[0]
    tk = _tile(tk, s)
    nk = s // tk

    def body(p_ref, d_ref, o_ref, acc):
        k = pl.program_id(1)
        part = lax.dot_general(p_ref[...], d_ref[...], _DOT_DIMS["tn"], preferred_element_type=F32)

        @pl.when(k == 0)
        def _():
            acc[...] = part

        @pl.when(k > 0)
        def _():
            acc[...] += part

        @pl.when(k == nk - 1)
        def _():
            o_ref[...] = acc[...].astype(o_ref.dtype)

    blk = pl.BlockSpec((tk, gd), lambda g, k: (k, g))
    return _pallas(body, name="poolmix_wgrad", grid=(N_POOL_GROUPS, nk), in_specs=[blk, blk],
                   out_specs=pl.BlockSpec((None, gd, gd), lambda g, k: (g, 0, 0)),
                   out_shape=jax.ShapeDtypeStruct((N_POOL_GROUPS, gd, gd), BF16),
                   scratch_shapes=[pltpu.VMEM((gd, gd), F32)],
                   compiler_params=_params("parallel", "arbitrary"))(pooled, da_pre)


def _head_mean(v):
    parts = []
    for q in range(v.shape[1] // CONV_HEAD_DIM):
        m = jnp.mean(v[:, q * CONV_HEAD_DIM:(q + 1) * CONV_HEAD_DIM], axis=-1, keepdims=True)
        parts.append(jnp.broadcast_to(m, (v.shape[0], CONV_HEAD_DIM)))
    return parts[0] if len(parts) == 1 else jnp.concatenate(parts, axis=1)


def _conv_fwd(proj, mixed, conv_w, conv_b, gnorm_g, s, width, cb, ch):
    nblk = width // cb

    def body(b_ref, c_ref, u_ref, w_ref, cb_ref, g_ref, mixed_in, o_ref, pad):
        del mixed_in
        _zero_pads(pad, s)
        pad[PAD_ROWS:PAD_ROWS + s, :] = c_ref[...] * u_ref[...]
        w = w_ref[...]
        for c0 in range(0, s, ch):
            base = PAD_ROWS + c0
            conv = (w[0:1] * pad[base - 1:base - 1 + ch, :] + w[1:2] * pad[base:base + ch, :]
                    + w[2:3] * pad[base + 1:base + 1 + ch, :] + cb_ref[...])
            bo = b_ref[c0:c0 + ch, :] * conv
            n = bo * lax.rsqrt(_head_mean(bo * bo) + EPS)
            o_ref[c0:c0 + ch, :] = (n * g_ref[...]).astype(o_ref.dtype)

    def part(p):
        return pl.BlockSpec((s, cb), lambda j: (0, p * nblk + j))

    vec = pl.BlockSpec((1, cb), lambda j: (0, j))
    return _pallas(body, name="conv_fwd", grid=(nblk,),
                   in_specs=[part(1), part(2), part(3), pl.BlockSpec((3, cb), lambda j: (0, j)), vec, vec,
                             pl.BlockSpec(memory_space=pl.ANY)],
                   out_specs=part(1), out_shape=jax.ShapeDtypeStruct(mixed.shape, mixed.dtype),
                   input_output_aliases={6: 0},
                   scratch_shapes=[pltpu.VMEM((s + 2 * PAD_ROWS, cb), F32)],
                   compiler_params=_params("parallel"))(proj, proj, proj, conv_w, conv_b, gnorm_g, mixed)


def _conv_bwd(dmixed, proj, conv_w, conv_b, gnorm_g, s, width, cb, ch):
    nblk = width // cb

    def body(dm_ref, b_ref, c_ref, u_ref, w_ref, cb_ref, g_ref, dproj_ref, dw_ref, dcb_ref, dg_ref,
             pad_cu, pad_dconv, db_buf, dc_buf, du_buf, sems):
        j = pl.program_id(0)
        _zero_pads(pad_cu, s)
        _zero_pads(pad_dconv, s)
        pad_cu[PAD_ROWS:PAD_ROWS + s, :] = c_ref[...] * u_ref[...]
        w, gv = w_ref[...], g_ref[...]
        zero = jnp.zeros((1, cb), F32)
        dw0, dw1, dw2, dcb, dg = zero, zero, zero, zero, zero
        for c0 in range(0, s, ch):
            base = PAD_ROWS + c0
            cu_prev, cu_here, cu_next = (pad_cu[base - 1:base - 1 + ch, :], pad_cu[base:base + ch, :],
                                         pad_cu[base + 1:base + 1 + ch, :])
            conv = w[0:1] * cu_prev + w[1:2] * cu_here + w[2:3] * cu_next + cb_ref[...]
            bg = b_ref[c0:c0 + ch, :]
            bo = bg * conv
            r = lax.rsqrt(_head_mean(bo * bo) + EPS)
            n = bo * r
            dm = dm_ref[c0:c0 + ch, :]
            dg = dg + _colsum(dm * n)
            dn = dm * gv
            dbo = r * (dn - n * _head_mean(dn * n))
            db_buf[c0:c0 + ch, :] = (dbo * conv).astype(BF16)
            dconv = dbo * bg
            pad_dconv[base:base + ch, :] = dconv
            dcb = dcb + _colsum(dconv)
            dw0 = dw0 + _colsum(dconv * cu_prev)
            dw1 = dw1 + _colsum(dconv * cu_here)
            dw2 = dw2 + _colsum(dconv * cu_next)
        dw_ref[0:1, :] = dw0
        dw_ref[1:2, :] = dw1
        dw_ref[2:3, :] = dw2
        dcb_ref[...] = dcb
        dg_ref[...] = dg
        for c0 in range(0, s, ch):
            base = PAD_ROWS + c0
            dcu = (w[0:1] * pad_dconv[base + 1:base + 1 + ch, :] + w[1:2] * pad_dconv[base:base + ch, :]
                   + w[2:3] * pad_dconv[base - 1:base - 1 + ch, :])
            dc_buf[c0:c0 + ch, :] = (dcu * u_ref[c0:c0 + ch, :]).astype(BF16)
            du_buf[c0:c0 + ch, :] = (dcu * c_ref[c0:c0 + ch, :]).astype(BF16)
        copies = []
        for p, buf in enumerate((db_buf, dc_buf, du_buf)):
            col = pl.multiple_of((p + 1) * width + j * cb, CONV_HEAD_DIM)
            copies.append(pltpu.make_async_copy(buf, dproj_ref.at[:, pl.ds(col, cb)], sems.at[p]))
            copies[-1].start()
        for cp in copies:
            cp.wait()

    def part(p):
        return pl.BlockSpec((s, cb), lambda j: (0, p * nblk + j))

    vec = pl.BlockSpec((1, cb), lambda j: (0, j))
    w_spec = pl.BlockSpec((3, cb), lambda j: (0, j))
    return _pallas(body, name="conv_bwd", grid=(nblk,),
                   in_specs=[part(1), part(1), part(2), part(3), w_spec, vec, vec],
                   out_specs=[pl.BlockSpec(memory_space=pl.ANY), w_spec, vec, vec],
                   out_shape=[jax.ShapeDtypeStruct((s, 4 * width), BF16), jax.ShapeDtypeStruct((3, width), F32),
                              jax.ShapeDtypeStruct((1, width), F32), jax.ShapeDtypeStruct((1, width), F32)],
                   scratch_shapes=[pltpu.VMEM((s + 2 * PAD_ROWS, cb), F32), pltpu.VMEM((s + 2 * PAD_ROWS, cb), F32),
                                   pltpu.VMEM((s, cb), BF16), pltpu.VMEM((s, cb), BF16), pltpu.VMEM((s, cb), BF16),
                                   pltpu.SemaphoreType.DMA((3,))],
                   compiler_params=_params("arbitrary"))(dmixed, proj, proj, proj, conv_w, conv_b, gnorm_g)


def _adamw(w, g, m, v):
    m = ADAM_B1 * m + (1.0 - ADAM_B1) * g
    v = ADAM_B2 * v + (1.0 - ADAM_B2) * (g * g)
    m_hat = m / (1.0 - ADAM_B1 ** ADAM_STEP)
    v_hat = v / (1.0 - ADAM_B2 ** ADAM_STEP)
    delta = -ADAM_LR * (m_hat / (jnp.sqrt(v_hat) + ADAM_EPS) + ADAM_WD * w)
    return delta, m, v


def _ada_fwd(c_rows, w, b, tn=512):
    rows, d = c_rows.shape
    n = w.shape[1]
    tn = _tile(tn, n)

    def body(c_ref, w_ref, b_ref, o_ref):
        cv = c_ref[...]
        act = (cv * jax.nn.sigmoid(cv)).astype(BF16)
        o_ref[...] = jnp.dot(act, w_ref[...].astype(BF16), preferred_element_type=F32) + b_ref[...]

    return _pallas(body, name="ada_fwd", grid=(n // tn,),
                   in_specs=[pl.BlockSpec((rows, d), lambda j: (0, 0)), pl.BlockSpec((d, tn), lambda j: (0, j)),
                             pl.BlockSpec((1, tn), lambda j: (0, j))],
                   out_specs=pl.BlockSpec((rows, tn), lambda j: (0, j)),
                   out_shape=jax.ShapeDtypeStruct((rows, n), F32), compiler_params=_params("parallel"))(c_rows, w, b)


def _ada_bwd_adam(c_cols, dmod, w, m, v, tr=512, tn=1024):
    d, rows = c_cols.shape
    n = w.shape[1]
    tr, tn = _tile(tr, d), _tile(tn, n)

    def body(c_ref, dm_ref, w_ref, m_ref, v_ref, g_ref, dl_ref, nm_ref, nv_ref):
        cv = c_ref[...]
        act = (cv * jax.nn.sigmoid(cv)).astype(BF16)
        g = jnp.dot(act, dm_ref[...].astype(BF16), preferred_element_type=F32)
        g_ref[...] = g
        dl_ref[...], nm_ref[...], nv_ref[...] = _adamw(w_ref[...], g, m_ref[...], v_ref[...])

    blk = pl.BlockSpec((tr, tn), lambda i, j: (i, j))
    shape = jax.ShapeDtypeStruct((d, n), F32)
    return _pallas(body, name="ada_bwd_adam", grid=(d // tr, n // tn),
                   in_specs=[pl.BlockSpec((tr, rows), lambda i, j: (i, 0)), pl.BlockSpec((rows, tn), lambda i, j: (0, j)),
                             blk, blk, blk],
                   out_specs=[blk] * 4, out_shape=[shape] * 4,
                   compiler_params=_params("parallel", "parallel"))(c_cols, dmod, w, m, v)


def _reduce_adam(name, pieces, w, m, v, tr=256, tc=1024):
    r, c = w.shape
    tr, tc = _tile(tr, r), _tile(tc, c)

    def body(p_ref, w_ref, m_ref, v_ref, g_ref, dl_ref, nm_ref, nv_ref):
        g = p_ref[0].astype(F32)
        for j in range(1, N_DEV):
            g = g + p_ref[j].astype(F32)
        g_ref[...] = g
        dl_ref[...], nm_ref[...], nv_ref[...] = _adamw(w_ref[...], g, m_ref[...], v_ref[...])

    blk = pl.BlockSpec((tr, tc), lambda i, j: (i, j))
    shape = jax.ShapeDtypeStruct((r, c), F32)
    return _pallas(body, name=name, grid=(r // tr, c // tc),
                   in_specs=[pl.BlockSpec((N_DEV, tr, tc), lambda i, j: (0, i, j)), blk, blk, blk],
                   out_specs=[blk] * 4, out_shape=[shape] * 4,
                   compiler_params=_params("parallel", "parallel"))(pieces, w, m, v)


def _sum_devices(parts):
    n = parts.shape[1]

    def body(p_ref, o_ref):
        acc = p_ref[0:1, :]
        for j in range(1, N_DEV):
            acc = acc + p_ref[j:j + 1, :]
        o_ref[...] = acc

    return _pallas(body, name="sum_devices", out_shape=jax.ShapeDtypeStruct((1, n), F32),
                   compiler_params=pltpu.CompilerParams(vmem_limit_bytes=VMEM_LIMIT_BYTES))(parts)


def _adam_small(name, g, w, m, v):
    def body(g_ref, w_ref, m_ref, v_ref, dl_ref, nm_ref, nv_ref):
        dl_ref[...], nm_ref[...], nv_ref[...] = _adamw(w_ref[...], g_ref[...], m_ref[...], v_ref[...])

    shape = jax.ShapeDtypeStruct(w.shape, F32)
    return _pallas(body, name=name, out_shape=[shape] * 3,
                   compiler_params=pltpu.CompilerParams(vmem_limit_bytes=VMEM_LIMIT_BYTES))(g, w, m, v)


def kernel(x, c, w_ada, b_ada, norm1_g, w_in, pool_mix_w, pool_scale, conv_w, conv_b, gnorm_pool_g, gnorm_conv_g, w_out, norm2_g, w_mlp_in, w_mlp_out, final_g, loss_target, m_w_ada, m_b_ada, m_norm1_g, m_w_in, m_pool_mix_w, m_pool_scale, m_conv_w, m_conv_b, m_gnorm_pool_g, m_gnorm_conv_g, m_w_out, m_norm2_g, m_w_mlp_in, m_w_mlp_out, m_final_g, v_w_ada, v_b_ada, v_norm1_g, v_w_in, v_pool_mix_w, v_pool_scale, v_conv_w, v_conv_b, v_gnorm_pool_g, v_gnorm_conv_g, v_w_out, v_norm2_g, v_w_mlp_in, v_w_mlp_out, v_final_g):
    s, d = x.shape[1], x.shape[2]
    width = d // 2
    gd = width // N_POOL_GROUPS
    d_ff = w_mlp_in.shape[2] * N_DEV
    n_proj = w_in.shape[2] * N_DEV
    ada_cols = w_ada.shape[2]
    conv_cols = conv_w.shape[2]
    assert n_proj == 4 * width and ada_cols * N_DEV == N_MOD * d and d_ff % N_DEV == 0
    assert width % CONV_HEAD_DIM == 0 and s % 8 == 0
    seq_chunk = _tile(512, s)
    pool_cb = _tile(256, gd)
    conv_cb = CONV_HEAD_DIM

    me = 4 * lax.axis_index("x") + 2 * lax.axis_index("y") + lax.axis_index("c")
    x2d, target = x[0], loss_target[0]

    gathered = _exchange(
        "gather_weights",
        [w_in[0].astype(BF16), w_out[0].astype(BF16), w_mlp_in[0].astype(BF16), w_mlp_out[0].astype(BF16),
         pool_mix_w[0].astype(BF16), conv_w[0], c],
        ["gather"] * 7)
    w_in_all, w_out_all, w1_all, w2_all, wmix_all, conv_w_all, c_all = gathered
    w_out_full = w_out_all.reshape(d, d)
    w2_full = w2_all.reshape(d_ff, d)
    wmix_full = jnp.transpose(wmix_all, (1, 0, 2, 3)).reshape(N_POOL_GROUPS, gd, gd)
    conv_w_full = jnp.transpose(conv_w_all, (1, 0, 2)).reshape(3, width)
    c_rows = jnp.concatenate([c_all.reshape(N_DEV, d), jnp.zeros((N_DEV, d), F32)], axis=0)

    b_mine = lax.dynamic_slice(b_ada, (0, me * ada_cols), (1, ada_cols))
    mod_part = _ada_fwd(c_rows, w_ada[0], b_mine)
    (mod_all,) = _exchange("scatter_mod", [mod_part[:N_DEV].reshape(N_DEV, 1, ada_cols)], ["a2a"])
    mod = mod_all.reshape(1, N_MOD * d)
    shift1, scale1, gate1, shift2, scale2, gate2 = [mod[:, i * d:(i + 1) * d] for i in range(N_MOD)]

    h1 = _norm_mod("norm1_fwd", x2d, norm1_g, scale1, shift1)
    (proj,) = _mm_nn("in_proj", h1, w_in_all, n_proj, True, [jax.ShapeDtypeStruct((s, n_proj), F32)], _store(F32))
    pooled = _pool_fwd(proj, s, gd, pool_cb, seq_chunk)
    a_pre, mixed = _poolmix_fwd(pooled, wmix_full, pool_scale, gnorm_pool_g, d)
    mixed = _conv_fwd(proj, mixed, conv_w_full, conv_b, gnorm_conv_g, s, width, conv_cb, seq_chunk)

    def residual_specs(tm, tn):
        return [pl.BlockSpec((tm, tn), lambda i, j, k: (i, j)), pl.BlockSpec((1, tn), lambda i, j, k: (0, j))]

    sd_f32 = jax.ShapeDtypeStruct((s, d), F32)
    attn, x_mid = _mm_nn("out_proj", mixed, w_out_full, d, False, [sd_f32, sd_f32], _residual_epilogue,
                         extras=(x2d, gate1), extra_specs=residual_specs)
    h2 = _norm_mod("norm2_fwd", x_mid, norm2_g, scale2, shift2)
    sf_bf16 = jax.ShapeDtypeStruct((s, d_ff), BF16)
    relu, hid = _mm_nn("mlp_in", h2, w1_all, d_ff, True, [sf_bf16, sf_bf16], _relu2_epilogue)
    mlp, x_last = _mm_nn("mlp_out", hid, w2_full, d, False, [sd_f32, sd_f32], _residual_epilogue,
                         extras=(x_mid, gate2), extra_specs=residual_specs)

    dx_last, dmlp, d_final_g, dgate2, loss_row = _loss_head(x_last, target, final_g.reshape(1, d), gate2, mlp)

    def relu_specs(tm, tn):
        return [pl.BlockSpec((tm, tn), lambda i, j, k: (i, j))]

    (dhpre,) = _mm_nt("mlp_out_dx", dmlp, w2_full, d_ff, False, [sf_bf16], _relu2_bwd_epilogue,
                      extras=(relu,), extra_specs=relu_specs)
    g_w2 = _mm_tn("mlp_out_dw", hid, dmlp, False)
    (dh2,) = _mm_nt("mlp_in_dx", dhpre, w1_all, d, True, [sd_f32], _store(F32))
    g_w1 = _mm_tn("mlp_in_dw", h2, dhpre, True)
    dx_mid, dshift2, dscale2, d_norm2_g, dattn, dgate1 = _norm_mod_bwd(
        "norm2_bwd", dh2, x_mid, norm2_g, scale2, dx_last, branch=attn, gate=gate1)

    (dmixed,) = _mm_nt("out_proj_dx", dattn, w_out_full, d, False, [sd_f32], _store(F32))
    g_w_out = _mm_tn("out_proj_dw", mixed, dattn, False)
    dproj, d_conv_w, d_conv_b, d_gnorm_conv = _conv_bwd(dmixed, proj, conv_w_full, conv_b, gnorm_conv_g,
                                                        s, width, conv_cb, seq_chunk)
    da_pre, dpooled, d_pool_scale, d_gnorm_pool = _poolmix_bwd(dmixed, a_pre, wmix_full, pool_scale, gnorm_pool_g)
    g_wmix = _poolmix_wgrad(pooled, da_pre, gd)
    dproj = _pool_bwd(dpooled, dproj, s, gd, pool_cb, seq_chunk)

    (dh1,) = _mm_nt("in_proj_dx", dproj, w_in_all, d, True, [sd_f32], _store(F32))
    g_w_in = _mm_tn("in_proj_dw", h1, dproj, True)
    grad_x, dshift1, dscale1, d_norm1_g = _norm_mod_bwd("norm1_bwd", dh1, x2d, norm1_g, scale1, dx_mid)

    rows_mix = gd // N_DEV
    g_wmix_split = jnp.transpose(g_wmix.reshape(N_POOL_GROUPS, N_DEV, rows_mix, gd), (1, 0, 2, 3))
    g_wmix_split = g_wmix_split.reshape(N_DEV, N_POOL_GROUPS * rows_mix, gd)
    p_w_in, p_w_out, p_w1, p_w2, p_wmix = _exchange(
        "scatter_grads",
        [g_w_in, g_w_out.reshape(N_DEV, d // N_DEV, d), g_w1, g_w2.reshape(N_DEV, d_ff // N_DEV, d), g_wmix_split],
        ["a2a"] * 5)

    out_w_in = _reduce_adam("adam_w_in", p_w_in, w_in[0], m_w_in[0], v_w_in[0])
    out_w_out = _reduce_adam("adam_w_out", p_w_out, w_out[0], m_w_out[0], v_w_out[0])
    out_w1 = _reduce_adam("adam_w_mlp_in", p_w1, w_mlp_in[0], m_w_mlp_in[0], v_w_mlp_in[0])
    out_w2 = _reduce_adam("adam_w_mlp_out", p_w2, w_mlp_out[0], m_w_mlp_out[0], v_w_mlp_out[0])
    mix_shape = (N_POOL_GROUPS * rows_mix, gd)
    out_wmix = _reduce_adam("adam_pool_mix", p_wmix, pool_mix_w.reshape(mix_shape), m_pool_mix_w.reshape(mix_shape),
                            v_pool_mix_w.reshape(mix_shape))
    out_wmix = [a.reshape(pool_mix_w.shape) for a in out_wmix]

    loss_pad = jnp.concatenate([loss_row[:, :1], jnp.zeros((1, 127), F32)], axis=1)
    dmod = jnp.concatenate([dshift1, dscale1, dgate1, dshift2, dscale2, dgate2], axis=1)
    small = jnp.concatenate([dmod, d_norm1_g, d_pool_scale, d_conv_b, d_gnorm_pool, d_gnorm_conv, d_norm2_g,
                             d_final_g, d_conv_w.reshape(1, 3 * width), loss_pad], axis=1)
    (small_all,) = _exchange("gather_small", [small], ["gather"])
    small_all = small_all.reshape(N_DEV, small.shape[1])
    small_sum = _sum_devices(small_all)

    n_rep = (N_MOD + 1) * d + 4 * width + 2 * d
    loss = small_sum[0, n_rep + 3 * width]
    rep_names_w = [b_ada, norm1_g, pool_scale, conv_b, gnorm_pool_g, gnorm_conv_g, norm2_g, final_g.reshape(1, d)]
    rep_names_m = [m_b_ada, m_norm1_g, m_pool_scale, m_conv_b, m_gnorm_pool_g, m_gnorm_conv_g, m_norm2_g,
                   m_final_g.reshape(1, d)]
    rep_names_v = [v_b_ada, v_norm1_g, v_pool_scale, v_conv_b, v_gnorm_pool_g, v_gnorm_conv_g, v_norm2_g,
                   v_final_g.reshape(1, d)]
    rep_grad = small_sum[:, :n_rep]
    rep_delta, rep_m, rep_v = _adam_small("adam_replicated", rep_grad, jnp.concatenate(rep_names_w, axis=1),
                                          jnp.concatenate(rep_names_m, axis=1), jnp.concatenate(rep_names_v, axis=1))

    def split_rep(vec):
        out, off = [], 0
        for wgt in rep_names_w:
            n = wgt.shape[1]
            out.append(vec[:, off:off + n])
            off += n
        out[-1] = out[-1].reshape(d)
        return out

    conv_grad_full = small_sum[:, n_rep:n_rep + 3 * width].reshape(3, width)
    g_conv_w = lax.dynamic_slice(conv_grad_full, (0, me * conv_cols), (3, conv_cols))
    g_conv_w8 = jnp.concatenate([g_conv_w, jnp.zeros((5, conv_cols), F32)], axis=0)

    def pad8(a):
        return jnp.concatenate([a[0], jnp.zeros((5, conv_cols), F32)], axis=0)

    conv_delta, conv_m, conv_v = _adam_small("adam_conv_w", g_conv_w8, pad8(conv_w), pad8(m_conv_w), pad8(v_conv_w))

    dmod_all = small_all[:, :N_MOD * d]
    dmod_mine = lax.dynamic_slice(dmod_all, (0, me * ada_cols), (N_DEV, ada_cols))
    dmod_rows = jnp.concatenate([dmod_mine, jnp.zeros((N_DEV, ada_cols), F32)], axis=0)
    out_ada = _ada_bwd_adam(jnp.transpose(c_rows), dmod_rows, w_ada[0], m_w_ada[0], v_w_ada[0])

    rep_all = [split_rep(rep_grad), split_rep(rep_delta), split_rep(rep_m), split_rep(rep_v)]
    conv_all = [g_conv_w[None], conv_delta[None, :3], conv_m[None, :3], conv_v[None, :3]]
    outs = [loss, grad_x[None]]
    for kind in range(4):
        b_ada_o, norm1_o, pool_scale_o, conv_b_o, gpool_o, gconv_o, norm2_o, final_o = rep_all[kind]
        outs += [out_ada[kind][None], b_ada_o, norm1_o, out_w_in[kind][None], out_wmix[kind], pool_scale_o,
                 conv_all[kind], conv_b_o, gpool_o, gconv_o, out_w_out[kind][None], norm2_o, out_w1[kind][None],
                 out_w2[kind][None], final_o]
    return tuple(outs)
```

```python
import jax
import jax.numpy as jnp
from jax import lax
from jax.experimental import pallas as pl
from jax.experimental.pallas import tpu as pltpu

F32 = jnp.float32
BF16 = jnp.bfloat16
MESH = pl.DeviceIdType.MESH

N_DEV = 8
N_MOD = 6
EPS = 1e-6
POOL_WINDOWS = (2, 4, 8, 16)
N_POOL_GROUPS = len(POOL_WINDOWS)
CONV_HEAD_DIM = 128
PAD_ROWS = 16

ADAM_LR = 0.001
ADAM_B1 = 0.9
ADAM_B2 = 0.999
ADAM_EPS = 1e-08
ADAM_WD = 0.01
ADAM_STEP = 10

VMEM_LIMIT_BYTES = 56 * 1024 * 1024
MM_TM, MM_TN, MM_TK = 1024, 512, 4096


def _pallas(body, deps=(), **kw):
    if not deps:
        return pl.pallas_call(body, **kw)
    n_in = len(kw["in_specs"])

    def with_deps(*refs):
        body(*refs[:n_in], *refs[n_in + len(deps):])

    kw["in_specs"] = list(kw["in_specs"]) + [pl.BlockSpec(memory_space=pl.ANY)] * len(deps)
    call = pl.pallas_call(with_deps, **kw)
    return lambda *operands: call(*operands, *deps)


def _params(*sem):
    return pltpu.CompilerParams(dimension_semantics=sem, vmem_limit_bytes=VMEM_LIMIT_BYTES)


def _tile(pref, dim):
    if dim <= pref:
        return dim
    for t in range(pref - pref % 128, 0, -128):
        if dim % t == 0:
            return t
    return dim


def _exchange(name, arrays, modes, deps=()):
    n = len(arrays)
    out_shape = []
    for a, mode in zip(arrays, modes):
        piece = a.shape if mode == "gather" else a.shape[1:]
        out_shape.append(jax.ShapeDtypeStruct((N_DEV,) + tuple(piece), a.dtype))

    def body(*refs):
        srcs, dsts = refs[:n], refs[n:2 * n]
        send_sems, recv_sems, local_sems = refs[2 * n:]
        x, y, c = lax.axis_index("x"), lax.axis_index("y"), lax.axis_index("c")
        me = 4 * x + 2 * y + c
        copies = []
        for i in range(n):
            gather = modes[i] == "gather"
            local = pltpu.make_async_copy(srcs[i] if gather else srcs[i].at[me], dsts[i].at[me], local_sems.at[i])
            local.start()
            copies.append(local)
            for k in range(1, N_DEV):
                kx, ky, kc = (k >> 2) & 1, (k >> 1) & 1, k & 1
                peer = (1 - x if kx else x, 1 - y if ky else y, 1 - c if kc else c)
                peer_idx = 4 * peer[0] + 2 * peer[1] + peer[2]
                remote = pltpu.make_async_remote_copy(
                    src_ref=srcs[i] if gather else srcs[i].at[peer_idx],
                    dst_ref=dsts[i].at[me],
                    send_sem=send_sems.at[i * (N_DEV - 1) + k - 1],
                    recv_sem=recv_sems.at[i * (N_DEV - 1) + k - 1],
                    device_id=peer, device_id_type=MESH)
                remote.start()
                copies.append(remote)
        for cp in copies:
            cp.wait()

    any_spec = pl.BlockSpec(memory_space=pl.ANY)
    return _pallas(
        body, deps, name=name, out_shape=out_shape,
        in_specs=[any_spec] * n, out_specs=[any_spec] * n,
        scratch_shapes=[pltpu.SemaphoreType.DMA((n * (N_DEV - 1),)),
                        pltpu.SemaphoreType.DMA((n * (N_DEV - 1),)),
                        pltpu.SemaphoreType.DMA((n,))],
    )(*arrays)


_HBM = pl.BlockSpec(memory_space=pltpu.HBM)
_SEM = pl.BlockSpec(memory_space=pltpu.SEMAPHORE)
_EFFECT = pltpu.SideEffectType.DATAFLOW_SIDE_EFFECTING


def _push_copies(src_ref, land_ref, send_sems, recv_sems, a2a):
    x, y, c = lax.axis_index("x"), lax.axis_index("y"), lax.axis_index("c")
    me = 4 * x + 2 * y + c
    copies = []
    for k in range(1, N_DEV):
        kx, ky, kc = (k >> 2) & 1, (k >> 1) & 1, k & 1
        peer = (1 - x if kx else x, 1 - y if ky else y, 1 - c if kc else c)
        peer_idx = 4 * peer[0] + 2 * peer[1] + peer[2]
        copies.append(pltpu.make_async_remote_copy(
            src_ref=src_ref.at[peer_idx] if a2a else src_ref, dst_ref=land_ref.at[me],
            send_sem=send_sems.at[k - 1], recv_sem=recv_sems.at[k - 1], device_id=peer, device_id_type=MESH))
    return copies


def _push_start(name, src, land, a2a, deps=()):
    def body(src_ref, land_ref, send_sems, recv_sems, src_thru, land_thru, token):
        del src_thru, land_thru
        for cp in _push_copies(src_ref, land_ref, send_sems, recv_sems, a2a):
            cp.start()
        token[...] = jnp.zeros_like(token)

    return _pallas(
        body, deps, name=name,
        out_shape=(pltpu.SemaphoreType.DMA((N_DEV - 1,)), pltpu.SemaphoreType.DMA((N_DEV - 1,)),
                   pltpu.HBM(src.shape, src.dtype), pltpu.HBM(land.shape, land.dtype),
                   jax.ShapeDtypeStruct((8, 128), F32)),
        in_specs=(_HBM, _HBM), out_specs=(_SEM, _SEM, _HBM, _HBM, pl.BlockSpec(memory_space=pltpu.VMEM)),
        input_output_aliases={0: 2, 1: 3},
        compiler_params=pltpu.CompilerParams(has_side_effects=_EFFECT),
    )(pltpu.with_memory_space_constraint(src, pltpu.HBM), pltpu.with_memory_space_constraint(land, pltpu.HBM))


def _push_wait(name, started, after, a2a):
    send_sems, recv_sems, src, land, _ = started

    def body(src_ref, land_ref, send_sems, recv_sems, after_ref, src_dead, land_out):
        del after_ref, src_dead, land_out
        for cp in _push_copies(src_ref, land_ref, send_sems, recv_sems, a2a):
            cp.wait_send()
            cp.wait_recv()

    return _pallas(
        body, name=name,
        out_shape=(pltpu.HBM(src.shape, src.dtype), pltpu.HBM(land.shape, land.dtype)),
        in_specs=(_HBM, _HBM, _SEM, _SEM, pl.BlockSpec(memory_space=pl.ANY)), out_specs=(_HBM, _HBM),
        input_output_aliases={0: 0, 1: 1},
        compiler_params=pltpu.CompilerParams(has_side_effects=_EFFECT),
    )(src, land, send_sems, recv_sems, after)[1]


def _landing(own, me, a2a):
    piece = lax.dynamic_index_in_dim(own, me, 0, keepdims=True) if a2a else own[None]
    land = lax.empty((N_DEV,) + piece.shape[1:], own.dtype)
    return lax.dynamic_update_slice(land, piece, (me,) + (0,) * (piece.ndim - 1))


_DOT_DIMS = {"nn": (((1,), (0,)), ((), ())), "nt": (((1,), (1,)), ((), ())), "tn": (((0,), (0,)), ((), ()))}


def _matmul(name, mode, operands, in_specs, out_shape, out_specs, grid, acc_shape, epilogue, deps=()):
    n_in, n_out, nk = len(operands), len(out_shape), grid[2]
    dims = _DOT_DIMS[mode]

    def body(*refs):
        a_ref, b_ref = refs[0], refs[1]
        extras, outs = refs[2:n_in], refs[n_in:n_in + n_out]
        part = lax.dot_general(a_ref[...], b_ref[...], dims, preferred_element_type=F32)
        if nk == 1:
            epilogue(part, extras, outs)
            return
        acc = refs[-1]
        k = pl.program_id(2)

        @pl.when(k == 0)
        def _():
            acc[...] = part

        @pl.when(jnp.logical_and(k > 0, k < nk - 1))
        def _():
            acc[...] += part

        @pl.when(k == nk - 1)
        def _():
            epilogue(acc[...] + part, extras, outs)

    return _pallas(body, deps, name=name, grid=grid, in_specs=in_specs, out_specs=out_specs, out_shape=out_shape,
                   scratch_shapes=[pltpu.VMEM(acc_shape, F32)] if nk > 1 else [],
                   compiler_params=_params("parallel", "parallel", "arbitrary"))(*operands)


def _store(dtype):
    def epilogue(acc, extras, outs):
        outs[0][...] = acc.astype(dtype)
    return epilogue


def _residual_epilogue(acc, extras, outs):
    x_ref, gate_ref = extras
    outs[0][...] = acc
    outs[1][...] = x_ref[...] + gate_ref[...] * acc


def _relu2_epilogue(acc, extras, outs):
    r = jnp.maximum(acc, 0.0)
    outs[0][...] = r.astype(outs[0].dtype)
    outs[1][...] = (r * r).astype(outs[1].dtype)


def _relu2_bwd_epilogue(acc, extras, outs):
    outs[0][...] = (acc * (2.0 * extras[0][...].astype(F32))).astype(outs[0].dtype)


def _no_extra_specs(tm, tn):
    return []


def _mm_nn(name, a, b, n_total, b_split, out_shape, epilogue, extras=(), extra_specs=_no_extra_specs, tm=MM_TM, tn=MM_TN, tk=MM_TK,
           deps=()):
    m, kdim = a.shape
    tm, tk = _tile(tm, m), _tile(tk, kdim)
    if b_split:
        piece = b.shape[2]
        tn = _tile(tn, piece)
        per = piece // tn
        b_spec = pl.BlockSpec((None, tk, tn), lambda i, j, k: (j // per, k, j % per))
    else:
        tn = _tile(tn, n_total)
        b_spec = pl.BlockSpec((tk, tn), lambda i, j, k: (k, j))
    in_specs = [pl.BlockSpec((tm, tk), lambda i, j, k: (i, k)), b_spec] + list(extra_specs(tm, tn))
    out_specs = [pl.BlockSpec((tm, tn), lambda i, j, k: (i, j)) for _ in out_shape]
    return _matmul(name, "nn", (a, b) + tuple(extras), in_specs, out_shape, out_specs,
                   (m // tm, n_total // tn, kdim // tk), (tm, tn), epilogue, deps)


def _mm_nt(name, a, b, n_total, b_split, out_shape, epilogue, extras=(), extra_specs=_no_extra_specs, tm=MM_TM, tn=MM_TN, tk=MM_TK,
           deps=()):
    m, kdim = a.shape
    tm, tn = _tile(tm, m), _tile(tn, n_total)
    if b_split:
        piece = b.shape[2]
        tk = _tile(tk, piece)
        per = piece // tk
        b_spec = pl.BlockSpec((None, tn, tk), lambda i, j, k: (k // per, j, k % per))
    else:
        tk = _tile(tk, kdim)
        b_spec = pl.BlockSpec((tn, tk), lambda i, j, k: (j, k))
    in_specs = [pl.BlockSpec((tm, tk), lambda i, j, k: (i, k)), b_spec] + list(extra_specs(tm, tn))
    out_specs = [pl.BlockSpec((tm, tn), lambda i, j, k: (i, j)) for _ in out_shape]
    return _matmul(name, "nt", (a, b) + tuple(extras), in_specs, out_shape, out_specs,
                   (m // tm, n_total // tn, kdim // tk), (tm, tn), epilogue, deps)


def _mm_tn(name, a, b, out_split, tm=MM_TM, tn=MM_TN, tk=MM_TK):
    kdim, m = a.shape
    n_total = b.shape[1]
    tm, tk = _tile(tm, m), _tile(tk, kdim)
    if out_split:
        piece = n_total // N_DEV
        tn = _tile(tn, piece)
        per = piece // tn
        out_shape = [jax.ShapeDtypeStruct((N_DEV, m, piece), BF16)]
        out_specs = [pl.BlockSpec((None, tm, tn), lambda i, j, k: (j // per, i, j % per))]
    else:
        tn = _tile(tn, n_total)
        out_shape = [jax.ShapeDtypeStruct((m, n_total), BF16)]
        out_specs = [pl.BlockSpec((tm, tn), lambda i, j, k: (i, j))]
    in_specs = [pl.BlockSpec((tk, tm), lambda i, j, k: (k, i)), pl.BlockSpec((tk, tn), lambda i, j, k: (k, j))]
    return _matmul(name, "tn", (a, b), in_specs, out_shape, out_specs,
                   (m // tm, n_total // tn, kdim // tk), (tm, tn), _store(BF16))[0]


def _rms(xv):
    return lax.rsqrt(jnp.mean(xv * xv, axis=-1, keepdims=True) + EPS)


def _colsum(v):
    return jnp.sum(v, axis=0, keepdims=True)


def _norm_mod(name, x, g, scale, shift, tr=256, deps=()):
    s, d = x.shape
    tr = _tile(tr, s)

    def body(x_ref, g_ref, sc_ref, sh_ref, h_ref):
        xv = x_ref[...]
        h = (xv * _rms(xv)) * g_ref[...]
        h_ref[...] = (h * (1.0 + sc_ref[...]) + sh_ref[...]).astype(h_ref.dtype)

    row = pl.BlockSpec((tr, d), lambda i: (i, 0))
    vec = pl.BlockSpec((1, d), lambda i: (0, 0))
    return _pallas(body, deps, name=name, grid=(s // tr,), in_specs=[row, vec, vec, vec], out_specs=row,
                   out_shape=jax.ShapeDtypeStruct((s, d), BF16), compiler_params=_params("parallel"))(x, g, scale, shift)


def _loss_head(x3, target, gf, gate2, mlp, tr=128):
    s, d = x3.shape
    tr = _tile(tr, s)

    def body(x_ref, t_ref, gf_ref, gate_ref, mlp_ref, dx_ref, dbr_ref, dgf_ref, dgate_ref, loss_ref):
        @pl.when(pl.program_id(0) == 0)
        def _():
            dgf_ref[...] = jnp.zeros_like(dgf_ref)
            dgate_ref[...] = jnp.zeros_like(dgate_ref)
            loss_ref[...] = jnp.zeros_like(loss_ref)

        xv = x_ref[...]
        r = _rms(xv)
        xn = xv * r
        gfv = gf_ref[...]
        err = xn * gfv - t_ref[...]
        loss_ref[...] += 0.5 * _colsum(jnp.mean(err * err, axis=-1, keepdims=True))
        dy = err * (1.0 / d)
        dgf_ref[...] += _colsum(dy * xn)
        dxn = dy * gfv
        dx = r * (dxn - xn * jnp.mean(dxn * xn, axis=-1, keepdims=True))
        dx_ref[...] = dx
        dbr_ref[...] = (dx * gate_ref[...]).astype(dbr_ref.dtype)
        dgate_ref[...] += _colsum(dx * mlp_ref[...])

    row = pl.BlockSpec((tr, d), lambda i: (i, 0))
    vec = pl.BlockSpec((1, d), lambda i: (0, 0))
    return _pallas(
        body, name="loss_head", grid=(s // tr,), in_specs=[row, row, vec, vec, row],
        out_specs=[row, row, vec, vec, pl.BlockSpec((1, 128), lambda i: (0, 0))],
        out_shape=[jax.ShapeDtypeStruct((s, d), F32), jax.ShapeDtypeStruct((s, d), BF16),
                   jax.ShapeDtypeStruct((1, d), F32), jax.ShapeDtypeStruct((1, d), F32),
                   jax.ShapeDtypeStruct((1, 128), F32)],
        compiler_params=_params("arbitrary"))(x3, target, gf, gate2, mlp)


def _norm_mod_bwd(name, dh, xin, g, scale, dx_up, branch=None, gate=None, tr=128, deps=()):
    s, d = xin.shape
    tr = _tile(tr, s)
    with_gate = branch is not None

    def body(*refs):
        dh_ref, x_ref, g_ref, sc_ref, up_ref = refs[:5]
        if with_gate:
            br_ref, gate_ref = refs[5:7]
            dx_ref, dsh_ref, dsc_ref, dg_ref, dbr_ref, dgate_ref = refs[7:]
            sums = (dsh_ref, dsc_ref, dg_ref, dgate_ref)
        else:
            dx_ref, dsh_ref, dsc_ref, dg_ref = refs[5:]
            sums = (dsh_ref, dsc_ref, dg_ref)

        @pl.when(pl.program_id(0) == 0)
        def _():
            for ref in sums:
                ref[...] = jnp.zeros_like(ref)

        xv, dhv, gv = x_ref[...], dh_ref[...], g_ref[...]
        r = _rms(xv)
        xn = xv * r
        one_sc = 1.0 + sc_ref[...]
        dsh_ref[...] += _colsum(dhv)
        dsc_ref[...] += _colsum(dhv * (xn * gv))
        dg_ref[...] += _colsum(dhv * one_sc * xn)
        dxn = dhv * one_sc * gv
        dx = up_ref[...] + r * (dxn - xn * jnp.mean(dxn * xn, axis=-1, keepdims=True))
        dx_ref[...] = dx
        if with_gate:
            dbr_ref[...] = (dx * gate_ref[...]).astype(dbr_ref.dtype)
            dgate_ref[...] += _colsum(dx * br_ref[...])

    row = pl.BlockSpec((tr, d), lambda i: (i, 0))
    vec = pl.BlockSpec((1, d), lambda i: (0, 0))
    vshape = jax.ShapeDtypeStruct((1, d), F32)
    operands = [dh, xin, g, scale, dx_up]
    in_specs = [row, row, vec, vec, row]
    out_shape = [jax.ShapeDtypeStruct((s, d), F32), vshape, vshape, vshape]
    out_specs = [row, vec, vec, vec]
    if with_gate:
        operands += [branch, gate]
        in_specs += [row, vec]
        out_shape += [jax.ShapeDtypeStruct((s, d), BF16), vshape]
        out_specs += [row, vec]
    return _pallas(body, deps, name=name, grid=(s // tr,), in_specs=in_specs, out_specs=out_specs, out_shape=out_shape,
                   compiler_params=_params("arbitrary"))(*operands)


def _window_count(c0, rows, half, s):
    t = c0 + lax.broadcasted_iota(jnp.int32, (rows, 1), 0)
    return (jnp.minimum(t + half, s) - jnp.maximum(t - half, 0)).astype(F32)


def _zero_pads(pad, s):
    zeros = jnp.zeros((PAD_ROWS, pad.shape[1]), pad.dtype)
    pad[0:PAD_ROWS, :] = zeros
    pad[PAD_ROWS + s:PAD_ROWS + s + PAD_ROWS, :] = zeros


def _pool_fwd(proj, s, gd, cb, ch):
    nsub = gd // cb

    def body(v_ref, o_ref, pad):
        g = pl.program_id(0)
        _zero_pads(pad, s)
        pad[PAD_ROWS:PAD_ROWS + s, :] = v_ref[...]
        for gi, window in enumerate(POOL_WINDOWS):
            half = window // 2

            @pl.when(g == gi)
            def _(half=half):
                for c0 in range(0, s, ch):
                    base = PAD_ROWS + c0
                    acc = pad[base - half:base - half + ch, :]
                    for j in range(-half + 1, half):
                        acc = acc + pad[base + j:base + j + ch, :]
                    out = acc / _window_count(c0, ch, half, s) - v_ref[c0:c0 + ch, :]
                    o_ref[c0:c0 + ch, :] = out.astype(o_ref.dtype)

    spec = pl.BlockSpec((s, cb), lambda g, j: (0, g * nsub + j))
    return _pallas(body, name="pool_fwd", grid=(N_POOL_GROUPS, nsub), in_specs=[spec], out_specs=spec,
                   out_shape=jax.ShapeDtypeStruct((s, N_POOL_GROUPS * gd), BF16),
                   scratch_shapes=[pltpu.VMEM((s + 2 * PAD_ROWS, cb), F32)],
                   compiler_params=_params("parallel", "parallel"))(proj)


def _pool_bwd(dpooled, dproj, s, gd, cb, ch):
    nsub = gd // cb

    def body(dp_ref, dproj_in, o_ref, pad):
        del dproj_in
        g = pl.program_id(0)
        _zero_pads(pad, s)
        for gi, window in enumerate(POOL_WINDOWS):
            half = window // 2

            @pl.when(g == gi)
            def _(half=half):
                for c0 in range(0, s, ch):
                    pad[PAD_ROWS + c0:PAD_ROWS + c0 + ch, :] = dp_ref[c0:c0 + ch, :] / _window_count(c0, ch, half, s)
                for c0 in range(0, s, ch):
                    base = PAD_ROWS + c0
                    acc = pad[base - half + 1:base - half + 1 + ch, :]
                    for j in range(-half + 2, half + 1):
                        acc = acc + pad[base + j:base + j + ch, :]
                    o_ref[c0:c0 + ch, :] = (acc - dp_ref[c0:c0 + ch, :]).astype(o_ref.dtype)

    spec = pl.BlockSpec((s, cb), lambda g, j: (0, g * nsub + j))
    return _pallas(body, name="pool_bwd", grid=(N_POOL_GROUPS, nsub),
                   in_specs=[spec, pl.BlockSpec(memory_space=pl.ANY)], out_specs=spec,
                   out_shape=jax.ShapeDtypeStruct(dproj.shape, dproj.dtype), input_output_aliases={1: 0},
                   scratch_shapes=[pltpu.VMEM((s + 2 * PAD_ROWS, cb), F32)],
                   compiler_params=_params("parallel", "parallel"))(dpooled, dproj)


def _poolmix_fwd(pooled, wmix, pool_scale, gnorm_g, d_model, tm=512):
    s = pooled.shape[0]
    gd = wmix.shape[1]
    tm = _tile(tm, s)

    def body(p_ref, w_ref, ps_ref, g_ref, apre_ref, mixed_ref):
        a_pre = jnp.dot(p_ref[...], w_ref[...], preferred_element_type=F32)
        apre_ref[...] = a_pre
        a_out = a_pre * ps_ref[...]
        mixed_ref[...] = ((a_out * _rms(a_out)) * g_ref[...]).astype(mixed_ref.dtype)

    blk = pl.BlockSpec((tm, gd), lambda g, i: (i, g))
    vec = pl.BlockSpec((1, gd), lambda g, i: (0, g))
    return _pallas(body, name="poolmix_fwd", grid=(N_POOL_GROUPS, s // tm),
                   in_specs=[blk, pl.BlockSpec((None, gd, gd), lambda g, i: (g, 0, 0)), vec, vec],
                   out_specs=[blk, blk],
                   out_shape=[jax.ShapeDtypeStruct((s, N_POOL_GROUPS * gd), F32), jax.ShapeDtypeStruct((s, d_model), BF16)],
                   compiler_params=_params("parallel", "parallel"))(pooled, wmix, pool_scale, gnorm_g)


def _poolmix_bwd(dmixed, a_pre, wmix, pool_scale, gnorm_g, tm=512):
    s = a_pre.shape[0]
    gd = wmix.shape[1]
    tm = _tile(tm, s)

    def body(dm_ref, apre_ref, w_ref, ps_ref, g_ref, dapre_ref, dpooled_ref, dps_ref, dg_ref):
        @pl.when(pl.program_id(1) == 0)
        def _():
            dps_ref[...] = jnp.zeros_like(dps_ref)
            dg_ref[...] = jnp.zeros_like(dg_ref)

        a_pre, dm, ps = apre_ref[...], dm_ref[...], ps_ref[...]
        a_out = a_pre * ps
        r = _rms(a_out)
        n = a_out * r
        dg_ref[...] += _colsum(dm * n)
        dn = dm * g_ref[...]
        da_out = r * (dn - n * jnp.mean(dn * n, axis=-1, keepdims=True))
        dps_ref[...] += _colsum(da_out * a_pre)
        da_pre = (da_out * ps).astype(BF16)
        dapre_ref[...] = da_pre
        dpooled_ref[...] = lax.dot_general(da_pre, w_ref[...], _DOT_DIMS["nt"], preferred_element_type=F32)

    blk = pl.BlockSpec((tm, gd), lambda g, i: (i, g))
    vec = pl.BlockSpec((1, gd), lambda g, i: (0, g))
    width = N_POOL_GROUPS * gd
    return _pallas(body, name="poolmix_bwd", grid=(N_POOL_GROUPS, s // tm),
                   in_specs=[blk, blk, pl.BlockSpec((None, gd, gd), lambda g, i: (g, 0, 0)), vec, vec],
                   out_specs=[blk, blk, vec, vec],
                   out_shape=[jax.ShapeDtypeStruct((s, width), BF16), jax.ShapeDtypeStruct((s, width), F32),
                              jax.ShapeDtypeStruct((1, width), F32), jax.ShapeDtypeStruct((1, width), F32)],
                   compiler_params=_params("parallel", "arbitrary"))(dmixed, a_pre, wmix, pool_scale, gnorm_g)


def _poolmix_wgrad(pooled, da_pre, gd, tk=1024):
    s = pooled.shape[0]
    tk = _tile(tk, s)
    nk = s // tk

    def body(p_ref, d_ref, o_ref, acc):
        k = pl.program_id(1)
        part = lax.dot_general(p_ref[...], d_ref[...], _DOT_DIMS["tn"], preferred_element_type=F32)

        @pl.when(k == 0)
        def _():
            acc[...] = part

        @pl.when(k > 0)
        def _():
            acc[...] += part

        @pl.when(k == nk - 1)
        def _():
            o_ref[...] = acc[...].astype(o_ref.dtype)

    blk = pl.BlockSpec((tk, gd), lambda g, k: (k, g))
    return _pallas(body, name="poolmix_wgrad", grid=(N_POOL_GROUPS, nk), in_specs=[blk, blk],
                   out_specs=pl.BlockSpec((None, gd, gd), lambda g, k: (g, 0, 0)),
                   out_shape=jax.ShapeDtypeStruct((N_POOL_GROUPS, gd, gd), BF16),
                   scratch_shapes=[pltpu.VMEM((gd, gd), F32)],
                   compiler_params=_params("parallel", "arbitrary"))(pooled, da_pre)


def _head_mean(v):
    parts = []
    for q in range(v.shape[1] // CONV_HEAD_DIM):
        m = jnp.mean(v[:, q * CONV_HEAD_DIM:(q + 1) * CONV_HEAD_DIM], axis=-1, keepdims=True)
        parts.append(jnp.broadcast_to(m, (v.shape[0], CONV_HEAD_DIM)))
    return parts[0] if len(parts) == 1 else jnp.concatenate(parts, axis=1)


def _conv_fwd(proj, mixed, conv_w, conv_b, gnorm_g, s, width, cb, ch):
    nblk = width // cb

    def body(b_ref, c_ref, u_ref, w_ref, cb_ref, g_ref, mixed_in, o_ref, pad):
        del mixed_in
        _zero_pads(pad, s)
        pad[PAD_ROWS:PAD_ROWS + s, :] = c_ref[...] * u_ref[...]
        w = w_ref[...]
        for c0 in range(0, s, ch):
            base = PAD_ROWS + c0
            conv = (w[0:1] * pad[base - 1:base - 1 + ch, :] + w[1:2] * pad[base:base + ch, :]
                    + w[2:3] * pad[base + 1:base + 1 + ch, :] + cb_ref[...])
            bo = b_ref[c0:c0 + ch, :] * conv
            n = bo * lax.rsqrt(_head_mean(bo * bo) + EPS)
            o_ref[c0:c0 + ch, :] = (n * g_ref[...]).astype(o_ref.dtype)

    def part(p):
        return pl.BlockSpec((s, cb), lambda j: (0, p * nblk + j))

    vec = pl.BlockSpec((1, cb), lambda j: (0, j))
    return _pallas(body, name="conv_fwd", grid=(nblk,),
                   in_specs=[part(1), part(2), part(3), pl.BlockSpec((3, cb), lambda j: (0, j)), vec, vec,
                             pl.BlockSpec(memory_space=pl.ANY)],
                   out_specs=part(1), out_shape=jax.ShapeDtypeStruct(mixed.shape, mixed.dtype),
                   input_output_aliases={6: 0},
                   scratch_shapes=[pltpu.VMEM((s + 2 * PAD_ROWS, cb), F32)],
                   compiler_params=_params("parallel"))(proj, proj, proj, conv_w, conv_b, gnorm_g, mixed)


def _conv_bwd(dmixed, proj, conv_w, conv_b, gnorm_g, s, width, cb, ch, deps=()):
    nblk = width // cb

    def body(dm_ref, b_ref, c_ref, u_ref, w_ref, cb_ref, g_ref, dproj_ref, dw_ref, dcb_ref, dg_ref,
             pad_cu, pad_dconv, db_buf, dc_buf, du_buf, sems):
        j = pl.program_id(0)
        _zero_pads(pad_cu, s)
        _zero_pads(pad_dconv, s)
        pad_cu[PAD_ROWS:PAD_ROWS + s, :] = c_ref[...] * u_ref[...]
        w, gv = w_ref[...], g_ref[...]
        zero = jnp.zeros((1, cb), F32)
        dw0, dw1, dw2, dcb, dg = zero, zero, zero, zero, zero
        for c0 in range(0, s, ch):
            base = PAD_ROWS + c0
            cu_prev, cu_here, cu_next = (pad_cu[base - 1:base - 1 + ch, :], pad_cu[base:base + ch, :],
                                         pad_cu[base + 1:base + 1 + ch, :])
            conv = w[0:1] * cu_prev + w[1:2] * cu_here + w[2:3] * cu_next + cb_ref[...]
            bg = b_ref[c0:c0 + ch, :]
            bo = bg * conv
            r = lax.rsqrt(_head_mean(bo * bo) + EPS)
            n = bo * r
            dm = dm_ref[c0:c0 + ch, :]
            dg = dg + _colsum(dm * n)
            dn = dm * gv
            dbo = r * (dn - n * _head_mean(dn * n))
            db_buf[c0:c0 + ch, :] = (dbo * conv).astype(BF16)
            dconv = dbo * bg
            pad_dconv[base:base + ch, :] = dconv
            dcb = dcb + _colsum(dconv)
            dw0 = dw0 + _colsum(dconv * cu_prev)
            dw1 = dw1 + _colsum(dconv * cu_here)
            dw2 = dw2 + _colsum(dconv * cu_next)
        dw_ref[0:1, :] = dw0
        dw_ref[1:2, :] = dw1
        dw_ref[2:3, :] = dw2
        dcb_ref[...] = dcb
        dg_ref[...] = dg
        for c0 in range(0, s, ch):
            base = PAD_ROWS + c0
            dcu = (w[0:1] * pad_dconv[base + 1:base + 1 + ch, :] + w[1:2] * pad_dconv[base:base + ch, :]
                   + w[2:3] * pad_dconv[base - 1:base - 1 + ch, :])
            dc_buf[c0:c0 + ch, :] = (dcu * u_ref[c0:c0 + ch, :]).astype(BF16)
            du_buf[c0:c0 + ch, :] = (dcu * c_ref[c0:c0 + ch, :]).astype(BF16)
        copies = []
        for p, buf in enumerate((db_buf, dc_buf, du_buf)):
            col = pl.multiple_of((p + 1) * width + j * cb, CONV_HEAD_DIM)
            copies.append(pltpu.make_async_copy(buf, dproj_ref.at[:, pl.ds(col, cb)], sems.at[p]))
            copies[-1].start()
        for cp in copies:
            cp.wait()

    def part(p):
        return pl.BlockSpec((s, cb), lambda j: (0, p * nblk + j))

    vec = pl.BlockSpec((1, cb), lambda j: (0, j))
    w_spec = pl.BlockSpec((3, cb), lambda j: (0, j))
    return _pallas(body, deps, name="conv_bwd", grid=(nblk,),
                   in_specs=[part(1), part(1), part(2), part(3), w_spec, vec, vec],
                   out_specs=[pl.BlockSpec(memory_space=pl.ANY), w_spec, vec, vec],
                   out_shape=[jax.ShapeDtypeStruct((s, 4 * width), BF16), jax.ShapeDtypeStruct((3, width), F32),
                              jax.ShapeDtypeStruct((1, width), F32), jax.ShapeDtypeStruct((1, width), F32)],
                   scratch_shapes=[pltpu.VMEM((s + 2 * PAD_ROWS, cb), F32), pltpu.VMEM((s + 2 * PAD_ROWS, cb), F32),
                                   pltpu.VMEM((s, cb), BF16), pltpu.VMEM((s, cb), BF16), pltpu.VMEM((s, cb), BF16),
                                   pltpu.SemaphoreType.DMA((3,))],
                   compiler_params=_params("arbitrary"))(dmixed, proj, proj, proj, conv_w, conv_b, gnorm_g)


def _adamw(w, g, m, v):
    m = ADAM_B1 * m + (1.0 - ADAM_B1) * g
    v = ADAM_B2 * v + (1.0 - ADAM_B2) * (g * g)
    m_hat = m / (1.0 - ADAM_B1 ** ADAM_STEP)
    v_hat = v / (1.0 - ADAM_B2 ** ADAM_STEP)
    delta = -ADAM_LR * (m_hat / (jnp.sqrt(v_hat) + ADAM_EPS) + ADAM_WD * w)
    return delta, m, v


def _ada_fwd(c_rows, w, b, tn=512):
    rows, d = c_rows.shape
    n = w.shape[1]
    tn = _tile(tn, n)

    def body(c_ref, w_ref, b_ref, o_ref):
        cv = c_ref[...]
        act = (cv * jax.nn.sigmoid(cv)).astype(BF16)
        o_ref[...] = jnp.dot(act, w_ref[...].astype(BF16), preferred_element_type=F32) + b_ref[...]

    return _pallas(body, name="ada_fwd", grid=(n // tn,),
                   in_specs=[pl.BlockSpec((rows, d), lambda j: (0, 0)), pl.BlockSpec((d, tn), lambda j: (0, j)),
                             pl.BlockSpec((1, tn), lambda j: (0, j))],
                   out_specs=pl.BlockSpec((rows, tn), lambda j: (0, j)),
                   out_shape=jax.ShapeDtypeStruct((rows, n), F32), compiler_params=_params("parallel"))(c_rows, w, b)


def _ada_bwd_adam(c_cols, dmod, w, m, v, tr=512, tn=1024):
    d, rows = c_cols.shape
    n = w.shape[1]
    tr, tn = _tile(tr, d), _tile(tn, n)

    def body(c_ref, dm_ref, w_ref, m_ref, v_ref, g_ref, dl_ref, nm_ref, nv_ref):
        cv = c_ref[...]
        act = (cv * jax.nn.sigmoid(cv)).astype(BF16)
        g = jnp.dot(act, dm_ref[...].astype(BF16), preferred_element_type=F32)
        g_ref[...] = g
        dl_ref[...], nm_ref[...], nv_ref[...] = _adamw(w_ref[...], g, m_ref[...], v_ref[...])

    blk = pl.BlockSpec((tr, tn), lambda i, j: (i, j))
    shape = jax.ShapeDtypeStruct((d, n), F32)
    return _pallas(body, name="ada_bwd_adam", grid=(d // tr, n // tn),
                   in_specs=[pl.BlockSpec((tr, rows), lambda i, j: (i, 0)), pl.BlockSpec((rows, tn), lambda i, j: (0, j)),
                             blk, blk, blk],
                   out_specs=[blk] * 4, out_shape=[shape] * 4,
                   compiler_params=_params("parallel", "parallel"))(c_cols, dmod, w, m, v)


def _reduce_adam(name, pieces, w, m, v, tr=256, tc=1024):
    r, c = w.shape
    tr, tc = _tile(tr, r), _tile(tc, c)

    def body(p_ref, w_ref, m_ref, v_ref, g_ref, dl_ref, nm_ref, nv_ref):
        g = p_ref[0].astype(F32)
        for j in range(1, N_DEV):
            g = g + p_ref[j].astype(F32)
        g_ref[...] = g
        dl_ref[...], nm_ref[...], nv_ref[...] = _adamw(w_ref[...], g, m_ref[...], v_ref[...])

    blk = pl.BlockSpec((tr, tc), lambda i, j: (i, j))
    shape = jax.ShapeDtypeStruct((r, c), F32)
    return _pallas(body, name=name, grid=(r // tr, c // tc),
                   in_specs=[pl.BlockSpec((N_DEV, tr, tc), lambda i, j: (0, i, j)), blk, blk, blk],
                   out_specs=[blk] * 4, out_shape=[shape] * 4,
                   compiler_params=_params("parallel", "parallel"))(pieces, w, m, v)


def _sum_devices(parts):
    n = parts.shape[1]

    def body(p_ref, o_ref):
        acc = p_ref[0:1, :]
        for j in range(1, N_DEV):
            acc = acc + p_ref[j:j + 1, :]
        o_ref[...] = acc

    return _pallas(body, name="sum_devices", out_shape=jax.ShapeDtypeStruct((1, n), F32),
                   compiler_params=pltpu.CompilerParams(vmem_limit_bytes=VMEM_LIMIT_BYTES))(parts)


def _adam_small(name, g, w, m, v):
    def body(g_ref, w_ref, m_ref, v_ref, dl_ref, nm_ref, nv_ref):
        dl_ref[...], nm_ref[...], nv_ref[...] = _adamw(w_ref[...], g_ref[...], m_ref[...], v_ref[...])

    shape = jax.ShapeDtypeStruct(w.shape, F32)
    return _pallas(body, name=name, out_shape=[shape] * 3,
                   compiler_params=pltpu.CompilerParams(vmem_limit_bytes=VMEM_LIMIT_BYTES))(g, w, m, v)


def kernel(x, c, w_ada, b_ada, norm1_g, w_in, pool_mix_w, pool_scale, conv_w, conv_b, gnorm_pool_g, gnorm_conv_g, w_out, norm2_g, w_mlp_in, w_mlp_out, final_g, loss_target, m_w_ada, m_b_ada, m_norm1_g, m_w_in, m_pool_mix_w, m_pool_scale, m_conv_w, m_conv_b, m_gnorm_pool_g, m_gnorm_conv_g, m_w_out, m_norm2_g, m_w_mlp_in, m_w_mlp_out, m_final_g, v_w_ada, v_b_ada, v_norm1_g, v_w_in, v_pool_mix_w, v_pool_scale, v_conv_w, v_conv_b, v_gnorm_pool_g, v_gnorm_conv_g, v_w_out, v_norm2_g, v_w_mlp_in, v_w_mlp_out, v_final_g):
    s, d = x.shape[1], x.shape[2]
    width = d // 2
    gd = width // N_POOL_GROUPS
    d_ff = w_mlp_in.shape[2] * N_DEV
    n_proj = w_in.shape[2] * N_DEV
    ada_cols = w_ada.shape[2]
    conv_cols = conv_w.shape[2]
    assert n_proj == 4 * width and ada_cols * N_DEV == N_MOD * d and d_ff % N_DEV == 0
    assert width % CONV_HEAD_DIM == 0 and s % 8 == 0
    seq_chunk = _tile(512, s)
    pool_cb = _tile(256, gd)
    conv_cb = CONV_HEAD_DIM

    me = 4 * lax.axis_index("x") + 2 * lax.axis_index("y") + lax.axis_index("c")
    x2d, target = x[0], loss_target[0]

    wmix_all, conv_w_all, c_all = _exchange(
        "gather_small_weights", [pool_mix_w[0].astype(BF16), conv_w[0], c], ["gather"] * 3)
    wmix_full = jnp.transpose(wmix_all, (1, 0, 2, 3)).reshape(N_POOL_GROUPS, gd, gd)
    conv_w_full = jnp.transpose(conv_w_all, (1, 0, 2)).reshape(3, width)
    c_rows = jnp.concatenate([c_all.reshape(N_DEV, d), jnp.zeros((N_DEV, d), F32)], axis=0)

    b_mine = lax.dynamic_slice(b_ada, (0, me * ada_cols), (1, ada_cols))
    mod_part = _ada_fwd(c_rows, w_ada[0], b_mine)
    (mod_all,) = _exchange("scatter_mod", [mod_part[:N_DEV].reshape(N_DEV, 1, ada_cols)], ["a2a"])
    mod = mod_all.reshape(1, N_MOD * d)

    started = {}
    for wname, wgt in (("w_in", w_in), ("w_out", w_out), ("w_mlp_in", w_mlp_in), ("w_mlp_out", w_mlp_out)):
        own = wgt[0].astype(BF16)
        started[wname] = _push_start("gather_" + wname + "_start", own, _landing(own, me, False), False, deps=(mod,))
    gather_tokens = tuple(st[4] for st in started.values())
    shift1, scale1, gate1, shift2, scale2, gate2 = [mod[:, i * d:(i + 1) * d] for i in range(N_MOD)]

    h1 = _norm_mod("norm1_fwd", x2d, norm1_g, scale1, shift1, deps=gather_tokens)
    w_in_all = _push_wait("gather_w_in_wait", started["w_in"], h1, False)
    (proj,) = _mm_nn("in_proj", h1, w_in_all, n_proj, True, [jax.ShapeDtypeStruct((s, n_proj), F32)], _store(F32))
    pooled = _pool_fwd(proj, s, gd, pool_cb, seq_chunk)
    a_pre, mixed = _poolmix_fwd(pooled, wmix_full, pool_scale, gnorm_pool_g, d)
    mixed = _conv_fwd(proj, mixed, conv_w_full, conv_b, gnorm_conv_g, s, width, conv_cb, seq_chunk)

    def residual_specs(tm, tn):
        return [pl.BlockSpec((tm, tn), lambda i, j, k: (i, j)), pl.BlockSpec((1, tn), lambda i, j, k: (0, j))]

    sd_f32 = jax.ShapeDtypeStruct((s, d), F32)
    w_out_full = _push_wait("gather_w_out_wait", started["w_out"], mixed, False).reshape(d, d)
    attn, x_mid = _mm_nn("out_proj", mixed, w_out_full, d, False, [sd_f32, sd_f32], _residual_epilogue,
                         extras=(x2d, gate1), extra_specs=residual_specs)
    h2 = _norm_mod("norm2_fwd", x_mid, norm2_g, scale2, shift2)
    sf_bf16 = jax.ShapeDtypeStruct((s, d_ff), BF16)
    w1_all = _push_wait("gather_w_mlp_in_wait", started["w_mlp_in"], h2, False)
    relu, hid = _mm_nn("mlp_in", h2, w1_all, d_ff, True, [sf_bf16, sf_bf16], _relu2_epilogue)
    w2_full = _push_wait("gather_w_mlp_out_wait", started["w_mlp_out"], hid, False).reshape(d_ff, d)
    mlp, x_last = _mm_nn("mlp_out", hid, w2_full, d, False, [sd_f32, sd_f32], _residual_epilogue,
                         extras=(x_mid, gate2), extra_specs=residual_specs)

    dx_last, dmlp, d_final_g, dgate2, loss_row = _loss_head(x_last, target, final_g.reshape(1, d), gate2, mlp)

    def relu_specs(tm, tn):
        return [pl.BlockSpec((tm, tn), lambda i, j, k: (i, j))]

    def scatter_start(wname, grad):
        return _push_start("scatter_" + wname + "_start", grad, _landing(grad, me, True), True)

    (dhpre,) = _mm_nt("mlp_out_dx", dmlp, w2_full, d_ff, False, [sf_bf16], _relu2_bwd_epilogue,
                      extras=(relu,), extra_specs=relu_specs)
    g_w2 = _mm_tn("mlp_out_dw", hid, dmlp, False)
    sent_w2 = scatter_start("w_mlp_out", g_w2.reshape(N_DEV, d_ff // N_DEV, d))
    (dh2,) = _mm_nt("mlp_in_dx", dhpre, w1_all, d, True, [sd_f32], _store(F32), tn=1024, deps=(sent_w2[4],))
    g_w1 = _mm_tn("mlp_in_dw", h2, dhpre, True)
    sent_w1 = scatter_start("w_mlp_in", g_w1)
    dx_mid, dshift2, dscale2, d_norm2_g, dattn, dgate1 = _norm_mod_bwd(
        "norm2_bwd", dh2, x_mid, norm2_g, scale2, dx_last, branch=attn, gate=gate1, deps=(sent_w1[4],))

    (dmixed,) = _mm_nt("out_proj_dx", dattn, w_out_full, d, False, [sd_f32], _store(F32))
    g_w_out = _mm_tn("out_proj_dw", mixed, dattn, False)
    sent_w_out = scatter_start("w_out", g_w_out.reshape(N_DEV, d // N_DEV, d))
    dproj, d_conv_w, d_conv_b, d_gnorm_conv = _conv_bwd(dmixed, proj, conv_w_full, conv_b, gnorm_conv_g,
                                                        s, width, conv_cb, seq_chunk, deps=(sent_w_out[4],))
    da_pre, dpooled, d_pool_scale, d_gnorm_pool = _poolmix_bwd(dmixed, a_pre, wmix_full, pool_scale, gnorm_pool_g)
    g_wmix = _poolmix_wgrad(pooled, da_pre, gd)
    dproj = _pool_bwd(dpooled, dproj, s, gd, pool_cb, seq_chunk)

    g_w_in = _mm_tn("in_proj_dw", h1, dproj, True)
    sent_w_in = scatter_start("w_in", g_w_in)
    (dh1,) = _mm_nt("in_proj_dx", dproj, w_in_all, d, True, [sd_f32], _store(F32), tn=1024, deps=(sent_w_in[4],))
    grad_x, dshift1, dscale1, d_norm1_g = _norm_mod_bwd("norm1_bwd", dh1, x2d, norm1_g, scale1, dx_mid)

    p_w2 = _push_wait("scatter_w_mlp_out_wait", sent_w2, grad_x, True)
    out_w2 = _reduce_adam("adam_w_mlp_out", p_w2, w_mlp_out[0], m_w_mlp_out[0], v_w_mlp_out[0])
    p_w1 = _push_wait("scatter_w_mlp_in_wait", sent_w1, out_w2[0], True)
    out_w1 = _reduce_adam("adam_w_mlp_in", p_w1, w_mlp_in[0], m_w_mlp_in[0], v_w_mlp_in[0])
    p_w_out = _push_wait("scatter_w_out_wait", sent_w_out, out_w1[0], True)
    out_w_out = _reduce_adam("adam_w_out", p_w_out, w_out[0], m_w_out[0], v_w_out[0])
    p_w_in = _push_wait("scatter_w_in_wait", sent_w_in, out_w_out[0], True)
    out_w_in = _reduce_adam("adam_w_in", p_w_in, w_in[0], m_w_in[0], v_w_in[0])

    rows_mix = gd // N_DEV
    g_wmix_split = jnp.transpose(g_wmix.reshape(N_POOL_GROUPS, N_DEV, rows_mix, gd), (1, 0, 2, 3))
    g_wmix_split = g_wmix_split.reshape(N_DEV, N_POOL_GROUPS * rows_mix, gd)
    loss_pad = jnp.concatenate([loss_row[:, :1], jnp.zeros((1, 127), F32)], axis=1)
    dmod = jnp.concatenate([dshift1, dscale1, dgate1, dshift2, dscale2, dgate2], axis=1)
    small = jnp.concatenate([dmod, d_norm1_g, d_pool_scale, d_conv_b, d_gnorm_pool, d_gnorm_conv, d_norm2_g,
                             d_final_g, d_conv_w.reshape(1, 3 * width), loss_pad], axis=1)
    p_wmix, small_all = _exchange("exchange_small_grads", [g_wmix_split, small], ["a2a", "gather"],
                                  deps=(out_w_in[0],))
    mix_shape = (N_POOL_GROUPS * rows_mix, gd)
    out_wmix = _reduce_adam("adam_pool_mix", p_wmix, pool_mix_w.reshape(mix_shape), m_pool_mix_w.reshape(mix_shape),
                            v_pool_mix_w.reshape(mix_shape))
    out_wmix = [a.reshape(pool_mix_w.shape) for a in out_wmix]
    small_all = small_all.reshape(N_DEV, small.shape[1])
    small_sum = _sum_devices(small_all)

    n_rep = (N_MOD + 1) * d + 4 * width + 2 * d
    loss = small_sum[0, n_rep + 3 * width]
    rep_names_w = [b_ada, norm1_g, pool_scale, conv_b, gnorm_pool_g, gnorm_conv_g, norm2_g, final_g.reshape(1, d)]
    rep_names_m = [m_b_ada, m_norm1_g, m_pool_scale, m_conv_b, m_gnorm_pool_g, m_gnorm_conv_g, m_norm2_g,
                   m_final_g.reshape(1, d)]
    rep_names_v = [v_b_ada, v_norm1_g, v_pool_scale, v_conv_b, v_gnorm_pool_g, v_gnorm_conv_g, v_norm2_g,
                   v_final_g.reshape(1, d)]
    rep_grad = small_sum[:, :n_rep]
    rep_delta, rep_m, rep_v = _adam_small("adam_replicated", rep_grad, jnp.concatenate(rep_names_w, axis=1),
                                          jnp.concatenate(rep_names_m, axis=1), jnp.concatenate(rep_names_v, axis=1))

    def split_rep(vec):
        out, off = [], 0
        for wgt in rep_names_w:
            n = wgt.shape[1]
            out.append(vec[:, off:off + n])
            off += n
        out[-1] = out[-1].reshape(d)
        return out

    conv_grad_full = small_sum[:, n_rep:n_rep + 3 * width].reshape(3, width)
    g_conv_w = lax.dynamic_slice(conv_grad_full, (0, me * conv_cols), (3, conv_cols))
    g_conv_w8 = jnp.concatenate([g_conv_w, jnp.zeros((5, conv_cols), F32)], axis=0)

    def pad8(a):
        return jnp.concatenate([a[0], jnp.zeros((5, conv_cols), F32)], axis=0)

    conv_delta, conv_m, conv_v = _adam_small("adam_conv_w", g_conv_w8, pad8(conv_w), pad8(m_conv_w), pad8(v_conv_w))

    dmod_all = small_all[:, :N_MOD * d]
    dmod_mine = lax.dynamic_slice(dmod_all, (0, me * ada_cols), (N_DEV, ada_cols))
    dmod_rows = jnp.concatenate([dmod_mine, jnp.zeros((N_DEV, ada_cols), F32)], axis=0)
    out_ada = _ada_bwd_adam(jnp.transpose(c_rows), dmod_rows, w_ada[0], m_w_ada[0], v_w_ada[0])

    rep_all = [split_rep(rep_grad), split_rep(rep_delta), split_rep(rep_m), split_rep(rep_v)]
    conv_all = [g_conv_w[None], conv_delta[None, :3], conv_m[None, :3], conv_v[None, :3]]
    outs = [loss, grad_x[None]]
    for kind in range(4):
        b_ada_o, norm1_o, pool_scale_o, conv_b_o, gpool_o, gconv_o, norm2_o, final_o = rep_all[kind]
        outs += [out_ada[kind][None], b_ada_o, norm1_o, out_w_in[kind][None], out_wmix[kind], pool_scale_o,
                 conv_all[kind], conv_b_o, gpool_o, gconv_o, out_w_out[kind][None], norm2_o, out_w1[kind][None],
                 out_w2[kind][None], final_o]
    return tuple(outs)
```

```python
import jax
import jax.numpy as jnp
from jax import lax
from jax.experimental import pallas as pl
from jax.experimental.pallas import tpu as pltpu

F32 = jnp.float32
BF16 = jnp.bfloat16
MESH = pl.DeviceIdType.MESH

N_DEV = 8
N_MOD = 6
EPS = 1e-6
POOL_WINDOWS = (2, 4, 8, 16)
N_POOL_GROUPS = len(POOL_WINDOWS)
CONV_HEAD_DIM = 128
PAD_ROWS = 16

ADAM_LR = 0.001
ADAM_B1 = 0.9
ADAM_B2 = 0.999
ADAM_EPS = 1e-08
ADAM_WD = 0.01
ADAM_STEP = 10

VMEM_LIMIT_BYTES = 56 * 1024 * 1024
MM_TM, MM_TN, MM_TK = 1024, 512, 4096


def _pallas(body, deps=(), **kw):
    if not deps:
        return pl.pallas_call(body, **kw)
    n_in = len(kw["in_specs"])

    def with_deps(*refs):
        body(*refs[:n_in], *refs[n_in + len(deps):])

    kw["in_specs"] = list(kw["in_specs"]) + [pl.BlockSpec(memory_space=pl.ANY)] * len(deps)
    call = pl.pallas_call(with_deps, **kw)
    return lambda *operands: call(*operands, *deps)


def _params(*sem):
    return pltpu.CompilerParams(dimension_semantics=sem, vmem_limit_bytes=VMEM_LIMIT_BYTES)


def _tile(pref, dim):
    if dim <= pref:
        return dim
    for t in range(pref - pref % 128, 0, -128):
        if dim % t == 0:
            return t
    return dim


def _exchange(name, arrays, modes, deps=()):
    n = len(arrays)
    out_shape = []
    for a, mode in zip(arrays, modes):
        piece = a.shape if mode == "gather" else a.shape[1:]
        out_shape.append(jax.ShapeDtypeStruct((N_DEV,) + tuple(piece), a.dtype))

    def body(*refs):
        srcs, dsts = refs[:n], refs[n:2 * n]
        send_sems, recv_sems, local_sems = refs[2 * n:]
        x, y, c = lax.axis_index("x"), lax.axis_index("y"), lax.axis_index("c")
        me = 4 * x + 2 * y + c
        copies = []
        for i in range(n):
            gather = modes[i] == "gather"
            local = pltpu.make_async_copy(srcs[i] if gather else srcs[i].at[me], dsts[i].at[me], local_sems.at[i])
            local.start()
            copies.append(local)
            for k in range(1, N_DEV):
                kx, ky, kc = (k >> 2) & 1, (k >> 1) & 1, k & 1
                peer = (1 - x if kx else x, 1 - y if ky else y, 1 - c if kc else c)
                peer_idx = 4 * peer[0] + 2 * peer[1] + peer[2]
                remote = pltpu.make_async_remote_copy(
                    src_ref=srcs[i] if gather else srcs[i].at[peer_idx],
                    dst_ref=dsts[i].at[me],
                    send_sem=send_sems.at[i * (N_DEV - 1) + k - 1],
                    recv_sem=recv_sems.at[i * (N_DEV - 1) + k - 1],
                    device_id=peer, device_id_type=MESH)
                remote.start()
                copies.append(remote)
        for cp in copies:
            cp.wait()

    any_spec = pl.BlockSpec(memory_space=pl.ANY)
    return _pallas(
        body, deps, name=name, out_shape=out_shape,
        in_specs=[any_spec] * n, out_specs=[any_spec] * n,
        scratch_shapes=[pltpu.SemaphoreType.DMA((n * (N_DEV - 1),)),
                        pltpu.SemaphoreType.DMA((n * (N_DEV - 1),)),
                        pltpu.SemaphoreType.DMA((n,))],
    )(*arrays)


_HBM = pl.BlockSpec(memory_space=pltpu.HBM)
_SEM = pl.BlockSpec(memory_space=pltpu.SEMAPHORE)
_TOKEN = pl.BlockSpec(memory_space=pltpu.VMEM)
_EFFECT = pltpu.SideEffectType.DATAFLOW_SIDE_EFFECTING
N_CHIP = N_DEV // 2
_OTHER_CHIPS = (1, 2, 3)


def _place():
    x, y, c = lax.axis_index("x"), lax.axis_index("y"), lax.axis_index("c")
    return x, y, c, (x, y, 1 - c)


def _same_core_of(x, y, c, k):
    px = 1 - x if k & 2 else x
    py = 1 - y if k & 1 else y
    return (px, py, c), 2 * px + py


def _remote(src, dst, send_sem, recv_sem, device):
    return pltpu.make_async_remote_copy(src_ref=src, dst_ref=dst, send_sem=send_sem, recv_sem=recv_sem,
                                        device_id=device, device_id_type=MESH)


def _token_shape():
    return jax.ShapeDtypeStruct((8, 128), F32)


def _split_call(body, deps, name, operands, in_specs, out_shape, out_specs, aliases):
    return _pallas(body, deps, name=name, out_shape=out_shape, in_specs=in_specs, out_specs=out_specs,
                   input_output_aliases=aliases,
                   compiler_params=pltpu.CompilerParams(has_side_effects=_EFFECT))(*operands)


def _gather_start(name, land, deps):
    def body(land_ref, send_sems, recv_ici, recv_d2d, land_thru, token):
        del land_thru
        x, y, c, sibling = _place()
        mine = land_ref.at[4 * x + 2 * y + c]
        for k in _OTHER_CHIPS:
            _remote(mine, mine, send_sems.at[k - 1], recv_ici.at[k - 1], _same_core_of(x, y, c, k)[0]).start()
        _remote(mine, mine, send_sems.at[3], recv_d2d.at[0], sibling).start()
        token[...] = jnp.zeros_like(token)

    return _split_call(
        body, deps, name, (pltpu.with_memory_space_constraint(land, pltpu.HBM),), (_HBM,),
        (pltpu.SemaphoreType.DMA((4,)), pltpu.SemaphoreType.DMA((3,)), pltpu.SemaphoreType.DMA((1,)),
         pltpu.HBM(land.shape, land.dtype), _token_shape()),
        (_SEM, _SEM, _SEM, _HBM, _TOKEN), {0: 3})


def _gather_relay(name, started, after):
    _, recv_ici, _, land, _ = started

    def body(land_ref, recv_ici, after_ref, send_fwd, recv_fwd, land_thru):
        del after_ref, land_thru
        x, y, c, sibling = _place()
        mine = land_ref.at[4 * x + 2 * y + c]
        for k in _OTHER_CHIPS:
            _remote(mine, mine, send_fwd.at[k - 1], recv_ici.at[k - 1], sibling).wait_recv()
        for k in _OTHER_CHIPS:
            piece = land_ref.at[2 * _same_core_of(x, y, c, k)[1] + c]
            _remote(piece, piece, send_fwd.at[k - 1], recv_fwd.at[k - 1], sibling).start()

    return _split_call(
        body, (), name, (land, recv_ici, after), (_HBM, _SEM, pl.BlockSpec(memory_space=pl.ANY)),
        (pltpu.SemaphoreType.DMA((3,)), pltpu.SemaphoreType.DMA((3,)), pltpu.HBM(land.shape, land.dtype)),
        (_SEM, _SEM, _HBM), {0: 2})


def _gather_wait(name, started, relayed):
    send_sems, _, recv_d2d, _, _ = started
    send_fwd, recv_fwd, land = relayed

    def body(land_ref, send_sems, recv_d2d, send_fwd, recv_fwd, land_out):
        del land_out
        x, y, c, sibling = _place()
        mine = land_ref.at[4 * x + 2 * y + c]
        for i in range(4):
            _remote(mine, mine, send_sems.at[i], recv_d2d.at[0], sibling).wait_send()
        _remote(mine, mine, send_sems.at[3], recv_d2d.at[0], sibling).wait_recv()
        for k in _OTHER_CHIPS:
            relay = _remote(mine, mine, send_fwd.at[k - 1], recv_fwd.at[k - 1], sibling)
            relay.wait_send()
            relay.wait_recv()

    return _split_call(
        body, (), name, (land, send_sems, recv_d2d, send_fwd, recv_fwd), (_HBM, _SEM, _SEM, _SEM, _SEM),
        (pltpu.HBM(land.shape, land.dtype),), (_HBM,), {0: 0})[0]


def _landing(own, me):
    land = lax.empty((N_DEV,) + own.shape, own.dtype)
    return lax.dynamic_update_slice(land, own[None], (me,) + (0,) * own.ndim)


def _pair_start(name, grad, deps=()):
    pair = lax.empty((N_CHIP,) + grad.shape[1:], grad.dtype)

    def body(g_ref, pair_ref, send_sems, recv_sems, g_thru, pair_thru, token):
        del g_thru, pair_thru
        x, y, c, sibling = _place()
        for q in range(N_CHIP):
            _remote(g_ref.at[2 * q + 1 - c], pair_ref.at[q], send_sems.at[q], recv_sems.at[q], sibling).start()
        token[...] = jnp.zeros_like(token)

    return _split_call(
        body, deps, name,
        (pltpu.with_memory_space_constraint(grad, pltpu.HBM), pltpu.with_memory_space_constraint(pair, pltpu.HBM)),
        (_HBM, _HBM),
        (pltpu.SemaphoreType.DMA((N_CHIP,)), pltpu.SemaphoreType.DMA((N_CHIP,)),
         pltpu.HBM(grad.shape, grad.dtype), pltpu.HBM(pair.shape, pair.dtype), _token_shape()),
        (_SEM, _SEM, _HBM, _HBM, _TOKEN), {0: 2, 1: 3})


def _pair_wait(name, started, after):
    send_sems, recv_sems, grad, pair, _ = started

    def body(g_ref, pair_ref, send_sems, recv_sems, after_ref, g_out, pair_out):
        del after_ref, g_out, pair_out
        x, y, c, sibling = _place()
        for q in range(N_CHIP):
            cp = _remote(g_ref.at[2 * q + 1 - c], pair_ref.at[q], send_sems.at[q], recv_sems.at[q], sibling)
            cp.wait_send()
            cp.wait_recv()

    return _split_call(
        body, (), name, (grad, pair, send_sems, recv_sems, after),
        (_HBM, _HBM, _SEM, _SEM, pl.BlockSpec(memory_space=pl.ANY)),
        (pltpu.HBM(grad.shape, grad.dtype), pltpu.HBM(pair.shape, pair.dtype)), (_HBM, _HBM), {0: 0, 1: 1})


def _pair_sum(name, grad, pair, tr=512, tc=1024):
    _, r, c = pair.shape
    tr, tc = _tile(tr, r), _tile(tc, c)

    def body(g_ref, p_ref, o_ref):
        o_ref[...] = (g_ref[...].astype(F32) + p_ref[...].astype(F32)).astype(o_ref.dtype)

    return _pallas(
        body, name=name, grid=(N_CHIP, r // tr, c // tc),
        in_specs=[pl.BlockSpec((None, None, tr, tc), lambda q, i, j: (q, lax.axis_index("c"), i, j)),
                  pl.BlockSpec((None, tr, tc), lambda q, i, j: (q, i, j))],
        out_specs=pl.BlockSpec((None, tr, tc), lambda q, i, j: (q, i, j)),
        out_shape=jax.ShapeDtypeStruct(pair.shape, pair.dtype),
        compiler_params=_params("parallel", "parallel", "parallel"))(grad.reshape((N_CHIP, 2) + grad.shape[1:]), pair)


def _chip_start(name, sums, deps=()):
    land = lax.empty((3,) + sums.shape[1:], sums.dtype)

    def body(s_ref, land_ref, send_sems, recv_sems, s_thru, land_thru, token):
        del s_thru, land_thru
        x, y, c, _ = _place()
        for k in _OTHER_CHIPS:
            peer, chip = _same_core_of(x, y, c, k)
            _remote(s_ref.at[chip], land_ref.at[k - 1], send_sems.at[k - 1], recv_sems.at[k - 1], peer).start()
        token[...] = jnp.zeros_like(token)

    return _split_call(
        body, deps, name,
        (pltpu.with_memory_space_constraint(sums, pltpu.HBM), pltpu.with_memory_space_constraint(land, pltpu.HBM)),
        (_HBM, _HBM),
        (pltpu.SemaphoreType.DMA((3,)), pltpu.SemaphoreType.DMA((3,)),
         pltpu.HBM(sums.shape, sums.dtype), pltpu.HBM(land.shape, land.dtype), _token_shape()),
        (_SEM, _SEM, _HBM, _HBM, _TOKEN), {0: 2, 1: 3})


def _chip_wait(name, started, after):
    send_sems, recv_sems, sums, land, _ = started

    def body(s_ref, land_ref, send_sems, recv_sems, after_ref, s_out, land_out):
        del after_ref, s_out, land_out
        x, y, c, _ = _place()
        for k in _OTHER_CHIPS:
            peer, chip = _same_core_of(x, y, c, k)
            cp = _remote(s_ref.at[chip], land_ref.at[k - 1], send_sems.at[k - 1], recv_sems.at[k - 1], peer)
            cp.wait_send()
            cp.wait_recv()

    return _split_call(
        body, (), name, (sums, land, send_sems, recv_sems, after),
        (_HBM, _HBM, _SEM, _SEM, pl.BlockSpec(memory_space=pl.ANY)),
        (pltpu.HBM(sums.shape, sums.dtype), pltpu.HBM(land.shape, land.dtype)), (_HBM, _HBM), {0: 0, 1: 1})


_DOT_DIMS = {"nn": (((1,), (0,)), ((), ())), "nt": (((1,), (1,)), ((), ())), "tn": (((0,), (0,)), ((), ()))}


def _matmul(name, mode, operands, in_specs, out_shape, out_specs, grid, acc_shape, epilogue, deps=()):
    n_in, n_out, nk = len(operands), len(out_shape), grid[2]
    dims = _DOT_DIMS[mode]

    def body(*refs):
        a_ref, b_ref = refs[0], refs[1]
        extras, outs = refs[2:n_in], refs[n_in:n_in + n_out]
        part = lax.dot_general(a_ref[...], b_ref[...], dims, preferred_element_type=F32)
        if nk == 1:
            epilogue(part, extras, outs)
            return
        acc = refs[-1]
        k = pl.program_id(2)

        @pl.when(k == 0)
        def _():
            acc[...] = part

        @pl.when(jnp.logical_and(k > 0, k < nk - 1))
        def _():
            acc[...] += part

        @pl.when(k == nk - 1)
        def _():
            epilogue(acc[...] + part, extras, outs)

    return _pallas(body, deps, name=name, grid=grid, in_specs=in_specs, out_specs=out_specs, out_shape=out_shape,
                   scratch_shapes=[pltpu.VMEM(acc_shape, F32)] if nk > 1 else [],
                   compiler_params=_params("parallel", "parallel", "arbitrary"))(*operands)


def _store(dtype):
    def epilogue(acc, extras, outs):
        outs[0][...] = acc.astype(dtype)
    return epilogue


def _residual_epilogue(acc, extras, outs):
    x_ref, gate_ref = extras
    outs[0][...] = acc
    outs[1][...] = x_ref[...] + gate_ref[...] * acc


def _relu2_epilogue(acc, extras, outs):
    r = jnp.maximum(acc, 0.0)
    outs[0][...] = r.astype(outs[0].dtype)
    outs[1][...] = (r * r).astype(outs[1].dtype)


def _relu2_bwd_epilogue(acc, extras, outs):
    outs[0][...] = (acc * (2.0 * extras[0][...].astype(F32))).astype(outs[0].dtype)


def _no_extra_specs(tm, tn):
    return []


def _mm_nn(name, a, b, n_total, b_split, out_shape, epilogue, extras=(), extra_specs=_no_extra_specs, tm=MM_TM, tn=MM_TN, tk=MM_TK,
           deps=()):
    m, kdim = a.shape
    tm, tk = _tile(tm, m), _tile(tk, kdim)
    if b_split:
        piece = b.shape[2]
        tn = _tile(tn, piece)
        per = piece // tn
        b_spec = pl.BlockSpec((None, tk, tn), lambda i, j, k: (j // per, k, j % per))
    else:
        tn = _tile(tn, n_total)
        b_spec = pl.BlockSpec((tk, tn), lambda i, j, k: (k, j))
    in_specs = [pl.BlockSpec((tm, tk), lambda i, j, k: (i, k)), b_spec] + list(extra_specs(tm, tn))
    out_specs = [pl.BlockSpec((tm, tn), lambda i, j, k: (i, j)) for _ in out_shape]
    return _matmul(name, "nn", (a, b) + tuple(extras), in_specs, out_shape, out_specs,
                   (m // tm, n_total // tn, kdim // tk), (tm, tn), epilogue, deps)


def _mm_nt(name, a, b, n_total, b_split, out_shape, epilogue, extras=(), extra_specs=_no_extra_specs, tm=MM_TM, tn=MM_TN, tk=MM_TK,
           deps=()):
    m, kdim = a.shape
    tm, tn = _tile(tm, m), _tile(tn, n_total)
    if b_split:
        piece = b.shape[2]
        tk = _tile(tk, piece)
        per = piece // tk
        b_spec = pl.BlockSpec((None, tn, tk), lambda i, j, k: (k // per, j, k % per))
    else:
        tk = _tile(tk, kdim)
        b_spec = pl.BlockSpec((tn, tk), lambda i, j, k: (j, k))
    in_specs = [pl.BlockSpec((tm, tk), lambda i, j, k: (i, k)), b_spec] + list(extra_specs(tm, tn))
    out_specs = [pl.BlockSpec((tm, tn), lambda i, j, k: (i, j)) for _ in out_shape]
    return _matmul(name, "nt", (a, b) + tuple(extras), in_specs, out_shape, out_specs,
                   (m // tm, n_total // tn, kdim // tk), (tm, tn), epilogue, deps)


def _mm_tn(name, a, b, out_split, tm=MM_TM, tn=MM_TN, tk=MM_TK):
    kdim, m = a.shape
    n_total = b.shape[1]
    tm, tk = _tile(tm, m), _tile(tk, kdim)
    if out_split:
        piece = n_total // N_DEV
        tn = _tile(tn, piece)
        per = piece // tn
        out_shape = [jax.ShapeDtypeStruct((N_DEV, m, piece), BF16)]
        out_specs = [pl.BlockSpec((None, tm, tn), lambda i, j, k: (j // per, i, j % per))]
    else:
        tn = _tile(tn, n_total)
        out_shape = [jax.ShapeDtypeStruct((m, n_total), BF16)]
        out_specs = [pl.BlockSpec((tm, tn), lambda i, j, k: (i, j))]
    in_specs = [pl.BlockSpec((tk, tm), lambda i, j, k: (k, i)), pl.BlockSpec((tk, tn), lambda i, j, k: (k, j))]
    return _matmul(name, "tn", (a, b), in_specs, out_shape, out_specs,
                   (m // tm, n_total // tn, kdim // tk), (tm, tn), _store(BF16))[0]


def _rms(xv):
    return lax.rsqrt(jnp.mean(xv * xv, axis=-1, keepdims=True) + EPS)


def _colsum(v):
    return jnp.sum(v, axis=0, keepdims=True)


def _norm_mod(name, x, g, scale, shift, tr=256, deps=()):
    s, d = x.shape
    tr = _tile(tr, s)

    def body(x_ref, g_ref, sc_ref, sh_ref, h_ref):
        xv = x_ref[...]
        h = (xv * _rms(xv)) * g_ref[...]
        h_ref[...] = (h * (1.0 + sc_ref[...]) + sh_ref[...]).astype(h_ref.dtype)

    row = pl.BlockSpec((tr, d), lambda i: (i, 0))
    vec = pl.BlockSpec((1, d), lambda i: (0, 0))
    return _pallas(body, deps, name=name, grid=(s // tr,), in_specs=[row, vec, vec, vec], out_specs=row,
                   out_shape=jax.ShapeDtypeStruct((s, d), BF16), compiler_params=_params("parallel"))(x, g, scale, shift)


def _loss_head(x3, target, gf, gate2, mlp, tr=128):
    s, d = x3.shape
    tr = _tile(tr, s)

    def body(x_ref, t_ref, gf_ref, gate_ref, mlp_ref, dx_ref, dbr_ref, dgf_ref, dgate_ref, loss_ref):
        @pl.when(pl.program_id(0) == 0)
        def _():
            dgf_ref[...] = jnp.zeros_like(dgf_ref)
            dgate_ref[...] = jnp.zeros_like(dgate_ref)
            loss_ref[...] = jnp.zeros_like(loss_ref)

        xv = x_ref[...]
        r = _rms(xv)
        xn = xv * r
        gfv = gf_ref[...]
        err = xn * gfv - t_ref[...]
        loss_ref[...] += 0.5 * _colsum(jnp.mean(err * err, axis=-1, keepdims=True))
        dy = err * (1.0 / d)
        dgf_ref[...] += _colsum(dy * xn)
        dxn = dy * gfv
        dx = r * (dxn - xn * jnp.mean(dxn * xn, axis=-1, keepdims=True))
        dx_ref[...] = dx
        dbr_ref[...] = (dx * gate_ref[...]).astype(dbr_ref.dtype)
        dgate_ref[...] += _colsum(dx * mlp_ref[...])

    row = pl.BlockSpec((tr, d), lambda i: (i, 0))
    vec = pl.BlockSpec((1, d), lambda i: (0, 0))
    return _pallas(
        body, name="loss_head", grid=(s // tr,), in_specs=[row, row, vec, vec, row],
        out_specs=[row, row, vec, vec, pl.BlockSpec((1, 128), lambda i: (0, 0))],
        out_shape=[jax.ShapeDtypeStruct((s, d), F32), jax.ShapeDtypeStruct((s, d), BF16),
                   jax.ShapeDtypeStruct((1, d), F32), jax.ShapeDtypeStruct((1, d), F32),
                   jax.ShapeDtypeStruct((1, 128), F32)],
        compiler_params=_params("arbitrary"))(x3, target, gf, gate2, mlp)


def _norm_mod_bwd(name, dh, xin, g, scale, dx_up, branch=None, gate=None, tr=128, deps=()):
    s, d = xin.shape
    tr = _tile(tr, s)
    with_gate = branch is not None

    def body(*refs):
        dh_ref, x_ref, g_ref, sc_ref, up_ref = refs[:5]
        if with_gate:
            br_ref, gate_ref = refs[5:7]
            dx_ref, dsh_ref, dsc_ref, dg_ref, dbr_ref, dgate_ref = refs[7:]
            sums = (dsh_ref, dsc_ref, dg_ref, dgate_ref)
        else:
            dx_ref, dsh_ref, dsc_ref, dg_ref = refs[5:]
            sums = (dsh_ref, dsc_ref, dg_ref)

        @pl.when(pl.program_id(0) == 0)
        def _():
            for ref in sums:
                ref[...] = jnp.zeros_like(ref)

        xv, dhv, gv = x_ref[...], dh_ref[...], g_ref[...]
        r = _rms(xv)
        xn = xv * r
        one_sc = 1.0 + sc_ref[...]
        dsh_ref[...] += _colsum(dhv)
        dsc_ref[...] += _colsum(dhv * (xn * gv))
        dg_ref[...] += _colsum(dhv * one_sc * xn)
        dxn = dhv * one_sc * gv
        dx = up_ref[...] + r * (dxn - xn * jnp.mean(dxn * xn, axis=-1, keepdims=True))
        dx_ref[...] = dx
        if with_gate:
            dbr_ref[...] = (dx * gate_ref[...]).astype(dbr_ref.dtype)
            dgate_ref[...] += _colsum(dx * br_ref[...])

    row = pl.BlockSpec((tr, d), lambda i: (i, 0))
    vec = pl.BlockSpec((1, d), lambda i: (0, 0))
    vshape = jax.ShapeDtypeStruct((1, d), F32)
    operands = [dh, xin, g, scale, dx_up]
    in_specs = [row, row, vec, vec, row]
    out_shape = [jax.ShapeDtypeStruct((s, d), F32), vshape, vshape, vshape]
    out_specs = [row, vec, vec, vec]
    if with_gate:
        operands += [branch, gate]
        in_specs += [row, vec]
        out_shape += [jax.ShapeDtypeStruct((s, d), BF16), vshape]
        out_specs += [row, vec]
    return _pallas(body, deps, name=name, grid=(s // tr,), in_specs=in_specs, out_specs=out_specs, out_shape=out_shape,
                   compiler_params=_params("arbitrary"))(*operands)


def _window_count(c0, rows, half, s):
    t = c0 + lax.broadcasted_iota(jnp.int32, (rows, 1), 0)
    return (jnp.minimum(t + half, s) - jnp.maximum(t - half, 0)).astype(F32)


def _zero_pads(pad, s):
    zeros = jnp.zeros((PAD_ROWS, pad.shape[1]), pad.dtype)
    pad[0:PAD_ROWS, :] = zeros
    pad[PAD_ROWS + s:PAD_ROWS + s + PAD_ROWS, :] = zeros


def _pool_fwd(proj, s, gd, cb, ch):
    nsub = gd // cb

    def body(v_ref, o_ref, pad):
        g = pl.program_id(0)
        _zero_pads(pad, s)
        pad[PAD_ROWS:PAD_ROWS + s, :] = v_ref[...]
        for gi, window in enumerate(POOL_WINDOWS):
            half = window // 2

            @pl.when(g == gi)
            def _(half=half):
                for c0 in range(0, s, ch):
                    base = PAD_ROWS + c0
                    acc = pad[base - half:base - half + ch, :]
                    for j in range(-half + 1, half):
                        acc = acc + pad[base + j:base + j + ch, :]
                    out = acc / _window_count(c0, ch, half, s) - v_ref[c0:c0 + ch, :]
                    o_ref[c0:c0 + ch, :] = out.astype(o_ref.dtype)

    spec = pl.BlockSpec((s, cb), lambda g, j: (0, g * nsub + j))
    return _pallas(body, name="pool_fwd", grid=(N_POOL_GROUPS, nsub), in_specs=[spec], out_specs=spec,
                   out_shape=jax.ShapeDtypeStruct((s, N_POOL_GROUPS * gd), BF16),
                   scratch_shapes=[pltpu.VMEM((s + 2 * PAD_ROWS, cb), F32)],
                   compiler_params=_params("parallel", "parallel"))(proj)


def _pool_bwd(dpooled, dproj, s, gd, cb, ch):
    nsub = gd // cb

    def body(dp_ref, dproj_in, o_ref, pad):
        del dproj_in
        g = pl.program_id(0)
        _zero_pads(pad, s)
        for gi, window in enumerate(POOL_WINDOWS):
            half = window // 2

            @pl.when(g == gi)
            def _(half=half):
                for c0 in range(0, s, ch):
                    pad[PAD_ROWS + c0:PAD_ROWS + c0 + ch, :] = dp_ref[c0:c0 + ch, :] / _window_count(c0, ch, half, s)
                for c0 in range(0, s, ch):
                    base = PAD_ROWS + c0
                    acc = pad[base - half + 1:base - half + 1 + ch, :]
                    for j in range(-half + 2, half + 1):
                        acc = acc + pad[base + j:base + j + ch, :]
                    o_ref[c0:c0 + ch, :] = (acc - dp_ref[c0:c0 + ch, :]).astype(o_ref.dtype)

    spec = pl.BlockSpec((s, cb), lambda g, j: (0, g * nsub + j))
    return _pallas(body, name="pool_bwd", grid=(N_POOL_GROUPS, nsub),
                   in_specs=[spec, pl.BlockSpec(memory_space=pl.ANY)], out_specs=spec,
                   out_shape=jax.ShapeDtypeStruct(dproj.shape, dproj.dtype), input_output_aliases={1: 0},
                   scratch_shapes=[pltpu.VMEM((s + 2 * PAD_ROWS, cb), F32)],
                   compiler_params=_params("parallel", "parallel"))(dpooled, dproj)


def _poolmix_fwd(pooled, wmix, pool_scale, gnorm_g, d_model, tm=512):
    s = pooled.shape[0]
    gd = wmix.shape[1]
    tm = _tile(tm, s)

    def body(p_ref, w_ref, ps_ref, g_ref, apre_ref, mixed_ref):
        a_pre = jnp.dot(p_ref[...], w_ref[...], preferred_element_type=F32)
        apre_ref[...] = a_pre
        a_out = a_pre * ps_ref[...]
        mixed_ref[...] = ((a_out * _rms(a_out)) * g_ref[...]).astype(mixed_ref.dtype)

    blk = pl.BlockSpec((tm, gd), lambda g, i: (i, g))
    vec = pl.BlockSpec((1, gd), lambda g, i: (0, g))
    return _pallas(body, name="poolmix_fwd", grid=(N_POOL_GROUPS, s // tm),
                   in_specs=[blk, pl.BlockSpec((None, gd, gd), lambda g, i: (g, 0, 0)), vec, vec],
                   out_specs=[blk, blk],
                   out_shape=[jax.ShapeDtypeStruct((s, N_POOL_GROUPS * gd), F32), jax.ShapeDtypeStruct((s, d_model), BF16)],
                   compiler_params=_params("parallel", "parallel"))(pooled, wmix, pool_scale, gnorm_g)


def _poolmix_bwd(dmixed, a_pre, wmix, pool_scale, gnorm_g, tm=512):
    s = a_pre.shape[0]
    gd = wmix.shape[1]
    tm = _tile(tm, s)

    def body(dm_ref, apre_ref, w_ref, ps_ref, g_ref, dapre_ref, dpooled_ref, dps_ref, dg_ref):
        @pl.when(pl.program_id(1) == 0)
        def _():
            dps_ref[...] = jnp.zeros_like(dps_ref)
            dg_ref[...] = jnp.zeros_like(dg_ref)

        a_pre, dm, ps = apre_ref[...], dm_ref[...], ps_ref[...]
        a_out = a_pre * ps
        r = _rms(a_out)
        n = a_out * r
        dg_ref[...] += _colsum(dm * n)
        dn = dm * g_ref[...]
        da_out = r * (dn - n * jnp.mean(dn * n, axis=-1, keepdims=True))
        dps_ref[...] += _colsum(da_out * a_pre)
        da_pre = (da_out * ps).astype(BF16)
        dapre_ref[...] = da_pre
        dpooled_ref[...] = lax.dot_general(da_pre, w_ref[...], _DOT_DIMS["nt"], preferred_element_type=F32)

    blk = pl.BlockSpec((tm, gd), lambda g, i: (i, g))
    vec = pl.BlockSpec((1, gd), lambda g, i: (0, g))
    width = N_POOL_GROUPS * gd
    return _pallas(body, name="poolmix_bwd", grid=(N_POOL_GROUPS, s // tm),
                   in_specs=[blk, blk, pl.BlockSpec((None, gd, gd), lambda g, i: (g, 0, 0)), vec, vec],
                   out_specs=[blk, blk, vec, vec],
                   out_shape=[jax.ShapeDtypeStruct((s, width), BF16), jax.ShapeDtypeStruct((s, width), F32),
                              jax.ShapeDtypeStruct((1, width), F32), jax.ShapeDtypeStruct((1, width), F32)],
                   compiler_params=_params("parallel", "arbitrary"))(dmixed, a_pre, wmix, pool_scale, gnorm_g)


def _poolmix_wgrad(pooled, da_pre, gd, tk=1024):
    s = pooled.shape[0]
    tk = _tile(tk, s)
    nk = s // tk

    def body(p_ref, d_ref, o_ref, acc):
        k = pl.program_id(1)
        part = lax.dot_general(p_ref[...], d_ref[...], _DOT_DIMS["tn"], preferred_element_type=F32)

        @pl.when(k == 0)
        def _():
            acc[...] = part

        @pl.when(k > 0)
        def _():
            acc[...] += part

        @pl.when(k == nk - 1)
        def _():
            o_ref[...] = acc[...].astype(o_ref.dtype)

    blk = pl.BlockSpec((tk, gd), lambda g, k: (k, g))
    return _pallas(body, name="poolmix_wgrad", grid=(N_POOL_GROUPS, nk), in_specs=[blk, blk],
                   out_specs=pl.BlockSpec((None, gd, gd), lambda g, k: (g, 0, 0)),
                   out_shape=jax.ShapeDtypeStruct((N_POOL_GROUPS, gd, gd), BF16),
                   scratch_shapes=[pltpu.VMEM((gd, gd), F32)],
                   compiler_params=_params("parallel", "arbitrary"))(pooled, da_pre)


def _head_mean(v):
    parts = []
    for q in range(v.shape[1] // CONV_HEAD_DIM):
        m = jnp.mean(v[:, q * CONV_HEAD_DIM:(q + 1) * CONV_HEAD_DIM], axis=-1, keepdims=True)
        parts.append(jnp.broadcast_to(m, (v.shape[0], CONV_HEAD_DIM)))
    return parts[0] if len(parts) == 1 else jnp.concatenate(parts, axis=1)


def _conv_fwd(proj, mixed, conv_w, conv_b, gnorm_g, s, width, cb, ch):
    nblk = width // cb

    def body(b_ref, c_ref, u_ref, w_ref, cb_ref, g_ref, mixed_in, o_ref, pad):
        del mixed_in
        _zero_pads(pad, s)
        pad[PAD_ROWS:PAD_ROWS + s, :] = c_ref[...] * u_ref[...]
        w = w_ref[...]
        for c0 in range(0, s, ch):
            base = PAD_ROWS + c0
            conv = (w[0:1] * pad[base - 1:base - 1 + ch, :] + w[1:2] * pad[base:base + ch, :]
                    + w[2:3] * pad[base + 1:base + 1 + ch, :] + cb_ref[...])
            bo = b_ref[c0:c0 + ch, :] * conv
            n = bo * lax.rsqrt(_head_mean(bo * bo) + EPS)
            o_ref[c0:c0 + ch, :] = (n * g_ref[...]).astype(o_ref.dtype)

    def part(p):
        return pl.BlockSpec((s, cb), lambda j: (0, p * nblk + j))

    vec = pl.BlockSpec((1, cb), lambda j: (0, j))
    return _pallas(body, name="conv_fwd", grid=(nblk,),
                   in_specs=[part(1), part(2), part(3), pl.BlockSpec((3, cb), lambda j: (0, j)), vec, vec,
                             pl.BlockSpec(memory_space=pl.ANY)],
                   out_specs=part(1), out_shape=jax.ShapeDtypeStruct(mixed.shape, mixed.dtype),
                   input_output_aliases={6: 0},
                   scratch_shapes=[pltpu.VMEM((s + 2 * PAD_ROWS, cb), F32)],
                   compiler_params=_params("parallel"))(proj, proj, proj, conv_w, conv_b, gnorm_g, mixed)


def _conv_bwd(dmixed, proj, conv_w, conv_b, gnorm_g, s, width, cb, ch, deps=()):
    nblk = width // cb

    def body(dm_ref, b_ref, c_ref, u_ref, w_ref, cb_ref, g_ref, dproj_ref, dw_ref, dcb_ref, dg_ref,
             pad_cu, pad_dconv, db_buf, dc_buf, du_buf, sems):
        j = pl.program_id(0)
        _zero_pads(pad_cu, s)
        _zero_pads(pad_dconv, s)
        pad_cu[PAD_ROWS:PAD_ROWS + s, :] = c_ref[...] * u_ref[...]
        w, gv = w_ref[...], g_ref[...]
        zero = jnp.zeros((1, cb), F32)
        dw0, dw1, dw2, dcb, dg = zero, zero, zero, zero, zero
        for c0 in range(0, s, ch):
            base = PAD_ROWS + c0
            cu_prev, cu_here, cu_next = (pad_cu[base - 1:base - 1 + ch, :], pad_cu[base:base + ch, :],
                                         pad_cu[base + 1:base + 1 + ch, :])
            conv = w[0:1] * cu_prev + w[1:2] * cu_here + w[2:3] * cu_next + cb_ref[...]
            bg = b_ref[c0:c0 + ch, :]
            bo = bg * conv
            r = lax.rsqrt(_head_mean(bo * bo) + EPS)
            n = bo * r
            dm = dm_ref[c0:c0 + ch, :]
            dg = dg + _colsum(dm * n)
            dn = dm * gv
            dbo = r * (dn - n * _head_mean(dn * n))
            db_buf[c0:c0 + ch, :] = (dbo * conv).astype(BF16)
            dconv = dbo * bg
            pad_dconv[base:base + ch, :] = dconv
            dcb = dcb + _colsum(dconv)
            dw0 = dw0 + _colsum(dconv * cu_prev)
            dw1 = dw1 + _colsum(dconv * cu_here)
            dw2 = dw2 + _colsum(dconv * cu_next)
        dw_ref[0:1, :] = dw0
        dw_ref[1:2, :] = dw1
        dw_ref[2:3, :] = dw2
        dcb_ref[...] = dcb
        dg_ref[...] = dg
        for c0 in range(0, s, ch):
            base = PAD_ROWS + c0
            dcu = (w[0:1] * pad_dconv[base + 1:base + 1 + ch, :] + w[1:2] * pad_dconv[base:base + ch, :]
                   + w[2:3] * pad_dconv[base - 1:base - 1 + ch, :])
            dc_buf[c0:c0 + ch, :] = (dcu * u_ref[c0:c0 + ch, :]).astype(BF16)
            du_buf[c0:c0 + ch, :] = (dcu * c_ref[c0:c0 + ch, :]).astype(BF16)
        copies = []
        for p, buf in enumerate((db_buf, dc_buf, du_buf)):
            col = pl.multiple_of((p + 1) * width + j * cb, CONV_HEAD_DIM)
            copies.append(pltpu.make_async_copy(buf, dproj_ref.at[:, pl.ds(col, cb)], sems.at[p]))
            copies[-1].start()
        for cp in copies:
            cp.wait()

    def part(p):
        return pl.BlockSpec((s, cb), lambda j: (0, p * nblk + j))

    vec = pl.BlockSpec((1, cb), lambda j: (0, j))
    w_spec = pl.BlockSpec((3, cb), lambda j: (0, j))
    return _pallas(body, deps, name="conv_bwd", grid=(nblk,),
                   in_specs=[part(1), part(1), part(2), part(3), w_spec, vec, vec],
                   out_specs=[pl.BlockSpec(memory_space=pl.ANY), w_spec, vec, vec],
                   out_shape=[jax.ShapeDtypeStruct((s, 4 * width), BF16), jax.ShapeDtypeStruct((3, width), F32),
                              jax.ShapeDtypeStruct((1, width), F32), jax.ShapeDtypeStruct((1, width), F32)],
                   scratch_shapes=[pltpu.VMEM((s + 2 * PAD_ROWS, cb), F32), pltpu.VMEM((s + 2 * PAD_ROWS, cb), F32),
                                   pltpu.VMEM((s, cb), BF16), pltpu.VMEM((s, cb), BF16), pltpu.VMEM((s, cb), BF16),
                                   pltpu.SemaphoreType.DMA((3,))],
                   compiler_params=_params("arbitrary"))(dmixed, proj, proj, proj, conv_w, conv_b, gnorm_g)


def _adamw(w, g, m, v):
    m = ADAM_B1 * m + (1.0 - ADAM_B1) * g
    v = ADAM_B2 * v + (1.0 - ADAM_B2) * (g * g)
    m_hat = m / (1.0 - ADAM_B1 ** ADAM_STEP)
    v_hat = v / (1.0 - ADAM_B2 ** ADAM_STEP)
    delta = -ADAM_LR * (m_hat / (jnp.sqrt(v_hat) + ADAM_EPS) + ADAM_WD * w)
    return delta, m, v


def _ada_fwd(c_rows, w, b, tn=512):
    rows, d = c_rows.shape
    n = w.shape[1]
    tn = _tile(tn, n)

    def body(c_ref, w_ref, b_ref, o_ref):
        cv = c_ref[...]
        act = (cv * jax.nn.sigmoid(cv)).astype(BF16)
        o_ref[...] = jnp.dot(act, w_ref[...].astype(BF16), preferred_element_type=F32) + b_ref[...]

    return _pallas(body, name="ada_fwd", grid=(n // tn,),
                   in_specs=[pl.BlockSpec((rows, d), lambda j: (0, 0)), pl.BlockSpec((d, tn), lambda j: (0, j)),
                             pl.BlockSpec((1, tn), lambda j: (0, j))],
                   out_specs=pl.BlockSpec((rows, tn), lambda j: (0, j)),
                   out_shape=jax.ShapeDtypeStruct((rows, n), F32), compiler_params=_params("parallel"))(c_rows, w, b)


def _ada_bwd_adam(c_cols, dmod, w, m, v, tr=512, tn=1024):
    d, rows = c_cols.shape
    n = w.shape[1]
    tr, tn = _tile(tr, d), _tile(tn, n)

    def body(c_ref, dm_ref, w_ref, m_ref, v_ref, g_ref, dl_ref, nm_ref, nv_ref):
        cv = c_ref[...]
        act = (cv * jax.nn.sigmoid(cv)).astype(BF16)
        g = jnp.dot(act, dm_ref[...].astype(BF16), preferred_element_type=F32)
        g_ref[...] = g
        dl_ref[...], nm_ref[...], nv_ref[...] = _adamw(w_ref[...], g, m_ref[...], v_ref[...])

    blk = pl.BlockSpec((tr, tn), lambda i, j: (i, j))
    shape = jax.ShapeDtypeStruct((d, n), F32)
    return _pallas(body, name="ada_bwd_adam", grid=(d // tr, n // tn),
                   in_specs=[pl.BlockSpec((tr, rows), lambda i, j: (i, 0)), pl.BlockSpec((rows, tn), lambda i, j: (0, j)),
                             blk, blk, blk],
                   out_specs=[blk] * 4, out_shape=[shape] * 4,
                   compiler_params=_params("parallel", "parallel"))(c_cols, dmod, w, m, v)


def _reduce_adam(name, pieces, w, m, v, tr=256, tc=1024):
    r, c = w.shape
    tr, tc = _tile(tr, r), _tile(tc, c)

    def body(p_ref, w_ref, m_ref, v_ref, g_ref, dl_ref, nm_ref, nv_ref):
        g = p_ref[0].astype(F32)
        for j in range(1, N_DEV):
            g = g + p_ref[j].astype(F32)
        g_ref[...] = g
        dl_ref[...], nm_ref[...], nv_ref[...] = _adamw(w_ref[...], g, m_ref[...], v_ref[...])

    blk = pl.BlockSpec((tr, tc), lambda i, j: (i, j))
    shape = jax.ShapeDtypeStruct((r, c), F32)
    return _pallas(body, name=name, grid=(r // tr, c // tc),
                   in_specs=[pl.BlockSpec((N_DEV, tr, tc), lambda i, j: (0, i, j)), blk, blk, blk],
                   out_specs=[blk] * 4, out_shape=[shape] * 4,
                   compiler_params=_params("parallel", "parallel"))(pieces, w, m, v)


def _reduce_adam_chips(name, sums, land, w, m, v, tr=256, tc=1024):
    r, c = w.shape
    tr, tc = _tile(tr, r), _tile(tc, c)

    def body(s_ref, l_ref, w_ref, m_ref, v_ref, g_ref, dl_ref, nm_ref, nv_ref):
        g = s_ref[...].astype(F32)
        for k in range(3):
            g = g + l_ref[k].astype(F32)
        g_ref[...] = g
        dl_ref[...], nm_ref[...], nv_ref[...] = _adamw(w_ref[...], g, m_ref[...], v_ref[...])

    blk = pl.BlockSpec((tr, tc), lambda i, j: (i, j))
    shape = jax.ShapeDtypeStruct((r, c), F32)
    mine = pl.BlockSpec((None, tr, tc), lambda i, j: (2 * lax.axis_index("x") + lax.axis_index("y"), i, j))
    return _pallas(body, name=name, grid=(r // tr, c // tc),
                   in_specs=[mine, pl.BlockSpec((3, tr, tc), lambda i, j: (0, i, j)), blk, blk, blk],
                   out_specs=[blk] * 4, out_shape=[shape] * 4,
                   compiler_params=_params("parallel", "parallel"))(sums, land, w, m, v)


def _sum_devices(parts):
    n = parts.shape[1]

    def body(p_ref, o_ref):
        acc = p_ref[0:1, :]
        for j in range(1, N_DEV):
            acc = acc + p_ref[j:j + 1, :]
        o_ref[...] = acc

    return _pallas(body, name="sum_devices", out_shape=jax.ShapeDtypeStruct((1, n), F32),
                   compiler_params=pltpu.CompilerParams(vmem_limit_bytes=VMEM_LIMIT_BYTES))(parts)


def _adam_small(name, g, w, m, v):
    def body(g_ref, w_ref, m_ref, v_ref, dl_ref, nm_ref, nv_ref):
        dl_ref[...], nm_ref[...], nv_ref[...] = _adamw(w_ref[...], g_ref[...], m_ref[...], v_ref[...])

    shape = jax.ShapeDtypeStruct(w.shape, F32)
    return _pallas(body, name=name, out_shape=[shape] * 3,
                   compiler_params=pltpu.CompilerParams(vmem_limit_bytes=VMEM_LIMIT_BYTES))(g, w, m, v)


def kernel(x, c, w_ada, b_ada, norm1_g, w_in, pool_mix_w, pool_scale, conv_w, conv_b, gnorm_pool_g, gnorm_conv_g, w_out, norm2_g, w_mlp_in, w_mlp_out, final_g, loss_target, m_w_ada, m_b_ada, m_norm1_g, m_w_in, m_pool_mix_w, m_pool_scale, m_conv_w, m_conv_b, m_gnorm_pool_g, m_gnorm_conv_g, m_w_out, m_norm2_g, m_w_mlp_in, m_w_mlp_out, m_final_g, v_w_ada, v_b_ada, v_norm1_g, v_w_in, v_pool_mix_w, v_pool_scale, v_conv_w, v_conv_b, v_gnorm_pool_g, v_gnorm_conv_g, v_w_out, v_norm2_g, v_w_mlp_in, v_w_mlp_out, v_final_g):
    s, d = x.shape[1], x.shape[2]
    width = d // 2
    gd = width // N_POOL_GROUPS
    d_ff = w_mlp_in.shape[2] * N_DEV
    n_proj = w_in.shape[2] * N_DEV
    ada_cols = w_ada.shape[2]
    conv_cols = conv_w.shape[2]
    assert n_proj == 4 * width and ada_cols * N_DEV == N_MOD * d and d_ff % N_DEV == 0
    assert width % CONV_HEAD_DIM == 0 and s % 8 == 0
    seq_chunk = _tile(512, s)
    pool_cb = _tile(256, gd)
    conv_cb = CONV_HEAD_DIM

    me = 4 * lax.axis_index("x") + 2 * lax.axis_index("y") + lax.axis_index("c")
    x2d, target = x[0], loss_target[0]

    wmix_all, conv_w_all, c_all = _exchange(
        "gather_small_weights", [pool_mix_w[0].astype(BF16), conv_w[0], c], ["gather"] * 3)
    wmix_full = jnp.transpose(wmix_all, (1, 0, 2, 3)).reshape(N_POOL_GROUPS, gd, gd)
    conv_w_full = jnp.transpose(conv_w_all, (1, 0, 2)).reshape(3, width)
    c_rows = jnp.concatenate([c_all.reshape(N_DEV, d), jnp.zeros((N_DEV, d), F32)], axis=0)

    b_mine = lax.dynamic_slice(b_ada, (0, me * ada_cols), (1, ada_cols))
    mod_part = _ada_fwd(c_rows, w_ada[0], b_mine)
    (mod_all,) = _exchange("scatter_mod", [mod_part[:N_DEV].reshape(N_DEV, 1, ada_cols)], ["a2a"])
    mod = mod_all.reshape(1, N_MOD * d)

    started = {}
    for wname, wgt in (("w_in", w_in), ("w_out", w_out), ("w_mlp_in", w_mlp_in), ("w_mlp_out", w_mlp_out)):
        land = _landing(wgt[0].astype(BF16), me)
        started[wname] = _gather_start("gather_" + wname + "_start", land, deps=(mod,))
    gather_tokens = tuple(st[4] for st in started.values())

    def gathered(wname, after):
        relayed = _gather_relay("gather_" + wname + "_relay", started[wname], after)
        return _gather_wait("gather_" + wname + "_wait", started[wname], relayed)
    shift1, scale1, gate1, shift2, scale2, gate2 = [mod[:, i * d:(i + 1) * d] for i in range(N_MOD)]

    h1 = _norm_mod("norm1_fwd", x2d, norm1_g, scale1, shift1, deps=gather_tokens)
    w_in_all = gathered("w_in", h1)
    (proj,) = _mm_nn("in_proj", h1, w_in_all, n_proj, True, [jax.ShapeDtypeStruct((s, n_proj), F32)], _store(F32))
    pooled = _pool_fwd(proj, s, gd, pool_cb, seq_chunk)
    a_pre, mixed = _poolmix_fwd(pooled, wmix_full, pool_scale, gnorm_pool_g, d)
    mixed = _conv_fwd(proj, mixed, conv_w_full, conv_b, gnorm_conv_g, s, width, conv_cb, seq_chunk)

    def residual_specs(tm, tn):
        return [pl.BlockSpec((tm, tn), lambda i, j, k: (i, j)), pl.BlockSpec((1, tn), lambda i, j, k: (0, j))]

    sd_f32 = jax.ShapeDtypeStruct((s, d), F32)
    w_out_full = gathered("w_out", mixed).reshape(d, d)
    attn, x_mid = _mm_nn("out_proj", mixed, w_out_full, d, False, [sd_f32, sd_f32], _residual_epilogue,
                         extras=(x2d, gate1), extra_specs=residual_specs)
    h2 = _norm_mod("norm2_fwd", x_mid, norm2_g, scale2, shift2)
    sf_bf16 = jax.ShapeDtypeStruct((s, d_ff), BF16)
    w1_all = gathered("w_mlp_in", h2)
    relu, hid = _mm_nn("mlp_in", h2, w1_all, d_ff, True, [sf_bf16, sf_bf16], _relu2_epilogue)
    w2_full = gathered("w_mlp_out", hid).reshape(d_ff, d)
    mlp, x_last = _mm_nn("mlp_out", hid, w2_full, d, False, [sd_f32, sd_f32], _residual_epilogue,
                         extras=(x_mid, gate2), extra_specs=residual_specs)

    dx_last, dmlp, d_final_g, dgate2, loss_row = _loss_head(x_last, target, final_g.reshape(1, d), gate2, mlp)

    def relu_specs(tm, tn):
        return [pl.BlockSpec((tm, tn), lambda i, j, k: (i, j))]

    def scatter_pairs(wname, grad, deps=()):
        return _pair_start("scatter_" + wname + "_pair_start", grad, deps)

    def scatter_chips(wname, pairs, after):
        grad, pair = _pair_wait("scatter_" + wname + "_pair_wait", pairs, after)
        return _chip_start("scatter_" + wname + "_chip_start", _pair_sum("pair_sum_" + wname, grad, pair))

    (dhpre,) = _mm_nt("mlp_out_dx", dmlp, w2_full, d_ff, False, [sf_bf16], _relu2_bwd_epilogue,
                      extras=(relu,), extra_specs=relu_specs)
    g_w2 = _mm_tn("mlp_out_dw", hid, dmlp, False)
    pairs_w2 = scatter_pairs("w_mlp_out", g_w2.reshape(N_DEV, d_ff // N_DEV, d))
    (dh2,) = _mm_nt("mlp_in_dx", dhpre, w1_all, d, True, [sd_f32], _store(F32), tn=1024, deps=(pairs_w2[4],))
    g_w1 = _mm_tn("mlp_in_dw", h2, dhpre, True)
    chips_w2 = scatter_chips("w_mlp_out", pairs_w2, g_w1)
    pairs_w1 = scatter_pairs("w_mlp_in", g_w1)
    dx_mid, dshift2, dscale2, d_norm2_g, dattn, dgate1 = _norm_mod_bwd(
        "norm2_bwd", dh2, x_mid, norm2_g, scale2, dx_last, branch=attn, gate=gate1,
        deps=(chips_w2[4], pairs_w1[4]))

    (dmixed,) = _mm_nt("out_proj_dx", dattn, w_out_full, d, False, [sd_f32], _store(F32))
    g_w_out = _mm_tn("out_proj_dw", mixed, dattn, False)
    chips_w1 = scatter_chips("w_mlp_in", pairs_w1, g_w_out)
    pairs_w_out = scatter_pairs("w_out", g_w_out.reshape(N_DEV, d // N_DEV, d))
    dproj, d_conv_w, d_conv_b, d_gnorm_conv = _conv_bwd(dmixed, proj, conv_w_full, conv_b, gnorm_conv_g,
                                                        s, width, conv_cb, seq_chunk,
                                                        deps=(chips_w1[4], pairs_w_out[4]))
    da_pre, dpooled, d_pool_scale, d_gnorm_pool = _poolmix_bwd(dmixed, a_pre, wmix_full, pool_scale, gnorm_pool_g)
    g_wmix = _poolmix_wgrad(pooled, da_pre, gd)
    dproj = _pool_bwd(dpooled, dproj, s, gd, pool_cb, seq_chunk)

    g_w_in = _mm_tn("in_proj_dw", h1, dproj, True)
    chips_w_out = scatter_chips("w_out", pairs_w_out, g_w_in)
    pairs_w_in = scatter_pairs("w_in", g_w_in, deps=(chips_w_out[4],))
    chips_w_in = scatter_chips("w_in", pairs_w_in, pairs_w_in[4])
    (dh1,) = _mm_nt("in_proj_dx", dproj, w_in_all, d, True, [sd_f32], _store(F32), tn=1024,
                    deps=(chips_w_out[4], chips_w_in[4]))
    grad_x, dshift1, dscale1, d_norm1_g = _norm_mod_bwd("norm1_bwd", dh1, x2d, norm1_g, scale1, dx_mid)

    sums, landed = _chip_wait("scatter_w_mlp_out_chip_wait", chips_w2, grad_x)
    out_w2 = _reduce_adam_chips("adam_w_mlp_out", sums, landed, w_mlp_out[0], m_w_mlp_out[0], v_w_mlp_out[0])
    sums, landed = _chip_wait("scatter_w_mlp_in_chip_wait", chips_w1, out_w2[0])
    out_w1 = _reduce_adam_chips("adam_w_mlp_in", sums, landed, w_mlp_in[0], m_w_mlp_in[0], v_w_mlp_in[0])
    sums, landed = _chip_wait("scatter_w_out_chip_wait", chips_w_out, out_w1[0])
    out_w_out = _reduce_adam_chips("adam_w_out", sums, landed, w_out[0], m_w_out[0], v_w_out[0])
    sums, landed = _chip_wait("scatter_w_in_chip_wait", chips_w_in, out_w_out[0])
    out_w_in = _reduce_adam_chips("adam_w_in", sums, landed, w_in[0], m_w_in[0], v_w_in[0])

    rows_mix = gd // N_DEV
    g_wmix_split = jnp.transpose(g_wmix.reshape(N_POOL_GROUPS, N_DEV, rows_mix, gd), (1, 0, 2, 3))
    g_wmix_split = g_wmix_split.reshape(N_DEV, N_POOL_GROUPS * rows_mix, gd)
    loss_pad = jnp.concatenate([loss_row[:, :1], jnp.zeros((1, 127), F32)], axis=1)
    dmod = jnp.concatenate([dshift1, dscale1, dgate1, dshift2, dscale2, dgate2], axis=1)
    small = jnp.concatenate([dmod, d_norm1_g, d_pool_scale, d_conv_b, d_gnorm_pool, d_gnorm_conv, d_norm2_g,
                             d_final_g, d_conv_w.reshape(1, 3 * width), loss_pad], axis=1)
    p_wmix, small_all = _exchange("exchange_small_grads", [g_wmix_split, small], ["a2a", "gather"],
                                  deps=(out_w_in[0],))
    mix_shape = (N_POOL_GROUPS * rows_mix, gd)
    out_wmix = _reduce_adam("adam_pool_mix", p_wmix, pool_mix_w.reshape(mix_shape), m_pool_mix_w.reshape(mix_shape),
                            v_pool_mix_w.reshape(mix_shape))
    out_wmix = [a.reshape(pool_mix_w.shape) for a in out_wmix]
    small_all = small_all.reshape(N_DEV, small.shape[1])
    small_sum = _sum_devices(small_all)

    n_rep = (N_MOD + 1) * d + 4 * width + 2 * d
    loss = small_sum[0, n_rep + 3 * width]
    rep_names_w = [b_ada, norm1_g, pool_scale, conv_b, gnorm_pool_g, gnorm_conv_g, norm2_g, final_g.reshape(1, d)]
    rep_names_m = [m_b_ada, m_norm1_g, m_pool_scale, m_conv_b, m_gnorm_pool_g, m_gnorm_conv_g, m_norm2_g,
                   m_final_g.reshape(1, d)]
    rep_names_v = [v_b_ada, v_norm1_g, v_pool_scale, v_conv_b, v_gnorm_pool_g, v_gnorm_conv_g, v_norm2_g,
                   v_final_g.reshape(1, d)]
    rep_grad = small_sum[:, :n_rep]
    rep_delta, rep_m, rep_v = _adam_small("adam_replicated", rep_grad, jnp.concatenate(rep_names_w, axis=1),
                                          jnp.concatenate(rep_names_m, axis=1), jnp.concatenate(rep_names_v, axis=1))

    def split_rep(vec):
        out, off = [], 0
        for wgt in rep_names_w:
            n = wgt.shape[1]
            out.append(vec[:, off:off + n])
            off += n
        out[-1] = out[-1].reshape(d)
        return out

    conv_grad_full = small_sum[:, n_rep:n_rep + 3 * width].reshape(3, width)
    g_conv_w = lax.dynamic_slice(conv_grad_full, (0, me * conv_cols), (3, conv_cols))
    g_conv_w8 = jnp.concatenate([g_conv_w, jnp.zeros((5, conv_cols), F32)], axis=0)

    def pad8(a):
        return jnp.concatenate([a[0], jnp.zeros((5, conv_cols), F32)], axis=0)

    conv_delta, conv_m, conv_v = _adam_small("adam_conv_w", g_conv_w8, pad8(conv_w), pad8(m_conv_w), pad8(v_conv_w))

    dmod_all = small_all[:, :N_MOD * d]
    dmod_mine = lax.dynamic_slice(dmod_all, (0, me * ada_cols), (N_DEV, ada_cols))
    dmod_rows = jnp.concatenate([dmod_mine, jnp.zeros((N_DEV, ada_cols), F32)], axis=0)
    out_ada = _ada_bwd_adam(jnp.transpose(c_rows), dmod_rows, w_ada[0], m_w_ada[0], v_w_ada[0])

    rep_all = [split_rep(rep_grad), split_rep(rep_delta), split_rep(rep_m), split_rep(rep_v)]
    conv_all = [g_conv_w[None], conv_delta[None, :3], conv_m[None, :3], conv_v[None, :3]]
    outs = [loss, grad_x[None]]
    for kind in range(4):
        b_ada_o, norm1_o, pool_scale_o, conv_b_o, gpool_o, gconv_o, norm2_o, final_o = rep_all[kind]
        outs += [out_ada[kind][None], b_ada_o, norm1_o, out_w_in[kind][None], out_wmix[kind], pool_scale_o,
                 conv_all[kind], conv_b_o, gpool_o, gconv_o, out_w_out[kind][None], norm2_o, out_w1[kind][None],
                 out_w2[kind][None], final_o]
    return tuple(outs)
```

```python
import jax
import jax.numpy as jnp
from jax import lax
from jax.experimental import pallas as pl
from jax.experimental.pallas import tpu as pltpu

F32 = jnp.float32
BF16 = jnp.bfloat16
MESH = pl.DeviceIdType.MESH

N_DEV = 8
N_MOD = 6
EPS = 1e-6
POOL_WINDOWS = (2, 4, 8, 16)
N_POOL_GROUPS = len(POOL_WINDOWS)
CONV_HEAD_DIM = 128
PAD_ROWS = 16

ADAM_LR = 0.001
ADAM_B1 = 0.9
ADAM_B2 = 0.999
ADAM_EPS = 1e-08
ADAM_WD = 0.01
ADAM_STEP = 10

VMEM_LIMIT_BYTES = 56 * 1024 * 1024
MM_TM, MM_TN, MM_TK = 1024, 512, 4096


def _pallas(body, deps=(), **kw):
    if not deps:
        return pl.pallas_call(body, **kw)
    n_in = len(kw["in_specs"])

    def with_deps(*refs):
        body(*refs[:n_in], *refs[n_in + len(deps):])

    kw["in_specs"] = list(kw["in_specs"]) + [pl.BlockSpec(memory_space=pl.ANY)] * len(deps)
    call = pl.pallas_call(with_deps, **kw)
    return lambda *operands: call(*operands, *deps)


def _params(*sem):
    return pltpu.CompilerParams(dimension_semantics=sem, vmem_limit_bytes=VMEM_LIMIT_BYTES)


def _tile(pref, dim):
    if dim <= pref:
        return dim
    for t in range(pref - pref % 128, 0, -128):
        if dim % t == 0:
            return t
    return dim


def _exchange(name, arrays, modes, deps=()):
    n = len(arrays)
    out_shape = []
    for a, mode in zip(arrays, modes):
        piece = a.shape if mode == "gather" else a.shape[1:]
        out_shape.append(jax.ShapeDtypeStruct((N_DEV,) + tuple(piece), a.dtype))

    def body(*refs):
        srcs, dsts = refs[:n], refs[n:2 * n]
        send_sems, recv_sems, local_sems = refs[2 * n:]
        x, y, c = lax.axis_index("x"), lax.axis_index("y"), lax.axis_index("c")
        me = 4 * x + 2 * y + c
        copies = []
        for i in range(n):
            gather = modes[i] == "gather"
            local = pltpu.make_async_copy(srcs[i] if gather else srcs[i].at[me], dsts[i].at[me], local_sems.at[i])
            local.start()
            copies.append(local)
            for k in range(1, N_DEV):
                kx, ky, kc = (k >> 2) & 1, (k >> 1) & 1, k & 1
                peer = (1 - x if kx else x, 1 - y if ky else y, 1 - c if kc else c)
                peer_idx = 4 * peer[0] + 2 * peer[1] + peer[2]
                remote = pltpu.make_async_remote_copy(
                    src_ref=srcs[i] if gather else srcs[i].at[peer_idx],
                    dst_ref=dsts[i].at[me],
                    send_sem=send_sems.at[i * (N_DEV - 1) + k - 1],
                    recv_sem=recv_sems.at[i * (N_DEV - 1) + k - 1],
                    device_id=peer, device_id_type=MESH)
                remote.start()
                copies.append(remote)
        for cp in copies:
            cp.wait()

    any_spec = pl.BlockSpec(memory_space=pl.ANY)
    return _pallas(
        body, deps, name=name, out_shape=out_shape,
        in_specs=[any_spec] * n, out_specs=[any_spec] * n,
        scratch_shapes=[pltpu.SemaphoreType.DMA((n * (N_DEV - 1),)),
                        pltpu.SemaphoreType.DMA((n * (N_DEV - 1),)),
                        pltpu.SemaphoreType.DMA((n,))],
    )(*arrays)


_HBM = pl.BlockSpec(memory_space=pltpu.HBM)
_SEM = pl.BlockSpec(memory_space=pltpu.SEMAPHORE)
_TOKEN = pl.BlockSpec(memory_space=pltpu.VMEM)
_EFFECT = pltpu.SideEffectType.DATAFLOW_SIDE_EFFECTING
N_CHIP = N_DEV // 2
_OTHER_CHIPS = (1, 2, 3)


def _place():
    x, y, c = lax.axis_index("x"), lax.axis_index("y"), lax.axis_index("c")
    return x, y, c, (x, y, 1 - c)


def _same_core_of(x, y, c, k):
    px = 1 - x if k & 2 else x
    py = 1 - y if k & 1 else y
    return (px, py, c), 2 * px + py


def _remote(src, dst, send_sem, recv_sem, device):
    return pltpu.make_async_remote_copy(src_ref=src, dst_ref=dst, send_sem=send_sem, recv_sem=recv_sem,
                                        device_id=device, device_id_type=MESH)


def _token_shape():
    return jax.ShapeDtypeStruct((8, 128), F32)


def _split_call(body, deps, name, operands, in_specs, out_shape, out_specs, aliases):
    return _pallas(body, deps, name=name, out_shape=out_shape, in_specs=in_specs, out_specs=out_specs,
                   input_output_aliases=aliases,
                   compiler_params=pltpu.CompilerParams(has_side_effects=_EFFECT))(*operands)


def _gather_start(name, land, deps):
    def body(land_ref, send_sems, recv_ici, recv_d2d, land_thru, token):
        del land_thru
        x, y, c, sibling = _place()
        mine = land_ref.at[4 * x + 2 * y + c]
        for k in _OTHER_CHIPS:
            _remote(mine, mine, send_sems.at[k - 1], recv_ici.at[k - 1], _same_core_of(x, y, c, k)[0]).start()
        _remote(mine, mine, send_sems.at[3], recv_d2d.at[0], sibling).start()
        token[...] = jnp.zeros_like(token)

    return _split_call(
        body, deps, name, (pltpu.with_memory_space_constraint(land, pltpu.HBM),), (_HBM,),
        (pltpu.SemaphoreType.DMA((4,)), pltpu.SemaphoreType.DMA((3,)), pltpu.SemaphoreType.DMA((1,)),
         pltpu.HBM(land.shape, land.dtype), _token_shape()),
        (_SEM, _SEM, _SEM, _HBM, _TOKEN), {0: 3})


def _gather_wait_local(name, started, after):
    _, _, recv_d2d, land, _ = started

    def body(land_ref, recv_d2d, after_ref, land_out):
        del after_ref, land_out
        x, y, c, sibling = _place()
        mine = land_ref.at[4 * x + 2 * y + c]
        _remote(mine, mine, recv_d2d.at[0], recv_d2d.at[0], sibling).wait_recv()

    return _split_call(
        body, (), name, (land, recv_d2d, after), (_HBM, _SEM, pl.BlockSpec(memory_space=pl.ANY)),
        (pltpu.HBM(land.shape, land.dtype),), (_HBM,), {0: 0})[0]


def _gather_relay(name, started, land, after):
    recv_ici = started[1]

    def body(land_ref, recv_ici, after_ref, send_fwd, recv_fwd, land_thru):
        del after_ref, land_thru
        x, y, c, sibling = _place()
        mine = land_ref.at[4 * x + 2 * y + c]
        for k in _OTHER_CHIPS:
            _remote(mine, mine, send_fwd.at[k - 1], recv_ici.at[k - 1], sibling).wait_recv()
        for k in _OTHER_CHIPS:
            piece = land_ref.at[2 * _same_core_of(x, y, c, k)[1] + c]
            _remote(piece, piece, send_fwd.at[k - 1], recv_fwd.at[k - 1], sibling).start()

    return _split_call(
        body, (), name, (land, recv_ici, after), (_HBM, _SEM, pl.BlockSpec(memory_space=pl.ANY)),
        (pltpu.SemaphoreType.DMA((3,)), pltpu.SemaphoreType.DMA((3,)), pltpu.HBM(land.shape, land.dtype)),
        (_SEM, _SEM, _HBM), {0: 2})


def _gather_wait(name, started, relayed, local_waited):
    send_sems, _, recv_d2d, _, _ = started
    send_fwd, recv_fwd, land = relayed

    def body(land_ref, send_sems, recv_d2d, send_fwd, recv_fwd, land_out):
        del land_out
        x, y, c, sibling = _place()
        mine = land_ref.at[4 * x + 2 * y + c]
        for i in range(4):
            _remote(mine, mine, send_sems.at[i], recv_d2d.at[0], sibling).wait_send()
        if not local_waited:
            _remote(mine, mine, send_sems.at[3], recv_d2d.at[0], sibling).wait_recv()
        for k in _OTHER_CHIPS:
            relay = _remote(mine, mine, send_fwd.at[k - 1], recv_fwd.at[k - 1], sibling)
            relay.wait_send()
            relay.wait_recv()

    return _split_call(
        body, (), name, (land, send_sems, recv_d2d, send_fwd, recv_fwd), (_HBM, _SEM, _SEM, _SEM, _SEM),
        (pltpu.HBM(land.shape, land.dtype),), (_HBM,), {0: 0})[0]


def _landing(own, me):
    land = lax.empty((N_DEV,) + own.shape, own.dtype)
    return lax.dynamic_update_slice(land, own[None], (me,) + (0,) * own.ndim)


def _pair_start(name, far, deps=()):
    pair = lax.empty(far.shape, far.dtype)

    def body(far_ref, pair_ref, send_sems, recv_sems, far_thru, pair_thru, token):
        del far_thru, pair_thru
        _remote(far_ref, pair_ref, send_sems.at[0], recv_sems.at[0], _place()[3]).start()
        token[...] = jnp.zeros_like(token)

    return _split_call(
        body, deps, name,
        (pltpu.with_memory_space_constraint(far, pltpu.HBM), pltpu.with_memory_space_constraint(pair, pltpu.HBM)),
        (_HBM, _HBM),
        (pltpu.SemaphoreType.DMA((1,)), pltpu.SemaphoreType.DMA((1,)),
         pltpu.HBM(far.shape, far.dtype), pltpu.HBM(pair.shape, pair.dtype), _token_shape()),
        (_SEM, _SEM, _HBM, _HBM, _TOKEN), {0: 2, 1: 3})


def _pair_wait(name, started, after):
    send_sems, recv_sems, far, pair, _ = started

    def body(far_ref, pair_ref, send_sems, recv_sems, after_ref, far_out, pair_out):
        del after_ref, far_out, pair_out
        cp = _remote(far_ref, pair_ref, send_sems.at[0], recv_sems.at[0], _place()[3])
        cp.wait_send()
        cp.wait_recv()

    return _split_call(
        body, (), name, (far, pair, send_sems, recv_sems, after),
        (_HBM, _HBM, _SEM, _SEM, pl.BlockSpec(memory_space=pl.ANY)),
        (pltpu.HBM(far.shape, far.dtype), pltpu.HBM(pair.shape, pair.dtype)), (_HBM, _HBM), {0: 0, 1: 1})[1]


def _chip_start(name, sums, deps=()):
    land = lax.empty((3,) + sums.shape[1:], sums.dtype)

    def body(s_ref, land_ref, send_sems, recv_sems, s_thru, land_thru, token):
        del s_thru, land_thru
        x, y, c, _ = _place()
        for k in _OTHER_CHIPS:
            peer, chip = _same_core_of(x, y, c, k)
            _remote(s_ref.at[chip], land_ref.at[k - 1], send_sems.at[k - 1], recv_sems.at[k - 1], peer).start()
        token[...] = jnp.zeros_like(token)

    return _split_call(
        body, deps, name,
        (pltpu.with_memory_space_constraint(sums, pltpu.HBM), pltpu.with_memory_space_constraint(land, pltpu.HBM)),
        (_HBM, _HBM),
        (pltpu.SemaphoreType.DMA((3,)), pltpu.SemaphoreType.DMA((3,)),
         pltpu.HBM(sums.shape, sums.dtype), pltpu.HBM(land.shape, land.dtype), _token_shape()),
        (_SEM, _SEM, _HBM, _HBM, _TOKEN), {0: 2, 1: 3})


def _chip_wait(name, started, after):
    send_sems, recv_sems, sums, land, _ = started

    def body(s_ref, land_ref, send_sems, recv_sems, after_ref, s_out, land_out):
        del after_ref, s_out, land_out
        x, y, c, _ = _place()
        for k in _OTHER_CHIPS:
            peer, chip = _same_core_of(x, y, c, k)
            cp = _remote(s_ref.at[chip], land_ref.at[k - 1], send_sems.at[k - 1], recv_sems.at[k - 1], peer)
            cp.wait_send()
            cp.wait_recv()

    return _split_call(
        body, (), name, (sums, land, send_sems, recv_sems, after),
        (_HBM, _HBM, _SEM, _SEM, pl.BlockSpec(memory_space=pl.ANY)),
        (pltpu.HBM(sums.shape, sums.dtype), pltpu.HBM(land.shape, land.dtype)), (_HBM, _HBM), {0: 0, 1: 1})


_DOT_DIMS = {"nn": (((1,), (0,)), ((), ())), "nt": (((1,), (1,)), ((), ())), "tn": (((0,), (0,)), ((), ()))}


def _matmul(name, mode, operands, in_specs, out_shape, out_specs, grid, acc_shape, epilogue, deps=(), carry=()):
    n_in, n_out, nk = len(operands), len(out_shape), grid[2]
    dims = _DOT_DIMS[mode]

    def body(*refs):
        a_ref, b_ref = refs[0], refs[1]
        extras, outs = refs[2:n_in], refs[n_in:n_in + n_out]
        part = lax.dot_general(a_ref[...], b_ref[...], dims, preferred_element_type=F32)
        if nk == 1:
            epilogue(part, extras, outs)
            return
        acc = refs[-1]
        k = pl.program_id(2)

        @pl.when(k == 0)
        def _():
            acc[...] = part

        @pl.when(jnp.logical_and(k > 0, k < nk - 1))
        def _():
            acc[...] += part

        @pl.when(k == nk - 1)
        def _():
            epilogue(acc[...] + part, extras, outs)

    aliases = {n_in + len(deps) + i: i for i in range(len(carry))}
    return _pallas(body, tuple(deps) + tuple(carry), name=name, grid=grid, in_specs=in_specs, out_specs=out_specs,
                   out_shape=out_shape, input_output_aliases=aliases,
                   scratch_shapes=[pltpu.VMEM(acc_shape, F32)] if nk > 1 else [],
                   compiler_params=_params("parallel", "parallel", "arbitrary"))(*operands)


def _store(dtype):
    def epilogue(acc, extras, outs):
        outs[0][...] = acc.astype(dtype)
    return epilogue


def _residual_epilogue(acc, extras, outs):
    x_ref, gate_ref = extras
    outs[0][...] = acc
    outs[1][...] = x_ref[...] + gate_ref[...] * acc


def _relu2_epilogue(acc, extras, outs):
    r = jnp.maximum(acc, 0.0)
    outs[0][...] = r.astype(outs[0].dtype)
    outs[1][...] = (r * r).astype(outs[1].dtype)


def _relu2_bwd_epilogue(acc, extras, outs):
    outs[0][...] = (acc * (2.0 * extras[0][...].astype(F32))).astype(outs[0].dtype)


def _no_extra_specs(tm, tn):
    return []


def _mm_nn(name, a, b, n_total, b_split, out_shape, epilogue, extras=(), extra_specs=_no_extra_specs, tm=MM_TM, tn=MM_TN, tk=MM_TK,
           deps=(), pieces=None, carry=()):
    m, kdim = a.shape
    tm, tk = _tile(tm, m), _tile(tk, kdim)
    n_blocks = None
    if b_split:
        piece = b.shape[2]
        tn = _tile(tn, piece)
        per = piece // tn
        if pieces is None:
            b_spec = pl.BlockSpec((None, tk, tn), lambda i, j, k: (j // per, k, j % per))
            out_spec = pl.BlockSpec((tm, tn), lambda i, j, k: (i, j))
        else:
            first, count = pieces
            n_blocks = count * per

            def which(j):
                return (first() + j // per) % N_DEV

            b_spec = pl.BlockSpec((None, tk, tn), lambda i, j, k: (which(j), k, j % per))
            out_spec = pl.BlockSpec((tm, tn), lambda i, j, k: (i, which(j) * per + j % per))
    else:
        tn = _tile(tn, n_total)
        b_spec = pl.BlockSpec((tk, tn), lambda i, j, k: (k, j))
        out_spec = pl.BlockSpec((tm, tn), lambda i, j, k: (i, j))
    if n_blocks is None:
        n_blocks = n_total // tn
    in_specs = [pl.BlockSpec((tm, tk), lambda i, j, k: (i, k)), b_spec] + list(extra_specs(tm, tn))
    return _matmul(name, "nn", (a, b) + tuple(extras), in_specs, out_shape, [out_spec] * len(out_shape),
                   (m // tm, n_blocks, kdim // tk), (tm, tn), epilogue, deps, carry)


def _mm_nt(name, a, b, n_total, b_split, out_shape, epilogue, extras=(), extra_specs=_no_extra_specs, tm=MM_TM, tn=MM_TN, tk=MM_TK,
           deps=()):
    m, kdim = a.shape
    tm, tn = _tile(tm, m), _tile(tn, n_total)
    if b_split:
        piece = b.shape[2]
        tk = _tile(tk, piece)
        per = piece // tk
        b_spec = pl.BlockSpec((None, tn, tk), lambda i, j, k: (k // per, j, k % per))
    else:
        tk = _tile(tk, kdim)
        b_spec = pl.BlockSpec((tn, tk), lambda i, j, k: (j, k))
    in_specs = [pl.BlockSpec((tm, tk), lambda i, j, k: (i, k)), b_spec] + list(extra_specs(tm, tn))
    out_specs = [pl.BlockSpec((tm, tn), lambda i, j, k: (i, j)) for _ in out_shape]
    return _matmul(name, "nt", (a, b) + tuple(extras), in_specs, out_shape, out_specs,
                   (m // tm, n_total // tn, kdim // tk), (tm, tn), epilogue, deps)


def _add_pair_epilogue(acc, extras, outs):
    outs[0][...] = (acc + extras[0][...].astype(F32)).astype(outs[0].dtype)


def _mm_tn_half(name, a, b, col_pieces, near, pair=None, tm=MM_TM, tn=MM_TN, tk=MM_TK, deps=()):
    kdim, m = a.shape
    n_total = b.shape[1]
    tk = _tile(tk, kdim)

    def core():
        c = lax.axis_index("c")
        return c if near else 1 - c

    if col_pieces:
        piece = n_total // N_DEV
        tm, tn = _tile(tm, m), _tile(tn, piece)
        per = piece // tn
        grid = (m // tm, N_CHIP * per, kdim // tk)
        a_spec = pl.BlockSpec((tk, tm), lambda i, j, k: (k, i))
        b_spec = pl.BlockSpec((tk, tn), lambda i, j, k: (k, (2 * (j // per) + core()) * per + j % per))
        out_spec = pl.BlockSpec((None, tm, tn), lambda i, j, k: (j // per, i, j % per))
        out_shape = [jax.ShapeDtypeStruct((N_CHIP, m, piece), BF16)]
    else:
        piece = m // N_DEV
        tm, tn = _tile(tm, piece), _tile(tn, n_total)
        per = piece // tm
        grid = (N_CHIP * per, n_total // tn, kdim // tk)
        a_spec = pl.BlockSpec((tk, tm), lambda i, j, k: (k, (2 * (i // per) + core()) * per + i % per))
        b_spec = pl.BlockSpec((tk, tn), lambda i, j, k: (k, j))
        out_spec = pl.BlockSpec((None, tm, tn), lambda i, j, k: (i // per, i % per, j))
        out_shape = [jax.ShapeDtypeStruct((N_CHIP, piece, n_total), BF16)]
    operands, in_specs, epilogue = (a, b), [a_spec, b_spec], _store(BF16)
    if pair is not None:
        operands, in_specs, epilogue = (a, b, pair), [a_spec, b_spec, out_spec], _add_pair_epilogue
    return _matmul(name, "tn", operands, in_specs, out_shape, [out_spec], grid, (tm, tn), epilogue, deps)[0]


def _rms(xv):
    return lax.rsqrt(jnp.mean(xv * xv, axis=-1, keepdims=True) + EPS)


def _colsum(v):
    return jnp.sum(v, axis=0, keepdims=True)


def _norm_mod(name, x, g, scale, shift, tr=256, deps=()):
    s, d = x.shape
    tr = _tile(tr, s)

    def body(x_ref, g_ref, sc_ref, sh_ref, h_ref):
        xv = x_ref[...]
        h = (xv * _rms(xv)) * g_ref[...]
        h_ref[...] = (h * (1.0 + sc_ref[...]) + sh_ref[...]).astype(h_ref.dtype)

    row = pl.BlockSpec((tr, d), lambda i: (i, 0))
    vec = pl.BlockSpec((1, d), lambda i: (0, 0))
    return _pallas(body, deps, name=name, grid=(s // tr,), in_specs=[row, vec, vec, vec], out_specs=row,
                   out_shape=jax.ShapeDtypeStruct((s, d), BF16), compiler_params=_params("parallel"))(x, g, scale, shift)


def _loss_head(x3, target, gf, gate2, mlp, tr=128):
    s, d = x3.shape
    tr = _tile(tr, s)

    def body(x_ref, t_ref, gf_ref, gate_ref, mlp_ref, dx_ref, dbr_ref, dgf_ref, dgate_ref, loss_ref):
        @pl.when(pl.program_id(0) == 0)
        def _():
            dgf_ref[...] = jnp.zeros_like(dgf_ref)
            dgate_ref[...] = jnp.zeros_like(dgate_ref)
            loss_ref[...] = jnp.zeros_like(loss_ref)

        xv = x_ref[...]
        r = _rms(xv)
        xn = xv * r
        gfv = gf_ref[...]
        err = xn * gfv - t_ref[...]
        loss_ref[...] += 0.5 * _colsum(jnp.mean(err * err, axis=-1, keepdims=True))
        dy = err * (1.0 / d)
        dgf_ref[...] += _colsum(dy * xn)
        dxn = dy * gfv
        dx = r * (dxn - xn * jnp.mean(dxn * xn, axis=-1, keepdims=True))
        dx_ref[...] = dx
        dbr_ref[...] = (dx * gate_ref[...]).astype(dbr_ref.dtype)
        dgate_ref[...] += _colsum(dx * mlp_ref[...])

    row = pl.BlockSpec((tr, d), lambda i: (i, 0))
    vec = pl.BlockSpec((1, d), lambda i: (0, 0))
    return _pallas(
        body, name="loss_head", grid=(s // tr,), in_specs=[row, row, vec, vec, row],
        out_specs=[row, row, vec, vec, pl.BlockSpec((1, 128), lambda i: (0, 0))],
        out_shape=[jax.ShapeDtypeStruct((s, d), F32), jax.ShapeDtypeStruct((s, d), BF16),
                   jax.ShapeDtypeStruct((1, d), F32), jax.ShapeDtypeStruct((1, d), F32),
                   jax.ShapeDtypeStruct((1, 128), F32)],
        compiler_params=_params("arbitrary"))(x3, target, gf, gate2, mlp)


def _norm_mod_bwd(name, dh, xin, g, scale, dx_up, branch=None, gate=None, tr=128, deps=()):
    s, d = xin.shape
    tr = _tile(tr, s)
    with_gate = branch is not None

    def body(*refs):
        dh_ref, x_ref, g_ref, sc_ref, up_ref = refs[:5]
        if with_gate:
            br_ref, gate_ref = refs[5:7]
            dx_ref, dsh_ref, dsc_ref, dg_ref, dbr_ref, dgate_ref = refs[7:]
            sums = (dsh_ref, dsc_ref, dg_ref, dgate_ref)
        else:
            dx_ref, dsh_ref, dsc_ref, dg_ref = refs[5:]
            sums = (dsh_ref, dsc_ref, dg_ref)

        @pl.when(pl.program_id(0) == 0)
        def _():
            for ref in sums:
                ref[...] = jnp.zeros_like(ref)

        xv, dhv, gv = x_ref[...], dh_ref[...], g_ref[...]
        r = _rms(xv)
        xn = xv * r
        one_sc = 1.0 + sc_ref[...]
        dsh_ref[...] += _colsum(dhv)
        dsc_ref[...] += _colsum(dhv * (xn * gv))
        dg_ref[...] += _colsum(dhv * one_sc * xn)
        dxn = dhv * one_sc * gv
        dx = up_ref[...] + r * (dxn - xn * jnp.mean(dxn * xn, axis=-1, keepdims=True))
        dx_ref[...] = dx
        if with_gate:
            dbr_ref[...] = (dx * gate_ref[...]).astype(dbr_ref.dtype)
            dgate_ref[...] += _colsum(dx * br_ref[...])

    row = pl.BlockSpec((tr, d), lambda i: (i, 0))
    vec = pl.BlockSpec((1, d), lambda i: (0, 0))
    vshape = jax.ShapeDtypeStruct((1, d), F32)
    operands = [dh, xin, g, scale, dx_up]
    in_specs = [row, row, vec, vec, row]
    out_shape = [jax.ShapeDtypeStruct((s, d), F32), vshape, vshape, vshape]
    out_specs = [row, vec, vec, vec]
    if with_gate:
        operands += [branch, gate]
        in_specs += [row, vec]
        out_shape += [jax.ShapeDtypeStruct((s, d), BF16), vshape]
        out_specs += [row, vec]
    return _pallas(body, deps, name=name, grid=(s // tr,), in_specs=in_specs, out_specs=out_specs, out_shape=out_shape,
                   compiler_params=_params("arbitrary"))(*operands)


def _window_count(c0, rows, half, s):
    t = c0 + lax.broadcasted_iota(jnp.int32, (rows, 1), 0)
    return (jnp.minimum(t + half, s) - jnp.maximum(t - half, 0)).astype(F32)


def _zero_pads(pad, s):
    zeros = jnp.zeros((PAD_ROWS, pad.shape[1]), pad.dtype)
    pad[0:PAD_ROWS, :] = zeros
    pad[PAD_ROWS + s:PAD_ROWS + s + PAD_ROWS, :] = zeros


def _pool_fwd(proj, s, gd, cb, ch):
    nsub = gd // cb

    def body(v_ref, o_ref, pad):
        g = pl.program_id(0)
        _zero_pads(pad, s)
        pad[PAD_ROWS:PAD_ROWS + s, :] = v_ref[...]
        for gi, window in enumerate(POOL_WINDOWS):
            half = window // 2

            @pl.when(g == gi)
            def _(half=half):
                for c0 in range(0, s, ch):
                    base = PAD_ROWS + c0
                    acc = pad[base - half:base - half + ch, :]
                    for j in range(-half + 1, half):
                        acc = acc + pad[base + j:base + j + ch, :]
                    out = acc / _window_count(c0, ch, half, s) - v_ref[c0:c0 + ch, :]
                    o_ref[c0:c0 + ch, :] = out.astype(o_ref.dtype)

    spec = pl.BlockSpec((s, cb), lambda g, j: (0, g * nsub + j))
    return _pallas(body, name="pool_fwd", grid=(N_POOL_GROUPS, nsub), in_specs=[spec], out_specs=spec,
                   out_shape=jax.ShapeDtypeStruct((s, N_POOL_GROUPS * gd), BF16),
                   scratch_shapes=[pltpu.VMEM((s + 2 * PAD_ROWS, cb), F32)],
                   compiler_params=_params("parallel", "parallel"))(proj)


def _pool_bwd(dpooled, dproj, s, gd, cb, ch):
    nsub = gd // cb

    def body(dp_ref, dproj_in, o_ref, pad):
        del dproj_in
        g = pl.program_id(0)
        _zero_pads(pad, s)
        for gi, window in enumerate(POOL_WINDOWS):
            half = window // 2

            @pl.when(g == gi)
            def _(half=half):
                for c0 in range(0, s, ch):
                    pad[PAD_ROWS + c0:PAD_ROWS + c0 + ch, :] = dp_ref[c0:c0 + ch, :] / _window_count(c0, ch, half, s)
                for c0 in range(0, s, ch):
                    base = PAD_ROWS + c0
                    acc = pad[base - half + 1:base - half + 1 + ch, :]
                    for j in range(-half + 2, half + 1):
                        acc = acc + pad[base + j:base + j + ch, :]
                    o_ref[c0:c0 + ch, :] = (acc - dp_ref[c0:c0 + ch, :]).astype(o_ref.dtype)

    spec = pl.BlockSpec((s, cb), lambda g, j: (0, g * nsub + j))
    return _pallas(body, name="pool_bwd", grid=(N_POOL_GROUPS, nsub),
                   in_specs=[spec, pl.BlockSpec(memory_space=pl.ANY)], out_specs=spec,
                   out_shape=jax.ShapeDtypeStruct(dproj.shape, dproj.dtype), input_output_aliases={1: 0},
                   scratch_shapes=[pltpu.VMEM((s + 2 * PAD_ROWS, cb), F32)],
                   compiler_params=_params("parallel", "parallel"))(dpooled, dproj)


def _poolmix_fwd(pooled, wmix, pool_scale, gnorm_g, d_model, tm=512):
    s = pooled.shape[0]
    gd = wmix.shape[1]
    tm = _tile(tm, s)

    def body(p_ref, w_ref, ps_ref, g_ref, apre_ref, mixed_ref):
        a_pre = jnp.dot(p_ref[...], w_ref[...], preferred_element_type=F32)
        apre_ref[...] = a_pre
        a_out = a_pre * ps_ref[...]
        mixed_ref[...] = ((a_out * _rms(a_out)) * g_ref[...]).astype(mixed_ref.dtype)

    blk = pl.BlockSpec((tm, gd), lambda g, i: (i, g))
    vec = pl.BlockSpec((1, gd), lambda g, i: (0, g))
    return _pallas(body, name="poolmix_fwd", grid=(N_POOL_GROUPS, s // tm),
                   in_specs=[blk, pl.BlockSpec((None, gd, gd), lambda g, i: (g, 0, 0)), vec, vec],
                   out_specs=[blk, blk],
                   out_shape=[jax.ShapeDtypeStruct((s, N_POOL_GROUPS * gd), F32), jax.ShapeDtypeStruct((s, d_model), BF16)],
                   compiler_params=_params("parallel", "parallel"))(pooled, wmix, pool_scale, gnorm_g)


def _poolmix_bwd(dmixed, a_pre, wmix, pool_scale, gnorm_g, tm=512):
    s = a_pre.shape[0]
    gd = wmix.shape[1]
    tm = _tile(tm, s)

    def body(dm_ref, apre_ref, w_ref, ps_ref, g_ref, dapre_ref, dpooled_ref, dps_ref, dg_ref):
        @pl.when(pl.program_id(1) == 0)
        def _():
            dps_ref[...] = jnp.zeros_like(dps_ref)
            dg_ref[...] = jnp.zeros_like(dg_ref)

        a_pre, dm, ps = apre_ref[...], dm_ref[...], ps_ref[...]
        a_out = a_pre * ps
        r = _rms(a_out)
        n = a_out * r
        dg_ref[...] += _colsum(dm * n)
        dn = dm * g_ref[...]
        da_out = r * (dn - n * jnp.mean(dn * n, axis=-1, keepdims=True))
        dps_ref[...] += _colsum(da_out * a_pre)
        da_pre = (da_out * ps).astype(BF16)
        dapre_ref[...] = da_pre
        dpooled_ref[...] = lax.dot_general(da_pre, w_ref[...], _DOT_DIMS["nt"], preferred_element_type=F32)

    blk = pl.BlockSpec((tm, gd), lambda g, i: (i, g))
    vec = pl.BlockSpec((1, gd), lambda g, i: (0, g))
    width = N_POOL_GROUPS * gd
    return _pallas(body, name="poolmix_bwd", grid=(N_POOL_GROUPS, s // tm),
                   in_specs=[blk, blk, pl.BlockSpec((None, gd, gd), lambda g, i: (g, 0, 0)), vec, vec],
                   out_specs=[blk, blk, vec, vec],
                   out_shape=[jax.ShapeDtypeStruct((s, width), BF16), jax.ShapeDtypeStruct((s, width), F32),
                              jax.ShapeDtypeStruct((1, width), F32), jax.ShapeDtypeStruct((1, width), F32)],
                   compiler_params=_params("parallel", "arbitrary"))(dmixed, a_pre, wmix, pool_scale, gnorm_g)


def _poolmix_wgrad(pooled, da_pre, gd, tk=1024):
    s = pooled.shape[0]
    tk = _tile(tk, s)
    nk = s // tk

    def body(p_ref, d_ref, o_ref, acc):
        k = pl.program_id(1)
        part = lax.dot_general(p_ref[...], d_ref[...], _DOT_DIMS["tn"], preferred_element_type=F32)

        @pl.when(k == 0)
        def _():
            acc[...] = part

        @pl.when(k > 0)
        def _():
            acc[...] += part

        @pl.when(k == nk - 1)
        def _():
            o_ref[...] = acc[...].astype(o_ref.dtype)

    blk = pl.BlockSpec((tk, gd), lambda g, k: (k, g))
    return _pallas(body, name="poolmix_wgrad", grid=(N_POOL_GROUPS, nk), in_specs=[blk, blk],
                   out_specs=pl.BlockSpec((None, gd, gd), lambda g, k: (g, 0, 0)),
                   out_shape=jax.ShapeDtypeStruct((N_POOL_GROUPS, gd, gd), BF16),
                   scratch_shapes=[pltpu.VMEM((gd, gd), F32)],
                   compiler_params=_params("parallel", "arbitrary"))(pooled, da_pre)


def _head_mean(v):
    parts = []
    for q in range(v.shape[1] // CONV_HEAD_DIM):
        m = jnp.mean(v[:, q * CONV_HEAD_DIM:(q + 1) * CONV_HEAD_DIM], axis=-1, keepdims=True)
        parts.append(jnp.broadcast_to(m, (v.shape[0], CONV_HEAD_DIM)))
    return parts[0] if len(parts) == 1 else jnp.concatenate(parts, axis=1)


def _conv_fwd(proj, mixed, conv_w, conv_b, gnorm_g, s, width, cb, ch):
    nblk = width // cb

    def body(b_ref, c_ref, u_ref, w_ref, cb_ref, g_ref, mixed_in, o_ref, pad):
        del mixed_in
        _zero_pads(pad, s)
        pad[PAD_ROWS:PAD_ROWS + s, :] = c_ref[...] * u_ref[...]
        w = w_ref[...]
        for c0 in range(0, s, ch):
            base = PAD_ROWS + c0
            conv = (w[0:1] * pad[base - 1:base - 1 + ch, :] + w[1:2] * pad[base:base + ch, :]
                    + w[2:3] * pad[base + 1:base + 1 + ch, :] + cb_ref[...])
            bo = b_ref[c0:c0 + ch, :] * conv
            n = bo * lax.rsqrt(_head_mean(bo * bo) + EPS)
            o_ref[c0:c0 + ch, :] = (n * g_ref[...]).astype(o_ref.dtype)

    def part(p):
        return pl.BlockSpec((s, cb), lambda j: (0, p * nblk + j))

    vec = pl.BlockSpec((1, cb), lambda j: (0, j))
    return _pallas(body, name="conv_fwd", grid=(nblk,),
                   in_specs=[part(1), part(2), part(3), pl.BlockSpec((3, cb), lambda j: (0, j)), vec, vec,
                             pl.BlockSpec(memory_space=pl.ANY)],
                   out_specs=part(1), out_shape=jax.ShapeDtypeStruct(mixed.shape, mixed.dtype),
                   input_output_aliases={6: 0},
                   scratch_shapes=[pltpu.VMEM((s + 2 * PAD_ROWS, cb), F32)],
                   compiler_params=_params("parallel"))(proj, proj, proj, conv_w, conv_b, gnorm_g, mixed)


def _conv_bwd(dmixed, proj, conv_w, conv_b, gnorm_g, s, width, cb, ch, deps=()):
    nblk = width // cb

    def body(dm_ref, b_ref, c_ref, u_ref, w_ref, cb_ref, g_ref, dproj_ref, dw_ref, dcb_ref, dg_ref,
             pad_cu, pad_dconv, db_buf, dc_buf, du_buf, sems):
        j = pl.program_id(0)
        _zero_pads(pad_cu, s)
        _zero_pads(pad_dconv, s)
        pad_cu[PAD_ROWS:PAD_ROWS + s, :] = c_ref[...] * u_ref[...]
        w, gv = w_ref[...], g_ref[...]
        zero = jnp.zeros((1, cb), F32)
        dw0, dw1, dw2, dcb, dg = zero, zero, zero, zero, zero
        for c0 in range(0, s, ch):
            base = PAD_ROWS + c0
            cu_prev, cu_here, cu_next = (pad_cu[base - 1:base - 1 + ch, :], pad_cu[base:base + ch, :],
                                         pad_cu[base + 1:base + 1 + ch, :])
            conv = w[0:1] * cu_prev + w[1:2] * cu_here + w[2:3] * cu_next + cb_ref[...]
            bg = b_ref[c0:c0 + ch, :]
            bo = bg * conv
            r = lax.rsqrt(_head_mean(bo * bo) + EPS)
            n = bo * r
            dm = dm_ref[c0:c0 + ch, :]
            dg = dg + _colsum(dm * n)
            dn = dm * gv
            dbo = r * (dn - n * _head_mean(dn * n))
            db_buf[c0:c0 + ch, :] = (dbo * conv).astype(BF16)
            dconv = dbo * bg
            pad_dconv[base:base + ch, :] = dconv
            dcb = dcb + _colsum(dconv)
            dw0 = dw0 + _colsum(dconv * cu_prev)
            dw1 = dw1 + _colsum(dconv * cu_here)
            dw2 = dw2 + _colsum(dconv * cu_next)
        dw_ref[0:1, :] = dw0
        dw_ref[1:2, :] = dw1
        dw_ref[2:3, :] = dw2
        dcb_ref[...] = dcb
        dg_ref[...] = dg
        for c0 in range(0, s, ch):
            base = PAD_ROWS + c0
            dcu = (w[0:1] * pad_dconv[base + 1:base + 1 + ch, :] + w[1:2] * pad_dconv[base:base + ch, :]
                   + w[2:3] * pad_dconv[base - 1:base - 1 + ch, :])
            dc_buf[c0:c0 + ch, :] = (dcu * u_ref[c0:c0 + ch, :]).astype(BF16)
            du_buf[c0:c0 + ch, :] = (dcu * c_ref[c0:c0 + ch, :]).astype(BF16)
        copies = []
        for p, buf in enumerate((db_buf, dc_buf, du_buf)):
            col = pl.multiple_of((p + 1) * width + j * cb, CONV_HEAD_DIM)
            copies.append(pltpu.make_async_copy(buf, dproj_ref.at[:, pl.ds(col, cb)], sems.at[p]))
            copies[-1].start()
        for cp in copies:
            cp.wait()

    def part(p):
        return pl.BlockSpec((s, cb), lambda j: (0, p * nblk + j))

    vec = pl.BlockSpec((1, cb), lambda j: (0, j))
    w_spec = pl.BlockSpec((3, cb), lambda j: (0, j))
    return _pallas(body, deps, name="conv_bwd", grid=(nblk,),
                   in_specs=[part(1), part(1), part(2), part(3), w_spec, vec, vec],
                   out_specs=[pl.BlockSpec(memory_space=pl.ANY), w_spec, vec, vec],
                   out_shape=[jax.ShapeDtypeStruct((s, 4 * width), BF16), jax.ShapeDtypeStruct((3, width), F32),
                              jax.ShapeDtypeStruct((1, width), F32), jax.ShapeDtypeStruct((1, width), F32)],
                   scratch_shapes=[pltpu.VMEM((s + 2 * PAD_ROWS, cb), F32), pltpu.VMEM((s + 2 * PAD_ROWS, cb), F32),
                                   pltpu.VMEM((s, cb), BF16), pltpu.VMEM((s, cb), BF16), pltpu.VMEM((s, cb), BF16),
                                   pltpu.SemaphoreType.DMA((3,))],
                   compiler_params=_params("arbitrary"))(dmixed, proj, proj, proj, conv_w, conv_b, gnorm_g)


def _adamw(w, g, m, v):
    m = ADAM_B1 * m + (1.0 - ADAM_B1) * g
    v = ADAM_B2 * v + (1.0 - ADAM_B2) * (g * g)
    m_hat = m / (1.0 - ADAM_B1 ** ADAM_STEP)
    v_hat = v / (1.0 - ADAM_B2 ** ADAM_STEP)
    delta = -ADAM_LR * (m_hat / (jnp.sqrt(v_hat) + ADAM_EPS) + ADAM_WD * w)
    return delta, m, v


def _ada_fwd(c_rows, w, b, tn=512):
    rows, d = c_rows.shape
    n = w.shape[1]
    tn = _tile(tn, n)

    def body(c_ref, w_ref, b_ref, o_ref):
        cv = c_ref[...]
        act = (cv * jax.nn.sigmoid(cv)).astype(BF16)
        o_ref[...] = jnp.dot(act, w_ref[...].astype(BF16), preferred_element_type=F32) + b_ref[...]

    return _pallas(body, name="ada_fwd", grid=(n // tn,),
                   in_specs=[pl.BlockSpec((rows, d), lambda j: (0, 0)), pl.BlockSpec((d, tn), lambda j: (0, j)),
                             pl.BlockSpec((1, tn), lambda j: (0, j))],
                   out_specs=pl.BlockSpec((rows, tn), lambda j: (0, j)),
                   out_shape=jax.ShapeDtypeStruct((rows, n), F32), compiler_params=_params("parallel"))(c_rows, w, b)


def _ada_bwd_adam(c_cols, dmod, w, m, v, tr=512, tn=1024):
    d, rows = c_cols.shape
    n = w.shape[1]
    tr, tn = _tile(tr, d), _tile(tn, n)

    def body(c_ref, dm_ref, w_ref, m_ref, v_ref, g_ref, dl_ref, nm_ref, nv_ref):
        cv = c_ref[...]
        act = (cv * jax.nn.sigmoid(cv)).astype(BF16)
        g = jnp.dot(act, dm_ref[...].astype(BF16), preferred_element_type=F32)
        g_ref[...] = g
        dl_ref[...], nm_ref[...], nv_ref[...] = _adamw(w_ref[...], g, m_ref[...], v_ref[...])

    blk = pl.BlockSpec((tr, tn), lambda i, j: (i, j))
    shape = jax.ShapeDtypeStruct((d, n), F32)
    return _pallas(body, name="ada_bwd_adam", grid=(d // tr, n // tn),
                   in_specs=[pl.BlockSpec((tr, rows), lambda i, j: (i, 0)), pl.BlockSpec((rows, tn), lambda i, j: (0, j)),
                             blk, blk, blk],
                   out_specs=[blk] * 4, out_shape=[shape] * 4,
                   compiler_params=_params("parallel", "parallel"))(c_cols, dmod, w, m, v)


def _reduce_adam(name, pieces, w, m, v, tr=256, tc=1024):
    r, c = w.shape
    tr, tc = _tile(tr, r), _tile(tc, c)

    def body(p_ref, w_ref, m_ref, v_ref, g_ref, dl_ref, nm_ref, nv_ref):
        g = p_ref[0].astype(F32)
        for j in range(1, N_DEV):
            g = g + p_ref[j].astype(F32)
        g_ref[...] = g
        dl_ref[...], nm_ref[...], nv_ref[...] = _adamw(w_ref[...], g, m_ref[...], v_ref[...])

    blk = pl.BlockSpec((tr, tc), lambda i, j: (i, j))
    shape = jax.ShapeDtypeStruct((r, c), F32)
    return _pallas(body, name=name, grid=(r // tr, c // tc),
                   in_specs=[pl.BlockSpec((N_DEV, tr, tc), lambda i, j: (0, i, j)), blk, blk, blk],
                   out_specs=[blk] * 4, out_shape=[shape] * 4,
                   compiler_params=_params("parallel", "parallel"))(pieces, w, m, v)


def _reduce_adam_chips(name, sums, land, w, m, v, tr=256, tc=1024):
    r, c = w.shape
    tr, tc = _tile(tr, r), _tile(tc, c)

    def body(s_ref, l_ref, w_ref, m_ref, v_ref, g_ref, dl_ref, nm_ref, nv_ref):
        g = s_ref[...].astype(F32)
        for k in range(3):
            g = g + l_ref[k].astype(F32)
        g_ref[...] = g
        dl_ref[...], nm_ref[...], nv_ref[...] = _adamw(w_ref[...], g, m_ref[...], v_ref[...])

    blk = pl.BlockSpec((tr, tc), lambda i, j: (i, j))
    shape = jax.ShapeDtypeStruct((r, c), F32)
    mine = pl.BlockSpec((None, tr, tc), lambda i, j: (2 * lax.axis_index("x") + lax.axis_index("y"), i, j))
    return _pallas(body, name=name, grid=(r // tr, c // tc),
                   in_specs=[mine, pl.BlockSpec((3, tr, tc), lambda i, j: (0, i, j)), blk, blk, blk],
                   out_specs=[blk] * 4, out_shape=[shape] * 4,
                   compiler_params=_params("parallel", "parallel"))(sums, land, w, m, v)


def _sum_devices(parts):
    n = parts.shape[1]

    def body(p_ref, o_ref):
        acc = p_ref[0:1, :]
        for j in range(1, N_DEV):
            acc = acc + p_ref[j:j + 1, :]
        o_ref[...] = acc

    return _pallas(body, name="sum_devices", out_shape=jax.ShapeDtypeStruct((1, n), F32),
                   compiler_params=pltpu.CompilerParams(vmem_limit_bytes=VMEM_LIMIT_BYTES))(parts)


def _adam_small(name, g, w, m, v):
    def body(g_ref, w_ref, m_ref, v_ref, dl_ref, nm_ref, nv_ref):
        dl_ref[...], nm_ref[...], nv_ref[...] = _adamw(w_ref[...], g_ref[...], m_ref[...], v_ref[...])

    shape = jax.ShapeDtypeStruct(w.shape, F32)
    return _pallas(body, name=name, out_shape=[shape] * 3,
                   compiler_params=pltpu.CompilerParams(vmem_limit_bytes=VMEM_LIMIT_BYTES))(g, w, m, v)


def kernel(x, c, w_ada, b_ada, norm1_g, w_in, pool_mix_w, pool_scale, conv_w, conv_b, gnorm_pool_g, gnorm_conv_g, w_out, norm2_g, w_mlp_in, w_mlp_out, final_g, loss_target, m_w_ada, m_b_ada, m_norm1_g, m_w_in, m_pool_mix_w, m_pool_scale, m_conv_w, m_conv_b, m_gnorm_pool_g, m_gnorm_conv_g, m_w_out, m_norm2_g, m_w_mlp_in, m_w_mlp_out, m_final_g, v_w_ada, v_b_ada, v_norm1_g, v_w_in, v_pool_mix_w, v_pool_scale, v_conv_w, v_conv_b, v_gnorm_pool_g, v_gnorm_conv_g, v_w_out, v_norm2_g, v_w_mlp_in, v_w_mlp_out, v_final_g):
    s, d = x.shape[1], x.shape[2]
    width = d // 2
    gd = width // N_POOL_GROUPS
    d_ff = w_mlp_in.shape[2] * N_DEV
    n_proj = w_in.shape[2] * N_DEV
    ada_cols = w_ada.shape[2]
    conv_cols = conv_w.shape[2]
    assert n_proj == 4 * width and ada_cols * N_DEV == N_MOD * d and d_ff % N_DEV == 0
    assert width % CONV_HEAD_DIM == 0 and s % 8 == 0
    seq_chunk = _tile(512, s)
    pool_cb = _tile(256, gd)
    conv_cb = CONV_HEAD_DIM

    me = 4 * lax.axis_index("x") + 2 * lax.axis_index("y") + lax.axis_index("c")
    x2d, target = x[0], loss_target[0]

    wmix_all, conv_w_all, c_all = _exchange(
        "gather_small_weights", [pool_mix_w[0].astype(BF16), conv_w[0], c], ["gather"] * 3)
    wmix_full = jnp.transpose(wmix_all, (1, 0, 2, 3)).reshape(N_POOL_GROUPS, gd, gd)
    conv_w_full = jnp.transpose(conv_w_all, (1, 0, 2)).reshape(3, width)
    c_rows = jnp.concatenate([c_all.reshape(N_DEV, d), jnp.zeros((N_DEV, d), F32)], axis=0)

    b_mine = lax.dynamic_slice(b_ada, (0, me * ada_cols), (1, ada_cols))
    mod_part = _ada_fwd(c_rows, w_ada[0], b_mine)
    (mod_all,) = _exchange("scatter_mod", [mod_part[:N_DEV].reshape(N_DEV, 1, ada_cols)], ["a2a"])
    mod = mod_all.reshape(1, N_MOD * d)

    started = {}
    for wname, wgt in (("w_in", w_in), ("w_out", w_out), ("w_mlp_in", w_mlp_in), ("w_mlp_out", w_mlp_out)):
        land = _landing(wgt[0].astype(BF16), me)
        started[wname] = _gather_start("gather_" + wname + "_start", land, deps=(mod,))
    gather_tokens = tuple(st[4] for st in started.values())

    def gathered(wname, land, after, local_waited=False):
        relayed = _gather_relay("gather_" + wname + "_relay", started[wname], land, after)
        return _gather_wait("gather_" + wname + "_wait", started[wname], relayed, local_waited)

    def first_local():
        return 2 * (2 * lax.axis_index("x") + lax.axis_index("y"))

    def first_remote():
        return first_local() + 2
    shift1, scale1, gate1, shift2, scale2, gate2 = [mod[:, i * d:(i + 1) * d] for i in range(N_MOD)]

    h1 = _norm_mod("norm1_fwd", x2d, norm1_g, scale1, shift1, deps=gather_tokens)
    proj_shape = [jax.ShapeDtypeStruct((s, n_proj), F32)]
    w_in_local = _gather_wait_local("gather_w_in_local", started["w_in"], h1)
    (proj,) = _mm_nn("in_proj_local", h1, w_in_local, n_proj, True, proj_shape, _store(F32), pieces=(first_local, 2))
    w_in_all = gathered("w_in", w_in_local, proj, True)
    (proj,) = _mm_nn("in_proj", h1, w_in_all, n_proj, True, proj_shape, _store(F32), pieces=(first_remote, 6),
                     carry=(proj,))
    pooled = _pool_fwd(proj, s, gd, pool_cb, seq_chunk)
    a_pre, mixed = _poolmix_fwd(pooled, wmix_full, pool_scale, gnorm_pool_g, d)
    mixed = _conv_fwd(proj, mixed, conv_w_full, conv_b, gnorm_conv_g, s, width, conv_cb, seq_chunk)

    def residual_specs(tm, tn):
        return [pl.BlockSpec((tm, tn), lambda i, j, k: (i, j)), pl.BlockSpec((1, tn), lambda i, j, k: (0, j))]

    sd_f32 = jax.ShapeDtypeStruct((s, d), F32)
    w_out_full = gathered("w_out", started["w_out"][3], mixed).reshape(d, d)
    attn, x_mid = _mm_nn("out_proj", mixed, w_out_full, d, False, [sd_f32, sd_f32], _residual_epilogue,
                         extras=(x2d, gate1), extra_specs=residual_specs)
    h2 = _norm_mod("norm2_fwd", x_mid, norm2_g, scale2, shift2)
    sf_bf16 = jax.ShapeDtypeStruct((s, d_ff), BF16)
    w1_local = _gather_wait_local("gather_w_mlp_in_local", started["w_mlp_in"], h2)
    relu, hid = _mm_nn("mlp_in_local", h2, w1_local, d_ff, True, [sf_bf16, sf_bf16], _relu2_epilogue,
                       pieces=(first_local, 2))
    w1_all = gathered("w_mlp_in", w1_local, hid, True)
    relu, hid = _mm_nn("mlp_in", h2, w1_all, d_ff, True, [sf_bf16, sf_bf16], _relu2_epilogue,
                       pieces=(first_remote, 6), carry=(relu, hid))
    w2_full = gathered("w_mlp_out", started["w_mlp_out"][3], hid).reshape(d_ff, d)
    mlp, x_last = _mm_nn("mlp_out", hid, w2_full, d, False, [sd_f32, sd_f32], _residual_epilogue,
                         extras=(x_mid, gate2), extra_specs=residual_specs)

    dx_last, dmlp, d_final_g, dgate2, loss_row = _loss_head(x_last, target, final_g.reshape(1, d), gate2, mlp)

    def relu_specs(tm, tn):
        return [pl.BlockSpec((tm, tn), lambda i, j, k: (i, j))]

    def reduce_start(wname, a, b, col_pieces, deps=()):
        far = _mm_tn_half(wname + "_dw_far", a, b, col_pieces, near=False, deps=deps)
        return _pair_start("scatter_" + wname + "_pair_start", far)

    def reduce_chips(wname, a, b, col_pieces, pairs, after):
        pair = _pair_wait("scatter_" + wname + "_pair_wait", pairs, after)
        sums = _mm_tn_half(wname + "_dw_near", a, b, col_pieces, near=True, pair=pair)
        return _chip_start("scatter_" + wname + "_chip_start", sums)

    pairs_w2 = reduce_start("mlp_out", hid, dmlp, False)
    (dhpre,) = _mm_nt("mlp_out_dx", dmlp, w2_full, d_ff, False, [sf_bf16], _relu2_bwd_epilogue,
                      extras=(relu,), extra_specs=relu_specs, deps=(pairs_w2[4],))
    chips_w2 = reduce_chips("mlp_out", hid, dmlp, False, pairs_w2, dhpre)
    pairs_w1 = reduce_start("mlp_in", h2, dhpre, True, deps=(chips_w2[4],))
    (dh2,) = _mm_nt("mlp_in_dx", dhpre, w1_all, d, True, [sd_f32], _store(F32), tn=1024, deps=(pairs_w1[4],))
    chips_w1 = reduce_chips("mlp_in", h2, dhpre, True, pairs_w1, dh2)
    dx_mid, dshift2, dscale2, d_norm2_g, dattn, dgate1 = _norm_mod_bwd(
        "norm2_bwd", dh2, x_mid, norm2_g, scale2, dx_last, branch=attn, gate=gate1, deps=(chips_w1[4],))

    pairs_w_out = reduce_start("out_proj", mixed, dattn, False)
    (dmixed,) = _mm_nt("out_proj_dx", dattn, w_out_full, d, False, [sd_f32], _store(F32), deps=(pairs_w_out[4],))
    chips_w_out = reduce_chips("out_proj", mixed, dattn, False, pairs_w_out, dmixed)
    dproj, d_conv_w, d_conv_b, d_gnorm_conv = _conv_bwd(dmixed, proj, conv_w_full, conv_b, gnorm_conv_g,
                                                        s, width, conv_cb, seq_chunk, deps=(chips_w_out[4],))
    da_pre, dpooled, d_pool_scale, d_gnorm_pool = _poolmix_bwd(dmixed, a_pre, wmix_full, pool_scale, gnorm_pool_g)
    g_wmix = _poolmix_wgrad(pooled, da_pre, gd)
    dproj = _pool_bwd(dpooled, dproj, s, gd, pool_cb, seq_chunk)

    pairs_w_in = reduce_start("in_proj", h1, dproj, True)
    chips_w_in = reduce_chips("in_proj", h1, dproj, True, pairs_w_in, pairs_w_in[4])
    (dh1,) = _mm_nt("in_proj_dx", dproj, w_in_all, d, True, [sd_f32], _store(F32), tn=1024, deps=(chips_w_in[4],))
    grad_x, dshift1, dscale1, d_norm1_g = _norm_mod_bwd("norm1_bwd", dh1, x2d, norm1_g, scale1, dx_mid)

    sums, landed = _chip_wait("scatter_w_mlp_out_chip_wait", chips_w2, grad_x)
    out_w2 = _reduce_adam_chips("adam_w_mlp_out", sums, landed, w_mlp_out[0], m_w_mlp_out[0], v_w_mlp_out[0])
    sums, landed = _chip_wait("scatter_w_mlp_in_chip_wait", chips_w1, out_w2[0])
    out_w1 = _reduce_adam_chips("adam_w_mlp_in", sums, landed, w_mlp_in[0], m_w_mlp_in[0], v_w_mlp_in[0])
    sums, landed = _chip_wait("scatter_w_out_chip_wait", chips_w_out, out_w1[0])
    out_w_out = _reduce_adam_chips("adam_w_out", sums, landed, w_out[0], m_w_out[0], v_w_out[0])
    sums, landed = _chip_wait("scatter_w_in_chip_wait", chips_w_in, out_w_out[0])
    out_w_in = _reduce_adam_chips("adam_w_in", sums, landed, w_in[0], m_w_in[0], v_w_in[0])

    rows_mix = gd // N_DEV
    g_wmix_split = jnp.transpose(g_wmix.reshape(N_POOL_GROUPS, N_DEV, rows_mix, gd), (1, 0, 2, 3))
    g_wmix_split = g_wmix_split.reshape(N_DEV, N_POOL_GROUPS * rows_mix, gd)
    loss_pad = jnp.concatenate([loss_row[:, :1], jnp.zeros((1, 127), F32)], axis=1)
    dmod = jnp.concatenate([dshift1, dscale1, dgate1, dshift2, dscale2, dgate2], axis=1)
    small = jnp.concatenate([dmod, d_norm1_g, d_pool_scale, d_conv_b, d_gnorm_pool, d_gnorm_conv, d_norm2_g,
                             d_final_g, d_conv_w.reshape(1, 3 * width), loss_pad], axis=1)
    p_wmix, small_all = _exchange("exchange_small_grads", [g_wmix_split, small], ["a2a", "gather"],
                                  deps=(out_w_in[0],))
    mix_shape = (N_POOL_GROUPS * rows_mix, gd)
    out_wmix = _reduce_adam("adam_pool_mix", p_wmix, pool_mix_w.reshape(mix_shape), m_pool_mix_w.reshape(mix_shape),
                            v_pool_mix_w.reshape(mix_shape))
    out_wmix = [a.reshape(pool_mix_w.shape) for a in out_wmix]
    small_all = small_all.reshape(N_DEV, small.shape[1])
    small_sum = _sum_devices(small_all)

    n_rep = (N_MOD + 1) * d + 4 * width + 2 * d
    loss = small_sum[0, n_rep + 3 * width]
    rep_names_w = [b_ada, norm1_g, pool_scale, conv_b, gnorm_pool_g, gnorm_conv_g, norm2_g, final_g.reshape(1, d)]
    rep_names_m = [m_b_ada, m_norm1_g, m_pool_scale, m_conv_b, m_gnorm_pool_g, m_gnorm_conv_g, m_norm2_g,
                   m_final_g.reshape(1, d)]
    rep_names_v = [v_b_ada, v_norm1_g, v_pool_scale, v_conv_b, v_gnorm_pool_g, v_gnorm_conv_g, v_norm2_g,
                   v_final_g.reshape(1, d)]
    rep_grad = small_sum[:, :n_rep]
    rep_delta, rep_m, rep_v = _adam_small("adam_replicated", rep_grad, jnp.concatenate(rep_names_w, axis=1),
                                          jnp.concatenate(rep_names_m, axis=1), jnp.concatenate(rep_names_v, axis=1))

    def split_rep(vec):
        out, off = [], 0
        for wgt in rep_names_w:
            n = wgt.shape[1]
            out.append(vec[:, off:off + n])
            off += n
        out[-1] = out[-1].reshape(d)
        return out

    conv_grad_full = small_sum[:, n_rep:n_rep + 3 * width].reshape(3, width)
    g_conv_w = lax.dynamic_slice(conv_grad_full, (0, me * conv_cols), (3, conv_cols))
    g_conv_w8 = jnp.concatenate([g_conv_w, jnp.zeros((5, conv_cols), F32)], axis=0)

    def pad8(a):
        return jnp.concatenate([a[0], jnp.zeros((5, conv_cols), F32)], axis=0)

    conv_delta, conv_m, conv_v = _adam_small("adam_conv_w", g_conv_w8, pad8(conv_w), pad8(m_conv_w), pad8(v_conv_w))

    dmod_all = small_all[:, :N_MOD * d]
    dmod_mine = lax.dynamic_slice(dmod_all, (0, me * ada_cols), (N_DEV, ada_cols))
    dmod_rows = jnp.concatenate([dmod_mine, jnp.zeros((N_DEV, ada_cols), F32)], axis=0)
    out_ada = _ada_bwd_adam(jnp.transpose(c_rows), dmod_rows, w_ada[0], m_w_ada[0], v_w_ada[0])

    rep_all = [split_rep(rep_grad), split_rep(rep_delta), split_rep(rep_m), split_rep(rep_v)]
    conv_all = [g_conv_w[None], conv_delta[None, :3], conv_m[None, :3], conv_v[None, :3]]
    outs = [loss, grad_x[None]]
    for kind in range(4):
        b_ada_o, norm1_o, pool_scale_o, conv_b_o, gpool_o, gconv_o, norm2_o, final_o = rep_all[kind]
        outs += [out_ada[kind][None], b_ada_o, norm1_o, out_w_in[kind][None], out_wmix[kind], pool_scale_o,
                 conv_all[kind], conv_b_o, gpool_o, gconv_o, out_w_out[kind][None], norm2_o, out_w1[kind][None],
                 out_w2[kind][None], final_o]
    return tuple(outs)
```

```python
import jax
import jax.numpy as jnp
from jax import lax
from jax.experimental import pallas as pl
from jax.experimental.pallas import tpu as pltpu

F32 = jnp.float32
BF16 = jnp.bfloat16
MESH = pl.DeviceIdType.MESH

N_DEV = 8
N_MOD = 6
EPS = 1e-6
POOL_WINDOWS = (2, 4, 8, 16)
N_POOL_GROUPS = len(POOL_WINDOWS)
CONV_HEAD_DIM = 128
PAD_ROWS = 16

ADAM_LR = 0.001
ADAM_B1 = 0.9
ADAM_B2 = 0.999
ADAM_EPS = 1e-08
ADAM_WD = 0.01
ADAM_STEP = 10

VMEM_LIMIT_BYTES = 56 * 1024 * 1024
MM_TM, MM_TN, MM_TK = 1024, 512, 4096


def _pallas(body, deps=(), **kw):
    if not deps:
        return pl.pallas_call(body, **kw)
    n_in = len(kw["in_specs"])

    def with_deps(*refs):
        body(*refs[:n_in], *refs[n_in + len(deps):])

    kw["in_specs"] = list(kw["in_specs"]) + [pl.BlockSpec(memory_space=pl.ANY)] * len(deps)
    call = pl.pallas_call(with_deps, **kw)
    return lambda *operands: call(*operands, *deps)


def _params(*sem):
    return pltpu.CompilerParams(dimension_semantics=sem, vmem_limit_bytes=VMEM_LIMIT_BYTES)


def _tile(pref, dim):
    if dim <= pref:
        return dim
    for t in range(pref - pref % 128, 0, -128):
        if dim % t == 0:
            return t
    return dim


def _exchange(name, arrays, modes, deps=()):
    n = len(arrays)
    out_shape = []
    for a, mode in zip(arrays, modes):
        piece = a.shape if mode == "gather" else a.shape[1:]
        out_shape.append(jax.ShapeDtypeStruct((N_DEV,) + tuple(piece), a.dtype))

    def body(*refs):
        srcs, dsts = refs[:n], refs[n:2 * n]
        send_sems, recv_sems, local_sems = refs[2 * n:]
        x, y, c = lax.axis_index("x"), lax.axis_index("y"), lax.axis_index("c")
        me = 4 * x + 2 * y + c
        copies = []
        for i in range(n):
            gather = modes[i] == "gather"
            local = pltpu.make_async_copy(srcs[i] if gather else srcs[i].at[me], dsts[i].at[me], local_sems.at[i])
            local.start()
            copies.append(local)
            for k in range(1, N_DEV):
                kx, ky, kc = (k >> 2) & 1, (k >> 1) & 1, k & 1
                peer = (1 - x if kx else x, 1 - y if ky else y, 1 - c if kc else c)
                peer_idx = 4 * peer[0] + 2 * peer[1] + peer[2]
                remote = pltpu.make_async_remote_copy(
                    src_ref=srcs[i] if gather else srcs[i].at[peer_idx],
                    dst_ref=dsts[i].at[me],
                    send_sem=send_sems.at[i * (N_DEV - 1) + k - 1],
                    recv_sem=recv_sems.at[i * (N_DEV - 1) + k - 1],
                    device_id=peer, device_id_type=MESH)
                remote.start()
                copies.append(remote)
        for cp in copies:
            cp.wait()

    any_spec = pl.BlockSpec(memory_space=pl.ANY)
    return _pallas(
        body, deps, name=name, out_shape=out_shape,
        in_specs=[any_spec] * n, out_specs=[any_spec] * n,
        scratch_shapes=[pltpu.SemaphoreType.DMA((n * (N_DEV - 1),)),
                        pltpu.SemaphoreType.DMA((n * (N_DEV - 1),)),
                        pltpu.SemaphoreType.DMA((n,))],
    )(*arrays)


_HBM = pl.BlockSpec(memory_space=pltpu.HBM)
_SEM = pl.BlockSpec(memory_space=pltpu.SEMAPHORE)
_TOKEN = pl.BlockSpec(memory_space=pltpu.VMEM)
_EFFECT = pltpu.SideEffectType.DATAFLOW_SIDE_EFFECTING
N_CHIP = N_DEV // 2
_OTHER_CHIPS = (1, 2, 3)


def _place():
    x, y, c = lax.axis_index("x"), lax.axis_index("y"), lax.axis_index("c")
    return x, y, c, (x, y, 1 - c)


def _same_core_of(x, y, c, k):
    px = 1 - x if k & 2 else x
    py = 1 - y if k & 1 else y
    return (px, py, c), 2 * px + py


def _remote(src, dst, send_sem, recv_sem, device):
    return pltpu.make_async_remote_copy(src_ref=src, dst_ref=dst, send_sem=send_sem, recv_sem=recv_sem,
                                        device_id=device, device_id_type=MESH)


def _token_shape():
    return jax.ShapeDtypeStruct((8, 128), F32)


def _split_call(body, deps, name, operands, in_specs, out_shape, out_specs, aliases):
    return _pallas(body, deps, name=name, out_shape=out_shape, in_specs=in_specs, out_specs=out_specs,
                   input_output_aliases=aliases,
                   compiler_params=pltpu.CompilerParams(has_side_effects=_EFFECT))(*operands)


def _routes(x, y, c):
    first = (x + c - 2 * x * c, y + (1 - c) - 2 * y * (1 - c), c)
    second = (x + (1 - c) - 2 * x * (1 - c), y + c - 2 * y * c, c)
    return first, second, (1 - x, 1 - y, c)


def _index_of(device):
    return 4 * device[0] + 2 * device[1] + device[2]


def _gather_start(name, land, deps):
    def body(land_ref, send_sems, recv_first, recv_second, recv_d2d, land_thru, token):
        del land_thru
        x, y, c, sibling = _place()
        first, second, _ = _routes(x, y, c)
        mine = land_ref.at[4 * x + 2 * y + c]
        _remote(mine, mine, send_sems.at[0], recv_first.at[0], first).start()
        _remote(mine, mine, send_sems.at[1], recv_second.at[0], second).start()
        _remote(mine, mine, send_sems.at[2], recv_d2d.at[0], sibling).start()
        token[...] = jnp.zeros_like(token)

    one = pltpu.SemaphoreType.DMA((1,))
    return _split_call(
        body, deps, name, (pltpu.with_memory_space_constraint(land, pltpu.HBM),), (_HBM,),
        (pltpu.SemaphoreType.DMA((3,)), one, one, one, pltpu.HBM(land.shape, land.dtype), _token_shape()),
        (_SEM, _SEM, _SEM, _SEM, _HBM, _TOKEN), {0: 4})


def _gather_wait_local(name, started, after):
    recv_d2d, land = started[3], started[4]

    def body(land_ref, recv_d2d, after_ref, land_out):
        del after_ref, land_out
        x, y, c, sibling = _place()
        mine = land_ref.at[4 * x + 2 * y + c]
        _remote(mine, mine, recv_d2d.at[0], recv_d2d.at[0], sibling).wait_recv()

    return _split_call(
        body, (), name, (land, recv_d2d, after), (_HBM, _SEM, pl.BlockSpec(memory_space=pl.ANY)),
        (pltpu.HBM(land.shape, land.dtype),), (_HBM,), {0: 0})[0]


def _gather_hop(name, started, land, after):
    recv_first = started[1]

    def body(land_ref, recv_first, after_ref, send_hop, recv_hop, land_thru, token):
        del after_ref, land_thru
        x, y, c, _ = _place()
        first, second, _ = _routes(x, y, c)
        piece = land_ref.at[_index_of(first)]
        _remote(piece, piece, send_hop.at[0], recv_first.at[0], first).wait_recv()
        _remote(piece, piece, send_hop.at[0], recv_hop.at[0], second).start()
        token[...] = jnp.zeros_like(token)

    one = pltpu.SemaphoreType.DMA((1,))
    return _split_call(
        body, (), name, (land, recv_first, after), (_HBM, _SEM, pl.BlockSpec(memory_space=pl.ANY)),
        (one, one, pltpu.HBM(land.shape, land.dtype), _token_shape()), (_SEM, _SEM, _HBM, _TOKEN), {0: 2})


def _gather_relay(name, started, hopped, after):
    recv_second = started[2]
    _, recv_hop, land, _ = hopped

    def body(land_ref, recv_second, recv_hop, after_ref, send_fwd, recv_fwd, land_thru):
        del after_ref, land_thru
        x, y, c, sibling = _place()
        mine = land_ref.at[4 * x + 2 * y + c]
        _remote(mine, mine, send_fwd.at[0], recv_second.at[0], sibling).wait_recv()
        _remote(mine, mine, send_fwd.at[0], recv_hop.at[0], sibling).wait_recv()
        for i, device in enumerate(_routes(x, y, c)):
            piece = land_ref.at[_index_of(device)]
            _remote(piece, piece, send_fwd.at[i], recv_fwd.at[i], sibling).start()

    return _split_call(
        body, (), name, (land, recv_second, recv_hop, after), (_HBM, _SEM, _SEM, pl.BlockSpec(memory_space=pl.ANY)),
        (pltpu.SemaphoreType.DMA((3,)), pltpu.SemaphoreType.DMA((3,)), pltpu.HBM(land.shape, land.dtype)),
        (_SEM, _SEM, _HBM), {0: 2})


def _gather_wait(name, started, hopped, relayed, local_waited):
    send_sems, recv_d2d = started[0], started[3]
    send_hop = hopped[0]
    send_fwd, recv_fwd, land = relayed

    def body(land_ref, send_sems, recv_d2d, send_hop, send_fwd, recv_fwd, land_out):
        del land_out
        x, y, c, sibling = _place()
        mine = land_ref.at[4 * x + 2 * y + c]
        for i in range(3):
            _remote(mine, mine, send_sems.at[i], recv_d2d.at[0], sibling).wait_send()
        _remote(mine, mine, send_hop.at[0], recv_d2d.at[0], sibling).wait_send()
        if not local_waited:
            _remote(mine, mine, send_sems.at[2], recv_d2d.at[0], sibling).wait_recv()
        for i in range(3):
            relay = _remote(mine, mine, send_fwd.at[i], recv_fwd.at[i], sibling)
            relay.wait_send()
            relay.wait_recv()

    return _split_call(
        body, (), name, (land, send_sems, recv_d2d, send_hop, send_fwd, recv_fwd),
        (_HBM, _SEM, _SEM, _SEM, _SEM, _SEM), (pltpu.HBM(land.shape, land.dtype),), (_HBM,), {0: 0})[0]


def _landing(own, me):
    land = lax.empty((N_DEV,) + own.shape, own.dtype)
    return lax.dynamic_update_slice(land, own[None], (me,) + (0,) * own.ndim)


def _pair_start(name, far, deps=()):
    pair = lax.empty(far.shape, far.dtype)

    def body(far_ref, pair_ref, send_sems, recv_sems, far_thru, pair_thru, token):
        del far_thru, pair_thru
        _remote(far_ref, pair_ref, send_sems.at[0], recv_sems.at[0], _place()[3]).start()
        token[...] = jnp.zeros_like(token)

    return _split_call(
        body, deps, name,
        (pltpu.with_memory_space_constraint(far, pltpu.HBM), pltpu.with_memory_space_constraint(pair, pltpu.HBM)),
        (_HBM, _HBM),
        (pltpu.SemaphoreType.DMA((1,)), pltpu.SemaphoreType.DMA((1,)),
         pltpu.HBM(far.shape, far.dtype), pltpu.HBM(pair.shape, pair.dtype), _token_shape()),
        (_SEM, _SEM, _HBM, _HBM, _TOKEN), {0: 2, 1: 3})


def _pair_wait(name, started, after):
    send_sems, recv_sems, far, pair, _ = started

    def body(far_ref, pair_ref, send_sems, recv_sems, after_ref, far_out, pair_out):
        del after_ref, far_out, pair_out
        cp = _remote(far_ref, pair_ref, send_sems.at[0], recv_sems.at[0], _place()[3])
        cp.wait_send()
        cp.wait_recv()

    return _split_call(
        body, (), name, (far, pair, send_sems, recv_sems, after),
        (_HBM, _HBM, _SEM, _SEM, pl.BlockSpec(memory_space=pl.ANY)),
        (pltpu.HBM(far.shape, far.dtype), pltpu.HBM(pair.shape, pair.dtype)), (_HBM, _HBM), {0: 0, 1: 1})[1]


def _chip_start(name, sums, deps=()):
    land = lax.empty((3,) + sums.shape[1:], sums.dtype)

    def body(s_ref, land_ref, send_sems, recv_sems, s_thru, land_thru, token):
        del s_thru, land_thru
        x, y, c, _ = _place()
        for k in _OTHER_CHIPS:
            peer, chip = _same_core_of(x, y, c, k)
            _remote(s_ref.at[chip], land_ref.at[k - 1], send_sems.at[k - 1], recv_sems.at[k - 1], peer).start()
        token[...] = jnp.zeros_like(token)

    return _split_call(
        body, deps, name,
        (pltpu.with_memory_space_constraint(sums, pltpu.HBM), pltpu.with_memory_space_constraint(land, pltpu.HBM)),
        (_HBM, _HBM),
        (pltpu.SemaphoreType.DMA((3,)), pltpu.SemaphoreType.DMA((3,)),
         pltpu.HBM(sums.shape, sums.dtype), pltpu.HBM(land.shape, land.dtype), _token_shape()),
        (_SEM, _SEM, _HBM, _HBM, _TOKEN), {0: 2, 1: 3})


def _chip_wait(name, started, after):
    send_sems, recv_sems, sums, land, _ = started

    def body(s_ref, land_ref, send_sems, recv_sems, after_ref, s_out, land_out):
        del after_ref, s_out, land_out
        x, y, c, _ = _place()
        for k in _OTHER_CHIPS:
            peer, chip = _same_core_of(x, y, c, k)
            cp = _remote(s_ref.at[chip], land_ref.at[k - 1], send_sems.at[k - 1], recv_sems.at[k - 1], peer)
            cp.wait_send()
            cp.wait_recv()

    return _split_call(
        body, (), name, (sums, land, send_sems, recv_sems, after),
        (_HBM, _HBM, _SEM, _SEM, pl.BlockSpec(memory_space=pl.ANY)),
        (pltpu.HBM(sums.shape, sums.dtype), pltpu.HBM(land.shape, land.dtype)), (_HBM, _HBM), {0: 0, 1: 1})


_DOT_DIMS = {"nn": (((1,), (0,)), ((), ())), "nt": (((1,), (1,)), ((), ())), "tn": (((0,), (0,)), ((), ()))}


def _matmul(name, mode, operands, in_specs, out_shape, out_specs, grid, acc_shape, epilogue, deps=(), carry=()):
    n_in, n_out, nk = len(operands), len(out_shape), grid[2]
    dims = _DOT_DIMS[mode]

    def body(*refs):
        a_ref, b_ref = refs[0], refs[1]
        extras, outs = refs[2:n_in], refs[n_in:n_in + n_out]
        part = lax.dot_general(a_ref[...], b_ref[...], dims, preferred_element_type=F32)
        if nk == 1:
            epilogue(part, extras, outs)
            return
        acc = refs[-1]
        k = pl.program_id(2)

        @pl.when(k == 0)
        def _():
            acc[...] = part

        @pl.when(jnp.logical_and(k > 0, k < nk - 1))
        def _():
            acc[...] += part

        @pl.when(k == nk - 1)
        def _():
            epilogue(acc[...] + part, extras, outs)

    aliases = {n_in + len(deps) + i: i for i in range(len(carry))}
    return _pallas(body, tuple(deps) + tuple(carry), name=name, grid=grid, in_specs=in_specs, out_specs=out_specs,
                   out_shape=out_shape, input_output_aliases=aliases,
                   scratch_shapes=[pltpu.VMEM(acc_shape, F32)] if nk > 1 else [],
                   compiler_params=_params("parallel", "parallel", "arbitrary"))(*operands)


def _store(dtype):
    def epilogue(acc, extras, outs):
        outs[0][...] = acc.astype(dtype)
    return epilogue


def _residual_epilogue(acc, extras, outs):
    x_ref, gate_ref = extras
    outs[0][...] = acc
    outs[1][...] = x_ref[...] + gate_ref[...] * acc


def _relu2_epilogue(acc, extras, outs):
    r = jnp.maximum(acc, 0.0)
    outs[0][...] = r.astype(outs[0].dtype)
    outs[1][...] = (r * r).astype(outs[1].dtype)


def _relu2_bwd_epilogue(acc, extras, outs):
    outs[0][...] = (acc * (2.0 * extras[0][...].astype(F32))).astype(outs[0].dtype)


def _no_extra_specs(tm, tn):
    return []


def _mm_nn(name, a, b, n_total, b_split, out_shape, epilogue, extras=(), extra_specs=_no_extra_specs, tm=MM_TM, tn=MM_TN, tk=MM_TK,
           deps=(), pieces=None, carry=()):
    m, kdim = a.shape
    tm, tk = _tile(tm, m), _tile(tk, kdim)
    n_blocks = None
    if b_split:
        piece = b.shape[2]
        tn = _tile(tn, piece)
        per = piece // tn
        if pieces is None:
            b_spec = pl.BlockSpec((None, tk, tn), lambda i, j, k: (j // per, k, j % per))
            out_spec = pl.BlockSpec((tm, tn), lambda i, j, k: (i, j))
        else:
            first, count = pieces
            n_blocks = count * per

            def which(j):
                return (first() + j // per) % N_DEV

            b_spec = pl.BlockSpec((None, tk, tn), lambda i, j, k: (which(j), k, j % per))
            out_spec = pl.BlockSpec((tm, tn), lambda i, j, k: (i, which(j) * per + j % per))
    else:
        tn = _tile(tn, n_total)
        b_spec = pl.BlockSpec((tk, tn), lambda i, j, k: (k, j))
        out_spec = pl.BlockSpec((tm, tn), lambda i, j, k: (i, j))
    if n_blocks is None:
        n_blocks = n_total // tn
    in_specs = [pl.BlockSpec((tm, tk), lambda i, j, k: (i, k)), b_spec] + list(extra_specs(tm, tn))
    return _matmul(name, "nn", (a, b) + tuple(extras), in_specs, out_shape, [out_spec] * len(out_shape),
                   (m // tm, n_blocks, kdim // tk), (tm, tn), epilogue, deps, carry)


def _mm_nt(name, a, b, n_total, b_split, out_shape, epilogue, extras=(), extra_specs=_no_extra_specs, tm=MM_TM, tn=MM_TN, tk=MM_TK,
           deps=()):
    m, kdim = a.shape
    tm, tn = _tile(tm, m), _tile(tn, n_total)
    if b_split:
        piece = b.shape[2]
        tk = _tile(tk, piece)
        per = piece // tk
        b_spec = pl.BlockSpec((None, tn, tk), lambda i, j, k: (k // per, j, k % per))
    else:
        tk = _tile(tk, kdim)
        b_spec = pl.BlockSpec((tn, tk), lambda i, j, k: (j, k))
    in_specs = [pl.BlockSpec((tm, tk), lambda i, j, k: (i, k)), b_spec] + list(extra_specs(tm, tn))
    out_specs = [pl.BlockSpec((tm, tn), lambda i, j, k: (i, j)) for _ in out_shape]
    return _matmul(name, "nt", (a, b) + tuple(extras), in_specs, out_shape, out_specs,
                   (m // tm, n_total // tn, kdim // tk), (tm, tn), epilogue, deps)


def _add_pair_epilogue(acc, extras, outs):
    outs[0][...] = (acc + extras[0][...].astype(F32)).astype(outs[0].dtype)


def _mm_tn_half(name, a, b, col_pieces, near, pair=None, tm=MM_TM, tn=MM_TN, tk=MM_TK, deps=()):
    kdim, m = a.shape
    n_total = b.shape[1]
    tk = _tile(tk, kdim)

    def core():
        c = lax.axis_index("c")
        return c if near else 1 - c

    if col_pieces:
        piece = n_total // N_DEV
        tm, tn = _tile(tm, m), _tile(tn, piece)
        per = piece // tn
        grid = (m // tm, N_CHIP * per, kdim // tk)
        a_spec = pl.BlockSpec((tk, tm), lambda i, j, k: (k, i))
        b_spec = pl.BlockSpec((tk, tn), lambda i, j, k: (k, (2 * (j // per) + core()) * per + j % per))
        out_spec = pl.BlockSpec((None, tm, tn), lambda i, j, k: (j // per, i, j % per))
        out_shape = [jax.ShapeDtypeStruct((N_CHIP, m, piece), BF16)]
    else:
        piece = m // N_DEV
        tm, tn = _tile(tm, piece), _tile(tn, n_total)
        per = piece // tm
        grid = (N_CHIP * per, n_total // tn, kdim // tk)
        a_spec = pl.BlockSpec((tk, tm), lambda i, j, k: (k, (2 * (i // per) + core()) * per + i % per))
        b_spec = pl.BlockSpec((tk, tn), lambda i, j, k: (k, j))
        out_spec = pl.BlockSpec((None, tm, tn), lambda i, j, k: (i // per, i % per, j))
        out_shape = [jax.ShapeDtypeStruct((N_CHIP, piece, n_total), BF16)]
    operands, in_specs, epilogue = (a, b), [a_spec, b_spec], _store(BF16)
    if pair is not None:
        operands, in_specs, epilogue = (a, b, pair), [a_spec, b_spec, out_spec], _add_pair_epilogue
    return _matmul(name, "tn", operands, in_specs, out_shape, [out_spec], grid, (tm, tn), epilogue, deps)[0]


def _rms(xv):
    return lax.rsqrt(jnp.mean(xv * xv, axis=-1, keepdims=True) + EPS)


def _colsum(v):
    return jnp.sum(v, axis=0, keepdims=True)


def _norm_mod(name, x, g, scale, shift, tr=256, deps=()):
    s, d = x.shape
    tr = _tile(tr, s)

    def body(x_ref, g_ref, sc_ref, sh_ref, h_ref):
        xv = x_ref[...]
        h = (xv * _rms(xv)) * g_ref[...]
        h_ref[...] = (h * (1.0 + sc_ref[...]) + sh_ref[...]).astype(h_ref.dtype)

    row = pl.BlockSpec((tr, d), lambda i: (i, 0))
    vec = pl.BlockSpec((1, d), lambda i: (0, 0))
    return _pallas(body, deps, name=name, grid=(s // tr,), in_specs=[row, vec, vec, vec], out_specs=row,
                   out_shape=jax.ShapeDtypeStruct((s, d), BF16), compiler_params=_params("parallel"))(x, g, scale, shift)


def _loss_head(x3, target, gf, gate2, mlp, tr=128):
    s, d = x3.shape
    tr = _tile(tr, s)

    def body(x_ref, t_ref, gf_ref, gate_ref, mlp_ref, dx_ref, dbr_ref, dgf_ref, dgate_ref, loss_ref):
        @pl.when(pl.program_id(0) == 0)
        def _():
            dgf_ref[...] = jnp.zeros_like(dgf_ref)
            dgate_ref[...] = jnp.zeros_like(dgate_ref)
            loss_ref[...] = jnp.zeros_like(loss_ref)

        xv = x_ref[...]
        r = _rms(xv)
        xn = xv * r
        gfv = gf_ref[...]
        err = xn * gfv - t_ref[...]
        loss_ref[...] += 0.5 * _colsum(jnp.mean(err * err, axis=-1, keepdims=True))
        dy = err * (1.0 / d)
        dgf_ref[...] += _colsum(dy * xn)
        dxn = dy * gfv
        dx = r * (dxn - xn * jnp.mean(dxn * xn, axis=-1, keepdims=True))
        dx_ref[...] = dx
        dbr_ref[...] = (dx * gate_ref[...]).astype(dbr_ref.dtype)
        dgate_ref[...] += _colsum(dx * mlp_ref[...])

    row = pl.BlockSpec((tr, d), lambda i: (i, 0))
    vec = pl.BlockSpec((1, d), lambda i: (0, 0))
    return _pallas(
        body, name="loss_head", grid=(s // tr,), in_specs=[row, row, vec, vec, row],
        out_specs=[row, row, vec, vec, pl.BlockSpec((1, 128), lambda i: (0, 0))],
        out_shape=[jax.ShapeDtypeStruct((s, d), F32), jax.ShapeDtypeStruct((s, d), BF16),
                   jax.ShapeDtypeStruct((1, d), F32), jax.ShapeDtypeStruct((1, d), F32),
                   jax.ShapeDtypeStruct((1, 128), F32)],
        compiler_params=_params("arbitrary"))(x3, target, gf, gate2, mlp)


def _norm_mod_bwd(name, dh, xin, g, scale, dx_up, branch=None, gate=None, tr=128, deps=()):
    s, d = xin.shape
    tr = _tile(tr, s)
    with_gate = branch is not None

    def body(*refs):
        dh_ref, x_ref, g_ref, sc_ref, up_ref = refs[:5]
        if with_gate:
            br_ref, gate_ref = refs[5:7]
            dx_ref, dsh_ref, dsc_ref, dg_ref, dbr_ref, dgate_ref = refs[7:]
            sums = (dsh_ref, dsc_ref, dg_ref, dgate_ref)
        else:
            dx_ref, dsh_ref, dsc_ref, dg_ref = refs[5:]
            sums = (dsh_ref, dsc_ref, dg_ref)

        @pl.when(pl.program_id(0) == 0)
        def _():
            for ref in sums:
                ref[...] = jnp.zeros_like(ref)

        xv, dhv, gv = x_ref[...], dh_ref[...], g_ref[...]
        r = _rms(xv)
        xn = xv * r
        one_sc = 1.0 + sc_ref[...]
        dsh_ref[...] += _colsum(dhv)
        dsc_ref[...] += _colsum(dhv * (xn * gv))
        dg_ref[...] += _colsum(dhv * one_sc * xn)
        dxn = dhv * one_sc * gv
        dx = up_ref[...] + r * (dxn - xn * jnp.mean(dxn * xn, axis=-1, keepdims=True))
        dx_ref[...] = dx
        if with_gate:
            dbr_ref[...] = (dx * gate_ref[...]).astype(dbr_ref.dtype)
            dgate_ref[...] += _colsum(dx * br_ref[...])

    row = pl.BlockSpec((tr, d), lambda i: (i, 0))
    vec = pl.BlockSpec((1, d), lambda i: (0, 0))
    vshape = jax.ShapeDtypeStruct((1, d), F32)
    operands = [dh, xin, g, scale, dx_up]
    in_specs = [row, row, vec, vec, row]
    out_shape = [jax.ShapeDtypeStruct((s, d), F32), vshape, vshape, vshape]
    out_specs = [row, vec, vec, vec]
    if with_gate:
        operands += [branch, gate]
        in_specs += [row, vec]
        out_shape += [jax.ShapeDtypeStruct((s, d), BF16), vshape]
        out_specs += [row, vec]
    return _pallas(body, deps, name=name, grid=(s // tr,), in_specs=in_specs, out_specs=out_specs, out_shape=out_shape,
                   compiler_params=_params("arbitrary"))(*operands)


def _window_count(c0, rows, half, s):
    t = c0 + lax.broadcasted_iota(jnp.int32, (rows, 1), 0)
    return (jnp.minimum(t + half, s) - jnp.maximum(t - half, 0)).astype(F32)


def _zero_pads(pad, s):
    zeros = jnp.zeros((PAD_ROWS, pad.shape[1]), pad.dtype)
    pad[0:PAD_ROWS, :] = zeros
    pad[PAD_ROWS + s:PAD_ROWS + s + PAD_ROWS, :] = zeros


def _pool_fwd(proj, s, gd, cb, ch):
    nsub = gd // cb

    def body(v_ref, o_ref, pad):
        g = pl.program_id(0)
        _zero_pads(pad, s)
        pad[PAD_ROWS:PAD_ROWS + s, :] = v_ref[...]
        for gi, window in enumerate(POOL_WINDOWS):
            half = window // 2

            @pl.when(g == gi)
            def _(half=half):
                for c0 in range(0, s, ch):
                    base = PAD_ROWS + c0
                    acc = pad[base - half:base - half + ch, :]
                    for j in range(-half + 1, half):
                        acc = acc + pad[base + j:base + j + ch, :]
                    out = acc / _window_count(c0, ch, half, s) - v_ref[c0:c0 + ch, :]
                    o_ref[c0:c0 + ch, :] = out.astype(o_ref.dtype)

    spec = pl.BlockSpec((s, cb), lambda g, j: (0, g * nsub + j))
    return _pallas(body, name="pool_fwd", grid=(N_POOL_GROUPS, nsub), in_specs=[spec], out_specs=spec,
                   out_shape=jax.ShapeDtypeStruct((s, N_POOL_GROUPS * gd), BF16),
                   scratch_shapes=[pltpu.VMEM((s + 2 * PAD_ROWS, cb), F32)],
                   compiler_params=_params("parallel", "parallel"))(proj)


def _pool_bwd(dpooled, dproj, s, gd, cb, ch):
    nsub = gd // cb

    def body(dp_ref, dproj_in, o_ref, pad):
        del dproj_in
        g = pl.program_id(0)
        _zero_pads(pad, s)
        for gi, window in enumerate(POOL_WINDOWS):
            half = window // 2

            @pl.when(g == gi)
            def _(half=half):
                for c0 in range(0, s, ch):
                    pad[PAD_ROWS + c0:PAD_ROWS + c0 + ch, :] = dp_ref[c0:c0 + ch, :] / _window_count(c0, ch, half, s)
                for c0 in range(0, s, ch):
                    base = PAD_ROWS + c0
                    acc = pad[base - half + 1:base - half + 1 + ch, :]
                    for j in range(-half + 2, half + 1):
                        acc = acc + pad[base + j:base + j + ch, :]
                    o_ref[c0:c0 + ch, :] = (acc - dp_ref[c0:c0 + ch, :]).astype(o_ref.dtype)

    spec = pl.BlockSpec((s, cb), lambda g, j: (0, g * nsub + j))
    return _pallas(body, name="pool_bwd", grid=(N_POOL_GROUPS, nsub),
                   in_specs=[spec, pl.BlockSpec(memory_space=pl.ANY)], out_specs=spec,
                   out_shape=jax.ShapeDtypeStruct(dproj.shape, dproj.dtype), input_output_aliases={1: 0},
                   scratch_shapes=[pltpu.VMEM((s + 2 * PAD_ROWS, cb), F32)],
                   compiler_params=_params("parallel", "parallel"))(dpooled, dproj)


def _poolmix_fwd(pooled, wmix, pool_scale, gnorm_g, d_model, tm=512):
    s = pooled.shape[0]
    gd = wmix.shape[1]
    tm = _tile(tm, s)

    def body(p_ref, w_ref, ps_ref, g_ref, apre_ref, mixed_ref):
        a_pre = jnp.dot(p_ref[...], w_ref[...], preferred_element_type=F32)
        apre_ref[...] = a_pre
        a_out = a_pre * ps_ref[...]
        mixed_ref[...] = ((a_out * _rms(a_out)) * g_ref[...]).astype(mixed_ref.dtype)

    blk = pl.BlockSpec((tm, gd), lambda g, i: (i, g))
    vec = pl.BlockSpec((1, gd), lambda g, i: (0, g))
    return _pallas(body, name="poolmix_fwd", grid=(N_POOL_GROUPS, s // tm),
                   in_specs=[blk, pl.BlockSpec((None, gd, gd), lambda g, i: (g, 0, 0)), vec, vec],
                   out_specs=[blk, blk],
                   out_shape=[jax.ShapeDtypeStruct((s, N_POOL_GROUPS * gd), F32), jax.ShapeDtypeStruct((s, d_model), BF16)],
                   compiler_params=_params("parallel", "parallel"))(pooled, wmix, pool_scale, gnorm_g)


def _poolmix_bwd(dmixed, a_pre, wmix, pool_scale, gnorm_g, tm=512):
    s = a_pre.shape[0]
    gd = wmix.shape[1]
    tm = _tile(tm, s)

    def body(dm_ref, apre_ref, w_ref, ps_ref, g_ref, dapre_ref, dpooled_ref, dps_ref, dg_ref):
        @pl.when(pl.program_id(1) == 0)
        def _():
            dps_ref[...] = jnp.zeros_like(dps_ref)
            dg_ref[...] = jnp.zeros_like(dg_ref)

        a_pre, dm, ps = apre_ref[...], dm_ref[...], ps_ref[...]
        a_out = a_pre * ps
        r = _rms(a_out)
        n = a_out * r
        dg_ref[...] += _colsum(dm * n)
        dn = dm * g_ref[...]
        da_out = r * (dn - n * jnp.mean(dn * n, axis=-1, keepdims=True))
        dps_ref[...] += _colsum(da_out * a_pre)
        da_pre = (da_out * ps).astype(BF16)
        dapre_ref[...] = da_pre
        dpooled_ref[...] = lax.dot_general(da_pre, w_ref[...], _DOT_DIMS["nt"], preferred_element_type=F32)

    blk = pl.BlockSpec((tm, gd), lambda g, i: (i, g))
    vec = pl.BlockSpec((1, gd), lambda g, i: (0, g))
    width = N_POOL_GROUPS * gd
    return _pallas(body, name="poolmix_bwd", grid=(N_POOL_GROUPS, s // tm),
                   in_specs=[blk, blk, pl.BlockSpec((None, gd, gd), lambda g, i: (g, 0, 0)), vec, vec],
                   out_specs=[blk, blk, vec, vec],
                   out_shape=[jax.ShapeDtypeStruct((s, width), BF16), jax.ShapeDtypeStruct((s, width), F32),
                              jax.ShapeDtypeStruct((1, width), F32), jax.ShapeDtypeStruct((1, width), F32)],
                   compiler_params=_params("parallel", "arbitrary"))(dmixed, a_pre, wmix, pool_scale, gnorm_g)


def _poolmix_wgrad(pooled, da_pre, gd, tk=1024):
    s = pooled.shape[0]
    tk = _tile(tk, s)
    nk = s // tk

    def body(p_ref, d_ref, o_ref, acc):
        k = pl.program_id(1)
        part = lax.dot_general(p_ref[...], d_ref[...], _DOT_DIMS["tn"], preferred_element_type=F32)

        @pl.when(k == 0)
        def _():
            acc[...] = part

        @pl.when(k > 0)
        def _():
            acc[...] += part

        @pl.when(k == nk - 1)
        def _():
            o_ref[...] = acc[...].astype(o_ref.dtype)

    blk = pl.BlockSpec((tk, gd), lambda g, k: (k, g))
    return _pallas(body, name="poolmix_wgrad", grid=(N_POOL_GROUPS, nk), in_specs=[blk, blk],
                   out_specs=pl.BlockSpec((None, gd, gd), lambda g, k: (g, 0, 0)),
                   out_shape=jax.ShapeDtypeStruct((N_POOL_GROUPS, gd, gd), BF16),
                   scratch_shapes=[pltpu.VMEM((gd, gd), F32)],
                   compiler_params=_params("parallel", "arbitrary"))(pooled, da_pre)


def _head_mean(v):
    parts = []
    for q in range(v.shape[1] // CONV_HEAD_DIM):
        m = jnp.mean(v[:, q * CONV_HEAD_DIM:(q + 1) * CONV_HEAD_DIM], axis=-1, keepdims=True)
        parts.append(jnp.broadcast_to(m, (v.shape[0], CONV_HEAD_DIM)))
    return parts[0] if len(parts) == 1 else jnp.concatenate(parts, axis=1)


def _conv_fwd(proj, mixed, conv_w, conv_b, gnorm_g, s, width, cb, ch, deps=()):
    nblk = width // cb

    def body(b_ref, c_ref, u_ref, w_ref, cb_ref, g_ref, mixed_in, o_ref, pad):
        del mixed_in
        _zero_pads(pad, s)
        pad[PAD_ROWS:PAD_ROWS + s, :] = c_ref[...] * u_ref[...]
        w = w_ref[...]
        for c0 in range(0, s, ch):
            base = PAD_ROWS + c0
            conv = (w[0:1] * pad[base - 1:base - 1 + ch, :] + w[1:2] * pad[base:base + ch, :]
                    + w[2:3] * pad[base + 1:base + 1 + ch, :] + cb_ref[...])
            bo = b_ref[c0:c0 + ch, :] * conv
            n = bo * lax.rsqrt(_head_mean(bo * bo) + EPS)
            o_ref[c0:c0 + ch, :] = (n * g_ref[...]).astype(o_ref.dtype)

    def part(p):
        return pl.BlockSpec((s, cb), lambda j: (0, p * nblk + j))

    vec = pl.BlockSpec((1, cb), lambda j: (0, j))
    return _pallas(body, deps, name="conv_fwd", grid=(nblk,),
                   in_specs=[part(1), part(2), part(3), pl.BlockSpec((3, cb), lambda j: (0, j)), vec, vec,
                             pl.BlockSpec(memory_space=pl.ANY)],
                   out_specs=part(1), out_shape=jax.ShapeDtypeStruct(mixed.shape, mixed.dtype),
                   input_output_aliases={6: 0},
                   scratch_shapes=[pltpu.VMEM((s + 2 * PAD_ROWS, cb), F32)],
                   compiler_params=_params("parallel"))(proj, proj, proj, conv_w, conv_b, gnorm_g, mixed)


def _conv_bwd(dmixed, proj, conv_w, conv_b, gnorm_g, s, width, cb, ch, deps=()):
    nblk = width // cb

    def body(dm_ref, b_ref, c_ref, u_ref, w_ref, cb_ref, g_ref, dproj_ref, dw_ref, dcb_ref, dg_ref,
             pad_cu, pad_dconv, db_buf, dc_buf, du_buf, sems):
        j = pl.program_id(0)
        _zero_pads(pad_cu, s)
        _zero_pads(pad_dconv, s)
        pad_cu[PAD_ROWS:PAD_ROWS + s, :] = c_ref[...] * u_ref[...]
        w, gv = w_ref[...], g_ref[...]
        zero = jnp.zeros((1, cb), F32)
        dw0, dw1, dw2, dcb, dg = zero, zero, zero, zero, zero
        for c0 in range(0, s, ch):
            base = PAD_ROWS + c0
            cu_prev, cu_here, cu_next = (pad_cu[base - 1:base - 1 + ch, :], pad_cu[base:base + ch, :],
                                         pad_cu[base + 1:base + 1 + ch, :])
            conv = w[0:1] * cu_prev + w[1:2] * cu_here + w[2:3] * cu_next + cb_ref[...]
            bg = b_ref[c0:c0 + ch, :]
            bo = bg * conv
            r = lax.rsqrt(_head_mean(bo * bo) + EPS)
            n = bo * r
            dm = dm_ref[c0:c0 + ch, :]
            dg = dg + _colsum(dm * n)
            dn = dm * gv
            dbo = r * (dn - n * _head_mean(dn * n))
            db_buf[c0:c0 + ch, :] = (dbo * conv).astype(BF16)
            dconv = dbo * bg
            pad_dconv[base:base + ch, :] = dconv
            dcb = dcb + _colsum(dconv)
            dw0 = dw0 + _colsum(dconv * cu_prev)
            dw1 = dw1 + _colsum(dconv * cu_here)
            dw2 = dw2 + _colsum(dconv * cu_next)
        dw_ref[0:1, :] = dw0
        dw_ref[1:2, :] = dw1
        dw_ref[2:3, :] = dw2
        dcb_ref[...] = dcb
        dg_ref[...] = dg
        for c0 in range(0, s, ch):
            base = PAD_ROWS + c0
            dcu = (w[0:1] * pad_dconv[base + 1:base + 1 + ch, :] + w[1:2] * pad_dconv[base:base + ch, :]
                   + w[2:3] * pad_dconv[base - 1:base - 1 + ch, :])
            dc_buf[c0:c0 + ch, :] = (dcu * u_ref[c0:c0 + ch, :]).astype(BF16)
            du_buf[c0:c0 + ch, :] = (dcu * c_ref[c0:c0 + ch, :]).astype(BF16)
        copies = []
        for p, buf in enumerate((db_buf, dc_buf, du_buf)):
            col = pl.multiple_of((p + 1) * width + j * cb, CONV_HEAD_DIM)
            copies.append(pltpu.make_async_copy(buf, dproj_ref.at[:, pl.ds(col, cb)], sems.at[p]))
            copies[-1].start()
        for cp in copies:
            cp.wait()

    def part(p):
        return pl.BlockSpec((s, cb), lambda j: (0, p * nblk + j))

    vec = pl.BlockSpec((1, cb), lambda j: (0, j))
    w_spec = pl.BlockSpec((3, cb), lambda j: (0, j))
    return _pallas(body, deps, name="conv_bwd", grid=(nblk,),
                   in_specs=[part(1), part(1), part(2), part(3), w_spec, vec, vec],
                   out_specs=[pl.BlockSpec(memory_space=pl.ANY), w_spec, vec, vec],
                   out_shape=[jax.ShapeDtypeStruct((s, 4 * width), BF16), jax.ShapeDtypeStruct((3, width), F32),
                              jax.ShapeDtypeStruct((1, width), F32), jax.ShapeDtypeStruct((1, width), F32)],
                   scratch_shapes=[pltpu.VMEM((s + 2 * PAD_ROWS, cb), F32), pltpu.VMEM((s + 2 * PAD_ROWS, cb), F32),
                                   pltpu.VMEM((s, cb), BF16), pltpu.VMEM((s, cb), BF16), pltpu.VMEM((s, cb), BF16),
                                   pltpu.SemaphoreType.DMA((3,))],
                   compiler_params=_params("arbitrary"))(dmixed, proj, proj, proj, conv_w, conv_b, gnorm_g)


def _adamw(w, g, m, v):
    m = ADAM_B1 * m + (1.0 - ADAM_B1) * g
    v = ADAM_B2 * v + (1.0 - ADAM_B2) * (g * g)
    m_hat = m / (1.0 - ADAM_B1 ** ADAM_STEP)
    v_hat = v / (1.0 - ADAM_B2 ** ADAM_STEP)
    delta = -ADAM_LR * (m_hat / (jnp.sqrt(v_hat) + ADAM_EPS) + ADAM_WD * w)
    return delta, m, v


def _ada_fwd(c_rows, w, b, tn=512):
    rows, d = c_rows.shape
    n = w.shape[1]
    tn = _tile(tn, n)

    def body(c_ref, w_ref, b_ref, o_ref):
        cv = c_ref[...]
        act = (cv * jax.nn.sigmoid(cv)).astype(BF16)
        o_ref[...] = jnp.dot(act, w_ref[...].astype(BF16), preferred_element_type=F32) + b_ref[...]

    return _pallas(body, name="ada_fwd", grid=(n // tn,),
                   in_specs=[pl.BlockSpec((rows, d), lambda j: (0, 0)), pl.BlockSpec((d, tn), lambda j: (0, j)),
                             pl.BlockSpec((1, tn), lambda j: (0, j))],
                   out_specs=pl.BlockSpec((rows, tn), lambda j: (0, j)),
                   out_shape=jax.ShapeDtypeStruct((rows, n), F32), compiler_params=_params("parallel"))(c_rows, w, b)


def _ada_bwd_adam(c_cols, dmod, w, m, v, tr=512, tn=1024):
    d, rows = c_cols.shape
    n = w.shape[1]
    tr, tn = _tile(tr, d), _tile(tn, n)

    def body(c_ref, dm_ref, w_ref, m_ref, v_ref, g_ref, dl_ref, nm_ref, nv_ref):
        cv = c_ref[...]
        act = (cv * jax.nn.sigmoid(cv)).astype(BF16)
        g = jnp.dot(act, dm_ref[...].astype(BF16), preferred_element_type=F32)
        g_ref[...] = g
        dl_ref[...], nm_ref[...], nv_ref[...] = _adamw(w_ref[...], g, m_ref[...], v_ref[...])

    blk = pl.BlockSpec((tr, tn), lambda i, j: (i, j))
    shape = jax.ShapeDtypeStruct((d, n), F32)
    return _pallas(body, name="ada_bwd_adam", grid=(d // tr, n // tn),
                   in_specs=[pl.BlockSpec((tr, rows), lambda i, j: (i, 0)), pl.BlockSpec((rows, tn), lambda i, j: (0, j)),
                             blk, blk, blk],
                   out_specs=[blk] * 4, out_shape=[shape] * 4,
                   compiler_params=_params("parallel", "parallel"))(c_cols, dmod, w, m, v)


def _reduce_adam(name, pieces, w, m, v, tr=256, tc=1024):
    r, c = w.shape
    tr, tc = _tile(tr, r), _tile(tc, c)

    def body(p_ref, w_ref, m_ref, v_ref, g_ref, dl_ref, nm_ref, nv_ref):
        g = p_ref[0].astype(F32)
        for j in range(1, N_DEV):
            g = g + p_ref[j].astype(F32)
        g_ref[...] = g
        dl_ref[...], nm_ref[...], nv_ref[...] = _adamw(w_ref[...], g, m_ref[...], v_ref[...])

    blk = pl.BlockSpec((tr, tc), lambda i, j: (i, j))
    shape = jax.ShapeDtypeStruct((r, c), F32)
    return _pallas(body, name=name, grid=(r // tr, c // tc),
                   in_specs=[pl.BlockSpec((N_DEV, tr, tc), lambda i, j: (0, i, j)), blk, blk, blk],
                   out_specs=[blk] * 4, out_shape=[shape] * 4,
                   compiler_params=_params("parallel", "parallel"))(pieces, w, m, v)


def _reduce_adam_chips(name, sums, land, w, m, v, tr=256, tc=1024):
    r, c = w.shape
    tr, tc = _tile(tr, r), _tile(tc, c)

    def body(s_ref, l_ref, w_ref, m_ref, v_ref, g_ref, dl_ref, nm_ref, nv_ref):
        g = s_ref[...].astype(F32)
        for k in range(3):
            g = g + l_ref[k].astype(F32)
        g_ref[...] = g
        dl_ref[...], nm_ref[...], nv_ref[...] = _adamw(w_ref[...], g, m_ref[...], v_ref[...])

    blk = pl.BlockSpec((tr, tc), lambda i, j: (i, j))
    shape = jax.ShapeDtypeStruct((r, c), F32)
    mine = pl.BlockSpec((None, tr, tc), lambda i, j: (2 * lax.axis_index("x") + lax.axis_index("y"), i, j))
    return _pallas(body, name=name, grid=(r // tr, c // tc),
                   in_specs=[mine, pl.BlockSpec((3, tr, tc), lambda i, j: (0, i, j)), blk, blk, blk],
                   out_specs=[blk] * 4, out_shape=[shape] * 4,
                   compiler_params=_params("parallel", "parallel"))(sums, land, w, m, v)


def _sum_devices(parts):
    n = parts.shape[1]

    def body(p_ref, o_ref):
        acc = p_ref[0:1, :]
        for j in range(1, N_DEV):
            acc = acc + p_ref[j:j + 1, :]
        o_ref[...] = acc

    return _pallas(body, name="sum_devices", out_shape=jax.ShapeDtypeStruct((1, n), F32),
                   compiler_params=pltpu.CompilerParams(vmem_limit_bytes=VMEM_LIMIT_BYTES))(parts)


def _adam_small(name, g, w, m, v):
    def body(g_ref, w_ref, m_ref, v_ref, dl_ref, nm_ref, nv_ref):
        dl_ref[...], nm_ref[...], nv_ref[...] = _adamw(w_ref[...], g_ref[...], m_ref[...], v_ref[...])

    shape = jax.ShapeDtypeStruct(w.shape, F32)
    return _pallas(body, name=name, out_shape=[shape] * 3,
                   compiler_params=pltpu.CompilerParams(vmem_limit_bytes=VMEM_LIMIT_BYTES))(g, w, m, v)


def kernel(x, c, w_ada, b_ada, norm1_g, w_in, pool_mix_w, pool_scale, conv_w, conv_b, gnorm_pool_g, gnorm_conv_g, w_out, norm2_g, w_mlp_in, w_mlp_out, final_g, loss_target, m_w_ada, m_b_ada, m_norm1_g, m_w_in, m_pool_mix_w, m_pool_scale, m_conv_w, m_conv_b, m_gnorm_pool_g, m_gnorm_conv_g, m_w_out, m_norm2_g, m_w_mlp_in, m_w_mlp_out, m_final_g, v_w_ada, v_b_ada, v_norm1_g, v_w_in, v_pool_mix_w, v_pool_scale, v_conv_w, v_conv_b, v_gnorm_pool_g, v_gnorm_conv_g, v_w_out, v_norm2_g, v_w_mlp_in, v_w_mlp_out, v_final_g):
    s, d = x.shape[1], x.shape[2]
    width = d // 2
    gd = width // N_POOL_GROUPS
    d_ff = w_mlp_in.shape[2] * N_DEV
    n_proj = w_in.shape[2] * N_DEV
    ada_cols = w_ada.shape[2]
    conv_cols = conv_w.shape[2]
    assert n_proj == 4 * width and ada_cols * N_DEV == N_MOD * d and d_ff % N_DEV == 0
    assert width % CONV_HEAD_DIM == 0 and s % 8 == 0
    seq_chunk = _tile(512, s)
    pool_cb = _tile(256, gd)
    conv_cb = CONV_HEAD_DIM

    me = 4 * lax.axis_index("x") + 2 * lax.axis_index("y") + lax.axis_index("c")
    x2d, target = x[0], loss_target[0]

    wmix_all, conv_w_all, c_all = _exchange(
        "gather_small_weights", [pool_mix_w[0].astype(BF16), conv_w[0], c], ["gather"] * 3)
    wmix_full = jnp.transpose(wmix_all, (1, 0, 2, 3)).reshape(N_POOL_GROUPS, gd, gd)
    conv_w_full = jnp.transpose(conv_w_all, (1, 0, 2)).reshape(3, width)
    c_rows = jnp.concatenate([c_all.reshape(N_DEV, d), jnp.zeros((N_DEV, d), F32)], axis=0)

    b_mine = lax.dynamic_slice(b_ada, (0, me * ada_cols), (1, ada_cols))
    mod_part = _ada_fwd(c_rows, w_ada[0], b_mine)
    (mod_all,) = _exchange("scatter_mod", [mod_part[:N_DEV].reshape(N_DEV, 1, ada_cols)], ["a2a"])
    mod = mod_all.reshape(1, N_MOD * d)

    started, hopped = {}, {}

    def gather_start(wname, wgt, deps):
        land = _landing(wgt[0].astype(BF16), me)
        started[wname] = _gather_start("gather_" + wname + "_start", land, deps)

    def gather_hop(wname, land, after):
        hopped[wname] = _gather_hop("gather_" + wname + "_hop", started[wname], land, after)

    def gathered(wname, after, local_waited=False):
        relayed = _gather_relay("gather_" + wname + "_relay", started[wname], hopped[wname], after)
        return _gather_wait("gather_" + wname + "_wait", started[wname], hopped[wname], relayed, local_waited)

    def first_local():
        return 2 * (2 * lax.axis_index("x") + lax.axis_index("y"))

    def first_remote():
        return first_local() + 2

    gather_start("w_in", w_in, (mod,))
    gather_start("w_out", w_out, (mod,))
    shift1, scale1, gate1, shift2, scale2, gate2 = [mod[:, i * d:(i + 1) * d] for i in range(N_MOD)]

    h1 = _norm_mod("norm1_fwd", x2d, norm1_g, scale1, shift1, deps=(started["w_in"][5], started["w_out"][5]))
    proj_shape = [jax.ShapeDtypeStruct((s, n_proj), F32)]
    w_in_local = _gather_wait_local("gather_w_in_local", started["w_in"], h1)
    (proj,) = _mm_nn("in_proj_local", h1, w_in_local, n_proj, True, proj_shape, _store(F32), pieces=(first_local, 2))
    gather_hop("w_in", w_in_local, proj)
    gather_hop("w_out", started["w_out"][4], hopped["w_in"][3])
    gather_start("w_mlp_in", w_mlp_in, (hopped["w_out"][3],))
    w_in_all = gathered("w_in", started["w_mlp_in"][5], True)
    (proj,) = _mm_nn("in_proj", h1, w_in_all, n_proj, True, proj_shape, _store(F32), pieces=(first_remote, 6),
                     carry=(proj,))
    gather_hop("w_mlp_in", started["w_mlp_in"][4], proj)
    gather_start("w_mlp_out", w_mlp_out, (hopped["w_mlp_in"][3],))
    pooled = _pool_fwd(proj, s, gd, pool_cb, seq_chunk)
    a_pre, mixed = _poolmix_fwd(pooled, wmix_full, pool_scale, gnorm_pool_g, d)
    mixed = _conv_fwd(proj, mixed, conv_w_full, conv_b, gnorm_conv_g, s, width, conv_cb, seq_chunk,
                      deps=(started["w_mlp_out"][5],))


    def residual_specs(tm, tn):
        return [pl.BlockSpec((tm, tn), lambda i, j, k: (i, j)), pl.BlockSpec((1, tn), lambda i, j, k: (0, j))]

    sd_f32 = jax.ShapeDtypeStruct((s, d), F32)
    w_out_full = gathered("w_out", mixed).reshape(d, d)
    attn, x_mid = _mm_nn("out_proj", mixed, w_out_full, d, False, [sd_f32, sd_f32], _residual_epilogue,
                         extras=(x2d, gate1), extra_specs=residual_specs)
    h2 = _norm_mod("norm2_fwd", x_mid, norm2_g, scale2, shift2)
    sf_bf16 = jax.ShapeDtypeStruct((s, d_ff), BF16)
    w1_local = _gather_wait_local("gather_w_mlp_in_local", started["w_mlp_in"][:4] + (hopped["w_mlp_in"][2],), h2)
    relu, hid = _mm_nn("mlp_in_local", h2, w1_local, d_ff, True, [sf_bf16, sf_bf16], _relu2_epilogue,
                       pieces=(first_local, 2))
    hopped["w_mlp_in"] = hopped["w_mlp_in"][:2] + (w1_local,) + hopped["w_mlp_in"][3:]
    w1_all = gathered("w_mlp_in", hid, True)
    gather_hop("w_mlp_out", started["w_mlp_out"][4], w1_all)
    relu, hid = _mm_nn("mlp_in", h2, w1_all, d_ff, True, [sf_bf16, sf_bf16], _relu2_epilogue,
                       pieces=(first_remote, 6), carry=(relu, hid), deps=(hopped["w_mlp_out"][3],))
    w2_full = gathered("w_mlp_out", hid).reshape(d_ff, d)
    mlp, x_last = _mm_nn("mlp_out", hid, w2_full, d, False, [sd_f32, sd_f32], _residual_epilogue,
                         extras=(x_mid, gate2), extra_specs=residual_specs)

    dx_last, dmlp, d_final_g, dgate2, loss_row = _loss_head(x_last, target, final_g.reshape(1, d), gate2, mlp)

    def relu_specs(tm, tn):
        return [pl.BlockSpec((tm, tn), lambda i, j, k: (i, j))]

    def reduce_start(wname, a, b, col_pieces, deps=()):
        far = _mm_tn_half(wname + "_dw_far", a, b, col_pieces, near=False, deps=deps)
        return _pair_start("scatter_" + wname + "_pair_start", far)

    def reduce_chips(wname, a, b, col_pieces, pairs, after):
        pair = _pair_wait("scatter_" + wname + "_pair_wait", pairs, after)
        sums = _mm_tn_half(wname + "_dw_near", a, b, col_pieces, near=True, pair=pair)
        return _chip_start("scatter_" + wname + "_chip_start", sums)

    pairs_w2 = reduce_start("mlp_out", hid, dmlp, False)
    (dhpre,) = _mm_nt("mlp_out_dx", dmlp, w2_full, d_ff, False, [sf_bf16], _relu2_bwd_epilogue,
                      extras=(relu,), extra_specs=relu_specs, deps=(pairs_w2[4],))
    chips_w2 = reduce_chips("mlp_out", hid, dmlp, False, pairs_w2, dhpre)
    pairs_w1 = reduce_start("mlp_in", h2, dhpre, True, deps=(chips_w2[4],))
    (dh2,) = _mm_nt("mlp_in_dx", dhpre, w1_all, d, True, [sd_f32], _store(F32), tn=1024, deps=(pairs_w1[4],))
    chips_w1 = reduce_chips("mlp_in", h2, dhpre, True, pairs_w1, dh2)
    dx_mid, dshift2, dscale2, d_norm2_g, dattn, dgate1 = _norm_mod_bwd(
        "norm2_bwd", dh2, x_mid, norm2_g, scale2, dx_last, branch=attn, gate=gate1, deps=(chips_w1[4],))

    pairs_w_out = reduce_start("out_proj", mixed, dattn, False)
    (dmixed,) = _mm_nt("out_proj_dx", dattn, w_out_full, d, False, [sd_f32], _store(F32), deps=(pairs_w_out[4],))
    chips_w_out = reduce_chips("out_proj", mixed, dattn, False, pairs_w_out, dmixed)
    dproj, d_conv_w, d_conv_b, d_gnorm_conv = _conv_bwd(dmixed, proj, conv_w_full, conv_b, gnorm_conv_g,
                                                        s, width, conv_cb, seq_chunk, deps=(chips_w_out[4],))
    da_pre, dpooled, d_pool_scale, d_gnorm_pool = _poolmix_bwd(dmixed, a_pre, wmix_full, pool_scale, gnorm_pool_g)
    g_wmix = _poolmix_wgrad(pooled, da_pre, gd)
    dproj = _pool_bwd(dpooled, dproj, s, gd, pool_cb, seq_chunk)

    pairs_w_in = reduce_start("in_proj", h1, dproj, True)
    chips_w_in = reduce_chips("in_proj", h1, dproj, True, pairs_w_in, pairs_w_in[4])
    (dh1,) = _mm_nt("in_proj_dx", dproj, w_in_all, d, True, [sd_f32], _store(F32), tn=1024, deps=(chips_w_in[4],))
    grad_x, dshift1, dscale1, d_norm1_g = _norm_mod_bwd("norm1_bwd", dh1, x2d, norm1_g, scale1, dx_mid)

    sums, landed = _chip_wait("scatter_w_mlp_out_chip_wait", chips_w2, grad_x)
    out_w2 = _reduce_adam_chips("adam_w_mlp_out", sums, landed, w_mlp_out[0], m_w_mlp_out[0], v_w_mlp_out[0])
    sums, landed = _chip_wait("scatter_w_mlp_in_chip_wait", chips_w1, out_w2[0])
    out_w1 = _reduce_adam_chips("adam_w_mlp_in", sums, landed, w_mlp_in[0], m_w_mlp_in[0], v_w_mlp_in[0])
    sums, landed = _chip_wait("scatter_w_out_chip_wait", chips_w_out, out_w1[0])
    out_w_out = _reduce_adam_chips("adam_w_out", sums, landed, w_out[0], m_w_out[0], v_w_out[0])
    sums, landed = _chip_wait("scatter_w_in_chip_wait", chips_w_in, out_w_out[0])
    out_w_in = _reduce_adam_chips("adam_w_in", sums, landed, w_in[0], m_w_in[0], v_w_in[0])

    rows_mix = gd // N_DEV
    g_wmix_split = jnp.transpose(g_wmix.reshape(N_POOL_GROUPS, N_DEV, rows_mix, gd), (1, 0, 2, 3))
    g_wmix_split = g_wmix_split.reshape(N_DEV, N_POOL_GROUPS * rows_mix, gd)
    loss_pad = jnp.concatenate([loss_row[:, :1], jnp.zeros((1, 127), F32)], axis=1)
    dmod = jnp.concatenate([dshift1, dscale1, dgate1, dshift2, dscale2, dgate2], axis=1)
    small = jnp.concatenate([dmod, d_norm1_g, d_pool_scale, d_conv_b, d_gnorm_pool, d_gnorm_conv, d_norm2_g,
                             d_final_g, d_conv_w.reshape(1, 3 * width), loss_pad], axis=1)
    p_wmix, small_all = _exchange("exchange_small_grads", [g_wmix_split, small], ["a2a", "gather"],
                                  deps=(out_w_in[0],))
    mix_shape = (N_POOL_GROUPS * rows_mix, gd)
    out_wmix = _reduce_adam("adam_pool_mix", p_wmix, pool_mix_w.reshape(mix_shape), m_pool_mix_w.reshape(mix_shape),
                            v_pool_mix_w.reshape(mix_shape))
    out_wmix = [a.reshape(pool_mix_w.shape) for a in out_wmix]
    small_all = small_all.reshape(N_DEV, small.shape[1])
    small_sum = _sum_devices(small_all)

    n_rep = (N_MOD + 1) * d + 4 * width + 2 * d
    loss = small_sum[0, n_rep + 3 * width]
    rep_names_w = [b_ada, norm1_g, pool_scale, conv_b, gnorm_pool_g, gnorm_conv_g, norm2_g, final_g.reshape(1, d)]
    rep_names_m = [m_b_ada, m_norm1_g, m_pool_scale, m_conv_b, m_gnorm_pool_g, m_gnorm_conv_g, m_norm2_g,
                   m_final_g.reshape(1, d)]
    rep_names_v = [v_b_ada, v_norm1_g, v_pool_scale, v_conv_b, v_gnorm_pool_g, v_gnorm_conv_g, v_norm2_g,
                   v_final_g.reshape(1, d)]
    rep_grad = small_sum[:, :n_rep]
    rep_delta, rep_m, rep_v = _adam_small("adam_replicated", rep_grad, jnp.concatenate(rep_names_w, axis=1),
                                          jnp.concatenate(rep_names_m, axis=1), jnp.concatenate(rep_names_v, axis=1))

    def split_rep(vec):
        out, off = [], 0
        for wgt in rep_names_w:
            n = wgt.shape[1]
            out.append(vec[:, off:off + n])
            off += n
        out[-1] = out[-1].reshape(d)
        return out

    conv_grad_full = small_sum[:, n_rep:n_rep + 3 * width].reshape(3, width)
    g_conv_w = lax.dynamic_slice(conv_grad_full, (0, me * conv_cols), (3, conv_cols))
    g_conv_w8 = jnp.concatenate([g_conv_w, jnp.zeros((5, conv_cols), F32)], axis=0)

    def pad8(a):
        return jnp.concatenate([a[0], jnp.zeros((5, conv_cols), F32)], axis=0)

    conv_delta, conv_m, conv_v = _adam_small("adam_conv_w", g_conv_w8, pad8(conv_w), pad8(m_conv_w), pad8(v_conv_w))

    dmod_all = small_all[:, :N_MOD * d]
    dmod_mine = lax.dynamic_slice(dmod_all, (0, me * ada_cols), (N_DEV, ada_cols))
    dmod_rows = jnp.concatenate([dmod_mine, jnp.zeros((N_DEV, ada_cols), F32)], axis=0)
    out_ada = _ada_bwd_adam(jnp.transpose(c_rows), dmod_rows, w_ada[0], m_w_ada[0], v_w_ada[0])

    rep_all = [split_rep(rep_grad), split_rep(rep_delta), split_rep(rep_m), split_rep(rep_v)]
    conv_all = [g_conv_w[None], conv_delta[None, :3], conv_m[None, :3], conv_v[None, :3]]
    outs = [loss, grad_x[None]]
    for kind in range(4):
        b_ada_o, norm1_o, pool_scale_o, conv_b_o, gpool_o, gconv_o, norm2_o, final_o = rep_all[kind]
        outs += [out_ada[kind][None], b_ada_o, norm1_o, out_w_in[kind][None], out_wmix[kind], pool_scale_o,
                 conv_all[kind], conv_b_o, gpool_o, gconv_o, out_w_out[kind][None], norm2_o, out_w1[kind][None],
                 out_w2[kind][None], final_o]
    return tuple(outs)
```

```python
import jax
import jax.numpy as jnp
from jax import lax
from jax.experimental import pallas as pl
from jax.experimental.pallas import tpu as pltpu

F32 = jnp.float32
BF16 = jnp.bfloat16
MESH = pl.DeviceIdType.MESH

N_DEV = 8
N_MOD = 6
EPS = 1e-6
POOL_WINDOWS = (2, 4, 8, 16)
N_POOL_GROUPS = len(POOL_WINDOWS)
CONV_HEAD_DIM = 128
PAD_ROWS = 16

ADAM_LR = 0.001
ADAM_B1 = 0.9
ADAM_B2 = 0.999
ADAM_EPS = 1e-08
ADAM_WD = 0.01
ADAM_STEP = 10

VMEM_LIMIT_BYTES = 56 * 1024 * 1024
MM_TM, MM_TN, MM_TK = 1024, 512, 4096


def _pallas(body, deps=(), **kw):
    if not deps:
        return pl.pallas_call(body, **kw)
    n_in = len(kw["in_specs"])

    def with_deps(*refs):
        body(*refs[:n_in], *refs[n_in + len(deps):])

    kw["in_specs"] = list(kw["in_specs"]) + [pl.BlockSpec(memory_space=pl.ANY)] * len(deps)
    call = pl.pallas_call(with_deps, **kw)
    return lambda *operands: call(*operands, *deps)


def _params(*sem):
    return pltpu.CompilerParams(dimension_semantics=sem, vmem_limit_bytes=VMEM_LIMIT_BYTES)


def _tile(pref, dim):
    if dim <= pref:
        return dim
    for t in range(pref - pref % 128, 0, -128):
        if dim % t == 0:
            return t
    return dim


def _exchange(name, arrays, modes, deps=()):
    n = len(arrays)
    out_shape = []
    for a, mode in zip(arrays, modes):
        piece = a.shape if mode == "gather" else a.shape[1:]
        out_shape.append(jax.ShapeDtypeStruct((N_DEV,) + tuple(piece), a.dtype))

    def body(*refs):
        srcs, dsts = refs[:n], refs[n:2 * n]
        send_sems, recv_sems, local_sems = refs[2 * n:]
        x, y, c = lax.axis_index("x"), lax.axis_index("y"), lax.axis_index("c")
        me = 4 * x + 2 * y + c
        copies = []
        for i in range(n):
            gather = modes[i] == "gather"
            local = pltpu.make_async_copy(srcs[i] if gather else srcs[i].at[me], dsts[i].at[me], local_sems.at[i])
            local.start()
            copies.append(local)
            for k in range(1, N_DEV):
                kx, ky, kc = (k >> 2) & 1, (k >> 1) & 1, k & 1
                peer = (1 - x if kx else x, 1 - y if ky else y, 1 - c if kc else c)
                peer_idx = 4 * peer[0] + 2 * peer[1] + peer[2]
                remote = pltpu.make_async_remote_copy(
                    src_ref=srcs[i] if gather else srcs[i].at[peer_idx],
                    dst_ref=dsts[i].at[me],
                    send_sem=send_sems.at[i * (N_DEV - 1) + k - 1],
                    recv_sem=recv_sems.at[i * (N_DEV - 1) + k - 1],
                    device_id=peer, device_id_type=MESH)
                remote.start()
                copies.append(remote)
        for cp in copies:
            cp.wait()

    any_spec = pl.BlockSpec(memory_space=pl.ANY)
    return _pallas(
        body, deps, name=name, out_shape=out_shape,
        in_specs=[any_spec] * n, out_specs=[any_spec] * n,
        scratch_shapes=[pltpu.SemaphoreType.DMA((n * (N_DEV - 1),)),
                        pltpu.SemaphoreType.DMA((n * (N_DEV - 1),)),
                        pltpu.SemaphoreType.DMA((n,))],
    )(*arrays)


_HBM = pl.BlockSpec(memory_space=pltpu.HBM)
_SEM = pl.BlockSpec(memory_space=pltpu.SEMAPHORE)
_TOKEN = pl.BlockSpec(memory_space=pltpu.VMEM)
_EFFECT = pltpu.SideEffectType.DATAFLOW_SIDE_EFFECTING
N_CHIP = N_DEV // 2
_OTHER_CHIPS = (1, 2, 3)


def _place():
    x, y, c = lax.axis_index("x"), lax.axis_index("y"), lax.axis_index("c")
    return x, y, c, (x, y, 1 - c)


def _same_core_of(x, y, c, k):
    px = 1 - x if k & 2 else x
    py = 1 - y if k & 1 else y
    return (px, py, c), 2 * px + py


def _remote(src, dst, send_sem, recv_sem, device):
    return pltpu.make_async_remote_copy(src_ref=src, dst_ref=dst, send_sem=send_sem, recv_sem=recv_sem,
                                        device_id=device, device_id_type=MESH)


def _token_shape():
    return jax.ShapeDtypeStruct((8, 128), F32)


def _split_call(body, deps, name, operands, in_specs, out_shape, out_specs, aliases):
    return _pallas(body, deps, name=name, out_shape=out_shape, in_specs=in_specs, out_specs=out_specs,
                   input_output_aliases=aliases,
                   compiler_params=pltpu.CompilerParams(has_side_effects=_EFFECT))(*operands)


def _routes(x, y, c):
    first = (x + c - 2 * x * c, y + (1 - c) - 2 * y * (1 - c), c)
    second = (x + (1 - c) - 2 * x * (1 - c), y + c - 2 * y * c, c)
    return first, second, (1 - x, 1 - y, c)


def _index_of(device):
    return 4 * device[0] + 2 * device[1] + device[2]


def _gather_start(name, land, deps):
    def body(land_ref, send_sems, recv_first, recv_second, recv_d2d, land_thru, token):
        del land_thru
        x, y, c, sibling = _place()
        first, second, _ = _routes(x, y, c)
        mine = land_ref.at[4 * x + 2 * y + c]
        _remote(mine, mine, send_sems.at[0], recv_first.at[0], first).start()
        _remote(mine, mine, send_sems.at[1], recv_second.at[0], second).start()
        _remote(mine, mine, send_sems.at[2], recv_d2d.at[0], sibling).start()
        token[...] = jnp.zeros_like(token)

    one = pltpu.SemaphoreType.DMA((1,))
    return _split_call(
        body, deps, name, (pltpu.with_memory_space_constraint(land, pltpu.HBM),), (_HBM,),
        (pltpu.SemaphoreType.DMA((3,)), one, one, one, pltpu.HBM(land.shape, land.dtype), _token_shape()),
        (_SEM, _SEM, _SEM, _SEM, _HBM, _TOKEN), {0: 4})


def _gather_wait_local(name, started, land, after):
    recv_d2d = started[3]

    def body(land_ref, recv_d2d, after_ref, land_out):
        del after_ref, land_out
        x, y, c, sibling = _place()
        mine = land_ref.at[4 * x + 2 * y + c]
        _remote(mine, mine, recv_d2d.at[0], recv_d2d.at[0], sibling).wait_recv()

    return _split_call(
        body, (), name, (land, recv_d2d, after), (_HBM, _SEM, pl.BlockSpec(memory_space=pl.ANY)),
        (pltpu.HBM(land.shape, land.dtype),), (_HBM,), {0: 0})[0]


def _gather_hop(name, started, land, after):
    recv_first = started[1]

    def body(land_ref, recv_first, after_ref, send_hop, recv_hop, land_thru, token):
        del after_ref, land_thru
        x, y, c, _ = _place()
        first, second, _ = _routes(x, y, c)
        piece = land_ref.at[_index_of(first)]
        _remote(piece, piece, send_hop.at[0], recv_first.at[0], first).wait_recv()
        _remote(piece, piece, send_hop.at[0], recv_hop.at[0], second).start()
        token[...] = jnp.zeros_like(token)

    one = pltpu.SemaphoreType.DMA((1,))
    return _split_call(
        body, (), name, (land, recv_first, after), (_HBM, _SEM, pl.BlockSpec(memory_space=pl.ANY)),
        (one, one, pltpu.HBM(land.shape, land.dtype), _token_shape()), (_SEM, _SEM, _HBM, _TOKEN), {0: 2})


def _gather_relay(name, started, hopped, after):
    recv_second = started[2]
    _, recv_hop, land, _ = hopped

    def body(land_ref, recv_second, recv_hop, after_ref, send_fwd, recv_fwd, land_thru, token):
        del after_ref, land_thru
        token[...] = jnp.zeros_like(token)
        x, y, c, sibling = _place()
        mine = land_ref.at[4 * x + 2 * y + c]
        _remote(mine, mine, send_fwd.at[0], recv_second.at[0], sibling).wait_recv()
        _remote(mine, mine, send_fwd.at[0], recv_hop.at[0], sibling).wait_recv()
        for i, device in enumerate(_routes(x, y, c)):
            piece = land_ref.at[_index_of(device)]
            _remote(piece, piece, send_fwd.at[i], recv_fwd.at[i], sibling).start()

    return _split_call(
        body, (), name, (land, recv_second, recv_hop, after), (_HBM, _SEM, _SEM, pl.BlockSpec(memory_space=pl.ANY)),
        (pltpu.SemaphoreType.DMA((3,)), pltpu.SemaphoreType.DMA((3,)), pltpu.HBM(land.shape, land.dtype),
         _token_shape()),
        (_SEM, _SEM, _HBM, _TOKEN), {0: 2})


def _gather_wait(name, started, hopped, relayed, land, after, local_waited):
    send_sems, recv_d2d = started[0], started[3]
    send_hop = hopped[0]
    send_fwd, recv_fwd = relayed[0], relayed[1]

    def body(land_ref, send_sems, recv_d2d, send_hop, send_fwd, recv_fwd, after_ref, land_out):
        del after_ref, land_out
        x, y, c, sibling = _place()
        mine = land_ref.at[4 * x + 2 * y + c]
        for i in range(3):
            _remote(mine, mine, send_sems.at[i], recv_d2d.at[0], sibling).wait_send()
        _remote(mine, mine, send_hop.at[0], recv_d2d.at[0], sibling).wait_send()
        if not local_waited:
            _remote(mine, mine, send_sems.at[2], recv_d2d.at[0], sibling).wait_recv()
        for i in range(3):
            relay = _remote(mine, mine, send_fwd.at[i], recv_fwd.at[i], sibling)
            relay.wait_send()
            relay.wait_recv()

    return _split_call(
        body, (), name, (land, send_sems, recv_d2d, send_hop, send_fwd, recv_fwd, after),
        (_HBM, _SEM, _SEM, _SEM, _SEM, _SEM, pl.BlockSpec(memory_space=pl.ANY)),
        (pltpu.HBM(land.shape, land.dtype),), (_HBM,), {0: 0})[0]


def _landing(own, me):
    land = lax.empty((N_DEV,) + own.shape, own.dtype)
    return lax.dynamic_update_slice(land, own[None], (me,) + (0,) * own.ndim)


def _pair_start(name, far, deps=()):
    pair = lax.empty(far.shape, far.dtype)

    def body(far_ref, pair_ref, send_sems, recv_sems, far_thru, pair_thru, token):
        del far_thru, pair_thru
        _remote(far_ref, pair_ref, send_sems.at[0], recv_sems.at[0], _place()[3]).start()
        token[...] = jnp.zeros_like(token)

    return _split_call(
        body, deps, name,
        (pltpu.with_memory_space_constraint(far, pltpu.HBM), pltpu.with_memory_space_constraint(pair, pltpu.HBM)),
        (_HBM, _HBM),
        (pltpu.SemaphoreType.DMA((1,)), pltpu.SemaphoreType.DMA((1,)),
         pltpu.HBM(far.shape, far.dtype), pltpu.HBM(pair.shape, pair.dtype), _token_shape()),
        (_SEM, _SEM, _HBM, _HBM, _TOKEN), {0: 2, 1: 3})


def _pair_wait(name, started, after):
    send_sems, recv_sems, far, pair, _ = started

    def body(far_ref, pair_ref, send_sems, recv_sems, after_ref, far_out, pair_out):
        del after_ref, far_out, pair_out
        cp = _remote(far_ref, pair_ref, send_sems.at[0], recv_sems.at[0], _place()[3])
        cp.wait_send()
        cp.wait_recv()

    return _split_call(
        body, (), name, (far, pair, send_sems, recv_sems, after),
        (_HBM, _HBM, _SEM, _SEM, pl.BlockSpec(memory_space=pl.ANY)),
        (pltpu.HBM(far.shape, far.dtype), pltpu.HBM(pair.shape, pair.dtype)), (_HBM, _HBM), {0: 0, 1: 1})[1]


def _chip_start(name, sums, deps=()):
    land = lax.empty((3,) + sums.shape[1:], sums.dtype)

    def body(s_ref, land_ref, send_sems, recv_sems, s_thru, land_thru, token):
        del s_thru, land_thru
        x, y, c, _ = _place()
        for k in _OTHER_CHIPS:
            peer, chip = _same_core_of(x, y, c, k)
            _remote(s_ref.at[chip], land_ref.at[k - 1], send_sems.at[k - 1], recv_sems.at[k - 1], peer).start()
        token[...] = jnp.zeros_like(token)

    return _split_call(
        body, deps, name,
        (pltpu.with_memory_space_constraint(sums, pltpu.HBM), pltpu.with_memory_space_constraint(land, pltpu.HBM)),
        (_HBM, _HBM),
        (pltpu.SemaphoreType.DMA((3,)), pltpu.SemaphoreType.DMA((3,)),
         pltpu.HBM(sums.shape, sums.dtype), pltpu.HBM(land.shape, land.dtype), _token_shape()),
        (_SEM, _SEM, _HBM, _HBM, _TOKEN), {0: 2, 1: 3})


def _chip_wait(name, started, after):
    send_sems, recv_sems, sums, land, _ = started

    def body(s_ref, land_ref, send_sems, recv_sems, after_ref, s_out, land_out):
        del after_ref, s_out, land_out
        x, y, c, _ = _place()
        for k in _OTHER_CHIPS:
            peer, chip = _same_core_of(x, y, c, k)
            cp = _remote(s_ref.at[chip], land_ref.at[k - 1], send_sems.at[k - 1], recv_sems.at[k - 1], peer)
            cp.wait_send()
            cp.wait_recv()

    return _split_call(
        body, (), name, (sums, land, send_sems, recv_sems, after),
        (_HBM, _HBM, _SEM, _SEM, pl.BlockSpec(memory_space=pl.ANY)),
        (pltpu.HBM(sums.shape, sums.dtype), pltpu.HBM(land.shape, land.dtype)), (_HBM, _HBM), {0: 0, 1: 1})


_DOT_DIMS = {"nn": (((1,), (0,)), ((), ())), "nt": (((1,), (1,)), ((), ())), "tn": (((0,), (0,)), ((), ()))}


def _matmul(name, mode, operands, in_specs, out_shape, out_specs, grid, acc_shape, epilogue, deps=(), carry=()):
    n_in, n_out, nk = len(operands), len(out_shape), grid[2]
    dims = _DOT_DIMS[mode]

    def body(*refs):
        a_ref, b_ref = refs[0], refs[1]
        extras, outs = refs[2:n_in], refs[n_in:n_in + n_out]
        part = lax.dot_general(a_ref[...], b_ref[...], dims, preferred_element_type=F32)
        if nk == 1:
            epilogue(part, extras, outs)
            return
        acc = refs[-1]
        k = pl.program_id(2)

        @pl.when(k == 0)
        def _():
            acc[...] = part

        @pl.when(jnp.logical_and(k > 0, k < nk - 1))
        def _():
            acc[...] += part

        @pl.when(k == nk - 1)
        def _():
            epilogue(acc[...] + part, extras, outs)

    aliases = {n_in + len(deps) + i: i for i in range(len(carry))}
    return _pallas(body, tuple(deps) + tuple(carry), name=name, grid=grid, in_specs=in_specs, out_specs=out_specs,
                   out_shape=out_shape, input_output_aliases=aliases,
                   scratch_shapes=[pltpu.VMEM(acc_shape, F32)] if nk > 1 else [],
                   compiler_params=_params("parallel", "parallel", "arbitrary"))(*operands)


def _store(dtype):
    def epilogue(acc, extras, outs):
        outs[0][...] = acc.astype(dtype)
    return epilogue


def _residual_epilogue(acc, extras, outs):
    x_ref, gate_ref = extras
    outs[0][...] = acc
    outs[1][...] = x_ref[...] + gate_ref[...] * acc


def _relu2_epilogue(acc, extras, outs):
    r = jnp.maximum(acc, 0.0)
    outs[0][...] = r.astype(outs[0].dtype)
    outs[1][...] = (r * r).astype(outs[1].dtype)


def _relu2_bwd_epilogue(acc, extras, outs):
    outs[0][...] = (acc * (2.0 * extras[0][...].astype(F32))).astype(outs[0].dtype)


def _no_extra_specs(tm, tn):
    return []


def _mm_nn(name, a, b, n_total, b_split, out_shape, epilogue, extras=(), extra_specs=_no_extra_specs, tm=MM_TM, tn=MM_TN, tk=MM_TK,
           deps=(), pieces=None, carry=()):
    m, kdim = a.shape
    tm, tk = _tile(tm, m), _tile(tk, kdim)
    n_blocks = None
    if b_split:
        piece = b.shape[2]
        tn = _tile(tn, piece)
        per = piece // tn
        if pieces is None:
            b_spec = pl.BlockSpec((None, tk, tn), lambda i, j, k: (j // per, k, j % per))
            out_spec = pl.BlockSpec((tm, tn), lambda i, j, k: (i, j))
        else:
            first, count = pieces
            n_blocks = count * per

            def which(j):
                return (first() + j // per) % N_DEV

            b_spec = pl.BlockSpec((None, tk, tn), lambda i, j, k: (which(j), k, j % per))
            out_spec = pl.BlockSpec((tm, tn), lambda i, j, k: (i, which(j) * per + j % per))
    else:
        tn = _tile(tn, n_total)
        b_spec = pl.BlockSpec((tk, tn), lambda i, j, k: (k, j))
        out_spec = pl.BlockSpec((tm, tn), lambda i, j, k: (i, j))
    if n_blocks is None:
        n_blocks = n_total // tn
    in_specs = [pl.BlockSpec((tm, tk), lambda i, j, k: (i, k)), b_spec] + list(extra_specs(tm, tn))
    return _matmul(name, "nn", (a, b) + tuple(extras), in_specs, out_shape, [out_spec] * len(out_shape),
                   (m // tm, n_blocks, kdim // tk), (tm, tn), epilogue, deps, carry)


def _mm_nt(name, a, b, n_total, b_split, out_shape, epilogue, extras=(), extra_specs=_no_extra_specs, tm=MM_TM, tn=MM_TN, tk=MM_TK,
           deps=()):
    m, kdim = a.shape
    tm, tn = _tile(tm, m), _tile(tn, n_total)
    if b_split:
        piece = b.shape[2]
        tk = _tile(tk, piece)
        per = piece // tk
        b_spec = pl.BlockSpec((None, tn, tk), lambda i, j, k: (k // per, j, k % per))
    else:
        tk = _tile(tk, kdim)
        b_spec = pl.BlockSpec((tn, tk), lambda i, j, k: (j, k))
    in_specs = [pl.BlockSpec((tm, tk), lambda i, j, k: (i, k)), b_spec] + list(extra_specs(tm, tn))
    out_specs = [pl.BlockSpec((tm, tn), lambda i, j, k: (i, j)) for _ in out_shape]
    return _matmul(name, "nt", (a, b) + tuple(extras), in_specs, out_shape, out_specs,
                   (m // tm, n_total // tn, kdim // tk), (tm, tn), epilogue, deps)


def _add_pair_epilogue(acc, extras, outs):
    outs[0][...] = (acc + extras[0][...].astype(F32)).astype(outs[0].dtype)


def _mm_tn_half(name, a, b, col_pieces, near, pair=None, tm=MM_TM, tn=MM_TN, tk=MM_TK, deps=()):
    kdim, m = a.shape
    n_total = b.shape[1]
    tk = _tile(tk, kdim)

    def core():
        c = lax.axis_index("c")
        return c if near else 1 - c

    if col_pieces:
        piece = n_total // N_DEV
        tm, tn = _tile(tm, m), _tile(tn, piece)
        per = piece // tn
        grid = (m // tm, N_CHIP * per, kdim // tk)
        a_spec = pl.BlockSpec((tk, tm), lambda i, j, k: (k, i))
        b_spec = pl.BlockSpec((tk, tn), lambda i, j, k: (k, (2 * (j // per) + core()) * per + j % per))
        out_spec = pl.BlockSpec((None, tm, tn), lambda i, j, k: (j // per, i, j % per))
        out_shape = [jax.ShapeDtypeStruct((N_CHIP, m, piece), BF16)]
    else:
        piece = m // N_DEV
        tm, tn = _tile(tm, piece), _tile(tn, n_total)
        per = piece // tm
        grid = (N_CHIP * per, n_total // tn, kdim // tk)
        a_spec = pl.BlockSpec((tk, tm), lambda i, j, k: (k, (2 * (i // per) + core()) * per + i % per))
        b_spec = pl.BlockSpec((tk, tn), lambda i, j, k: (k, j))
        out_spec = pl.BlockSpec((None, tm, tn), lambda i, j, k: (i // per, i % per, j))
        out_shape = [jax.ShapeDtypeStruct((N_CHIP, piece, n_total), BF16)]
    operands, in_specs, epilogue = (a, b), [a_spec, b_spec], _store(BF16)
    if pair is not None:
        operands, in_specs, epilogue = (a, b, pair), [a_spec, b_spec, out_spec], _add_pair_epilogue
    return _matmul(name, "tn", operands, in_specs, out_shape, [out_spec], grid, (tm, tn), epilogue, deps)[0]


def _rms(xv):
    return lax.rsqrt(jnp.mean(xv * xv, axis=-1, keepdims=True) + EPS)


def _colsum(v):
    return jnp.sum(v, axis=0, keepdims=True)


def _norm_mod(name, x, g, scale, shift, tr=256, deps=()):
    s, d = x.shape
    tr = _tile(tr, s)

    def body(x_ref, g_ref, sc_ref, sh_ref, h_ref):
        xv = x_ref[...]
        h = (xv * _rms(xv)) * g_ref[...]
        h_ref[...] = (h * (1.0 + sc_ref[...]) + sh_ref[...]).astype(h_ref.dtype)

    row = pl.BlockSpec((tr, d), lambda i: (i, 0))
    vec = pl.BlockSpec((1, d), lambda i: (0, 0))
    return _pallas(body, deps, name=name, grid=(s // tr,), in_specs=[row, vec, vec, vec], out_specs=row,
                   out_shape=jax.ShapeDtypeStruct((s, d), BF16), compiler_params=_params("parallel"))(x, g, scale, shift)


def _loss_head(x3, target, gf, gate2, mlp, tr=128):
    s, d = x3.shape
    tr = _tile(tr, s)

    def body(x_ref, t_ref, gf_ref, gate_ref, mlp_ref, dx_ref, dbr_ref, dgf_ref, dgate_ref, loss_ref):
        @pl.when(pl.program_id(0) == 0)
        def _():
            dgf_ref[...] = jnp.zeros_like(dgf_ref)
            dgate_ref[...] = jnp.zeros_like(dgate_ref)
            loss_ref[...] = jnp.zeros_like(loss_ref)

        xv = x_ref[...]
        r = _rms(xv)
        xn = xv * r
        gfv = gf_ref[...]
        err = xn * gfv - t_ref[...]
        loss_ref[...] += 0.5 * _colsum(jnp.mean(err * err, axis=-1, keepdims=True))
        dy = err * (1.0 / d)
        dgf_ref[...] += _colsum(dy * xn)
        dxn = dy * gfv
        dx = r * (dxn - xn * jnp.mean(dxn * xn, axis=-1, keepdims=True))
        dx_ref[...] = dx
        dbr_ref[...] = (dx * gate_ref[...]).astype(dbr_ref.dtype)
        dgate_ref[...] += _colsum(dx * mlp_ref[...])

    row = pl.BlockSpec((tr, d), lambda i: (i, 0))
    vec = pl.BlockSpec((1, d), lambda i: (0, 0))
    return _pallas(
        body, name="loss_head", grid=(s // tr,), in_specs=[row, row, vec, vec, row],
        out_specs=[row, row, vec, vec, pl.BlockSpec((1, 128), lambda i: (0, 0))],
        out_shape=[jax.ShapeDtypeStruct((s, d), F32), jax.ShapeDtypeStruct((s, d), BF16),
                   jax.ShapeDtypeStruct((1, d), F32), jax.ShapeDtypeStruct((1, d), F32),
                   jax.ShapeDtypeStruct((1, 128), F32)],
        compiler_params=_params("arbitrary"))(x3, target, gf, gate2, mlp)


def _norm_mod_bwd(name, dh, xin, g, scale, dx_up, branch=None, gate=None, tr=128, deps=()):
    s, d = xin.shape
    tr = _tile(tr, s)
    with_gate = branch is not None

    def body(*refs):
        dh_ref, x_ref, g_ref, sc_ref, up_ref = refs[:5]
        if with_gate:
            br_ref, gate_ref = refs[5:7]
            dx_ref, dsh_ref, dsc_ref, dg_ref, dbr_ref, dgate_ref = refs[7:]
            sums = (dsh_ref, dsc_ref, dg_ref, dgate_ref)
        else:
            dx_ref, dsh_ref, dsc_ref, dg_ref = refs[5:]
            sums = (dsh_ref, dsc_ref, dg_ref)

        @pl.when(pl.program_id(0) == 0)
        def _():
            for ref in sums:
                ref[...] = jnp.zeros_like(ref)

        xv, dhv, gv = x_ref[...], dh_ref[...], g_ref[...]
        r = _rms(xv)
        xn = xv * r
        one_sc = 1.0 + sc_ref[...]
        dsh_ref[...] += _colsum(dhv)
        dsc_ref[...] += _colsum(dhv * (xn * gv))
        dg_ref[...] += _colsum(dhv * one_sc * xn)
        dxn = dhv * one_sc * gv
        dx = up_ref[...] + r * (dxn - xn * jnp.mean(dxn * xn, axis=-1, keepdims=True))
        dx_ref[...] = dx
        if with_gate:
            dbr_ref[...] = (dx * gate_ref[...]).astype(dbr_ref.dtype)
            dgate_ref[...] += _colsum(dx * br_ref[...])

    row = pl.BlockSpec((tr, d), lambda i: (i, 0))
    vec = pl.BlockSpec((1, d), lambda i: (0, 0))
    vshape = jax.ShapeDtypeStruct((1, d), F32)
    operands = [dh, xin, g, scale, dx_up]
    in_specs = [row, row, vec, vec, row]
    out_shape = [jax.ShapeDtypeStruct((s, d), F32), vshape, vshape, vshape]
    out_specs = [row, vec, vec, vec]
    if with_gate:
        operands += [branch, gate]
        in_specs += [row, vec]
        out_shape += [jax.ShapeDtypeStruct((s, d), BF16), vshape]
        out_specs += [row, vec]
    return _pallas(body, deps, name=name, grid=(s // tr,), in_specs=in_specs, out_specs=out_specs, out_shape=out_shape,
                   compiler_params=_params("arbitrary"))(*operands)


def _window_count(c0, rows, half, s):
    t = c0 + lax.broadcasted_iota(jnp.int32, (rows, 1), 0)
    return (jnp.minimum(t + half, s) - jnp.maximum(t - half, 0)).astype(F32)


def _zero_pads(pad, s):
    zeros = jnp.zeros((PAD_ROWS, pad.shape[1]), pad.dtype)
    pad[0:PAD_ROWS, :] = zeros
    pad[PAD_ROWS + s:PAD_ROWS + s + PAD_ROWS, :] = zeros


def _pool_fwd(proj, s, gd, cb, ch, deps=()):
    nsub = gd // cb

    def body(v_ref, o_ref, pad):
        g = pl.program_id(0)
        _zero_pads(pad, s)
        pad[PAD_ROWS:PAD_ROWS + s, :] = v_ref[...]
        for gi, window in enumerate(POOL_WINDOWS):
            half = window // 2

            @pl.when(g == gi)
            def _(half=half):
                for c0 in range(0, s, ch):
                    base = PAD_ROWS + c0
                    acc = pad[base - half:base - half + ch, :]
                    for j in range(-half + 1, half):
                        acc = acc + pad[base + j:base + j + ch, :]
                    out = acc / _window_count(c0, ch, half, s) - v_ref[c0:c0 + ch, :]
                    o_ref[c0:c0 + ch, :] = out.astype(o_ref.dtype)

    spec = pl.BlockSpec((s, cb), lambda g, j: (0, g * nsub + j))
    return _pallas(body, deps, name="pool_fwd", grid=(N_POOL_GROUPS, nsub), in_specs=[spec], out_specs=spec,
                   out_shape=jax.ShapeDtypeStruct((s, N_POOL_GROUPS * gd), BF16),
                   scratch_shapes=[pltpu.VMEM((s + 2 * PAD_ROWS, cb), F32)],
                   compiler_params=_params("parallel", "parallel"))(proj)


def _pool_bwd(dpooled, dproj, s, gd, cb, ch):
    nsub = gd // cb

    def body(dp_ref, dproj_in, o_ref, pad):
        del dproj_in
        g = pl.program_id(0)
        _zero_pads(pad, s)
        for gi, window in enumerate(POOL_WINDOWS):
            half = window // 2

            @pl.when(g == gi)
            def _(half=half):
                for c0 in range(0, s, ch):
                    pad[PAD_ROWS + c0:PAD_ROWS + c0 + ch, :] = dp_ref[c0:c0 + ch, :] / _window_count(c0, ch, half, s)
                for c0 in range(0, s, ch):
                    base = PAD_ROWS + c0
                    acc = pad[base - half + 1:base - half + 1 + ch, :]
                    for j in range(-half + 2, half + 1):
                        acc = acc + pad[base + j:base + j + ch, :]
                    o_ref[c0:c0 + ch, :] = (acc - dp_ref[c0:c0 + ch, :]).astype(o_ref.dtype)

    spec = pl.BlockSpec((s, cb), lambda g, j: (0, g * nsub + j))
    return _pallas(body, name="pool_bwd", grid=(N_POOL_GROUPS, nsub),
                   in_specs=[spec, pl.BlockSpec(memory_space=pl.ANY)], out_specs=spec,
                   out_shape=jax.ShapeDtypeStruct(dproj.shape, dproj.dtype), input_output_aliases={1: 0},
                   scratch_shapes=[pltpu.VMEM((s + 2 * PAD_ROWS, cb), F32)],
                   compiler_params=_params("parallel", "parallel"))(dpooled, dproj)


def _poolmix_fwd(pooled, wmix, pool_scale, gnorm_g, d_model, tm=512):
    s = pooled.shape[0]
    gd = wmix.shape[1]
    tm = _tile(tm, s)

    def body(p_ref, w_ref, ps_ref, g_ref, apre_ref, mixed_ref):
        a_pre = jnp.dot(p_ref[...], w_ref[...], preferred_element_type=F32)
        apre_ref[...] = a_pre
        a_out = a_pre * ps_ref[...]
        mixed_ref[...] = ((a_out * _rms(a_out)) * g_ref[...]).astype(mixed_ref.dtype)

    blk = pl.BlockSpec((tm, gd), lambda g, i: (i, g))
    vec = pl.BlockSpec((1, gd), lambda g, i: (0, g))
    return _pallas(body, name="poolmix_fwd", grid=(N_POOL_GROUPS, s // tm),
                   in_specs=[blk, pl.BlockSpec((None, gd, gd), lambda g, i: (g, 0, 0)), vec, vec],
                   out_specs=[blk, blk],
                   out_shape=[jax.ShapeDtypeStruct((s, N_POOL_GROUPS * gd), F32), jax.ShapeDtypeStruct((s, d_model), BF16)],
                   compiler_params=_params("parallel", "parallel"))(pooled, wmix, pool_scale, gnorm_g)


def _poolmix_bwd(dmixed, a_pre, wmix, pool_scale, gnorm_g, tm=512):
    s = a_pre.shape[0]
    gd = wmix.shape[1]
    tm = _tile(tm, s)

    def body(dm_ref, apre_ref, w_ref, ps_ref, g_ref, dapre_ref, dpooled_ref, dps_ref, dg_ref):
        @pl.when(pl.program_id(1) == 0)
        def _():
            dps_ref[...] = jnp.zeros_like(dps_ref)
            dg_ref[...] = jnp.zeros_like(dg_ref)

        a_pre, dm, ps = apre_ref[...], dm_ref[...], ps_ref[...]
        a_out = a_pre * ps
        r = _rms(a_out)
        n = a_out * r
        dg_ref[...] += _colsum(dm * n)
        dn = dm * g_ref[...]
        da_out = r * (dn - n * jnp.mean(dn * n, axis=-1, keepdims=True))
        dps_ref[...] += _colsum(da_out * a_pre)
        da_pre = (da_out * ps).astype(BF16)
        dapre_ref[...] = da_pre
        dpooled_ref[...] = lax.dot_general(da_pre, w_ref[...], _DOT_DIMS["nt"], preferred_element_type=F32)

    blk = pl.BlockSpec((tm, gd), lambda g, i: (i, g))
    vec = pl.BlockSpec((1, gd), lambda g, i: (0, g))
    width = N_POOL_GROUPS * gd
    return _pallas(body, name="poolmix_bwd", grid=(N_POOL_GROUPS, s // tm),
                   in_specs=[blk, blk, pl.BlockSpec((None, gd, gd), lambda g, i: (g, 0, 0)), vec, vec],
                   out_specs=[blk, blk, vec, vec],
                   out_shape=[jax.ShapeDtypeStruct((s, width), BF16), jax.ShapeDtypeStruct((s, width), F32),
                              jax.ShapeDtypeStruct((1, width), F32), jax.ShapeDtypeStruct((1, width), F32)],
                   compiler_params=_params("parallel", "arbitrary"))(dmixed, a_pre, wmix, pool_scale, gnorm_g)


def _poolmix_wgrad(pooled, da_pre, gd, tk=1024):
    s = pooled.shape[0]
    tk = _tile(tk, s)
    nk = s // tk

    def body(p_ref, d_ref, o_ref, acc):
        k = pl.program_id(1)
        part = lax.dot_general(p_ref[...], d_ref[...], _DOT_DIMS["tn"], preferred_element_type=F32)

        @pl.when(k == 0)
        def _():
            acc[...] = part

        @pl.when(k > 0)
        def _():
            acc[...] += part

        @pl.when(k == nk - 1)
        def _():
            o_ref[...] = acc[...].astype(o_ref.dtype)

    blk = pl.BlockSpec((tk, gd), lambda g, k: (k, g))
    return _pallas(body, name="poolmix_wgrad", grid=(N_POOL_GROUPS, nk), in_specs=[blk, blk],
                   out_specs=pl.BlockSpec((None, gd, gd), lambda g, k: (g, 0, 0)),
                   out_shape=jax.ShapeDtypeStruct((N_POOL_GROUPS, gd, gd), BF16),
                   scratch_shapes=[pltpu.VMEM((gd, gd), F32)],
                   compiler_params=_params("parallel", "arbitrary"))(pooled, da_pre)


def _head_mean(v):
    parts = []
    for q in range(v.shape[1] // CONV_HEAD_DIM):
        m = jnp.mean(v[:, q * CONV_HEAD_DIM:(q + 1) * CONV_HEAD_DIM], axis=-1, keepdims=True)
        parts.append(jnp.broadcast_to(m, (v.shape[0], CONV_HEAD_DIM)))
    return parts[0] if len(parts) == 1 else jnp.concatenate(parts, axis=1)


def _conv_fwd(proj, mixed, conv_w, conv_b, gnorm_g, s, width, cb, ch, deps=()):
    nblk = width // cb

    def body(b_ref, c_ref, u_ref, w_ref, cb_ref, g_ref, mixed_in, o_ref, pad):
        del mixed_in
        _zero_pads(pad, s)
        pad[PAD_ROWS:PAD_ROWS + s, :] = c_ref[...] * u_ref[...]
        w = w_ref[...]
        for c0 in range(0, s, ch):
            base = PAD_ROWS + c0
            conv = (w[0:1] * pad[base - 1:base - 1 + ch, :] + w[1:2] * pad[base:base + ch, :]
                    + w[2:3] * pad[base + 1:base + 1 + ch, :] + cb_ref[...])
            bo = b_ref[c0:c0 + ch, :] * conv
            n = bo * lax.rsqrt(_head_mean(bo * bo) + EPS)
            o_ref[c0:c0 + ch, :] = (n * g_ref[...]).astype(o_ref.dtype)

    def part(p):
        return pl.BlockSpec((s, cb), lambda j: (0, p * nblk + j))

    vec = pl.BlockSpec((1, cb), lambda j: (0, j))
    return _pallas(body, deps, name="conv_fwd", grid=(nblk,),
                   in_specs=[part(1), part(2), part(3), pl.BlockSpec((3, cb), lambda j: (0, j)), vec, vec,
                             pl.BlockSpec(memory_space=pl.ANY)],
                   out_specs=part(1), out_shape=jax.ShapeDtypeStruct(mixed.shape, mixed.dtype),
                   input_output_aliases={6: 0},
                   scratch_shapes=[pltpu.VMEM((s + 2 * PAD_ROWS, cb), F32)],
                   compiler_params=_params("parallel"))(proj, proj, proj, conv_w, conv_b, gnorm_g, mixed)


def _conv_bwd(dmixed, proj, conv_w, conv_b, gnorm_g, s, width, cb, ch, deps=()):
    nblk = width // cb

    def body(dm_ref, b_ref, c_ref, u_ref, w_ref, cb_ref, g_ref, dproj_ref, dw_ref, dcb_ref, dg_ref,
             pad_cu, pad_dconv, db_buf, dc_buf, du_buf, sems):
        j = pl.program_id(0)
        _zero_pads(pad_cu, s)
        _zero_pads(pad_dconv, s)
        pad_cu[PAD_ROWS:PAD_ROWS + s, :] = c_ref[...] * u_ref[...]
        w, gv = w_ref[...], g_ref[...]
        zero = jnp.zeros((1, cb), F32)
        dw0, dw1, dw2, dcb, dg = zero, zero, zero, zero, zero
        for c0 in range(0, s, ch):
            base = PAD_ROWS + c0
            cu_prev, cu_here, cu_next = (pad_cu[base - 1:base - 1 + ch, :], pad_cu[base:base + ch, :],
                                         pad_cu[base + 1:base + 1 + ch, :])
            conv = w[0:1] * cu_prev + w[1:2] * cu_here + w[2:3] * cu_next + cb_ref[...]
            bg = b_ref[c0:c0 + ch, :]
            bo = bg * conv
            r = lax.rsqrt(_head_mean(bo * bo) + EPS)
            n = bo * r
            dm = dm_ref[c0:c0 + ch, :]
            dg = dg + _colsum(dm * n)
            dn = dm * gv
            dbo = r * (dn - n * _head_mean(dn * n))
            db_buf[c0:c0 + ch, :] = (dbo * conv).astype(BF16)
            dconv = dbo * bg
            pad_dconv[base:base + ch, :] = dconv
            dcb = dcb + _colsum(dconv)
            dw0 = dw0 + _colsum(dconv * cu_prev)
            dw1 = dw1 + _colsum(dconv * cu_here)
            dw2 = dw2 + _colsum(dconv * cu_next)
        dw_ref[0:1, :] = dw0
        dw_ref[1:2, :] = dw1
        dw_ref[2:3, :] = dw2
        dcb_ref[...] = dcb
        dg_ref[...] = dg
        for c0 in range(0, s, ch):
            base = PAD_ROWS + c0
            dcu = (w[0:1] * pad_dconv[base + 1:base + 1 + ch, :] + w[1:2] * pad_dconv[base:base + ch, :]
                   + w[2:3] * pad_dconv[base - 1:base - 1 + ch, :])
            dc_buf[c0:c0 + ch, :] = (dcu * u_ref[c0:c0 + ch, :]).astype(BF16)
            du_buf[c0:c0 + ch, :] = (dcu * c_ref[c0:c0 + ch, :]).astype(BF16)
        copies = []
        for p, buf in enumerate((db_buf, dc_buf, du_buf)):
            col = pl.multiple_of((p + 1) * width + j * cb, CONV_HEAD_DIM)
            copies.append(pltpu.make_async_copy(buf, dproj_ref.at[:, pl.ds(col, cb)], sems.at[p]))
            copies[-1].start()
        for cp in copies:
            cp.wait()

    def part(p):
        return pl.BlockSpec((s, cb), lambda j: (0, p * nblk + j))

    vec = pl.BlockSpec((1, cb), lambda j: (0, j))
    w_spec = pl.BlockSpec((3, cb), lambda j: (0, j))
    return _pallas(body, deps, name="conv_bwd", grid=(nblk,),
                   in_specs=[part(1), part(1), part(2), part(3), w_spec, vec, vec],
                   out_specs=[pl.BlockSpec(memory_space=pl.ANY), w_spec, vec, vec],
                   out_shape=[jax.ShapeDtypeStruct((s, 4 * width), BF16), jax.ShapeDtypeStruct((3, width), F32),
                              jax.ShapeDtypeStruct((1, width), F32), jax.ShapeDtypeStruct((1, width), F32)],
                   scratch_shapes=[pltpu.VMEM((s + 2 * PAD_ROWS, cb), F32), pltpu.VMEM((s + 2 * PAD_ROWS, cb), F32),
                                   pltpu.VMEM((s, cb), BF16), pltpu.VMEM((s, cb), BF16), pltpu.VMEM((s, cb), BF16),
                                   pltpu.SemaphoreType.DMA((3,))],
                   compiler_params=_params("arbitrary"))(dmixed, proj, proj, proj, conv_w, conv_b, gnorm_g)


def _adamw(w, g, m, v):
    m = ADAM_B1 * m + (1.0 - ADAM_B1) * g
    v = ADAM_B2 * v + (1.0 - ADAM_B2) * (g * g)
    m_hat = m / (1.0 - ADAM_B1 ** ADAM_STEP)
    v_hat = v / (1.0 - ADAM_B2 ** ADAM_STEP)
    delta = -ADAM_LR * (m_hat / (jnp.sqrt(v_hat) + ADAM_EPS) + ADAM_WD * w)
    return delta, m, v


def _ada_fwd(c_rows, w, b, tn=512):
    rows, d = c_rows.shape
    n = w.shape[1]
    tn = _tile(tn, n)

    def body(c_ref, w_ref, b_ref, o_ref):
        cv = c_ref[...]
        act = (cv * jax.nn.sigmoid(cv)).astype(BF16)
        o_ref[...] = jnp.dot(act, w_ref[...].astype(BF16), preferred_element_type=F32) + b_ref[...]

    return _pallas(body, name="ada_fwd", grid=(n // tn,),
                   in_specs=[pl.BlockSpec((rows, d), lambda j: (0, 0)), pl.BlockSpec((d, tn), lambda j: (0, j)),
                             pl.BlockSpec((1, tn), lambda j: (0, j))],
                   out_specs=pl.BlockSpec((rows, tn), lambda j: (0, j)),
                   out_shape=jax.ShapeDtypeStruct((rows, n), F32), compiler_params=_params("parallel"))(c_rows, w, b)


def _ada_bwd_adam(c_cols, dmod, w, m, v, tr=512, tn=1024):
    d, rows = c_cols.shape
    n = w.shape[1]
    tr, tn = _tile(tr, d), _tile(tn, n)

    def body(c_ref, dm_ref, w_ref, m_ref, v_ref, g_ref, dl_ref, nm_ref, nv_ref):
        cv = c_ref[...]
        act = (cv * jax.nn.sigmoid(cv)).astype(BF16)
        g = jnp.dot(act, dm_ref[...].astype(BF16), preferred_element_type=F32)
        g_ref[...] = g
        dl_ref[...], nm_ref[...], nv_ref[...] = _adamw(w_ref[...], g, m_ref[...], v_ref[...])

    blk = pl.BlockSpec((tr, tn), lambda i, j: (i, j))
    shape = jax.ShapeDtypeStruct((d, n), F32)
    return _pallas(body, name="ada_bwd_adam", grid=(d // tr, n // tn),
                   in_specs=[pl.BlockSpec((tr, rows), lambda i, j: (i, 0)), pl.BlockSpec((rows, tn), lambda i, j: (0, j)),
                             blk, blk, blk],
                   out_specs=[blk] * 4, out_shape=[shape] * 4,
                   compiler_params=_params("parallel", "parallel"))(c_cols, dmod, w, m, v)


def _reduce_adam(name, pieces, w, m, v, tr=256, tc=1024):
    r, c = w.shape
    tr, tc = _tile(tr, r), _tile(tc, c)

    def body(p_ref, w_ref, m_ref, v_ref, g_ref, dl_ref, nm_ref, nv_ref):
        g = p_ref[0].astype(F32)
        for j in range(1, N_DEV):
            g = g + p_ref[j].astype(F32)
        g_ref[...] = g
        dl_ref[...], nm_ref[...], nv_ref[...] = _adamw(w_ref[...], g, m_ref[...], v_ref[...])

    blk = pl.BlockSpec((tr, tc), lambda i, j: (i, j))
    shape = jax.ShapeDtypeStruct((r, c), F32)
    return _pallas(body, name=name, grid=(r // tr, c // tc),
                   in_specs=[pl.BlockSpec((N_DEV, tr, tc), lambda i, j: (0, i, j)), blk, blk, blk],
                   out_specs=[blk] * 4, out_shape=[shape] * 4,
                   compiler_params=_params("parallel", "parallel"))(pieces, w, m, v)


def _reduce_adam_chips(name, sums, land, w, m, v, tr=256, tc=1024):
    r, c = w.shape
    tr, tc = _tile(tr, r), _tile(tc, c)

    def body(s_ref, l_ref, w_ref, m_ref, v_ref, g_ref, dl_ref, nm_ref, nv_ref):
        g = s_ref[...].astype(F32)
        for k in range(3):
            g = g + l_ref[k].astype(F32)
        g_ref[...] = g
        dl_ref[...], nm_ref[...], nv_ref[...] = _adamw(w_ref[...], g, m_ref[...], v_ref[...])

    blk = pl.BlockSpec((tr, tc), lambda i, j: (i, j))
    shape = jax.ShapeDtypeStruct((r, c), F32)
    mine = pl.BlockSpec((None, tr, tc), lambda i, j: (2 * lax.axis_index("x") + lax.axis_index("y"), i, j))
    return _pallas(body, name=name, grid=(r // tr, c // tc),
                   in_specs=[mine, pl.BlockSpec((3, tr, tc), lambda i, j: (0, i, j)), blk, blk, blk],
                   out_specs=[blk] * 4, out_shape=[shape] * 4,
                   compiler_params=_params("parallel", "parallel"))(sums, land, w, m, v)


def _sum_devices(parts):
    n = parts.shape[1]

    def body(p_ref, o_ref):
        acc = p_ref[0:1, :]
        for j in range(1, N_DEV):
            acc = acc + p_ref[j:j + 1, :]
        o_ref[...] = acc

    return _pallas(body, name="sum_devices", out_shape=jax.ShapeDtypeStruct((1, n), F32),
                   compiler_params=pltpu.CompilerParams(vmem_limit_bytes=VMEM_LIMIT_BYTES))(parts)


def _adam_small(name, g, w, m, v):
    def body(g_ref, w_ref, m_ref, v_ref, dl_ref, nm_ref, nv_ref):
        dl_ref[...], nm_ref[...], nv_ref[...] = _adamw(w_ref[...], g_ref[...], m_ref[...], v_ref[...])

    shape = jax.ShapeDtypeStruct(w.shape, F32)
    return _pallas(body, name=name, out_shape=[shape] * 3,
                   compiler_params=pltpu.CompilerParams(vmem_limit_bytes=VMEM_LIMIT_BYTES))(g, w, m, v)


def kernel(x, c, w_ada, b_ada, norm1_g, w_in, pool_mix_w, pool_scale, conv_w, conv_b, gnorm_pool_g, gnorm_conv_g, w_out, norm2_g, w_mlp_in, w_mlp_out, final_g, loss_target, m_w_ada, m_b_ada, m_norm1_g, m_w_in, m_pool_mix_w, m_pool_scale, m_conv_w, m_conv_b, m_gnorm_pool_g, m_gnorm_conv_g, m_w_out, m_norm2_g, m_w_mlp_in, m_w_mlp_out, m_final_g, v_w_ada, v_b_ada, v_norm1_g, v_w_in, v_pool_mix_w, v_pool_scale, v_conv_w, v_conv_b, v_gnorm_pool_g, v_gnorm_conv_g, v_w_out, v_norm2_g, v_w_mlp_in, v_w_mlp_out, v_final_g):
    s, d = x.shape[1], x.shape[2]
    width = d // 2
    gd = width // N_POOL_GROUPS
    d_ff = w_mlp_in.shape[2] * N_DEV
    n_proj = w_in.shape[2] * N_DEV
    ada_cols = w_ada.shape[2]
    conv_cols = conv_w.shape[2]
    assert n_proj == 4 * width and ada_cols * N_DEV == N_MOD * d and d_ff % N_DEV == 0
    assert width % CONV_HEAD_DIM == 0 and s % 8 == 0
    seq_chunk = _tile(512, s)
    pool_cb = _tile(256, gd)
    conv_cb = CONV_HEAD_DIM

    me = 4 * lax.axis_index("x") + 2 * lax.axis_index("y") + lax.axis_index("c")
    x2d, target = x[0], loss_target[0]

    wmix_all, conv_w_all, c_all = _exchange(
        "gather_small_weights", [pool_mix_w[0].astype(BF16), conv_w[0], c], ["gather"] * 3)
    wmix_full = jnp.transpose(wmix_all, (1, 0, 2, 3)).reshape(N_POOL_GROUPS, gd, gd)
    conv_w_full = jnp.transpose(conv_w_all, (1, 0, 2)).reshape(3, width)
    c_rows = jnp.concatenate([c_all.reshape(N_DEV, d), jnp.zeros((N_DEV, d), F32)], axis=0)

    b_mine = lax.dynamic_slice(b_ada, (0, me * ada_cols), (1, ada_cols))
    mod_part = _ada_fwd(c_rows, w_ada[0], b_mine)
    (mod_all,) = _exchange("scatter_mod", [mod_part[:N_DEV].reshape(N_DEV, 1, ada_cols)], ["a2a"])
    mod = mod_all.reshape(1, N_MOD * d)

    started, hopped, relayed = {}, {}, {}

    def gather_start(wname, wgt, deps):
        land = _landing(wgt[0].astype(BF16), me)
        started[wname] = _gather_start("gather_" + wname + "_start", land, deps)
        return started[wname][5]

    def gather_hop(wname, land, after):
        hopped[wname] = _gather_hop("gather_" + wname + "_hop", started[wname], land, after)
        return hopped[wname][3]

    def gather_relay(wname, after):
        relayed[wname] = _gather_relay("gather_" + wname + "_relay", started[wname], hopped[wname], after)
        return relayed[wname][3]

    def gather_wait(wname, land, after, local_waited=False):
        return _gather_wait("gather_" + wname + "_wait", started[wname], hopped[wname], relayed[wname], land, after,
                            local_waited)

    def first_local():
        return 2 * (2 * lax.axis_index("x") + lax.axis_index("y"))

    def first_remote():
        return first_local() + 2

    def last_remote():
        return first_local() + 5

    tok_w_in = gather_start("w_in", w_in, (mod,))
    shift1, scale1, gate1, shift2, scale2, gate2 = [mod[:, i * d:(i + 1) * d] for i in range(N_MOD)]

    h1 = _norm_mod("norm1_fwd", x2d, norm1_g, scale1, shift1, deps=(tok_w_in,))
    proj_shape = [jax.ShapeDtypeStruct((s, n_proj), F32)]
    w_in_local = _gather_wait_local("gather_w_in_local", started["w_in"], started["w_in"][4], h1)
    (proj,) = _mm_nn("in_proj_local", h1, w_in_local, n_proj, True, proj_shape, _store(F32), pieces=(first_local, 2))
    tok = gather_hop("w_in", w_in_local, proj)
    tok = gather_start("w_out", w_out, (tok,))
    tok = gather_start("w_mlp_in", w_mlp_in, (tok,))
    gather_relay("w_in", tok)
    w_in_all = gather_wait("w_in", relayed["w_in"][2], relayed["w_in"][3], True)
    (proj,) = _mm_nn("in_proj", h1, w_in_all, n_proj, True, proj_shape, _store(F32), pieces=(first_remote, 6),
                     carry=(proj,))
    tok = gather_hop("w_out", started["w_out"][4], proj)
    pooled = _pool_fwd(proj, s, gd, pool_cb, seq_chunk, deps=(tok,))
    a_pre, mixed = _poolmix_fwd(pooled, wmix_full, pool_scale, gnorm_pool_g, d)
    tok = gather_hop("w_mlp_in", started["w_mlp_in"][4], a_pre)
    tok = gather_start("w_mlp_out", w_mlp_out, (tok,))
    tok = gather_relay("w_out", tok)
    mixed = _conv_fwd(proj, mixed, conv_w_full, conv_b, gnorm_conv_g, s, width, conv_cb, seq_chunk, deps=(tok,))

    def residual_specs(tm, tn):
        return [pl.BlockSpec((tm, tn), lambda i, j, k: (i, j)), pl.BlockSpec((1, tn), lambda i, j, k: (0, j))]

    sd_f32 = jax.ShapeDtypeStruct((s, d), F32)
    w_out_full = gather_wait("w_out", relayed["w_out"][2], mixed).reshape(d, d)
    attn, x_mid = _mm_nn("out_proj", mixed, w_out_full, d, False, [sd_f32, sd_f32], _residual_epilogue,
                         extras=(x2d, gate1), extra_specs=residual_specs)
    h2 = _norm_mod("norm2_fwd", x_mid, norm2_g, scale2, shift2)
    sf_bf16 = [jax.ShapeDtypeStruct((s, d_ff), BF16)] * 2
    tok = gather_relay("w_mlp_in", h2)
    w1_local = _gather_wait_local("gather_w_mlp_in_local", started["w_mlp_in"], relayed["w_mlp_in"][2], tok)
    relu, hid = _mm_nn("mlp_in_local", h2, w1_local, d_ff, True, sf_bf16, _relu2_epilogue, pieces=(first_local, 2))
    w1_all = gather_wait("w_mlp_in", w1_local, hid, True)
    tok = gather_hop("w_mlp_out", started["w_mlp_out"][4], w1_all)
    relu, hid = _mm_nn("mlp_in_remote", h2, w1_all, d_ff, True, sf_bf16, _relu2_epilogue,
                       pieces=(first_remote, 3), carry=(relu, hid), deps=(tok,))
    tok = gather_relay("w_mlp_out", hid)
    relu, hid = _mm_nn("mlp_in", h2, w1_all, d_ff, True, sf_bf16, _relu2_epilogue,
                       pieces=(last_remote, 3), carry=(relu, hid), deps=(tok,))
    w2_full = gather_wait("w_mlp_out", relayed["w_mlp_out"][2], hid).reshape(d_ff, d)
    mlp, x_last = _mm_nn("mlp_out", hid, w2_full, d, False, [sd_f32, sd_f32], _residual_epilogue,
                         extras=(x_mid, gate2), extra_specs=residual_specs)

    dx_last, dmlp, d_final_g, dgate2, loss_row = _loss_head(x_last, target, final_g.reshape(1, d), gate2, mlp)

    def relu_specs(tm, tn):
        return [pl.BlockSpec((tm, tn), lambda i, j, k: (i, j))]

    def reduce_start(wname, a, b, col_pieces, deps=()):
        far = _mm_tn_half(wname + "_dw_far", a, b, col_pieces, near=False, deps=deps)
        return _pair_start("scatter_" + wname + "_pair_start", far)

    def reduce_chips(wname, a, b, col_pieces, pairs, after):
        pair = _pair_wait("scatter_" + wname + "_pair_wait", pairs, after)
        sums = _mm_tn_half(wname + "_dw_near", a, b, col_pieces, near=True, pair=pair)
        return _chip_start("scatter_" + wname + "_chip_start", sums)

    pairs_w2 = reduce_start("mlp_out", hid, dmlp, False)
    (dhpre,) = _mm_nt("mlp_out_dx", dmlp, w2_full, d_ff, False, sf_bf16[:1], _relu2_bwd_epilogue,
                      extras=(relu,), extra_specs=relu_specs, deps=(pairs_w2[4],))
    chips_w2 = reduce_chips("mlp_out", hid, dmlp, False, pairs_w2, dhpre)
    pairs_w1 = reduce_start("mlp_in", h2, dhpre, True, deps=(chips_w2[4],))
    (dh2,) = _mm_nt("mlp_in_dx", dhpre, w1_all, d, True, [sd_f32], _store(F32), tn=1024, deps=(pairs_w1[4],))
    chips_w1 = reduce_chips("mlp_in", h2, dhpre, True, pairs_w1, dh2)
    dx_mid, dshift2, dscale2, d_norm2_g, dattn, dgate1 = _norm_mod_bwd(
        "norm2_bwd", dh2, x_mid, norm2_g, scale2, dx_last, branch=attn, gate=gate1, deps=(chips_w1[4],))

    pairs_w_out = reduce_start("out_proj", mixed, dattn, False)
    (dmixed,) = _mm_nt("out_proj_dx", dattn, w_out_full, d, False, [sd_f32], _store(F32), deps=(pairs_w_out[4],))
    chips_w_out = reduce_chips("out_proj", mixed, dattn, False, pairs_w_out, dmixed)
    dproj, d_conv_w, d_conv_b, d_gnorm_conv = _conv_bwd(dmixed, proj, conv_w_full, conv_b, gnorm_conv_g,
                                                        s, width, conv_cb, seq_chunk, deps=(chips_w_out[4],))
    da_pre, dpooled, d_pool_scale, d_gnorm_pool = _poolmix_bwd(dmixed, a_pre, wmix_full, pool_scale, gnorm_pool_g)
    g_wmix = _poolmix_wgrad(pooled, da_pre, gd)
    dproj = _pool_bwd(dpooled, dproj, s, gd, pool_cb, seq_chunk)

    pairs_w_in = reduce_start("in_proj", h1, dproj, True)
    (dh1,) = _mm_nt("in_proj_dx", dproj, w_in_all, d, True, [sd_f32], _store(F32), tn=1024, deps=(pairs_w_in[4],))
    chips_w_in = reduce_chips("in_proj", h1, dproj, True, pairs_w_in, dh1)
    grad_x, dshift1, dscale1, d_norm1_g = _norm_mod_bwd("norm1_bwd", dh1, x2d, norm1_g, scale1, dx_mid,
                                                        deps=(chips_w_in[4],))

    sums, landed = _chip_wait("scatter_w_mlp_out_chip_wait", chips_w2, grad_x)
    out_w2 = _reduce_adam_chips("adam_w_mlp_out", sums, landed, w_mlp_out[0], m_w_mlp_out[0], v_w_mlp_out[0])
    sums, landed = _chip_wait("scatter_w_mlp_in_chip_wait", chips_w1, out_w2[0])
    out_w1 = _reduce_adam_chips("adam_w_mlp_in", sums, landed, w_mlp_in[0], m_w_mlp_in[0], v_w_mlp_in[0])
    sums, landed = _chip_wait("scatter_w_out_chip_wait", chips_w_out, out_w1[0])
    out_w_out = _reduce_adam_chips("adam_w_out", sums, landed, w_out[0], m_w_out[0], v_w_out[0])
    sums, landed = _chip_wait("scatter_w_in_chip_wait", chips_w_in, out_w_out[0])
    out_w_in = _reduce_adam_chips("adam_w_in", sums, landed, w_in[0], m_w_in[0], v_w_in[0])

    rows_mix = gd // N_DEV
    g_wmix_split = jnp.transpose(g_wmix.reshape(N_POOL_GROUPS, N_DEV, rows_mix, gd), (1, 0, 2, 3))
    g_wmix_split = g_wmix_split.reshape(N_DEV, N_POOL_GROUPS * rows_mix, gd)
    loss_pad = jnp.concatenate([loss_row[:, :1], jnp.zeros((1, 127), F32)], axis=1)
    dmod = jnp.concatenate([dshift1, dscale1, dgate1, dshift2, dscale2, dgate2], axis=1)
    small = jnp.concatenate([dmod, d_norm1_g, d_pool_scale, d_conv_b, d_gnorm_pool, d_gnorm_conv, d_norm2_g,
                             d_final_g, d_conv_w.reshape(1, 3 * width), loss_pad], axis=1)
    p_wmix, small_all = _exchange("exchange_small_grads", [g_wmix_split, small], ["a2a", "gather"],
                                  deps=(out_w_in[0],))
    mix_shape = (N_POOL_GROUPS * rows_mix, gd)
    out_wmix = _reduce_adam("adam_pool_mix", p_wmix, pool_mix_w.reshape(mix_shape), m_pool_mix_w.reshape(mix_shape),
                            v_pool_mix_w.reshape(mix_shape))
    out_wmix = [a.reshape(pool_mix_w.shape) for a in out_wmix]
    small_all = small_all.reshape(N_DEV, small.shape[1])
    small_sum = _sum_devices(small_all)

    n_rep = (N_MOD + 1) * d + 4 * width + 2 * d
    loss = small_sum[0, n_rep + 3 * width]
    rep_names_w = [b_ada, norm1_g, pool_scale, conv_b, gnorm_pool_g, gnorm_conv_g, norm2_g, final_g.reshape(1, d)]
    rep_names_m = [m_b_ada, m_norm1_g, m_pool_scale, m_conv_b, m_gnorm_pool_g, m_gnorm_conv_g, m_norm2_g,
                   m_final_g.reshape(1, d)]
    rep_names_v = [v_b_ada, v_norm1_g, v_pool_scale, v_conv_b, v_gnorm_pool_g, v_gnorm_conv_g, v_norm2_g,
                   v_final_g.reshape(1, d)]
    rep_grad = small_sum[:, :n_rep]
    rep_delta, rep_m, rep_v = _adam_small("adam_replicated", rep_grad, jnp.concatenate(rep_names_w, axis=1),
                                          jnp.concatenate(rep_names_m, axis=1), jnp.concatenate(rep_names_v, axis=1))

    def split_rep(vec):
        out, off = [], 0
        for wgt in rep_names_w:
            n = wgt.shape[1]
            out.append(vec[:, off:off + n])
            off += n
        out[-1] = out[-1].reshape(d)
        return out

    conv_grad_full = small_sum[:, n_rep:n_rep + 3 * width].reshape(3, width)
    g_conv_w = lax.dynamic_slice(conv_grad_full, (0, me * conv_cols), (3, conv_cols))
    g_conv_w8 = jnp.concatenate([g_conv_w, jnp.zeros((5, conv_cols), F32)], axis=0)

    def pad8(a):
        return jnp.concatenate([a[0], jnp.zeros((5, conv_cols), F32)], axis=0)

    conv_delta, conv_m, conv_v = _adam_small("adam_conv_w", g_conv_w8, pad8(conv_w), pad8(m_conv_w), pad8(v_conv_w))

    dmod_all = small_all[:, :N_MOD * d]
    dmod_mine = lax.dynamic_slice(dmod_all, (0, me * ada_cols), (N_DEV, ada_cols))
    dmod_rows = jnp.concatenate([dmod_mine, jnp.zeros((N_DEV, ada_cols), F32)], axis=0)
    out_ada = _ada_bwd_adam(jnp.transpose(c_rows), dmod_rows, w_ada[0], m_w_ada[0], v_w_ada[0])

    rep_all = [split_rep(rep_grad), split_rep(rep_delta), split_rep(rep_m), split_rep(rep_v)]
    conv_all = [g_conv_w[None], conv_delta[None, :3], conv_m[None, :3], conv_v[None, :3]]
    outs = [loss, grad_x[None]]
    for kind in range(4):
        b_ada_o, norm1_o, pool_scale_o, conv_b_o, gpool_o, gconv_o, norm2_o, final_o = rep_all[kind]
        outs += [out_ada[kind][None], b_ada_o, norm1_o, out_w_in[kind][None], out_wmix[kind], pool_scale_o,
                 conv_all[kind], conv_b_o, gpool_o, gconv_o, out_w_out[kind][None], norm2_o, out_w1[kind][None],
                 out_w2[kind][None], final_o]
    return tuple(outs)
```

```python
import jax
import jax.numpy as jnp
from jax import lax
from jax.experimental import pallas as pl
from jax.experimental.pallas import tpu as pltpu

F32 = jnp.float32
BF16 = jnp.bfloat16
MESH = pl.DeviceIdType.MESH

N_DEV = 8
N_MOD = 6
EPS = 1e-6
POOL_WINDOWS = (2, 4, 8, 16)
N_POOL_GROUPS = len(POOL_WINDOWS)
CONV_HEAD_DIM = 128
PAD_ROWS = 16

ADAM_LR = 0.001
ADAM_B1 = 0.9
ADAM_B2 = 0.999
ADAM_EPS = 1e-08
ADAM_WD = 0.01
ADAM_STEP = 10

VMEM_LIMIT_BYTES = 56 * 1024 * 1024
MM_TM, MM_TN, MM_TK = 1024, 512, 4096


def _pallas(body, deps=(), **kw):
    if not deps:
        return pl.pallas_call(body, **kw)
    n_in = len(kw["in_specs"])

    def with_deps(*refs):
        body(*refs[:n_in], *refs[n_in + len(deps):])

    kw["in_specs"] = list(kw["in_specs"]) + [pl.BlockSpec(memory_space=pl.ANY)] * len(deps)
    call = pl.pallas_call(with_deps, **kw)
    return lambda *operands: call(*operands, *deps)


def _params(*sem):
    return pltpu.CompilerParams(dimension_semantics=sem, vmem_limit_bytes=VMEM_LIMIT_BYTES)


def _tile(pref, dim):
    if dim <= pref:
        return dim
    for t in range(pref - pref % 128, 0, -128):
        if dim % t == 0:
            return t
    return dim


def _exchange(name, arrays, modes, deps=()):
    n = len(arrays)
    out_shape = []
    for a, mode in zip(arrays, modes):
        piece = a.shape if mode == "gather" else a.shape[1:]
        out_shape.append(jax.ShapeDtypeStruct((N_DEV,) + tuple(piece), a.dtype))

    def body(*refs):
        srcs, dsts = refs[:n], refs[n:2 * n]
        send_sems, recv_sems, local_sems = refs[2 * n:]
        x, y, c = lax.axis_index("x"), lax.axis_index("y"), lax.axis_index("c")
        me = 4 * x + 2 * y + c
        copies = []
        for i in range(n):
            gather = modes[i] == "gather"
            local = pltpu.make_async_copy(srcs[i] if gather else srcs[i].at[me], dsts[i].at[me], local_sems.at[i])
            local.start()
            copies.append(local)
            for k in range(1, N_DEV):
                kx, ky, kc = (k >> 2) & 1, (k >> 1) & 1, k & 1
                peer = (1 - x if kx else x, 1 - y if ky else y, 1 - c if kc else c)
                peer_idx = 4 * peer[0] + 2 * peer[1] + peer[2]
                remote = pltpu.make_async_remote_copy(
                    src_ref=srcs[i] if gather else srcs[i].at[peer_idx],
                    dst_ref=dsts[i].at[me],
                    send_sem=send_sems.at[i * (N_DEV - 1) + k - 1],
                    recv_sem=recv_sems.at[i * (N_DEV - 1) + k - 1],
                    device_id=peer, device_id_type=MESH)
                remote.start()
                copies.append(remote)
        for cp in copies:
            cp.wait()

    any_spec = pl.BlockSpec(memory_space=pl.ANY)
    return _pallas(
        body, deps, name=name, out_shape=out_shape,
        in_specs=[any_spec] * n, out_specs=[any_spec] * n,
        scratch_shapes=[pltpu.SemaphoreType.DMA((n * (N_DEV - 1),)),
                        pltpu.SemaphoreType.DMA((n * (N_DEV - 1),)),
                        pltpu.SemaphoreType.DMA((n,))],
    )(*arrays)


_HBM = pl.BlockSpec(memory_space=pltpu.HBM)
_SEM = pl.BlockSpec(memory_space=pltpu.SEMAPHORE)
_TOKEN = pl.BlockSpec(memory_space=pltpu.VMEM)
_EFFECT = pltpu.SideEffectType.DATAFLOW_SIDE_EFFECTING
N_CHIP = N_DEV // 2
_OTHER_CHIPS = (1, 2, 3)


def _place():
    x, y, c = lax.axis_index("x"), lax.axis_index("y"), lax.axis_index("c")
    return x, y, c, (x, y, 1 - c)


def _same_core_of(x, y, c, k):
    px = 1 - x if k & 2 else x
    py = 1 - y if k & 1 else y
    return (px, py, c), 2 * px + py


def _remote(src, dst, send_sem, recv_sem, device):
    return pltpu.make_async_remote_copy(src_ref=src, dst_ref=dst, send_sem=send_sem, recv_sem=recv_sem,
                                        device_id=device, device_id_type=MESH)


def _token_shape():
    return jax.ShapeDtypeStruct((8, 128), F32)


def _split_call(body, deps, name, operands, in_specs, out_shape, out_specs, aliases):
    return _pallas(body, deps, name=name, out_shape=out_shape, in_specs=in_specs, out_specs=out_specs,
                   input_output_aliases=aliases,
                   compiler_params=pltpu.CompilerParams(has_side_effects=_EFFECT))(*operands)


def _routes(x, y, c):
    first = (x + c - 2 * x * c, y + (1 - c) - 2 * y * (1 - c), c)
    second = (x + (1 - c) - 2 * x * (1 - c), y + c - 2 * y * c, c)
    return first, second, (1 - x, 1 - y, c)


def _index_of(device):
    return 4 * device[0] + 2 * device[1] + device[2]


def _gather_start(name, land, deps):
    def body(land_ref, send_sems, recv_first, recv_second, recv_d2d, land_thru, token):
        del land_thru
        x, y, c, sibling = _place()
        first, second, _ = _routes(x, y, c)
        mine = land_ref.at[4 * x + 2 * y + c]
        _remote(mine, mine, send_sems.at[0], recv_first.at[0], first).start()
        _remote(mine, mine, send_sems.at[1], recv_second.at[0], second).start()
        _remote(mine, mine, send_sems.at[2], recv_d2d.at[0], sibling).start()
        token[...] = jnp.zeros_like(token)

    one = pltpu.SemaphoreType.DMA((1,))
    return _split_call(
        body, deps, name, (pltpu.with_memory_space_constraint(land, pltpu.HBM),), (_HBM,),
        (pltpu.SemaphoreType.DMA((3,)), one, one, one, pltpu.HBM(land.shape, land.dtype), _token_shape()),
        (_SEM, _SEM, _SEM, _SEM, _HBM, _TOKEN), {0: 4})


def _gather_wait_local(name, started, land, after):
    recv_d2d = started[3]

    def body(land_ref, recv_d2d, after_ref, land_out):
        del after_ref, land_out
        x, y, c, sibling = _place()
        mine = land_ref.at[4 * x + 2 * y + c]
        _remote(mine, mine, recv_d2d.at[0], recv_d2d.at[0], sibling).wait_recv()

    return _split_call(
        body, (), name, (land, recv_d2d, after), (_HBM, _SEM, pl.BlockSpec(memory_space=pl.ANY)),
        (pltpu.HBM(land.shape, land.dtype),), (_HBM,), {0: 0})[0]


def _gather_hop(name, started, land, after):
    recv_first = started[1]

    def body(land_ref, recv_first, after_ref, send_hop, recv_hop, land_thru, token):
        del after_ref, land_thru
        x, y, c, _ = _place()
        first, second, _ = _routes(x, y, c)
        piece = land_ref.at[_index_of(first)]
        _remote(piece, piece, send_hop.at[0], recv_first.at[0], first).wait_recv()
        _remote(piece, piece, send_hop.at[0], recv_hop.at[0], second).start()
        token[...] = jnp.zeros_like(token)

    one = pltpu.SemaphoreType.DMA((1,))
    return _split_call(
        body, (), name, (land, recv_first, after), (_HBM, _SEM, pl.BlockSpec(memory_space=pl.ANY)),
        (one, one, pltpu.HBM(land.shape, land.dtype), _token_shape()), (_SEM, _SEM, _HBM, _TOKEN), {0: 2})


def _gather_relay(name, started, hopped, after):
    recv_second = started[2]
    _, recv_hop, land, _ = hopped

    def body(land_ref, recv_second, recv_hop, after_ref, send_fwd, recv_fwd, land_thru, token):
        del after_ref, land_thru
        token[...] = jnp.zeros_like(token)
        x, y, c, sibling = _place()
        mine = land_ref.at[4 * x + 2 * y + c]
        _remote(mine, mine, send_fwd.at[0], recv_second.at[0], sibling).wait_recv()
        _remote(mine, mine, send_fwd.at[0], recv_hop.at[0], sibling).wait_recv()
        for i, device in enumerate(_routes(x, y, c)):
            piece = land_ref.at[_index_of(device)]
            _remote(piece, piece, send_fwd.at[i], recv_fwd.at[i], sibling).start()

    return _split_call(
        body, (), name, (land, recv_second, recv_hop, after), (_HBM, _SEM, _SEM, pl.BlockSpec(memory_space=pl.ANY)),
        (pltpu.SemaphoreType.DMA((3,)), pltpu.SemaphoreType.DMA((3,)), pltpu.HBM(land.shape, land.dtype),
         _token_shape()),
        (_SEM, _SEM, _HBM, _TOKEN), {0: 2})


def _gather_wait(name, started, hopped, relayed, land, after, local_waited):
    send_sems, recv_d2d = started[0], started[3]
    send_hop = hopped[0]
    send_fwd, recv_fwd = relayed[0], relayed[1]

    def body(land_ref, send_sems, recv_d2d, send_hop, send_fwd, recv_fwd, after_ref, land_out):
        del after_ref, land_out
        x, y, c, sibling = _place()
        mine = land_ref.at[4 * x + 2 * y + c]
        for i in range(3):
            _remote(mine, mine, send_sems.at[i], recv_d2d.at[0], sibling).wait_send()
        _remote(mine, mine, send_hop.at[0], recv_d2d.at[0], sibling).wait_send()
        if not local_waited:
            _remote(mine, mine, send_sems.at[2], recv_d2d.at[0], sibling).wait_recv()
        for i in range(3):
            relay = _remote(mine, mine, send_fwd.at[i], recv_fwd.at[i], sibling)
            relay.wait_send()
            relay.wait_recv()

    return _split_call(
        body, (), name, (land, send_sems, recv_d2d, send_hop, send_fwd, recv_fwd, after),
        (_HBM, _SEM, _SEM, _SEM, _SEM, _SEM, pl.BlockSpec(memory_space=pl.ANY)),
        (pltpu.HBM(land.shape, land.dtype),), (_HBM,), {0: 0})[0]


def _landing(own, me):
    land = lax.empty((N_DEV,) + own.shape, own.dtype)
    return lax.dynamic_update_slice(land, own[None], (me,) + (0,) * own.ndim)


def _pair_start(name, far, deps=()):
    pair = lax.empty(far.shape, far.dtype)

    def body(far_ref, pair_ref, send_sems, recv_sems, far_thru, pair_thru, token):
        del far_thru, pair_thru
        _remote(far_ref, pair_ref, send_sems.at[0], recv_sems.at[0], _place()[3]).start()
        token[...] = jnp.zeros_like(token)

    return _split_call(
        body, deps, name,
        (pltpu.with_memory_space_constraint(far, pltpu.HBM), pltpu.with_memory_space_constraint(pair, pltpu.HBM)),
        (_HBM, _HBM),
        (pltpu.SemaphoreType.DMA((1,)), pltpu.SemaphoreType.DMA((1,)),
         pltpu.HBM(far.shape, far.dtype), pltpu.HBM(pair.shape, pair.dtype), _token_shape()),
        (_SEM, _SEM, _HBM, _HBM, _TOKEN), {0: 2, 1: 3})


def _pair_wait(name, started, after):
    send_sems, recv_sems, far, pair, _ = started

    def body(far_ref, pair_ref, send_sems, recv_sems, after_ref, far_out, pair_out):
        del after_ref, far_out, pair_out
        cp = _remote(far_ref, pair_ref, send_sems.at[0], recv_sems.at[0], _place()[3])
        cp.wait_send()
        cp.wait_recv()

    return _split_call(
        body, (), name, (far, pair, send_sems, recv_sems, after),
        (_HBM, _HBM, _SEM, _SEM, pl.BlockSpec(memory_space=pl.ANY)),
        (pltpu.HBM(far.shape, far.dtype), pltpu.HBM(pair.shape, pair.dtype)), (_HBM, _HBM), {0: 0, 1: 1})[1]


def _chip_start(name, sums, deps=()):
    land = lax.empty((3,) + sums.shape[1:], sums.dtype)

    def body(s_ref, land_ref, send_sems, recv_sems, s_thru, land_thru, token):
        del s_thru, land_thru
        x, y, c, _ = _place()
        for k in _OTHER_CHIPS:
            peer, chip = _same_core_of(x, y, c, k)
            _remote(s_ref.at[chip], land_ref.at[k - 1], send_sems.at[k - 1], recv_sems.at[k - 1], peer).start()
        token[...] = jnp.zeros_like(token)

    return _split_call(
        body, deps, name,
        (pltpu.with_memory_space_constraint(sums, pltpu.HBM), pltpu.with_memory_space_constraint(land, pltpu.HBM)),
        (_HBM, _HBM),
        (pltpu.SemaphoreType.DMA((3,)), pltpu.SemaphoreType.DMA((3,)),
         pltpu.HBM(sums.shape, sums.dtype), pltpu.HBM(land.shape, land.dtype), _token_shape()),
        (_SEM, _SEM, _HBM, _HBM, _TOKEN), {0: 2, 1: 3})


def _chip_wait(name, started, after):
    send_sems, recv_sems, sums, land, _ = started

    def body(s_ref, land_ref, send_sems, recv_sems, after_ref, s_out, land_out):
        del after_ref, s_out, land_out
        x, y, c, _ = _place()
        for k in _OTHER_CHIPS:
            peer, chip = _same_core_of(x, y, c, k)
            cp = _remote(s_ref.at[chip], land_ref.at[k - 1], send_sems.at[k - 1], recv_sems.at[k - 1], peer)
            cp.wait_send()
            cp.wait_recv()

    return _split_call(
        body, (), name, (sums, land, send_sems, recv_sems, after),
        (_HBM, _HBM, _SEM, _SEM, pl.BlockSpec(memory_space=pl.ANY)),
        (pltpu.HBM(sums.shape, sums.dtype), pltpu.HBM(land.shape, land.dtype)), (_HBM, _HBM), {0: 0, 1: 1})


_DOT_DIMS = {"nn": (((1,), (0,)), ((), ())), "nt": (((1,), (1,)), ((), ())), "tn": (((0,), (0,)), ((), ()))}


def _matmul(name, mode, operands, in_specs, out_shape, out_specs, grid, acc_shape, epilogue, deps=(), carry=()):
    n_in, n_out, nk = len(operands), len(out_shape), grid[2]
    dims = _DOT_DIMS[mode]

    def body(*refs):
        a_ref, b_ref = refs[0], refs[1]
        extras, outs = refs[2:n_in], refs[n_in:n_in + n_out]
        part = lax.dot_general(a_ref[...], b_ref[...], dims, preferred_element_type=F32)
        if nk == 1:
            epilogue(part, extras, outs)
            return
        acc = refs[-1]
        k = pl.program_id(2)

        @pl.when(k == 0)
        def _():
            acc[...] = part

        @pl.when(jnp.logical_and(k > 0, k < nk - 1))
        def _():
            acc[...] += part

        @pl.when(k == nk - 1)
        def _():
            epilogue(acc[...] + part, extras, outs)

    aliases = {n_in + len(deps) + i: i for i in range(len(carry))}
    return _pallas(body, tuple(deps) + tuple(carry), name=name, grid=grid, in_specs=in_specs, out_specs=out_specs,
                   out_shape=out_shape, input_output_aliases=aliases,
                   scratch_shapes=[pltpu.VMEM(acc_shape, F32)] if nk > 1 else [],
                   compiler_params=_params("parallel", "parallel", "arbitrary"))(*operands)


def _store(dtype):
    def epilogue(acc, extras, outs):
        outs[0][...] = acc.astype(dtype)
    return epilogue


def _residual_epilogue(acc, extras, outs):
    x_ref, gate_ref = extras
    outs[0][...] = acc.astype(outs[0].dtype)
    outs[1][...] = x_ref[...] + gate_ref[...] * acc


def _relu2_epilogue(acc, extras, outs):
    r = jnp.maximum(acc, 0.0)
    outs[0][...] = r.astype(outs[0].dtype)
    outs[1][...] = (r * r).astype(outs[1].dtype)


def _relu2_bwd_epilogue(acc, extras, outs):
    outs[0][...] = (acc * (2.0 * extras[0][...].astype(F32))).astype(outs[0].dtype)


def _no_extra_specs(tm, tn):
    return []


def _mm_nn(name, a, b, n_total, b_split, out_shape, epilogue, extras=(), extra_specs=_no_extra_specs, tm=MM_TM, tn=MM_TN, tk=MM_TK,
           deps=(), pieces=None, carry=()):
    m, kdim = a.shape
    tm, tk = _tile(tm, m), _tile(tk, kdim)
    n_blocks = None
    if b_split:
        piece = b.shape[2]
        tn = _tile(tn, piece)
        per = piece // tn
        if pieces is None:
            b_spec = pl.BlockSpec((None, tk, tn), lambda i, j, k: (j // per, k, j % per))
            out_spec = pl.BlockSpec((tm, tn), lambda i, j, k: (i, j))
        else:
            first, count = pieces
            n_blocks = count * per

            def which(j):
                return (first() + j // per) % N_DEV

            b_spec = pl.BlockSpec((None, tk, tn), lambda i, j, k: (which(j), k, j % per))
            out_spec = pl.BlockSpec((tm, tn), lambda i, j, k: (i, which(j) * per + j % per))
    else:
        tn = _tile(tn, n_total)
        b_spec = pl.BlockSpec((tk, tn), lambda i, j, k: (k, j))
        out_spec = pl.BlockSpec((tm, tn), lambda i, j, k: (i, j))
    if n_blocks is None:
        n_blocks = n_total // tn
    in_specs = [pl.BlockSpec((tm, tk), lambda i, j, k: (i, k)), b_spec] + list(extra_specs(tm, tn))
    return _matmul(name, "nn", (a, b) + tuple(extras), in_specs, out_shape, [out_spec] * len(out_shape),
                   (m // tm, n_blocks, kdim // tk), (tm, tn), epilogue, deps, carry)


def _mm_nt(name, a, b, n_total, b_split, out_shape, epilogue, extras=(), extra_specs=_no_extra_specs, tm=MM_TM, tn=MM_TN, tk=MM_TK,
           deps=()):
    m, kdim = a.shape
    tm, tn = _tile(tm, m), _tile(tn, n_total)
    if b_split:
        piece = b.shape[2]
        tk = _tile(tk, piece)
        per = piece // tk
        b_spec = pl.BlockSpec((None, tn, tk), lambda i, j, k: (k // per, j, k % per))
    else:
        tk = _tile(tk, kdim)
        b_spec = pl.BlockSpec((tn, tk), lambda i, j, k: (j, k))
    in_specs = [pl.BlockSpec((tm, tk), lambda i, j, k: (i, k)), b_spec] + list(extra_specs(tm, tn))
    out_specs = [pl.BlockSpec((tm, tn), lambda i, j, k: (i, j)) for _ in out_shape]
    return _matmul(name, "nt", (a, b) + tuple(extras), in_specs, out_shape, out_specs,
                   (m // tm, n_total // tn, kdim // tk), (tm, tn), epilogue, deps)


def _add_pair_epilogue(acc, extras, outs):
    outs[0][...] = (acc + extras[0][...].astype(F32)).astype(outs[0].dtype)


def _mm_tn_half(name, a, b, col_pieces, near, pair=None, tm=MM_TM, tn=MM_TN, tk=MM_TK, deps=()):
    kdim, m = a.shape
    n_total = b.shape[1]
    tk = _tile(tk, kdim)

    def core():
        c = lax.axis_index("c")
        return c if near else 1 - c

    if col_pieces:
        piece = n_total // N_DEV
        tm, tn = _tile(tm, m), _tile(tn, piece)
        per = piece // tn
        grid = (m // tm, N_CHIP * per, kdim // tk)
        a_spec = pl.BlockSpec((tk, tm), lambda i, j, k: (k, i))
        b_spec = pl.BlockSpec((tk, tn), lambda i, j, k: (k, (2 * (j // per) + core()) * per + j % per))
        out_spec = pl.BlockSpec((None, tm, tn), lambda i, j, k: (j // per, i, j % per))
        out_shape = [jax.ShapeDtypeStruct((N_CHIP, m, piece), BF16)]
    else:
        piece = m // N_DEV
        tm, tn = _tile(tm, piece), _tile(tn, n_total)
        per = piece // tm
        grid = (N_CHIP * per, n_total // tn, kdim // tk)
        a_spec = pl.BlockSpec((tk, tm), lambda i, j, k: (k, (2 * (i // per) + core()) * per + i % per))
        b_spec = pl.BlockSpec((tk, tn), lambda i, j, k: (k, j))
        out_spec = pl.BlockSpec((None, tm, tn), lambda i, j, k: (i // per, i % per, j))
        out_shape = [jax.ShapeDtypeStruct((N_CHIP, piece, n_total), BF16)]
    operands, in_specs, epilogue = (a, b), [a_spec, b_spec], _store(BF16)
    if pair is not None:
        operands, in_specs, epilogue = (a, b, pair), [a_spec, b_spec, out_spec], _add_pair_epilogue
    return _matmul(name, "tn", operands, in_specs, out_shape, [out_spec], grid, (tm, tn), epilogue, deps)[0]


def _rms(xv):
    return lax.rsqrt(jnp.mean(xv * xv, axis=-1, keepdims=True) + EPS)


def _colsum(v):
    return jnp.sum(v, axis=0, keepdims=True)


def _norm_mod(name, x, g, scale, shift, tr=256, deps=()):
    s, d = x.shape
    tr = _tile(tr, s)

    def body(x_ref, g_ref, sc_ref, sh_ref, h_ref):
        xv = x_ref[...]
        h = (xv * _rms(xv)) * g_ref[...]
        h_ref[...] = (h * (1.0 + sc_ref[...]) + sh_ref[...]).astype(h_ref.dtype)

    row = pl.BlockSpec((tr, d), lambda i: (i, 0))
    vec = pl.BlockSpec((1, d), lambda i: (0, 0))
    return _pallas(body, deps, name=name, grid=(s // tr,), in_specs=[row, vec, vec, vec], out_specs=row,
                   out_shape=jax.ShapeDtypeStruct((s, d), BF16), compiler_params=_params("parallel"))(x, g, scale, shift)


def _loss_head(x3, target, gf, gate2, mlp, tr=128):
    s, d = x3.shape
    tr = _tile(tr, s)

    def body(x_ref, t_ref, gf_ref, gate_ref, mlp_ref, dx_ref, dbr_ref, dgf_ref, dgate_ref, loss_ref):
        @pl.when(pl.program_id(0) == 0)
        def _():
            dgf_ref[...] = jnp.zeros_like(dgf_ref)
            dgate_ref[...] = jnp.zeros_like(dgate_ref)
            loss_ref[...] = jnp.zeros_like(loss_ref)

        xv = x_ref[...]
        r = _rms(xv)
        xn = xv * r
        gfv = gf_ref[...]
        err = xn * gfv - t_ref[...]
        loss_ref[...] += 0.5 * _colsum(jnp.mean(err * err, axis=-1, keepdims=True))
        dy = err * (1.0 / d)
        dgf_ref[...] += _colsum(dy * xn)
        dxn = dy * gfv
        dx = r * (dxn - xn * jnp.mean(dxn * xn, axis=-1, keepdims=True))
        dx_ref[...] = dx
        dbr_ref[...] = (dx * gate_ref[...]).astype(dbr_ref.dtype)
        dgate_ref[...] += _colsum(dx * mlp_ref[...].astype(F32))

    row = pl.BlockSpec((tr, d), lambda i: (i, 0))
    vec = pl.BlockSpec((1, d), lambda i: (0, 0))
    return _pallas(
        body, name="loss_head", grid=(s // tr,), in_specs=[row, row, vec, vec, row],
        out_specs=[row, row, vec, vec, pl.BlockSpec((1, 128), lambda i: (0, 0))],
        out_shape=[jax.ShapeDtypeStruct((s, d), F32), jax.ShapeDtypeStruct((s, d), BF16),
                   jax.ShapeDtypeStruct((1, d), F32), jax.ShapeDtypeStruct((1, d), F32),
                   jax.ShapeDtypeStruct((1, 128), F32)],
        compiler_params=_params("arbitrary"))(x3, target, gf, gate2, mlp)


def _norm_mod_bwd(name, dh, xin, g, scale, dx_up, branch=None, gate=None, tr=128, deps=()):
    s, d = xin.shape
    tr = _tile(tr, s)
    with_gate = branch is not None

    def body(*refs):
        dh_ref, x_ref, g_ref, sc_ref, up_ref = refs[:5]
        if with_gate:
            br_ref, gate_ref = refs[5:7]
            dx_ref, dsh_ref, dsc_ref, dg_ref, dbr_ref, dgate_ref = refs[7:]
            sums = (dsh_ref, dsc_ref, dg_ref, dgate_ref)
        else:
            dx_ref, dsh_ref, dsc_ref, dg_ref = refs[5:]
            sums = (dsh_ref, dsc_ref, dg_ref)

        @pl.when(pl.program_id(0) == 0)
        def _():
            for ref in sums:
                ref[...] = jnp.zeros_like(ref)

        xv, dhv, gv = x_ref[...], dh_ref[...].astype(F32), g_ref[...]
        r = _rms(xv)
        xn = xv * r
        one_sc = 1.0 + sc_ref[...]
        dsh_ref[...] += _colsum(dhv)
        dsc_ref[...] += _colsum(dhv * (xn * gv))
        dg_ref[...] += _colsum(dhv * one_sc * xn)
        dxn = dhv * one_sc * gv
        dx = up_ref[...] + r * (dxn - xn * jnp.mean(dxn * xn, axis=-1, keepdims=True))
        dx_ref[...] = dx
        if with_gate:
            dbr_ref[...] = (dx * gate_ref[...]).astype(dbr_ref.dtype)
            dgate_ref[...] += _colsum(dx * br_ref[...].astype(F32))

    row = pl.BlockSpec((tr, d), lambda i: (i, 0))
    vec = pl.BlockSpec((1, d), lambda i: (0, 0))
    vshape = jax.ShapeDtypeStruct((1, d), F32)
    operands = [dh, xin, g, scale, dx_up]
    in_specs = [row, row, vec, vec, row]
    out_shape = [jax.ShapeDtypeStruct((s, d), F32), vshape, vshape, vshape]
    out_specs = [row, vec, vec, vec]
    if with_gate:
        operands += [branch, gate]
        in_specs += [row, vec]
        out_shape += [jax.ShapeDtypeStruct((s, d), BF16), vshape]
        out_specs += [row, vec]
    return _pallas(body, deps, name=name, grid=(s // tr,), in_specs=in_specs, out_specs=out_specs, out_shape=out_shape,
                   compiler_params=_params("arbitrary"))(*operands)


def _window_count(c0, rows, half, s):
    t = c0 + lax.broadcasted_iota(jnp.int32, (rows, 1), 0)
    return (jnp.minimum(t + half, s) - jnp.maximum(t - half, 0)).astype(F32)


def _zero_pads(pad, s):
    zeros = jnp.zeros((PAD_ROWS, pad.shape[1]), pad.dtype)
    pad[0:PAD_ROWS, :] = zeros
    pad[PAD_ROWS + s:PAD_ROWS + s + PAD_ROWS, :] = zeros


def _pool_fwd(proj, s, gd, cb, ch, deps=()):
    nsub = gd // cb

    def body(v_ref, o_ref, pad):
        g = pl.program_id(0)
        _zero_pads(pad, s)
        pad[PAD_ROWS:PAD_ROWS + s, :] = v_ref[...].astype(F32)
        for gi, window in enumerate(POOL_WINDOWS):
            half = window // 2

            @pl.when(g == gi)
            def _(half=half):
                for c0 in range(0, s, ch):
                    base = PAD_ROWS + c0
                    acc = pad[base - half:base - half + ch, :]
                    for j in range(-half + 1, half):
                        acc = acc + pad[base + j:base + j + ch, :]
                    out = acc / _window_count(c0, ch, half, s) - pad[base:base + ch, :]
                    o_ref[c0:c0 + ch, :] = out.astype(o_ref.dtype)

    spec = pl.BlockSpec((s, cb), lambda g, j: (0, g * nsub + j))
    return _pallas(body, deps, name="pool_fwd", grid=(N_POOL_GROUPS, nsub), in_specs=[spec], out_specs=spec,
                   out_shape=jax.ShapeDtypeStruct((s, N_POOL_GROUPS * gd), BF16),
                   scratch_shapes=[pltpu.VMEM((s + 2 * PAD_ROWS, cb), F32)],
                   compiler_params=_params("parallel", "parallel"))(proj)


def _pool_bwd(dpooled, dproj, s, gd, cb, ch):
    nsub = gd // cb

    def body(dp_ref, dproj_in, o_ref, pad):
        del dproj_in
        g = pl.program_id(0)
        _zero_pads(pad, s)
        for gi, window in enumerate(POOL_WINDOWS):
            half = window // 2

            @pl.when(g == gi)
            def _(half=half):
                for c0 in range(0, s, ch):
                    pad[PAD_ROWS + c0:PAD_ROWS + c0 + ch, :] = dp_ref[c0:c0 + ch, :] / _window_count(c0, ch, half, s)
                for c0 in range(0, s, ch):
                    base = PAD_ROWS + c0
                    acc = pad[base - half + 1:base - half + 1 + ch, :]
                    for j in range(-half + 2, half + 1):
                        acc = acc + pad[base + j:base + j + ch, :]
                    o_ref[c0:c0 + ch, :] = (acc - dp_ref[c0:c0 + ch, :]).astype(o_ref.dtype)

    spec = pl.BlockSpec((s, cb), lambda g, j: (0, g * nsub + j))
    return _pallas(body, name="pool_bwd", grid=(N_POOL_GROUPS, nsub),
                   in_specs=[spec, pl.BlockSpec(memory_space=pl.ANY)], out_specs=spec,
                   out_shape=jax.ShapeDtypeStruct(dproj.shape, dproj.dtype), input_output_aliases={1: 0},
                   scratch_shapes=[pltpu.VMEM((s + 2 * PAD_ROWS, cb), F32)],
                   compiler_params=_params("parallel", "parallel"))(dpooled, dproj)


def _poolmix_fwd(pooled, wmix, pool_scale, gnorm_g, d_model, tm=512):
    s = pooled.shape[0]
    gd = wmix.shape[1]
    tm = _tile(tm, s)

    def body(p_ref, w_ref, ps_ref, g_ref, apre_ref, mixed_ref):
        a_pre = jnp.dot(p_ref[...], w_ref[...], preferred_element_type=F32)
        apre_ref[...] = a_pre
        a_out = a_pre * ps_ref[...]
        mixed_ref[...] = ((a_out * _rms(a_out)) * g_ref[...]).astype(mixed_ref.dtype)

    blk = pl.BlockSpec((tm, gd), lambda g, i: (i, g))
    vec = pl.BlockSpec((1, gd), lambda g, i: (0, g))
    return _pallas(body, name="poolmix_fwd", grid=(N_POOL_GROUPS, s // tm),
                   in_specs=[blk, pl.BlockSpec((None, gd, gd), lambda g, i: (g, 0, 0)), vec, vec],
                   out_specs=[blk, blk],
                   out_shape=[jax.ShapeDtypeStruct((s, N_POOL_GROUPS * gd), F32), jax.ShapeDtypeStruct((s, d_model), BF16)],
                   compiler_params=_params("parallel", "parallel"))(pooled, wmix, pool_scale, gnorm_g)


def _poolmix_bwd(dmixed, a_pre, wmix, pool_scale, gnorm_g, tm=512):
    s = a_pre.shape[0]
    gd = wmix.shape[1]
    tm = _tile(tm, s)

    def body(dm_ref, apre_ref, w_ref, ps_ref, g_ref, dapre_ref, dpooled_ref, dps_ref, dg_ref):
        @pl.when(pl.program_id(1) == 0)
        def _():
            dps_ref[...] = jnp.zeros_like(dps_ref)
            dg_ref[...] = jnp.zeros_like(dg_ref)

        a_pre, dm, ps = apre_ref[...], dm_ref[...].astype(F32), ps_ref[...]
        a_out = a_pre * ps
        r = _rms(a_out)
        n = a_out * r
        dg_ref[...] += _colsum(dm * n)
        dn = dm * g_ref[...]
        da_out = r * (dn - n * jnp.mean(dn * n, axis=-1, keepdims=True))
        dps_ref[...] += _colsum(da_out * a_pre)
        da_pre = (da_out * ps).astype(BF16)
        dapre_ref[...] = da_pre
        dpooled_ref[...] = lax.dot_general(da_pre, w_ref[...], _DOT_DIMS["nt"], preferred_element_type=F32)

    blk = pl.BlockSpec((tm, gd), lambda g, i: (i, g))
    vec = pl.BlockSpec((1, gd), lambda g, i: (0, g))
    width = N_POOL_GROUPS * gd
    return _pallas(body, name="poolmix_bwd", grid=(N_POOL_GROUPS, s // tm),
                   in_specs=[blk, blk, pl.BlockSpec((None, gd, gd), lambda g, i: (g, 0, 0)), vec, vec],
                   out_specs=[blk, blk, vec, vec],
                   out_shape=[jax.ShapeDtypeStruct((s, width), BF16), jax.ShapeDtypeStruct((s, width), F32),
                              jax.ShapeDtypeStruct((1, width), F32), jax.ShapeDtypeStruct((1, width), F32)],
                   compiler_params=_params("parallel", "arbitrary"))(dmixed, a_pre, wmix, pool_scale, gnorm_g)


def _poolmix_wgrad(pooled, da_pre, gd, tk=1024):
    s = pooled.shape[0]
    tk = _tile(tk, s)
    nk = s // tk

    def body(p_ref, d_ref, o_ref, acc):
        k = pl.program_id(1)
        part = lax.dot_general(p_ref[...], d_ref[...], _DOT_DIMS["tn"], preferred_element_type=F32)

        @pl.when(k == 0)
        def _():
            acc[...] = part

        @pl.when(k > 0)
        def _():
            acc[...] += part

        @pl.when(k == nk - 1)
        def _():
            o_ref[...] = acc[...].astype(o_ref.dtype)

    blk = pl.BlockSpec((tk, gd), lambda g, k: (k, g))
    return _pallas(body, name="poolmix_wgrad", grid=(N_POOL_GROUPS, nk), in_specs=[blk, blk],
                   out_specs=pl.BlockSpec((None, gd, gd), lambda g, k: (g, 0, 0)),
                   out_shape=jax.ShapeDtypeStruct((N_POOL_GROUPS, gd, gd), BF16),
                   scratch_shapes=[pltpu.VMEM((gd, gd), F32)],
                   compiler_params=_params("parallel", "arbitrary"))(pooled, da_pre)


def _head_mean(v):
    parts = []
    for q in range(v.shape[1] // CONV_HEAD_DIM):
        m = jnp.mean(v[:, q * CONV_HEAD_DIM:(q + 1) * CONV_HEAD_DIM], axis=-1, keepdims=True)
        parts.append(jnp.broadcast_to(m, (v.shape[0], CONV_HEAD_DIM)))
    return parts[0] if len(parts) == 1 else jnp.concatenate(parts, axis=1)


def _conv_fwd(proj, mixed, conv_w, conv_b, gnorm_g, s, width, cb, ch, deps=()):
    nblk = width // cb

    def body(b_ref, c_ref, u_ref, w_ref, cb_ref, g_ref, mixed_in, o_ref, pad):
        del mixed_in
        _zero_pads(pad, s)
        pad[PAD_ROWS:PAD_ROWS + s, :] = c_ref[...].astype(F32) * u_ref[...].astype(F32)
        w = w_ref[...]
        for c0 in range(0, s, ch):
            base = PAD_ROWS + c0
            conv = (w[0:1] * pad[base - 1:base - 1 + ch, :] + w[1:2] * pad[base:base + ch, :]
                    + w[2:3] * pad[base + 1:base + 1 + ch, :] + cb_ref[...])
            bo = b_ref[c0:c0 + ch, :].astype(F32) * conv
            n = bo * lax.rsqrt(_head_mean(bo * bo) + EPS)
            o_ref[c0:c0 + ch, :] = (n * g_ref[...]).astype(o_ref.dtype)

    def part(p):
        return pl.BlockSpec((s, cb), lambda j: (0, p * nblk + j))

    vec = pl.BlockSpec((1, cb), lambda j: (0, j))
    return _pallas(body, deps, name="conv_fwd", grid=(nblk,),
                   in_specs=[part(1), part(2), part(3), pl.BlockSpec((3, cb), lambda j: (0, j)), vec, vec,
                             pl.BlockSpec(memory_space=pl.ANY)],
                   out_specs=part(1), out_shape=jax.ShapeDtypeStruct(mixed.shape, mixed.dtype),
                   input_output_aliases={6: 0},
                   scratch_shapes=[pltpu.VMEM((s + 2 * PAD_ROWS, cb), F32)],
                   compiler_params=_params("parallel"))(proj, proj, proj, conv_w, conv_b, gnorm_g, mixed)


def _conv_bwd(dmixed, proj, conv_w, conv_b, gnorm_g, s, width, cb, ch, deps=()):
    nblk = width // cb

    def body(dm_ref, b_ref, c_ref, u_ref, w_ref, cb_ref, g_ref, dproj_ref, dw_ref, dcb_ref, dg_ref,
             pad_cu, pad_dconv, db_buf, dc_buf, du_buf, sems):
        j = pl.program_id(0)
        _zero_pads(pad_cu, s)
        _zero_pads(pad_dconv, s)
        pad_cu[PAD_ROWS:PAD_ROWS + s, :] = c_ref[...].astype(F32) * u_ref[...].astype(F32)
        w, gv = w_ref[...], g_ref[...]
        zero = jnp.zeros((1, cb), F32)
        dw0, dw1, dw2, dcb, dg = zero, zero, zero, zero, zero
        for c0 in range(0, s, ch):
            base = PAD_ROWS + c0
            cu_prev, cu_here, cu_next = (pad_cu[base - 1:base - 1 + ch, :], pad_cu[base:base + ch, :],
                                         pad_cu[base + 1:base + 1 + ch, :])
            conv = w[0:1] * cu_prev + w[1:2] * cu_here + w[2:3] * cu_next + cb_ref[...]
            bg = b_ref[c0:c0 + ch, :].astype(F32)
            bo = bg * conv
            r = lax.rsqrt(_head_mean(bo * bo) + EPS)
            n = bo * r
            dm = dm_ref[c0:c0 + ch, :].astype(F32)
            dg = dg + _colsum(dm * n)
            dn = dm * gv
            dbo = r * (dn - n * _head_mean(dn * n))
            db_buf[c0:c0 + ch, :] = (dbo * conv).astype(BF16)
            dconv = dbo * bg
            pad_dconv[base:base + ch, :] = dconv
            dcb = dcb + _colsum(dconv)
            dw0 = dw0 + _colsum(dconv * cu_prev)
            dw1 = dw1 + _colsum(dconv * cu_here)
            dw2 = dw2 + _colsum(dconv * cu_next)
        dw_ref[0:1, :] = dw0
        dw_ref[1:2, :] = dw1
        dw_ref[2:3, :] = dw2
        dcb_ref[...] = dcb
        dg_ref[...] = dg
        for c0 in range(0, s, ch):
            base = PAD_ROWS + c0
            dcu = (w[0:1] * pad_dconv[base + 1:base + 1 + ch, :] + w[1:2] * pad_dconv[base:base + ch, :]
                   + w[2:3] * pad_dconv[base - 1:base - 1 + ch, :])
            dc_buf[c0:c0 + ch, :] = (dcu * u_ref[c0:c0 + ch, :].astype(F32)).astype(BF16)
            du_buf[c0:c0 + ch, :] = (dcu * c_ref[c0:c0 + ch, :].astype(F32)).astype(BF16)
        copies = []
        for p, buf in enumerate((db_buf, dc_buf, du_buf)):
            col = pl.multiple_of((p + 1) * width + j * cb, CONV_HEAD_DIM)
            copies.append(pltpu.make_async_copy(buf, dproj_ref.at[:, pl.ds(col, cb)], sems.at[p]))
            copies[-1].start()
        for cp in copies:
            cp.wait()

    def part(p):
        return pl.BlockSpec((s, cb), lambda j: (0, p * nblk + j))

    vec = pl.BlockSpec((1, cb), lambda j: (0, j))
    w_spec = pl.BlockSpec((3, cb), lambda j: (0, j))
    return _pallas(body, deps, name="conv_bwd", grid=(nblk,),
                   in_specs=[part(1), part(1), part(2), part(3), w_spec, vec, vec],
                   out_specs=[pl.BlockSpec(memory_space=pl.ANY), w_spec, vec, vec],
                   out_shape=[jax.ShapeDtypeStruct((s, 4 * width), BF16), jax.ShapeDtypeStruct((3, width), F32),
                              jax.ShapeDtypeStruct((1, width), F32), jax.ShapeDtypeStruct((1, width), F32)],
                   scratch_shapes=[pltpu.VMEM((s + 2 * PAD_ROWS, cb), F32), pltpu.VMEM((s + 2 * PAD_ROWS, cb), F32),
                                   pltpu.VMEM((s, cb), BF16), pltpu.VMEM((s, cb), BF16), pltpu.VMEM((s, cb), BF16),
                                   pltpu.SemaphoreType.DMA((3,))],
                   compiler_params=_params("arbitrary"))(dmixed, proj, proj, proj, conv_w, conv_b, gnorm_g)


def _adamw(w, g, m, v):
    m = ADAM_B1 * m + (1.0 - ADAM_B1) * g
    v = ADAM_B2 * v + (1.0 - ADAM_B2) * (g * g)
    m_hat = m / (1.0 - ADAM_B1 ** ADAM_STEP)
    v_hat = v / (1.0 - ADAM_B2 ** ADAM_STEP)
    delta = -ADAM_LR * (m_hat / (jnp.sqrt(v_hat) + ADAM_EPS) + ADAM_WD * w)
    return delta, m, v


def _ada_fwd(c_rows, w, b, tn=512):
    rows, d = c_rows.shape
    n = w.shape[1]
    tn = _tile(tn, n)

    def body(c_ref, w_ref, b_ref, o_ref):
        cv = c_ref[...]
        act = (cv * jax.nn.sigmoid(cv)).astype(BF16)
        o_ref[...] = jnp.dot(act, w_ref[...].astype(BF16), preferred_element_type=F32) + b_ref[...]

    return _pallas(body, name="ada_fwd", grid=(n // tn,),
                   in_specs=[pl.BlockSpec((rows, d), lambda j: (0, 0)), pl.BlockSpec((d, tn), lambda j: (0, j)),
                             pl.BlockSpec((1, tn), lambda j: (0, j))],
                   out_specs=pl.BlockSpec((rows, tn), lambda j: (0, j)),
                   out_shape=jax.ShapeDtypeStruct((rows, n), F32), compiler_params=_params("parallel"))(c_rows, w, b)


def _ada_bwd_adam(c_cols, dmod, w, m, v, tr=512, tn=1024):
    d, rows = c_cols.shape
    n = w.shape[1]
    tr, tn = _tile(tr, d), _tile(tn, n)

    def body(c_ref, dm_ref, w_ref, m_ref, v_ref, g_ref, dl_ref, nm_ref, nv_ref):
        cv = c_ref[...]
        act = (cv * jax.nn.sigmoid(cv)).astype(BF16)
        g = jnp.dot(act, dm_ref[...].astype(BF16), preferred_element_type=F32)
        g_ref[...] = g
        dl_ref[...], nm_ref[...], nv_ref[...] = _adamw(w_ref[...], g, m_ref[...], v_ref[...])

    blk = pl.BlockSpec((tr, tn), lambda i, j: (i, j))
    shape = jax.ShapeDtypeStruct((d, n), F32)
    return _pallas(body, name="ada_bwd_adam", grid=(d // tr, n // tn),
                   in_specs=[pl.BlockSpec((tr, rows), lambda i, j: (i, 0)), pl.BlockSpec((rows, tn), lambda i, j: (0, j)),
                             blk, blk, blk],
                   out_specs=[blk] * 4, out_shape=[shape] * 4,
                   compiler_params=_params("parallel", "parallel"))(c_cols, dmod, w, m, v)


def _reduce_adam(name, pieces, w, m, v, tr=256, tc=1024):
    r, c = w.shape
    tr, tc = _tile(tr, r), _tile(tc, c)

    def body(p_ref, w_ref, m_ref, v_ref, g_ref, dl_ref, nm_ref, nv_ref):
        g = p_ref[0].astype(F32)
        for j in range(1, N_DEV):
            g = g + p_ref[j].astype(F32)
        g_ref[...] = g
        dl_ref[...], nm_ref[...], nv_ref[...] = _adamw(w_ref[...], g, m_ref[...], v_ref[...])

    blk = pl.BlockSpec((tr, tc), lambda i, j: (i, j))
    shape = jax.ShapeDtypeStruct((r, c), F32)
    return _pallas(body, name=name, grid=(r // tr, c // tc),
                   in_specs=[pl.BlockSpec((N_DEV, tr, tc), lambda i, j: (0, i, j)), blk, blk, blk],
                   out_specs=[blk] * 4, out_shape=[shape] * 4,
                   compiler_params=_params("parallel", "parallel"))(pieces, w, m, v)


def _reduce_adam_chips(name, sums, land, w, m, v, tr=256, tc=1024):
    r, c = w.shape
    tr, tc = _tile(tr, r), _tile(tc, c)

    def body(s_ref, l_ref, w_ref, m_ref, v_ref, g_ref, dl_ref, nm_ref, nv_ref):
        g = s_ref[...].astype(F32)
        for k in range(3):
            g = g + l_ref[k].astype(F32)
        g_ref[...] = g
        dl_ref[...], nm_ref[...], nv_ref[...] = _adamw(w_ref[...], g, m_ref[...], v_ref[...])

    blk = pl.BlockSpec((tr, tc), lambda i, j: (i, j))
    shape = jax.ShapeDtypeStruct((r, c), F32)
    mine = pl.BlockSpec((None, tr, tc), lambda i, j: (2 * lax.axis_index("x") + lax.axis_index("y"), i, j))
    return _pallas(body, name=name, grid=(r // tr, c // tc),
                   in_specs=[mine, pl.BlockSpec((3, tr, tc), lambda i, j: (0, i, j)), blk, blk, blk],
                   out_specs=[blk] * 4, out_shape=[shape] * 4,
                   compiler_params=_params("parallel", "parallel"))(sums, land, w, m, v)


def _sum_devices(parts):
    n = parts.shape[1]

    def body(p_ref, o_ref):
        acc = p_ref[0:1, :]
        for j in range(1, N_DEV):
            acc = acc + p_ref[j:j + 1, :]
        o_ref[...] = acc

    return _pallas(body, name="sum_devices", out_shape=jax.ShapeDtypeStruct((1, n), F32),
                   compiler_params=pltpu.CompilerParams(vmem_limit_bytes=VMEM_LIMIT_BYTES))(parts)


def _adam_small(name, g, w, m, v):
    def body(g_ref, w_ref, m_ref, v_ref, dl_ref, nm_ref, nv_ref):
        dl_ref[...], nm_ref[...], nv_ref[...] = _adamw(w_ref[...], g_ref[...], m_ref[...], v_ref[...])

    shape = jax.ShapeDtypeStruct(w.shape, F32)
    return _pallas(body, name=name, out_shape=[shape] * 3,
                   compiler_params=pltpu.CompilerParams(vmem_limit_bytes=VMEM_LIMIT_BYTES))(g, w, m, v)


def kernel(x, c, w_ada, b_ada, norm1_g, w_in, pool_mix_w, pool_scale, conv_w, conv_b, gnorm_pool_g, gnorm_conv_g, w_out, norm2_g, w_mlp_in, w_mlp_out, final_g, loss_target, m_w_ada, m_b_ada, m_norm1_g, m_w_in, m_pool_mix_w, m_pool_scale, m_conv_w, m_conv_b, m_gnorm_pool_g, m_gnorm_conv_g, m_w_out, m_norm2_g, m_w_mlp_in, m_w_mlp_out, m_final_g, v_w_ada, v_b_ada, v_norm1_g, v_w_in, v_pool_mix_w, v_pool_scale, v_conv_w, v_conv_b, v_gnorm_pool_g, v_gnorm_conv_g, v_w_out, v_norm2_g, v_w_mlp_in, v_w_mlp_out, v_final_g):
    s, d = x.shape[1], x.shape[2]
    width = d // 2
    gd = width // N_POOL_GROUPS
    d_ff = w_mlp_in.shape[2] * N_DEV
    n_proj = w_in.shape[2] * N_DEV
    ada_cols = w_ada.shape[2]
    conv_cols = conv_w.shape[2]
    assert n_proj == 4 * width and ada_cols * N_DEV == N_MOD * d and d_ff % N_DEV == 0
    assert width % CONV_HEAD_DIM == 0 and s % 8 == 0
    seq_chunk = _tile(512, s)
    pool_cb = _tile(256, gd)
    conv_cb = CONV_HEAD_DIM

    me = 4 * lax.axis_index("x") + 2 * lax.axis_index("y") + lax.axis_index("c")
    x2d, target = x[0], loss_target[0]

    wmix_all, conv_w_all, c_all = _exchange(
        "gather_small_weights", [pool_mix_w[0].astype(BF16), conv_w[0], c], ["gather"] * 3)
    wmix_full = jnp.transpose(wmix_all, (1, 0, 2, 3)).reshape(N_POOL_GROUPS, gd, gd)
    conv_w_full = jnp.transpose(conv_w_all, (1, 0, 2)).reshape(3, width)
    c_rows = jnp.concatenate([c_all.reshape(N_DEV, d), jnp.zeros((N_DEV, d), F32)], axis=0)

    b_mine = lax.dynamic_slice(b_ada, (0, me * ada_cols), (1, ada_cols))
    mod_part = _ada_fwd(c_rows, w_ada[0], b_mine)
    (mod_all,) = _exchange("scatter_mod", [mod_part[:N_DEV].reshape(N_DEV, 1, ada_cols)], ["a2a"])
    mod = mod_all.reshape(1, N_MOD * d)

    started, hopped, relayed = {}, {}, {}

    def gather_start(wname, wgt, deps):
        land = _landing(wgt[0].astype(BF16), me)
        started[wname] = _gather_start("gather_" + wname + "_start", land, deps)
        return started[wname][5]

    def gather_hop(wname, land, after):
        hopped[wname] = _gather_hop("gather_" + wname + "_hop", started[wname], land, after)
        return hopped[wname][3]

    def gather_relay(wname, after):
        relayed[wname] = _gather_relay("gather_" + wname + "_relay", started[wname], hopped[wname], after)
        return relayed[wname][3]

    def gather_wait(wname, land, after, local_waited=False):
        return _gather_wait("gather_" + wname + "_wait", started[wname], hopped[wname], relayed[wname], land, after,
                            local_waited)

    def first_local():
        return 2 * (2 * lax.axis_index("x") + lax.axis_index("y"))

    def first_remote():
        return first_local() + 2

    def last_remote():
        return first_local() + 5

    tok_w_in = gather_start("w_in", w_in, (mod,))
    shift1, scale1, gate1, shift2, scale2, gate2 = [mod[:, i * d:(i + 1) * d] for i in range(N_MOD)]

    h1 = _norm_mod("norm1_fwd", x2d, norm1_g, scale1, shift1, deps=(tok_w_in,))
    proj_shape = [jax.ShapeDtypeStruct((s, n_proj), BF16)]
    w_in_local = _gather_wait_local("gather_w_in_local", started["w_in"], started["w_in"][4], h1)
    (proj,) = _mm_nn("in_proj_local", h1, w_in_local, n_proj, True, proj_shape, _store(BF16), pieces=(first_local, 2))
    tok = gather_hop("w_in", w_in_local, proj)
    tok = gather_start("w_out", w_out, (tok,))
    gather_relay("w_in", tok)
    w_in_all = gather_wait("w_in", relayed["w_in"][2], relayed["w_in"][3], True)
    tok = gather_hop("w_out", started["w_out"][4], w_in_all)
    tok = gather_start("w_mlp_in", w_mlp_in, (tok,))
    (proj,) = _mm_nn("in_proj", h1, w_in_all, n_proj, True, proj_shape, _store(BF16), pieces=(first_remote, 6),
                     carry=(proj,), deps=(tok,))
    pooled = _pool_fwd(proj, s, gd, pool_cb, seq_chunk)
    a_pre, mixed = _poolmix_fwd(pooled, wmix_full, pool_scale, gnorm_pool_g, d)
    tok = gather_hop("w_mlp_in", started["w_mlp_in"][4], a_pre)
    tok = gather_start("w_mlp_out", w_mlp_out, (tok,))
    tok = gather_relay("w_out", tok)
    mixed = _conv_fwd(proj, mixed, conv_w_full, conv_b, gnorm_conv_g, s, width, conv_cb, seq_chunk, deps=(tok,))

    def residual_specs(tm, tn):
        return [pl.BlockSpec((tm, tn), lambda i, j, k: (i, j)), pl.BlockSpec((1, tn), lambda i, j, k: (0, j))]

    sd_f32 = jax.ShapeDtypeStruct((s, d), F32)
    w_out_full = gather_wait("w_out", relayed["w_out"][2], mixed).reshape(d, d)
    sd_bf16 = jax.ShapeDtypeStruct((s, d), BF16)
    attn, x_mid = _mm_nn("out_proj", mixed, w_out_full, d, False, [sd_bf16, sd_f32], _residual_epilogue,
                         extras=(x2d, gate1), extra_specs=residual_specs)
    h2 = _norm_mod("norm2_fwd", x_mid, norm2_g, scale2, shift2)
    sf_bf16 = [jax.ShapeDtypeStruct((s, d_ff), BF16)] * 2
    tok = gather_relay("w_mlp_in", h2)
    w1_local = _gather_wait_local("gather_w_mlp_in_local", started["w_mlp_in"], relayed["w_mlp_in"][2], tok)
    relu, hid = _mm_nn("mlp_in_local", h2, w1_local, d_ff, True, sf_bf16, _relu2_epilogue, pieces=(first_local, 2))
    w1_all = gather_wait("w_mlp_in", w1_local, hid, True)
    tok = gather_hop("w_mlp_out", started["w_mlp_out"][4], w1_all)
    relu, hid = _mm_nn("mlp_in_remote", h2, w1_all, d_ff, True, sf_bf16, _relu2_epilogue,
                       pieces=(first_remote, 3), carry=(relu, hid), deps=(tok,))
    tok = gather_relay("w_mlp_out", hid)
    relu, hid = _mm_nn("mlp_in", h2, w1_all, d_ff, True, sf_bf16, _relu2_epilogue,
                       pieces=(last_remote, 3), carry=(relu, hid), deps=(tok,))
    w2_full = gather_wait("w_mlp_out", relayed["w_mlp_out"][2], hid).reshape(d_ff, d)
    mlp, x_last = _mm_nn("mlp_out", hid, w2_full, d, False, [sd_bf16, sd_f32], _residual_epilogue,
                         extras=(x_mid, gate2), extra_specs=residual_specs)

    dx_last, dmlp, d_final_g, dgate2, loss_row = _loss_head(x_last, target, final_g.reshape(1, d), gate2, mlp)

    def relu_specs(tm, tn):
        return [pl.BlockSpec((tm, tn), lambda i, j, k: (i, j))]

    def reduce_start(wname, a, b, col_pieces, deps=()):
        far = _mm_tn_half(wname + "_dw_far", a, b, col_pieces, near=False, deps=deps)
        return _pair_start("scatter_" + wname + "_pair_start", far)

    def reduce_chips(wname, a, b, col_pieces, pairs, after):
        pair = _pair_wait("scatter_" + wname + "_pair_wait", pairs, after)
        sums = _mm_tn_half(wname + "_dw_near", a, b, col_pieces, near=True, pair=pair)
        return _chip_start("scatter_" + wname + "_chip_start", sums)

    pairs_w2 = reduce_start("mlp_out", hid, dmlp, False)
    (dhpre,) = _mm_nt("mlp_out_dx", dmlp, w2_full, d_ff, False, sf_bf16[:1], _relu2_bwd_epilogue,
                      extras=(relu,), extra_specs=relu_specs, deps=(pairs_w2[4],))
    chips_w2 = reduce_chips("mlp_out", hid, dmlp, False, pairs_w2, dhpre)
    pairs_w1 = reduce_start("mlp_in", h2, dhpre, True, deps=(chips_w2[4],))
    (dh2,) = _mm_nt("mlp_in_dx", dhpre, w1_all, d, True, [sd_bf16], _store(BF16), tn=1024, deps=(pairs_w1[4],))
    chips_w1 = reduce_chips("mlp_in", h2, dhpre, True, pairs_w1, dh2)
    dx_mid, dshift2, dscale2, d_norm2_g, dattn, dgate1 = _norm_mod_bwd(
        "norm2_bwd", dh2, x_mid, norm2_g, scale2, dx_last, branch=attn, gate=gate1, deps=(chips_w1[4],))

    pairs_w_out = reduce_start("out_proj", mixed, dattn, False)
    (dmixed,) = _mm_nt("out_proj_dx", dattn, w_out_full, d, False, [sd_bf16], _store(BF16), deps=(pairs_w_out[4],))
    chips_w_out = reduce_chips("out_proj", mixed, dattn, False, pairs_w_out, dmixed)
    dproj, d_conv_w, d_conv_b, d_gnorm_conv = _conv_bwd(dmixed, proj, conv_w_full, conv_b, gnorm_conv_g,
                                                        s, width, conv_cb, seq_chunk, deps=(chips_w_out[4],))
    da_pre, dpooled, d_pool_scale, d_gnorm_pool = _poolmix_bwd(dmixed, a_pre, wmix_full, pool_scale, gnorm_pool_g)
    g_wmix = _poolmix_wgrad(pooled, da_pre, gd)
    dproj = _pool_bwd(dpooled, dproj, s, gd, pool_cb, seq_chunk)

    pairs_w_in = reduce_start("in_proj", h1, dproj, True)
    (dh1,) = _mm_nt("in_proj_dx", dproj, w_in_all, d, True, [sd_bf16], _store(BF16), tn=1024, deps=(pairs_w_in[4],))
    chips_w_in = reduce_chips("in_proj", h1, dproj, True, pairs_w_in, dh1)
    grad_x, dshift1, dscale1, d_norm1_g = _norm_mod_bwd("norm1_bwd", dh1, x2d, norm1_g, scale1, dx_mid,
                                                        deps=(chips_w_in[4],))

    sums, landed = _chip_wait("scatter_w_mlp_out_chip_wait", chips_w2, grad_x)
    out_w2 = _reduce_adam_chips("adam_w_mlp_out", sums, landed, w_mlp_out[0], m_w_mlp_out[0], v_w_mlp_out[0])
    sums, landed = _chip_wait("scatter_w_mlp_in_chip_wait", chips_w1, out_w2[0])
    out_w1 = _reduce_adam_chips("adam_w_mlp_in", sums, landed, w_mlp_in[0], m_w_mlp_in[0], v_w_mlp_in[0])
    sums, landed = _chip_wait("scatter_w_out_chip_wait", chips_w_out, out_w1[0])
    out_w_out = _reduce_adam_chips("adam_w_out", sums, landed, w_out[0], m_w_out[0], v_w_out[0])
    sums, landed = _chip_wait("scatter_w_in_chip_wait", chips_w_in, out_w_out[0])
    out_w_in = _reduce_adam_chips("adam_w_in", sums, landed, w_in[0], m_w_in[0], v_w_in[0])

    rows_mix = gd // N_DEV
    g_wmix_split = jnp.transpose(g_wmix.reshape(N_POOL_GROUPS, N_DEV, rows_mix, gd), (1, 0, 2, 3))
    g_wmix_split = g_wmix_split.reshape(N_DEV, N_POOL_GROUPS * rows_mix, gd)
    loss_pad = jnp.concatenate([loss_row[:, :1], jnp.zeros((1, 127), F32)], axis=1)
    dmod = jnp.concatenate([dshift1, dscale1, dgate1, dshift2, dscale2, dgate2], axis=1)
    small = jnp.concatenate([dmod, d_norm1_g, d_pool_scale, d_conv_b, d_gnorm_pool, d_gnorm_conv, d_norm2_g,
                             d_final_g, d_conv_w.reshape(1, 3 * width), loss_pad], axis=1)
    p_wmix, small_all = _exchange("exchange_small_grads", [g_wmix_split, small], ["a2a", "gather"],
                                  deps=(out_w_in[0],))
    mix_shape = (N_POOL_GROUPS * rows_mix, gd)
    out_wmix = _reduce_adam("adam_pool_mix", p_wmix, pool_mix_w.reshape(mix_shape), m_pool_mix_w.reshape(mix_shape),
                            v_pool_mix_w.reshape(mix_shape))
    out_wmix = [a.reshape(pool_mix_w.shape) for a in out_wmix]
    small_all = small_all.reshape(N_DEV, small.shape[1])
    small_sum = _sum_devices(small_all)

    n_rep = (N_MOD + 1) * d + 4 * width + 2 * d
    loss = small_sum[0, n_rep + 3 * width]
    rep_names_w = [b_ada, norm1_g, pool_scale, conv_b, gnorm_pool_g, gnorm_conv_g, norm2_g, final_g.reshape(1, d)]
    rep_names_m = [m_b_ada, m_norm1_g, m_pool_scale, m_conv_b, m_gnorm_pool_g, m_gnorm_conv_g, m_norm2_g,
                   m_final_g.reshape(1, d)]
    rep_names_v = [v_b_ada, v_norm1_g, v_pool_scale, v_conv_b, v_gnorm_pool_g, v_gnorm_conv_g, v_norm2_g,
                   v_final_g.reshape(1, d)]
    rep_grad = small_sum[:, :n_rep]
    rep_delta, rep_m, rep_v = _adam_small("adam_replicated", rep_grad, jnp.concatenate(rep_names_w, axis=1),
                                          jnp.concatenate(rep_names_m, axis=1), jnp.concatenate(rep_names_v, axis=1))

    def split_rep(vec):
        out, off = [], 0
        for wgt in rep_names_w:
            n = wgt.shape[1]
            out.append(vec[:, off:off + n])
            off += n
        out[-1] = out[-1].reshape(d)
        return out

    conv_grad_full = small_sum[:, n_rep:n_rep + 3 * width].reshape(3, width)
    g_conv_w = lax.dynamic_slice(conv_grad_full, (0, me * conv_cols), (3, conv_cols))
    g_conv_w8 = jnp.concatenate([g_conv_w, jnp.zeros((5, conv_cols), F32)], axis=0)

    def pad8(a):
        return jnp.concatenate([a[0], jnp.zeros((5, conv_cols), F32)], axis=0)

    conv_delta, conv_m, conv_v = _adam_small("adam_conv_w", g_conv_w8, pad8(conv_w), pad8(m_conv_w), pad8(v_conv_w))

    dmod_all = small_all[:, :N_MOD * d]
    dmod_mine = lax.dynamic_slice(dmod_all, (0, me * ada_cols), (N_DEV, ada_cols))
    dmod_rows = jnp.concatenate([dmod_mine, jnp.zeros((N_DEV, ada_cols), F32)], axis=0)
    out_ada = _ada_bwd_adam(jnp.transpose(c_rows), dmod_rows, w_ada[0], m_w_ada[0], v_w_ada[0])

    rep_all = [split_rep(rep_grad), split_rep(rep_delta), split_rep(rep_m), split_rep(rep_v)]
    conv_all = [g_conv_w[None], conv_delta[None, :3], conv_m[None, :3], conv_v[None, :3]]
    outs = [loss, grad_x[None]]
    for kind in range(4):
        b_ada_o, norm1_o, pool_scale_o, conv_b_o, gpool_o, gconv_o, norm2_o, final_o = rep_all[kind]
        outs += [out_ada[kind][None], b_ada_o, norm1_o, out_w_in[kind][None], out_wmix[kind], pool_scale_o,
                 conv_all[kind], conv_b_o, gpool_o, gconv_o, out_w_out[kind][None], norm2_o, out_w1[kind][None],
                 out_w2[kind][None], final_o]
    return tuple(outs)
```

```python
import jax
import jax.numpy as jnp
from jax import lax
from jax.experimental import pallas as pl
from jax.experimental.pallas import tpu as pltpu

F32 = jnp.float32
BF16 = jnp.bfloat16
MESH = pl.DeviceIdType.MESH

N_DEV = 8
N_MOD = 6
EPS = 1e-6
POOL_WINDOWS = (2, 4, 8, 16)
N_POOL_GROUPS = len(POOL_WINDOWS)
CONV_HEAD_DIM = 128
PAD_ROWS = 16

ADAM_LR = 0.001
ADAM_B1 = 0.9
ADAM_B2 = 0.999
ADAM_EPS = 1e-08
ADAM_WD = 0.01
ADAM_STEP = 10

VMEM_LIMIT_BYTES = 56 * 1024 * 1024
MM_TM, MM_TN, MM_TK = 1024, 512, 4096


def _pallas(body, deps=(), **kw):
    if not deps:
        return pl.pallas_call(body, **kw)
    n_in = len(kw["in_specs"])

    def with_deps(*refs):
        body(*refs[:n_in], *refs[n_in + len(deps):])

    kw["in_specs"] = list(kw["in_specs"]) + [pl.BlockSpec(memory_space=pl.ANY)] * len(deps)
    call = pl.pallas_call(with_deps, **kw)
    return lambda *operands: call(*operands, *deps)


def _params(*sem):
    return pltpu.CompilerParams(dimension_semantics=sem, vmem_limit_bytes=VMEM_LIMIT_BYTES)


def _tile(pref, dim):
    if dim <= pref:
        return dim
    for t in range(pref - pref % 128, 0, -128):
        if dim % t == 0:
            return t
    return dim


def _exchange(name, arrays, modes, deps=()):
    n = len(arrays)
    out_shape = []
    for a, mode in zip(arrays, modes):
        piece = a.shape if mode == "gather" else a.shape[1:]
        out_shape.append(jax.ShapeDtypeStruct((N_DEV,) + tuple(piece), a.dtype))

    def body(*refs):
        srcs, dsts = refs[:n], refs[n:2 * n]
        send_sems, recv_sems, local_sems = refs[2 * n:]
        x, y, c = lax.axis_index("x"), lax.axis_index("y"), lax.axis_index("c")
        me = 4 * x + 2 * y + c
        copies = []
        for i in range(n):
            gather = modes[i] == "gather"
            local = pltpu.make_async_copy(srcs[i] if gather else srcs[i].at[me], dsts[i].at[me], local_sems.at[i])
            local.start()
            copies.append(local)
            for k in range(1, N_DEV):
                kx, ky, kc = (k >> 2) & 1, (k >> 1) & 1, k & 1
                peer = (1 - x if kx else x, 1 - y if ky else y, 1 - c if kc else c)
                peer_idx = 4 * peer[0] + 2 * peer[1] + peer[2]
                remote = pltpu.make_async_remote_copy(
                    src_ref=srcs[i] if gather else srcs[i].at[peer_idx],
                    dst_ref=dsts[i].at[me],
                    send_sem=send_sems.at[i * (N_DEV - 1) + k - 1],
                    recv_sem=recv_sems.at[i * (N_DEV - 1) + k - 1],
                    device_id=peer, device_id_type=MESH)
                remote.start()
                copies.append(remote)
        for cp in copies:
            cp.wait()

    any_spec = pl.BlockSpec(memory_space=pl.ANY)
    return _pallas(
        body, deps, name=name, out_shape=out_shape,
        in_specs=[any_spec] * n, out_specs=[any_spec] * n,
        scratch_shapes=[pltpu.SemaphoreType.DMA((n * (N_DEV - 1),)),
                        pltpu.SemaphoreType.DMA((n * (N_DEV - 1),)),
                        pltpu.SemaphoreType.DMA((n,))],
    )(*arrays)


_HBM = pl.BlockSpec(memory_space=pltpu.HBM)
_SEM = pl.BlockSpec(memory_space=pltpu.SEMAPHORE)
_TOKEN = pl.BlockSpec(memory_space=pltpu.VMEM)
_EFFECT = pltpu.SideEffectType.DATAFLOW_SIDE_EFFECTING
N_CHIP = N_DEV // 2
_OTHER_CHIPS = (1, 2, 3)


def _place():
    x, y, c = lax.axis_index("x"), lax.axis_index("y"), lax.axis_index("c")
    return x, y, c, (x, y, 1 - c)


def _same_core_of(x, y, c, k):
    px = 1 - x if k & 2 else x
    py = 1 - y if k & 1 else y
    return (px, py, c), 2 * px + py


def _remote(src, dst, send_sem, recv_sem, device):
    return pltpu.make_async_remote_copy(src_ref=src, dst_ref=dst, send_sem=send_sem, recv_sem=recv_sem,
                                        device_id=device, device_id_type=MESH)


def _token_shape():
    return jax.ShapeDtypeStruct((8, 128), F32)


def _split_call(body, deps, name, operands, in_specs, out_shape, out_specs, aliases):
    return _pallas(body, deps, name=name, out_shape=out_shape, in_specs=in_specs, out_specs=out_specs,
                   input_output_aliases=aliases,
                   compiler_params=pltpu.CompilerParams(has_side_effects=_EFFECT))(*operands)


def _routes(x, y, c):
    first = (x + c - 2 * x * c, y + (1 - c) - 2 * y * (1 - c), c)
    second = (x + (1 - c) - 2 * x * (1 - c), y + c - 2 * y * c, c)
    return first, second, (1 - x, 1 - y, c)


def _index_of(device):
    return 4 * device[0] + 2 * device[1] + device[2]


def _gather_start(name, land, deps):
    def body(land_ref, send_sems, recv_first, recv_second, recv_d2d, land_thru, token):
        del land_thru
        x, y, c, sibling = _place()
        first, second, _ = _routes(x, y, c)
        mine = land_ref.at[4 * x + 2 * y + c]
        _remote(mine, mine, send_sems.at[0], recv_first.at[0], first).start()
        _remote(mine, mine, send_sems.at[1], recv_second.at[0], second).start()
        _remote(mine, mine, send_sems.at[2], recv_d2d.at[0], sibling).start()
        token[...] = jnp.zeros_like(token)

    one = pltpu.SemaphoreType.DMA((1,))
    return _split_call(
        body, deps, name, (pltpu.with_memory_space_constraint(land, pltpu.HBM),), (_HBM,),
        (pltpu.SemaphoreType.DMA((3,)), one, one, one, pltpu.HBM(land.shape, land.dtype), _token_shape()),
        (_SEM, _SEM, _SEM, _SEM, _HBM, _TOKEN), {0: 4})


def _gather_wait_local(name, started, land, after):
    recv_d2d = started[3]

    def body(land_ref, recv_d2d, after_ref, land_out):
        del after_ref, land_out
        x, y, c, sibling = _place()
        mine = land_ref.at[4 * x + 2 * y + c]
        _remote(mine, mine, recv_d2d.at[0], recv_d2d.at[0], sibling).wait_recv()

    return _split_call(
        body, (), name, (land, recv_d2d, after), (_HBM, _SEM, pl.BlockSpec(memory_space=pl.ANY)),
        (pltpu.HBM(land.shape, land.dtype),), (_HBM,), {0: 0})[0]


def _gather_hop(name, started, land, after):
    recv_first = started[1]

    def body(land_ref, recv_first, after_ref, send_hop, recv_hop, land_thru, token):
        del after_ref, land_thru
        x, y, c, _ = _place()
        first, second, _ = _routes(x, y, c)
        piece = land_ref.at[_index_of(first)]
        _remote(piece, piece, send_hop.at[0], recv_first.at[0], first).wait_recv()
        _remote(piece, piece, send_hop.at[0], recv_hop.at[0], second).start()
        token[...] = jnp.zeros_like(token)

    one = pltpu.SemaphoreType.DMA((1,))
    return _split_call(
        body, (), name, (land, recv_first, after), (_HBM, _SEM, pl.BlockSpec(memory_space=pl.ANY)),
        (one, one, pltpu.HBM(land.shape, land.dtype), _token_shape()), (_SEM, _SEM, _HBM, _TOKEN), {0: 2})


def _gather_relay(name, started, hopped, after):
    recv_second = started[2]
    _, recv_hop, land, _ = hopped

    def body(land_ref, recv_second, recv_hop, after_ref, send_fwd, recv_fwd, land_thru, token):
        del after_ref, land_thru
        token[...] = jnp.zeros_like(token)
        x, y, c, sibling = _place()
        mine = land_ref.at[4 * x + 2 * y + c]
        _remote(mine, mine, send_fwd.at[0], recv_second.at[0], sibling).wait_recv()
        _remote(mine, mine, send_fwd.at[0], recv_hop.at[0], sibling).wait_recv()
        for i, device in enumerate(_routes(x, y, c)):
            piece = land_ref.at[_index_of(device)]
            _remote(piece, piece, send_fwd.at[i], recv_fwd.at[i], sibling).start()

    return _split_call(
        body, (), name, (land, recv_second, recv_hop, after), (_HBM, _SEM, _SEM, pl.BlockSpec(memory_space=pl.ANY)),
        (pltpu.SemaphoreType.DMA((3,)), pltpu.SemaphoreType.DMA((3,)), pltpu.HBM(land.shape, land.dtype),
         _token_shape()),
        (_SEM, _SEM, _HBM, _TOKEN), {0: 2})


def _gather_wait(name, started, hopped, relayed, land, after, local_waited):
    send_sems, recv_d2d = started[0], started[3]
    send_hop = hopped[0]
    send_fwd, recv_fwd = relayed[0], relayed[1]

    def body(land_ref, send_sems, recv_d2d, send_hop, send_fwd, recv_fwd, after_ref, land_out):
        del after_ref, land_out
        x, y, c, sibling = _place()
        mine = land_ref.at[4 * x + 2 * y + c]
        for i in range(3):
            _remote(mine, mine, send_sems.at[i], recv_d2d.at[0], sibling).wait_send()
        _remote(mine, mine, send_hop.at[0], recv_d2d.at[0], sibling).wait_send()
        if not local_waited:
            _remote(mine, mine, send_sems.at[2], recv_d2d.at[0], sibling).wait_recv()
        for i in range(3):
            relay = _remote(mine, mine, send_fwd.at[i], recv_fwd.at[i], sibling)
            relay.wait_send()
            relay.wait_recv()

    return _split_call(
        body, (), name, (land, send_sems, recv_d2d, send_hop, send_fwd, recv_fwd, after),
        (_HBM, _SEM, _SEM, _SEM, _SEM, _SEM, pl.BlockSpec(memory_space=pl.ANY)),
        (pltpu.HBM(land.shape, land.dtype),), (_HBM,), {0: 0})[0]


def _push_start(name, srcs, modes, me, deps=()):
    n = len(srcs)
    lands = [_landing(src if mode == "gather" else lax.dynamic_index_in_dim(src, me, 0, keepdims=False), me)
             for src, mode in zip(srcs, modes)]

    def body(*refs):
        src_refs, land_refs, send_sems, recv_sems, token = refs[:n], refs[n:2 * n], refs[2 * n], refs[2 * n + 1], refs[-1]
        for cp in _push_copies(src_refs, land_refs, send_sems, recv_sems, modes):
            cp.start()
        token[...] = jnp.zeros_like(token)

    count = pltpu.SemaphoreType.DMA((n * (N_DEV - 1),))
    operands = [pltpu.with_memory_space_constraint(a, pltpu.HBM) for a in list(srcs) + lands]
    return _split_call(
        body, deps, name, operands, (_HBM,) * (2 * n),
        (count, count) + tuple(pltpu.HBM(a.shape, a.dtype) for a in operands) + (_token_shape(),),
        (_SEM, _SEM) + (_HBM,) * (2 * n) + (_TOKEN,), {i: 2 + i for i in range(2 * n)})


def _push_wait(name, started, modes, after):
    n = len(modes)
    send_sems, recv_sems = started[0], started[1]
    arrays = started[2:2 + 2 * n]

    def body(*refs):
        src_refs, land_refs, send_sems, recv_sems = refs[:n], refs[n:2 * n], refs[2 * n], refs[2 * n + 1]
        for cp in _push_copies(src_refs, land_refs, send_sems, recv_sems, modes):
            cp.wait_send()
            cp.wait_recv()

    return _split_call(
        body, (), name, tuple(arrays) + (send_sems, recv_sems, after),
        (_HBM,) * (2 * n) + (_SEM, _SEM, pl.BlockSpec(memory_space=pl.ANY)),
        tuple(pltpu.HBM(a.shape, a.dtype) for a in arrays), (_HBM,) * (2 * n), {i: i for i in range(2 * n)})[n:]


def _push_copies(src_refs, land_refs, send_sems, recv_sems, modes):
    x, y, c, _ = _place()
    me = 4 * x + 2 * y + c
    copies = []
    for i, mode in enumerate(modes):
        for k in range(1, N_DEV):
            peer = (1 - x if k & 4 else x, 1 - y if k & 2 else y, 1 - c if k & 1 else c)
            src = src_refs[i] if mode == "gather" else src_refs[i].at[_index_of(peer)]
            sem = i * (N_DEV - 1) + k - 1
            copies.append(_remote(src, land_refs[i].at[me], send_sems.at[sem], recv_sems.at[sem], peer))
    return copies


def _landing(own, me):
    land = lax.empty((N_DEV,) + own.shape, own.dtype)
    return lax.dynamic_update_slice(land, own[None], (me,) + (0,) * own.ndim)


def _pair_start(name, far, deps=()):
    pair = lax.empty(far.shape, far.dtype)

    def body(far_ref, pair_ref, send_sems, recv_sems, far_thru, pair_thru, token):
        del far_thru, pair_thru
        _remote(far_ref, pair_ref, send_sems.at[0], recv_sems.at[0], _place()[3]).start()
        token[...] = jnp.zeros_like(token)

    return _split_call(
        body, deps, name,
        (pltpu.with_memory_space_constraint(far, pltpu.HBM), pltpu.with_memory_space_constraint(pair, pltpu.HBM)),
        (_HBM, _HBM),
        (pltpu.SemaphoreType.DMA((1,)), pltpu.SemaphoreType.DMA((1,)),
         pltpu.HBM(far.shape, far.dtype), pltpu.HBM(pair.shape, pair.dtype), _token_shape()),
        (_SEM, _SEM, _HBM, _HBM, _TOKEN), {0: 2, 1: 3})


def _pair_wait(name, started, after):
    send_sems, recv_sems, far, pair, _ = started

    def body(far_ref, pair_ref, send_sems, recv_sems, after_ref, far_out, pair_out):
        del after_ref, far_out, pair_out
        cp = _remote(far_ref, pair_ref, send_sems.at[0], recv_sems.at[0], _place()[3])
        cp.wait_send()
        cp.wait_recv()

    return _split_call(
        body, (), name, (far, pair, send_sems, recv_sems, after),
        (_HBM, _HBM, _SEM, _SEM, pl.BlockSpec(memory_space=pl.ANY)),
        (pltpu.HBM(far.shape, far.dtype), pltpu.HBM(pair.shape, pair.dtype)), (_HBM, _HBM), {0: 0, 1: 1})[1]


def _chip_start(name, sums, deps=()):
    land = lax.empty((3,) + sums.shape[1:], sums.dtype)

    def body(s_ref, land_ref, send_sems, recv_sems, s_thru, land_thru, token):
        del s_thru, land_thru
        x, y, c, _ = _place()
        for k in _OTHER_CHIPS:
            peer, chip = _same_core_of(x, y, c, k)
            _remote(s_ref.at[chip], land_ref.at[k - 1], send_sems.at[k - 1], recv_sems.at[k - 1], peer).start()
        token[...] = jnp.zeros_like(token)

    return _split_call(
        body, deps, name,
        (pltpu.with_memory_space_constraint(sums, pltpu.HBM), pltpu.with_memory_space_constraint(land, pltpu.HBM)),
        (_HBM, _HBM),
        (pltpu.SemaphoreType.DMA((3,)), pltpu.SemaphoreType.DMA((3,)),
         pltpu.HBM(sums.shape, sums.dtype), pltpu.HBM(land.shape, land.dtype), _token_shape()),
        (_SEM, _SEM, _HBM, _HBM, _TOKEN), {0: 2, 1: 3})


def _chip_wait(name, started, after):
    send_sems, recv_sems, sums, land, _ = started

    def body(s_ref, land_ref, send_sems, recv_sems, after_ref, s_out, land_out):
        del after_ref, s_out, land_out
        x, y, c, _ = _place()
        for k in _OTHER_CHIPS:
            peer, chip = _same_core_of(x, y, c, k)
            cp = _remote(s_ref.at[chip], land_ref.at[k - 1], send_sems.at[k - 1], recv_sems.at[k - 1], peer)
            cp.wait_send()
            cp.wait_recv()

    return _split_call(
        body, (), name, (sums, land, send_sems, recv_sems, after),
        (_HBM, _HBM, _SEM, _SEM, pl.BlockSpec(memory_space=pl.ANY)),
        (pltpu.HBM(sums.shape, sums.dtype), pltpu.HBM(land.shape, land.dtype)), (_HBM, _HBM), {0: 0, 1: 1})


_DOT_DIMS = {"nn": (((1,), (0,)), ((), ())), "nt": (((1,), (1,)), ((), ())), "tn": (((0,), (0,)), ((), ()))}


def _matmul(name, mode, operands, in_specs, out_shape, out_specs, grid, acc_shape, epilogue, deps=(), carry=()):
    n_in, n_out, nk = len(operands), len(out_shape), grid[2]
    dims = _DOT_DIMS[mode]

    def body(*refs):
        a_ref, b_ref = refs[0], refs[1]
        extras, outs = refs[2:n_in], refs[n_in:n_in + n_out]
        b_val = b_ref[...]
        if b_val.ndim == 3:
            b_val = jnp.concatenate([b_val[g] for g in range(b_val.shape[0])], axis=1)
        part = lax.dot_general(a_ref[...], b_val, dims, preferred_element_type=F32)
        if nk == 1:
            epilogue(part, extras, outs)
            return
        acc = refs[-1]
        k = pl.program_id(2)

        @pl.when(k == 0)
        def _():
            acc[...] = part

        @pl.when(jnp.logical_and(k > 0, k < nk - 1))
        def _():
            acc[...] += part

        @pl.when(k == nk - 1)
        def _():
            epilogue(acc[...] + part, extras, outs)

    aliases = {n_in + len(deps) + i: i for i in range(len(carry))}
    return _pallas(body, tuple(deps) + tuple(carry), name=name, grid=grid, in_specs=in_specs, out_specs=out_specs,
                   out_shape=out_shape, input_output_aliases=aliases,
                   scratch_shapes=[pltpu.VMEM(acc_shape, F32)] if nk > 1 else [],
                   compiler_params=_params("parallel", "parallel", "arbitrary"))(*operands)


def _store(dtype):
    def epilogue(acc, extras, outs):
        outs[0][...] = acc.astype(dtype)
    return epilogue


def _residual_epilogue(acc, extras, outs):
    x_ref, gate_ref = extras
    outs[0][...] = acc.astype(outs[0].dtype)
    outs[1][...] = x_ref[...] + gate_ref[...] * acc


def _relu2_epilogue(acc, extras, outs):
    r = jnp.maximum(acc, 0.0)
    outs[0][...] = r.astype(outs[0].dtype)
    outs[1][...] = (r * r).astype(outs[1].dtype)


def _relu2_bwd_epilogue(acc, extras, outs):
    outs[0][...] = (acc * (2.0 * extras[0][...].astype(F32))).astype(outs[0].dtype)


def _no_extra_specs(tm, tn):
    return []


def _mm_nn(name, a, b, n_total, b_split, out_shape, epilogue, extras=(), extra_specs=_no_extra_specs, tm=MM_TM, tn=MM_TN, tk=MM_TK,
           deps=(), pieces=None, carry=()):
    m, kdim = a.shape
    tm, tk = _tile(tm, m), _tile(tk, kdim)
    n_blocks = None
    if b_split:
        piece = b.shape[2]
        tn = _tile(tn, piece)
        per = piece // tn
        if pieces is None:
            b_spec = pl.BlockSpec((None, tk, tn), lambda i, j, k: (j // per, k, j % per))
            out_spec = pl.BlockSpec((tm, tn), lambda i, j, k: (i, j))
        else:
            first, count = pieces
            n_blocks = count * per

            def which(j):
                return (first() + j // per) % N_DEV

            b_spec = pl.BlockSpec((None, tk, tn), lambda i, j, k: (which(j), k, j % per))
            out_spec = pl.BlockSpec((tm, tn), lambda i, j, k: (i, which(j) * per + j % per))
    else:
        tn = _tile(tn, n_total)
        b_spec = pl.BlockSpec((tk, tn), lambda i, j, k: (k, j))
        out_spec = pl.BlockSpec((tm, tn), lambda i, j, k: (i, j))
    if n_blocks is None:
        n_blocks = n_total // tn
    in_specs = [pl.BlockSpec((tm, tk), lambda i, j, k: (i, k)), b_spec] + list(extra_specs(tm, tn))
    return _matmul(name, "nn", (a, b) + tuple(extras), in_specs, out_shape, [out_spec] * len(out_shape),
                   (m // tm, n_blocks, kdim // tk), (tm, tn), epilogue, deps, carry)


def _mm_nt(name, a, b, n_total, b_split, out_shape, epilogue, extras=(), extra_specs=_no_extra_specs, tm=MM_TM, tn=MM_TN, tk=MM_TK,
           deps=()):
    m, kdim = a.shape
    tm, tn = _tile(tm, m), _tile(tn, n_total)
    if b_split and tk >= 2 * b.shape[2]:
        piece = b.shape[2]
        group = _tile(tk, kdim) // piece
        tk = group * piece
        b_spec = pl.BlockSpec((group, tn, piece), lambda i, j, k: (k, j, 0))
    elif b_split:
        piece = b.shape[2]
        tk = _tile(tk, piece)
        per = piece // tk
        b_spec = pl.BlockSpec((None, tn, tk), lambda i, j, k: (k // per, j, k % per))
    else:
        tk = _tile(tk, kdim)
        b_spec = pl.BlockSpec((tn, tk), lambda i, j, k: (j, k))
    in_specs = [pl.BlockSpec((tm, tk), lambda i, j, k: (i, k)), b_spec] + list(extra_specs(tm, tn))
    out_specs = [pl.BlockSpec((tm, tn), lambda i, j, k: (i, j)) for _ in out_shape]
    return _matmul(name, "nt", (a, b) + tuple(extras), in_specs, out_shape, out_specs,
                   (m // tm, n_total // tn, kdim // tk), (tm, tn), epilogue, deps)


def _add_pair_epilogue(acc, extras, outs):
    outs[0][...] = (acc + extras[0][...].astype(F32)).astype(outs[0].dtype)


def _mm_tn_half(name, a, b, col_pieces, near, pair=None, tm=MM_TM, tn=MM_TN, tk=MM_TK, deps=()):
    kdim, m = a.shape
    n_total = b.shape[1]
    tk = _tile(tk, kdim)

    def core():
        c = lax.axis_index("c")
        return c if near else 1 - c

    if col_pieces:
        piece = n_total // N_DEV
        tm, tn = _tile(tm, m), _tile(tn, piece)
        per = piece // tn
        grid = (m // tm, N_CHIP * per, kdim // tk)
        a_spec = pl.BlockSpec((tk, tm), lambda i, j, k: (k, i))
        b_spec = pl.BlockSpec((tk, tn), lambda i, j, k: (k, (2 * (j // per) + core()) * per + j % per))
        out_spec = pl.BlockSpec((None, tm, tn), lambda i, j, k: (j // per, i, j % per))
        out_shape = [jax.ShapeDtypeStruct((N_CHIP, m, piece), BF16)]
    else:
        piece = m // N_DEV
        tm, tn = _tile(tm, piece), _tile(tn, n_total)
        per = piece // tm
        grid = (N_CHIP * per, n_total // tn, kdim // tk)
        a_spec = pl.BlockSpec((tk, tm), lambda i, j, k: (k, (2 * (i // per) + core()) * per + i % per))
        b_spec = pl.BlockSpec((tk, tn), lambda i, j, k: (k, j))
        out_spec = pl.BlockSpec((None, tm, tn), lambda i, j, k: (i // per, i % per, j))
        out_shape = [jax.ShapeDtypeStruct((N_CHIP, piece, n_total), BF16)]
    operands, in_specs, epilogue = (a, b), [a_spec, b_spec], _store(BF16)
    if pair is not None:
        operands, in_specs, epilogue = (a, b, pair), [a_spec, b_spec, out_spec], _add_pair_epilogue
    return _matmul(name, "tn", operands, in_specs, out_shape, [out_spec], grid, (tm, tn), epilogue, deps)[0]


def _rms(xv):
    return lax.rsqrt(jnp.mean(xv * xv, axis=-1, keepdims=True) + EPS)


def _colsum(v):
    return jnp.sum(v, axis=0, keepdims=True)


def _norm_mod(name, x, g, scale, shift, tr=256, deps=()):
    s, d = x.shape
    tr = _tile(tr, s)

    def body(x_ref, g_ref, sc_ref, sh_ref, h_ref):
        xv = x_ref[...]
        h = (xv * _rms(xv)) * g_ref[...]
        h_ref[...] = (h * (1.0 + sc_ref[...]) + sh_ref[...]).astype(h_ref.dtype)

    row = pl.BlockSpec((tr, d), lambda i: (i, 0))
    vec = pl.BlockSpec((1, d), lambda i: (0, 0))
    return _pallas(body, deps, name=name, grid=(s // tr,), in_specs=[row, vec, vec, vec], out_specs=row,
                   out_shape=jax.ShapeDtypeStruct((s, d), BF16), compiler_params=_params("parallel"))(x, g, scale, shift)


def _loss_head(x3, target, gf, gate2, mlp, tr=128):
    s, d = x3.shape
    tr = _tile(tr, s)

    def body(x_ref, t_ref, gf_ref, gate_ref, mlp_ref, dx_ref, dbr_ref, dgf_ref, dgate_ref, loss_ref):
        @pl.when(pl.program_id(0) == 0)
        def _():
            dgf_ref[...] = jnp.zeros_like(dgf_ref)
            dgate_ref[...] = jnp.zeros_like(dgate_ref)
            loss_ref[...] = jnp.zeros_like(loss_ref)

        xv = x_ref[...]
        r = _rms(xv)
        xn = xv * r
        gfv = gf_ref[...]
        err = xn * gfv - t_ref[...]
        loss_ref[...] += 0.5 * _colsum(jnp.mean(err * err, axis=-1, keepdims=True))
        dy = err * (1.0 / d)
        dgf_ref[...] += _colsum(dy * xn)
        dxn = dy * gfv
        dx = r * (dxn - xn * jnp.mean(dxn * xn, axis=-1, keepdims=True))
        dx_ref[...] = dx
        dbr_ref[...] = (dx * gate_ref[...]).astype(dbr_ref.dtype)
        dgate_ref[...] += _colsum(dx * mlp_ref[...].astype(F32))

    row = pl.BlockSpec((tr, d), lambda i: (i, 0))
    vec = pl.BlockSpec((1, d), lambda i: (0, 0))
    return _pallas(
        body, name="loss_head", grid=(s // tr,), in_specs=[row, row, vec, vec, row],
        out_specs=[row, row, vec, vec, pl.BlockSpec((1, 128), lambda i: (0, 0))],
        out_shape=[jax.ShapeDtypeStruct((s, d), F32), jax.ShapeDtypeStruct((s, d), BF16),
                   jax.ShapeDtypeStruct((1, d), F32), jax.ShapeDtypeStruct((1, d), F32),
                   jax.ShapeDtypeStruct((1, 128), F32)],
        compiler_params=_params("arbitrary"))(x3, target, gf, gate2, mlp)


def _norm_mod_bwd(name, dh, xin, g, scale, dx_up, branch=None, gate=None, tr=128, deps=()):
    s, d = xin.shape
    tr = _tile(tr, s)
    with_gate = branch is not None

    def body(*refs):
        dh_ref, x_ref, g_ref, sc_ref, up_ref = refs[:5]
        if with_gate:
            br_ref, gate_ref = refs[5:7]
            dx_ref, dsh_ref, dsc_ref, dg_ref, dbr_ref, dgate_ref = refs[7:]
            sums = (dsh_ref, dsc_ref, dg_ref, dgate_ref)
        else:
            dx_ref, dsh_ref, dsc_ref, dg_ref = refs[5:]
            sums = (dsh_ref, dsc_ref, dg_ref)

        @pl.when(pl.program_id(0) == 0)
        def _():
            for ref in sums:
                ref[...] = jnp.zeros_like(ref)

        xv, dhv, gv = x_ref[...], dh_ref[...].astype(F32), g_ref[...]
        r = _rms(xv)
        xn = xv * r
        one_sc = 1.0 + sc_ref[...]
        dsh_ref[...] += _colsum(dhv)
        dsc_ref[...] += _colsum(dhv * (xn * gv))
        dg_ref[...] += _colsum(dhv * one_sc * xn)
        dxn = dhv * one_sc * gv
        dx = up_ref[...] + r * (dxn - xn * jnp.mean(dxn * xn, axis=-1, keepdims=True))
        dx_ref[...] = dx
        if with_gate:
            dbr_ref[...] = (dx * gate_ref[...]).astype(dbr_ref.dtype)
            dgate_ref[...] += _colsum(dx * br_ref[...].astype(F32))

    row = pl.BlockSpec((tr, d), lambda i: (i, 0))
    vec = pl.BlockSpec((1, d), lambda i: (0, 0))
    vshape = jax.ShapeDtypeStruct((1, d), F32)
    operands = [dh, xin, g, scale, dx_up]
    in_specs = [row, row, vec, vec, row]
    out_shape = [jax.ShapeDtypeStruct((s, d), F32), vshape, vshape, vshape]
    out_specs = [row, vec, vec, vec]
    if with_gate:
        operands += [branch, gate]
        in_specs += [row, vec]
        out_shape += [jax.ShapeDtypeStruct((s, d), BF16), vshape]
        out_specs += [row, vec]
    return _pallas(body, deps, name=name, grid=(s // tr,), in_specs=in_specs, out_specs=out_specs, out_shape=out_shape,
                   compiler_params=_params("arbitrary"))(*operands)


def _window_count(c0, rows, half, s):
    t = c0 + lax.broadcasted_iota(jnp.int32, (rows, 1), 0)
    return (jnp.minimum(t + half, s) - jnp.maximum(t - half, 0)).astype(F32)


def _zero_pads(pad, s):
    zeros = jnp.zeros((PAD_ROWS, pad.shape[1]), pad.dtype)
    pad[0:PAD_ROWS, :] = zeros
    pad[PAD_ROWS + s:PAD_ROWS + s + PAD_ROWS, :] = zeros


def _pool_fwd(proj, s, gd, cb, ch, deps=()):
    nsub = gd // cb

    def body(v_ref, o_ref, pad):
        g = pl.program_id(0)
        _zero_pads(pad, s)
        pad[PAD_ROWS:PAD_ROWS + s, :] = v_ref[...].astype(F32)
        for gi, window in enumerate(POOL_WINDOWS):
            half = window // 2

            @pl.when(g == gi)
            def _(half=half):
                for c0 in range(0, s, ch):
                    base = PAD_ROWS + c0
                    acc = pad[base - half:base - half + ch, :]
                    for j in range(-half + 1, half):
                        acc = acc + pad[base + j:base + j + ch, :]
                    out = acc / _window_count(c0, ch, half, s) - pad[base:base + ch, :]
                    o_ref[c0:c0 + ch, :] = out.astype(o_ref.dtype)

    spec = pl.BlockSpec((s, cb), lambda g, j: (0, g * nsub + j))
    return _pallas(body, deps, name="pool_fwd", grid=(N_POOL_GROUPS, nsub), in_specs=[spec], out_specs=spec,
                   out_shape=jax.ShapeDtypeStruct((s, N_POOL_GROUPS * gd), BF16),
                   scratch_shapes=[pltpu.VMEM((s + 2 * PAD_ROWS, cb), F32)],
                   compiler_params=_params("parallel", "parallel"))(proj)


def _pool_bwd(dpooled, dproj, s, gd, cb, ch):
    nsub = gd // cb

    def body(dp_ref, dproj_in, o_ref, pad):
        del dproj_in
        g = pl.program_id(0)
        _zero_pads(pad, s)
        for gi, window in enumerate(POOL_WINDOWS):
            half = window // 2

            @pl.when(g == gi)
            def _(half=half):
                for c0 in range(0, s, ch):
                    pad[PAD_ROWS + c0:PAD_ROWS + c0 + ch, :] = dp_ref[c0:c0 + ch, :] / _window_count(c0, ch, half, s)
                for c0 in range(0, s, ch):
                    base = PAD_ROWS + c0
                    acc = pad[base - half + 1:base - half + 1 + ch, :]
                    for j in range(-half + 2, half + 1):
                        acc = acc + pad[base + j:base + j + ch, :]
                    o_ref[c0:c0 + ch, :] = (acc - dp_ref[c0:c0 + ch, :]).astype(o_ref.dtype)

    spec = pl.BlockSpec((s, cb), lambda g, j: (0, g * nsub + j))
    return _pallas(body, name="pool_bwd", grid=(N_POOL_GROUPS, nsub),
                   in_specs=[spec, pl.BlockSpec(memory_space=pl.ANY)], out_specs=spec,
                   out_shape=jax.ShapeDtypeStruct(dproj.shape, dproj.dtype), input_output_aliases={1: 0},
                   scratch_shapes=[pltpu.VMEM((s + 2 * PAD_ROWS, cb), F32)],
                   compiler_params=_params("parallel", "parallel"))(dpooled, dproj)


def _poolmix_fwd(pooled, wmix, pool_scale, gnorm_g, d_model, tm=512):
    s = pooled.shape[0]
    gd = wmix.shape[1]
    tm = _tile(tm, s)

    def body(p_ref, w_ref, ps_ref, g_ref, apre_ref, mixed_ref):
        a_pre = jnp.dot(p_ref[...], w_ref[...], preferred_element_type=F32)
        apre_ref[...] = a_pre
        a_out = a_pre * ps_ref[...]
        mixed_ref[...] = ((a_out * _rms(a_out)) * g_ref[...]).astype(mixed_ref.dtype)

    blk = pl.BlockSpec((tm, gd), lambda g, i: (i, g))
    vec = pl.BlockSpec((1, gd), lambda g, i: (0, g))
    return _pallas(body, name="poolmix_fwd", grid=(N_POOL_GROUPS, s // tm),
                   in_specs=[blk, pl.BlockSpec((None, gd, gd), lambda g, i: (g, 0, 0)), vec, vec],
                   out_specs=[blk, blk],
                   out_shape=[jax.ShapeDtypeStruct((s, N_POOL_GROUPS * gd), F32), jax.ShapeDtypeStruct((s, d_model), BF16)],
                   compiler_params=_params("parallel", "parallel"))(pooled, wmix, pool_scale, gnorm_g)


def _poolmix_bwd(dmixed, a_pre, wmix, pool_scale, gnorm_g, tm=512):
    s = a_pre.shape[0]
    gd = wmix.shape[1]
    tm = _tile(tm, s)

    def body(dm_ref, apre_ref, w_ref, ps_ref, g_ref, dapre_ref, dpooled_ref, dps_ref, dg_ref):
        @pl.when(pl.program_id(1) == 0)
        def _():
            dps_ref[...] = jnp.zeros_like(dps_ref)
            dg_ref[...] = jnp.zeros_like(dg_ref)

        a_pre, dm, ps = apre_ref[...], dm_ref[...].astype(F32), ps_ref[...]
        a_out = a_pre * ps
        r = _rms(a_out)
        n = a_out * r
        dg_ref[...] += _colsum(dm * n)
        dn = dm * g_ref[...]
        da_out = r * (dn - n * jnp.mean(dn * n, axis=-1, keepdims=True))
        dps_ref[...] += _colsum(da_out * a_pre)
        da_pre = (da_out * ps).astype(BF16)
        dapre_ref[...] = da_pre
        dpooled_ref[...] = lax.dot_general(da_pre, w_ref[...], _DOT_DIMS["nt"], preferred_element_type=F32)

    blk = pl.BlockSpec((tm, gd), lambda g, i: (i, g))
    vec = pl.BlockSpec((1, gd), lambda g, i: (0, g))
    width = N_POOL_GROUPS * gd
    return _pallas(body, name="poolmix_bwd", grid=(N_POOL_GROUPS, s // tm),
                   in_specs=[blk, blk, pl.BlockSpec((None, gd, gd), lambda g, i: (g, 0, 0)), vec, vec],
                   out_specs=[blk, blk, vec, vec],
                   out_shape=[jax.ShapeDtypeStruct((s, width), BF16), jax.ShapeDtypeStruct((s, width), F32),
                              jax.ShapeDtypeStruct((1, width), F32), jax.ShapeDtypeStruct((1, width), F32)],
                   compiler_params=_params("parallel", "arbitrary"))(dmixed, a_pre, wmix, pool_scale, gnorm_g)


def _poolmix_wgrad(pooled, da_pre, gd, tk=1024):
    s = pooled.shape[0]
    tk = _tile(tk, s)
    nk = s // tk

    def body(p_ref, d_ref, o_ref, acc):
        k = pl.program_id(1)
        part = lax.dot_general(p_ref[...], d_ref[...], _DOT_DIMS["tn"], preferred_element_type=F32)

        @pl.when(k == 0)
        def _():
            acc[...] = part

        @pl.when(k > 0)
        def _():
            acc[...] += part

        @pl.when(k == nk - 1)
        def _():
            o_ref[...] = acc[...].astype(o_ref.dtype)

    blk = pl.BlockSpec((tk, gd), lambda g, k: (k, g))
    return _pallas(body, name="poolmix_wgrad", grid=(N_POOL_GROUPS, nk), in_specs=[blk, blk],
                   out_specs=pl.BlockSpec((None, gd, gd), lambda g, k: (g, 0, 0)),
                   out_shape=jax.ShapeDtypeStruct((N_POOL_GROUPS, gd, gd), BF16),
                   scratch_shapes=[pltpu.VMEM((gd, gd), F32)],
                   compiler_params=_params("parallel", "arbitrary"))(pooled, da_pre)


def _head_mean(v):
    parts = []
    for q in range(v.shape[1] // CONV_HEAD_DIM):
        m = jnp.mean(v[:, q * CONV_HEAD_DIM:(q + 1) * CONV_HEAD_DIM], axis=-1, keepdims=True)
        parts.append(jnp.broadcast_to(m, (v.shape[0], CONV_HEAD_DIM)))
    return parts[0] if len(parts) == 1 else jnp.concatenate(parts, axis=1)


def _conv_fwd(proj, mixed, conv_w, conv_b, gnorm_g, s, width, cb, ch, deps=()):
    nblk = width // cb

    def body(b_ref, c_ref, u_ref, w_ref, cb_ref, g_ref, mixed_in, o_ref, pad):
        del mixed_in
        _zero_pads(pad, s)
        pad[PAD_ROWS:PAD_ROWS + s, :] = c_ref[...].astype(F32) * u_ref[...].astype(F32)
        w = w_ref[...]
        for c0 in range(0, s, ch):
            base = PAD_ROWS + c0
            conv = (w[0:1] * pad[base - 1:base - 1 + ch, :] + w[1:2] * pad[base:base + ch, :]
                    + w[2:3] * pad[base + 1:base + 1 + ch, :] + cb_ref[...])
            bo = b_ref[c0:c0 + ch, :].astype(F32) * conv
            n = bo * lax.rsqrt(_head_mean(bo * bo) + EPS)
            o_ref[c0:c0 + ch, :] = (n * g_ref[...]).astype(o_ref.dtype)

    def part(p):
        return pl.BlockSpec((s, cb), lambda j: (0, p * nblk + j))

    vec = pl.BlockSpec((1, cb), lambda j: (0, j))
    return _pallas(body, deps, name="conv_fwd", grid=(nblk,),
                   in_specs=[part(1), part(2), part(3), pl.BlockSpec((3, cb), lambda j: (0, j)), vec, vec,
                             pl.BlockSpec(memory_space=pl.ANY)],
                   out_specs=part(1), out_shape=jax.ShapeDtypeStruct(mixed.shape, mixed.dtype),
                   input_output_aliases={6: 0},
                   scratch_shapes=[pltpu.VMEM((s + 2 * PAD_ROWS, cb), F32)],
                   compiler_params=_params("parallel"))(proj, proj, proj, conv_w, conv_b, gnorm_g, mixed)


def _conv_bwd(dmixed, proj, conv_w, conv_b, gnorm_g, s, width, cb, ch, deps=()):
    nblk = width // cb

    def body(dm_ref, b_ref, c_ref, u_ref, w_ref, cb_ref, g_ref, dproj_ref, dw_ref, dcb_ref, dg_ref,
             pad_cu, pad_dconv, db_buf, dc_buf, du_buf, sems):
        j = pl.program_id(0)
        _zero_pads(pad_cu, s)
        _zero_pads(pad_dconv, s)
        pad_cu[PAD_ROWS:PAD_ROWS + s, :] = c_ref[...].astype(F32) * u_ref[...].astype(F32)
        w, gv = w_ref[...], g_ref[...]
        zero = jnp.zeros((1, cb), F32)
        dw0, dw1, dw2, dcb, dg = zero, zero, zero, zero, zero
        for c0 in range(0, s, ch):
            base = PAD_ROWS + c0
            cu_prev, cu_here, cu_next = (pad_cu[base - 1:base - 1 + ch, :], pad_cu[base:base + ch, :],
                                         pad_cu[base + 1:base + 1 + ch, :])
            conv = w[0:1] * cu_prev + w[1:2] * cu_here + w[2:3] * cu_next + cb_ref[...]
            bg = b_ref[c0:c0 + ch, :].astype(F32)
            bo = bg * conv
            r = lax.rsqrt(_head_mean(bo * bo) + EPS)
            n = bo * r
            dm = dm_ref[c0:c0 + ch, :].astype(F32)
            dg = dg + _colsum(dm * n)
            dn = dm * gv
            dbo = r * (dn - n * _head_mean(dn * n))
            db_buf[c0:c0 + ch, :] = (dbo * conv).astype(BF16)
            dconv = dbo * bg
            pad_dconv[base:base + ch, :] = dconv
            dcb = dcb + _colsum(dconv)
            dw0 = dw0 + _colsum(dconv * cu_prev)
            dw1 = dw1 + _colsum(dconv * cu_here)
            dw2 = dw2 + _colsum(dconv * cu_next)
        dw_ref[0:1, :] = dw0
        dw_ref[1:2, :] = dw1
        dw_ref[2:3, :] = dw2
        dcb_ref[...] = dcb
        dg_ref[...] = dg
        for c0 in range(0, s, ch):
            base = PAD_ROWS + c0
            dcu = (w[0:1] * pad_dconv[base + 1:base + 1 + ch, :] + w[1:2] * pad_dconv[base:base + ch, :]
                   + w[2:3] * pad_dconv[base - 1:base - 1 + ch, :])
            dc_buf[c0:c0 + ch, :] = (dcu * u_ref[c0:c0 + ch, :].astype(F32)).astype(BF16)
            du_buf[c0:c0 + ch, :] = (dcu * c_ref[c0:c0 + ch, :].astype(F32)).astype(BF16)
        copies = []
        for p, buf in enumerate((db_buf, dc_buf, du_buf)):
            col = pl.multiple_of((p + 1) * width + j * cb, CONV_HEAD_DIM)
            copies.append(pltpu.make_async_copy(buf, dproj_ref.at[:, pl.ds(col, cb)], sems.at[p]))
            copies[-1].start()
        for cp in copies:
            cp.wait()

    def part(p):
        return pl.BlockSpec((s, cb), lambda j: (0, p * nblk + j))

    vec = pl.BlockSpec((1, cb), lambda j: (0, j))
    w_spec = pl.BlockSpec((3, cb), lambda j: (0, j))
    return _pallas(body, deps, name="conv_bwd", grid=(nblk,),
                   in_specs=[part(1), part(1), part(2), part(3), w_spec, vec, vec],
                   out_specs=[pl.BlockSpec(memory_space=pl.ANY), w_spec, vec, vec],
                   out_shape=[jax.ShapeDtypeStruct((s, 4 * width), BF16), jax.ShapeDtypeStruct((3, width), F32),
                              jax.ShapeDtypeStruct((1, width), F32), jax.ShapeDtypeStruct((1, width), F32)],
                   scratch_shapes=[pltpu.VMEM((s + 2 * PAD_ROWS, cb), F32), pltpu.VMEM((s + 2 * PAD_ROWS, cb), F32),
                                   pltpu.VMEM((s, cb), BF16), pltpu.VMEM((s, cb), BF16), pltpu.VMEM((s, cb), BF16),
                                   pltpu.SemaphoreType.DMA((3,))],
                   compiler_params=_params("arbitrary"))(dmixed, proj, proj, proj, conv_w, conv_b, gnorm_g)


def _adamw(w, g, m, v):
    m = ADAM_B1 * m + (1.0 - ADAM_B1) * g
    v = ADAM_B2 * v + (1.0 - ADAM_B2) * (g * g)
    m_hat = m / (1.0 - ADAM_B1 ** ADAM_STEP)
    v_hat = v / (1.0 - ADAM_B2 ** ADAM_STEP)
    delta = -ADAM_LR * (m_hat / (jnp.sqrt(v_hat) + ADAM_EPS) + ADAM_WD * w)
    return delta, m, v


def _ada_fwd(c_rows, w, b, tn=512):
    rows, d = c_rows.shape
    n = w.shape[1]
    tn = _tile(tn, n)

    def body(c_ref, w_ref, b_ref, o_ref):
        cv = c_ref[...]
        act = (cv * jax.nn.sigmoid(cv)).astype(BF16)
        o_ref[...] = jnp.dot(act, w_ref[...].astype(BF16), preferred_element_type=F32) + b_ref[...]

    return _pallas(body, name="ada_fwd", grid=(n // tn,),
                   in_specs=[pl.BlockSpec((rows, d), lambda j: (0, 0)), pl.BlockSpec((d, tn), lambda j: (0, j)),
                             pl.BlockSpec((1, tn), lambda j: (0, j))],
                   out_specs=pl.BlockSpec((rows, tn), lambda j: (0, j)),
                   out_shape=jax.ShapeDtypeStruct((rows, n), F32), compiler_params=_params("parallel"))(c_rows, w, b)


def _ada_bwd_adam(c_cols, dmod, w, m, v, tr=512, tn=1024):
    d, rows = c_cols.shape
    n = w.shape[1]
    tr, tn = _tile(tr, d), _tile(tn, n)

    def body(c_ref, dm_ref, w_ref, m_ref, v_ref, g_ref, dl_ref, nm_ref, nv_ref):
        cv = c_ref[...]
        act = (cv * jax.nn.sigmoid(cv)).astype(BF16)
        g = jnp.dot(act, dm_ref[...].astype(BF16), preferred_element_type=F32)
        g_ref[...] = g
        dl_ref[...], nm_ref[...], nv_ref[...] = _adamw(w_ref[...], g, m_ref[...], v_ref[...])

    blk = pl.BlockSpec((tr, tn), lambda i, j: (i, j))
    shape = jax.ShapeDtypeStruct((d, n), F32)
    return _pallas(body, name="ada_bwd_adam", grid=(d // tr, n // tn),
                   in_specs=[pl.BlockSpec((tr, rows), lambda i, j: (i, 0)), pl.BlockSpec((rows, tn), lambda i, j: (0, j)),
                             blk, blk, blk],
                   out_specs=[blk] * 4, out_shape=[shape] * 4,
                   compiler_params=_params("parallel", "parallel"))(c_cols, dmod, w, m, v)


def _reduce_adam(name, pieces, w, m, v, tr=256, tc=1024):
    r, c = w.shape
    tr, tc = _tile(tr, r), _tile(tc, c)

    def body(p_ref, w_ref, m_ref, v_ref, g_ref, dl_ref, nm_ref, nv_ref):
        g = p_ref[0].astype(F32)
        for j in range(1, N_DEV):
            g = g + p_ref[j].astype(F32)
        g_ref[...] = g
        dl_ref[...], nm_ref[...], nv_ref[...] = _adamw(w_ref[...], g, m_ref[...], v_ref[...])

    blk = pl.BlockSpec((tr, tc), lambda i, j: (i, j))
    shape = jax.ShapeDtypeStruct((r, c), F32)
    return _pallas(body, name=name, grid=(r // tr, c // tc),
                   in_specs=[pl.BlockSpec((N_DEV, tr, tc), lambda i, j: (0, i, j)), blk, blk, blk],
                   out_specs=[blk] * 4, out_shape=[shape] * 4,
                   compiler_params=_params("parallel", "parallel"))(pieces, w, m, v)


def _reduce_adam_chips(name, sums, land, w, m, v, tr=256, tc=1024):
    r, c = w.shape
    tr, tc = _tile(tr, r), _tile(tc, c)

    def body(s_ref, l_ref, w_ref, m_ref, v_ref, g_ref, dl_ref, nm_ref, nv_ref):
        g = s_ref[...].astype(F32)
        for k in range(3):
            g = g + l_ref[k].astype(F32)
        g_ref[...] = g
        dl_ref[...], nm_ref[...], nv_ref[...] = _adamw(w_ref[...], g, m_ref[...], v_ref[...])

    blk = pl.BlockSpec((tr, tc), lambda i, j: (i, j))
    shape = jax.ShapeDtypeStruct((r, c), F32)
    mine = pl.BlockSpec((None, tr, tc), lambda i, j: (2 * lax.axis_index("x") + lax.axis_index("y"), i, j))
    return _pallas(body, name=name, grid=(r // tr, c // tc),
                   in_specs=[mine, pl.BlockSpec((3, tr, tc), lambda i, j: (0, i, j)), blk, blk, blk],
                   out_specs=[blk] * 4, out_shape=[shape] * 4,
                   compiler_params=_params("parallel", "parallel"))(sums, land, w, m, v)


def _sum_devices(parts):
    n = parts.shape[1]

    def body(p_ref, o_ref):
        acc = p_ref[0:1, :]
        for j in range(1, N_DEV):
            acc = acc + p_ref[j:j + 1, :]
        o_ref[...] = acc

    return _pallas(body, name="sum_devices", out_shape=jax.ShapeDtypeStruct((1, n), F32),
                   compiler_params=pltpu.CompilerParams(vmem_limit_bytes=VMEM_LIMIT_BYTES))(parts)


def _adam_small(name, g, w, m, v):
    def body(g_ref, w_ref, m_ref, v_ref, dl_ref, nm_ref, nv_ref):
        dl_ref[...], nm_ref[...], nv_ref[...] = _adamw(w_ref[...], g_ref[...], m_ref[...], v_ref[...])

    shape = jax.ShapeDtypeStruct(w.shape, F32)
    return _pallas(body, name=name, out_shape=[shape] * 3,
                   compiler_params=pltpu.CompilerParams(vmem_limit_bytes=VMEM_LIMIT_BYTES))(g, w, m, v)


def kernel(x, c, w_ada, b_ada, norm1_g, w_in, pool_mix_w, pool_scale, conv_w, conv_b, gnorm_pool_g, gnorm_conv_g, w_out, norm2_g, w_mlp_in, w_mlp_out, final_g, loss_target, m_w_ada, m_b_ada, m_norm1_g, m_w_in, m_pool_mix_w, m_pool_scale, m_conv_w, m_conv_b, m_gnorm_pool_g, m_gnorm_conv_g, m_w_out, m_norm2_g, m_w_mlp_in, m_w_mlp_out, m_final_g, v_w_ada, v_b_ada, v_norm1_g, v_w_in, v_pool_mix_w, v_pool_scale, v_conv_w, v_conv_b, v_gnorm_pool_g, v_gnorm_conv_g, v_w_out, v_norm2_g, v_w_mlp_in, v_w_mlp_out, v_final_g):
    s, d = x.shape[1], x.shape[2]
    width = d // 2
    gd = width // N_POOL_GROUPS
    d_ff = w_mlp_in.shape[2] * N_DEV
    n_proj = w_in.shape[2] * N_DEV
    ada_cols = w_ada.shape[2]
    conv_cols = conv_w.shape[2]
    assert n_proj == 4 * width and ada_cols * N_DEV == N_MOD * d and d_ff % N_DEV == 0
    assert width % CONV_HEAD_DIM == 0 and s % 8 == 0
    seq_chunk = _tile(512, s)
    pool_cb = _tile(256, gd)
    conv_cb = CONV_HEAD_DIM

    me = 4 * lax.axis_index("x") + 2 * lax.axis_index("y") + lax.axis_index("c")
    x2d, target = x[0], loss_target[0]

    wmix_all, conv_w_all, c_all = _exchange(
        "gather_small_weights", [pool_mix_w[0].astype(BF16), conv_w[0], c], ["gather"] * 3)
    wmix_full = jnp.transpose(wmix_all, (1, 0, 2, 3)).reshape(N_POOL_GROUPS, gd, gd)
    conv_w_full = jnp.transpose(conv_w_all, (1, 0, 2)).reshape(3, width)
    c_rows = jnp.concatenate([c_all.reshape(N_DEV, d), jnp.zeros((N_DEV, d), F32)], axis=0)

    b_mine = lax.dynamic_slice(b_ada, (0, me * ada_cols), (1, ada_cols))
    mod_part = _ada_fwd(c_rows, w_ada[0], b_mine)
    (mod_all,) = _exchange("scatter_mod", [mod_part[:N_DEV].reshape(N_DEV, 1, ada_cols)], ["a2a"])
    mod = mod_all.reshape(1, N_MOD * d)

    started, hopped, relayed = {}, {}, {}

    def gather_start(wname, wgt, deps):
        land = _landing(wgt[0].astype(BF16), me)
        started[wname] = _gather_start("gather_" + wname + "_start", land, deps)
        return started[wname][5]

    def gather_hop(wname, land, after):
        hopped[wname] = _gather_hop("gather_" + wname + "_hop", started[wname], land, after)
        return hopped[wname][3]

    def gather_relay(wname, after):
        relayed[wname] = _gather_relay("gather_" + wname + "_relay", started[wname], hopped[wname], after)
        return relayed[wname][3]

    def gather_wait(wname, land, after, local_waited=False):
        return _gather_wait("gather_" + wname + "_wait", started[wname], hopped[wname], relayed[wname], land, after,
                            local_waited)

    def first_local():
        return 2 * (2 * lax.axis_index("x") + lax.axis_index("y"))

    def first_remote():
        return first_local() + 2

    def last_remote():
        return first_local() + 5

    tok_w_in = gather_start("w_in", w_in, (mod,))
    shift1, scale1, gate1, shift2, scale2, gate2 = [mod[:, i * d:(i + 1) * d] for i in range(N_MOD)]

    h1 = _norm_mod("norm1_fwd", x2d, norm1_g, scale1, shift1, deps=(tok_w_in,))
    proj_shape = [jax.ShapeDtypeStruct((s, n_proj), BF16)]
    w_in_local = _gather_wait_local("gather_w_in_local", started["w_in"], started["w_in"][4], h1)
    (proj,) = _mm_nn("in_proj_local", h1, w_in_local, n_proj, True, proj_shape, _store(BF16), pieces=(first_local, 2))
    tok = gather_hop("w_in", w_in_local, proj)
    tok = gather_start("w_out", w_out, (tok,))
    tok = gather_relay("w_in", tok)
    tok = gather_hop("w_out", started["w_out"][4], tok)
    tok = gather_start("w_mlp_in", w_mlp_in, (tok,))
    w_in_all = gather_wait("w_in", relayed["w_in"][2], tok, True)
    (proj,) = _mm_nn("in_proj", h1, w_in_all, n_proj, True, proj_shape, _store(BF16), pieces=(first_remote, 6),
                     carry=(proj,))
    pooled = _pool_fwd(proj, s, gd, pool_cb, seq_chunk)
    a_pre, mixed = _poolmix_fwd(pooled, wmix_full, pool_scale, gnorm_pool_g, d)
    tok = gather_hop("w_mlp_in", started["w_mlp_in"][4], a_pre)
    tok = gather_start("w_mlp_out", w_mlp_out, (tok,))
    tok = gather_relay("w_out", tok)
    mixed = _conv_fwd(proj, mixed, conv_w_full, conv_b, gnorm_conv_g, s, width, conv_cb, seq_chunk, deps=(tok,))

    def residual_specs(tm, tn):
        return [pl.BlockSpec((tm, tn), lambda i, j, k: (i, j)), pl.BlockSpec((1, tn), lambda i, j, k: (0, j))]

    sd_f32 = jax.ShapeDtypeStruct((s, d), F32)
    w_out_full = gather_wait("w_out", relayed["w_out"][2], mixed).reshape(d, d)
    sd_bf16 = jax.ShapeDtypeStruct((s, d), BF16)
    attn, x_mid = _mm_nn("out_proj", mixed, w_out_full, d, False, [sd_bf16, sd_f32], _residual_epilogue,
                         extras=(x2d, gate1), extra_specs=residual_specs)
    h2 = _norm_mod("norm2_fwd", x_mid, norm2_g, scale2, shift2)
    sf_bf16 = [jax.ShapeDtypeStruct((s, d_ff), BF16)] * 2
    tok = gather_relay("w_mlp_in", h2)
    w1_local = _gather_wait_local("gather_w_mlp_in_local", started["w_mlp_in"], relayed["w_mlp_in"][2], tok)
    relu, hid = _mm_nn("mlp_in_local", h2, w1_local, d_ff, True, sf_bf16, _relu2_epilogue, pieces=(first_local, 2))
    w1_all = gather_wait("w_mlp_in", w1_local, hid, True)
    tok = gather_hop("w_mlp_out", started["w_mlp_out"][4], w1_all)
    relu, hid = _mm_nn("mlp_in_remote", h2, w1_all, d_ff, True, sf_bf16, _relu2_epilogue,
                       pieces=(first_remote, 3), carry=(relu, hid), deps=(tok,))
    tok = gather_relay("w_mlp_out", hid)
    relu, hid = _mm_nn("mlp_in", h2, w1_all, d_ff, True, sf_bf16, _relu2_epilogue,
                       pieces=(last_remote, 3), carry=(relu, hid), deps=(tok,))
    w2_full = gather_wait("w_mlp_out", relayed["w_mlp_out"][2], hid).reshape(d_ff, d)
    mlp, x_last = _mm_nn("mlp_out", hid, w2_full, d, False, [sd_bf16, sd_f32], _residual_epilogue,
                         extras=(x_mid, gate2), extra_specs=residual_specs)

    dx_last, dmlp, d_final_g, dgate2, loss_row = _loss_head(x_last, target, final_g.reshape(1, d), gate2, mlp)

    def relu_specs(tm, tn):
        return [pl.BlockSpec((tm, tn), lambda i, j, k: (i, j))]

    def reduce_start(wname, a, b, col_pieces, deps=()):
        far = _mm_tn_half(wname + "_dw_far", a, b, col_pieces, near=False, deps=deps)
        return _pair_start("scatter_" + wname + "_pair_start", far)

    def reduce_chips(wname, a, b, col_pieces, pairs, after):
        pair = _pair_wait("scatter_" + wname + "_pair_wait", pairs, after)
        sums = _mm_tn_half(wname + "_dw_near", a, b, col_pieces, near=True, pair=pair)
        return _chip_start("scatter_" + wname + "_chip_start", sums)

    pairs_w2 = reduce_start("mlp_out", hid, dmlp, False)
    (dhpre,) = _mm_nt("mlp_out_dx", dmlp, w2_full, d_ff, False, sf_bf16[:1], _relu2_bwd_epilogue,
                      extras=(relu,), extra_specs=relu_specs, deps=(pairs_w2[4],))
    chips_w2 = reduce_chips("mlp_out", hid, dmlp, False, pairs_w2, dhpre)
    pairs_w1 = reduce_start("mlp_in", h2, dhpre, True, deps=(chips_w2[4],))
    (dh2,) = _mm_nt("mlp_in_dx", dhpre, w1_all, d, True, [sd_bf16], _store(BF16), deps=(pairs_w1[4],))
    chips_w1 = reduce_chips("mlp_in", h2, dhpre, True, pairs_w1, dh2)
    dx_mid, dshift2, dscale2, d_norm2_g, dattn, dgate1 = _norm_mod_bwd(
        "norm2_bwd", dh2, x_mid, norm2_g, scale2, dx_last, branch=attn, gate=gate1, deps=(chips_w1[4],))

    pairs_w_out = reduce_start("out_proj", mixed, dattn, False)
    (dmixed,) = _mm_nt("out_proj_dx", dattn, w_out_full, d, False, [sd_bf16], _store(BF16), deps=(pairs_w_out[4],))
    chips_w_out = reduce_chips("out_proj", mixed, dattn, False, pairs_w_out, dmixed)
    dproj, d_conv_w, d_conv_b, d_gnorm_conv = _conv_bwd(dmixed, proj, conv_w_full, conv_b, gnorm_conv_g,
                                                        s, width, conv_cb, seq_chunk, deps=(chips_w_out[4],))
    da_pre, dpooled, d_pool_scale, d_gnorm_pool = _poolmix_bwd(dmixed, a_pre, wmix_full, pool_scale, gnorm_pool_g)
    g_wmix = _poolmix_wgrad(pooled, da_pre, gd)
    dproj = _pool_bwd(dpooled, dproj, s, gd, pool_cb, seq_chunk)

    pairs_w_in = reduce_start("in_proj", h1, dproj, True)
    (dh1,) = _mm_nt("in_proj_dx", dproj, w_in_all, d, True, [sd_bf16], _store(BF16), deps=(pairs_w_in[4],))
    chips_w_in = reduce_chips("in_proj", h1, dproj, True, pairs_w_in, dh1)
    grad_x, dshift1, dscale1, d_norm1_g = _norm_mod_bwd("norm1_bwd", dh1, x2d, norm1_g, scale1, dx_mid,
                                                        deps=(chips_w_in[4],))

    rows_mix = gd // N_DEV
    g_wmix_split = jnp.transpose(g_wmix.reshape(N_POOL_GROUPS, N_DEV, rows_mix, gd), (1, 0, 2, 3))
    g_wmix_split = g_wmix_split.reshape(N_DEV, N_POOL_GROUPS * rows_mix, gd)
    loss_pad = jnp.concatenate([loss_row[:, :1], jnp.zeros((1, 127), F32)], axis=1)
    dmod = jnp.concatenate([dshift1, dscale1, dgate1, dshift2, dscale2, dgate2], axis=1)
    small = jnp.concatenate([dmod, d_norm1_g, d_pool_scale, d_conv_b, d_gnorm_pool, d_gnorm_conv, d_norm2_g,
                             d_final_g, d_conv_w.reshape(1, 3 * width), loss_pad], axis=1)
    small_started = _push_start("exchange_small_grads_start", [g_wmix_split, small], ("a2a", "gather"), me)

    sums, landed = _chip_wait("scatter_w_mlp_out_chip_wait", chips_w2, small_started[-1])
    out_w2 = _reduce_adam_chips("adam_w_mlp_out", sums, landed, w_mlp_out[0], m_w_mlp_out[0], v_w_mlp_out[0])
    sums, landed = _chip_wait("scatter_w_mlp_in_chip_wait", chips_w1, out_w2[0])
    out_w1 = _reduce_adam_chips("adam_w_mlp_in", sums, landed, w_mlp_in[0], m_w_mlp_in[0], v_w_mlp_in[0])
    sums, landed = _chip_wait("scatter_w_out_chip_wait", chips_w_out, out_w1[0])
    out_w_out = _reduce_adam_chips("adam_w_out", sums, landed, w_out[0], m_w_out[0], v_w_out[0])
    sums, landed = _chip_wait("scatter_w_in_chip_wait", chips_w_in, out_w_out[0])
    out_w_in = _reduce_adam_chips("adam_w_in", sums, landed, w_in[0], m_w_in[0], v_w_in[0])

    p_wmix, small_all = _push_wait("exchange_small_grads_wait", small_started, ("a2a", "gather"), out_w_in[0])
    mix_shape = (N_POOL_GROUPS * rows_mix, gd)
    out_wmix = _reduce_adam("adam_pool_mix", p_wmix, pool_mix_w.reshape(mix_shape), m_pool_mix_w.reshape(mix_shape),
                            v_pool_mix_w.reshape(mix_shape))
    out_wmix = [a.reshape(pool_mix_w.shape) for a in out_wmix]
    small_all = small_all.reshape(N_DEV, small.shape[1])
    small_sum = _sum_devices(small_all)

    n_rep = (N_MOD + 1) * d + 4 * width + 2 * d
    loss = small_sum[0, n_rep + 3 * width]
    rep_names_w = [b_ada, norm1_g, pool_scale, conv_b, gnorm_pool_g, gnorm_conv_g, norm2_g, final_g.reshape(1, d)]
    rep_names_m = [m_b_ada, m_norm1_g, m_pool_scale, m_conv_b, m_gnorm_pool_g, m_gnorm_conv_g, m_norm2_g,
                   m_final_g.reshape(1, d)]
    rep_names_v = [v_b_ada, v_norm1_g, v_pool_scale, v_conv_b, v_gnorm_pool_g, v_gnorm_conv_g, v_norm2_g,
                   v_final_g.reshape(1, d)]
    rep_grad = small_sum[:, :n_rep]
    rep_delta, rep_m, rep_v = _adam_small("adam_replicated", rep_grad, jnp.concatenate(rep_names_w, axis=1),
                                          jnp.concatenate(rep_names_m, axis=1), jnp.concatenate(rep_names_v, axis=1))

    def split_rep(vec):
        out, off = [], 0
        for wgt in rep_names_w:
            n = wgt.shape[1]
            out.append(vec[:, off:off + n])
            off += n
        out[-1] = out[-1].reshape(d)
        return out

    conv_grad_full = small_sum[:, n_rep:n_rep + 3 * width].reshape(3, width)
    g_conv_w = lax.dynamic_slice(conv_grad_full, (0, me * conv_cols), (3, conv_cols))
    g_conv_w8 = jnp.concatenate([g_conv_w, jnp.zeros((5, conv_cols), F32)], axis=0)

    def pad8(a):
        return jnp.concatenate([a[0], jnp.zeros((5, conv_cols), F32)], axis=0)

    conv_delta, conv_m, conv_v = _adam_small("adam_conv_w", g_conv_w8, pad8(conv_w), pad8(m_conv_w), pad8(v_conv_w))

    dmod_all = small_all[:, :N_MOD * d]
    dmod_mine = lax.dynamic_slice(dmod_all, (0, me * ada_cols), (N_DEV, ada_cols))
    dmod_rows = jnp.concatenate([dmod_mine, jnp.zeros((N_DEV, ada_cols), F32)], axis=0)
    out_ada = _ada_bwd_adam(jnp.transpose(c_rows), dmod_rows, w_ada[0], m_w_ada[0], v_w_ada[0])

    rep_all = [split_rep(rep_grad), split_rep(rep_delta), split_rep(rep_m), split_rep(rep_v)]
    conv_all = [g_conv_w[None], conv_delta[None, :3], conv_m[None, :3], conv_v[None, :3]]
    outs = [loss, grad_x[None]]
    for kind in range(4):
        b_ada_o, norm1_o, pool_scale_o, conv_b_o, gpool_o, gconv_o, norm2_o, final_o = rep_all[kind]
        outs += [out_ada[kind][None], b_ada_o, norm1_o, out_w_in[kind][None], out_wmix[kind], pool_scale_o,
                 conv_all[kind], conv_b_o, gpool_o, gconv_o, out_w_out[kind][None], norm2_o, out_w1[kind][None],
                 out_w2[kind][None], final_o]
    return tuple(outs)
```

```python
import jax
import jax.numpy as jnp
from jax import lax
from jax.experimental import pallas as pl
from jax.experimental.pallas import tpu as pltpu

F32 = jnp.float32
BF16 = jnp.bfloat16
MESH = pl.DeviceIdType.MESH

N_DEV = 8
N_MOD = 6
EPS = 1e-6
POOL_WINDOWS = (2, 4, 8, 16)
N_POOL_GROUPS = len(POOL_WINDOWS)
CONV_HEAD_DIM = 128
PAD_ROWS = 16

ADAM_LR = 0.001
ADAM_B1 = 0.9
ADAM_B2 = 0.999
ADAM_EPS = 1e-08
ADAM_WD = 0.01
ADAM_STEP = 10

VMEM_LIMIT_BYTES = 56 * 1024 * 1024
MM_TM, MM_TN, MM_TK = 1024, 512, 4096


def _pallas(body, deps=(), **kw):
    if not deps:
        return pl.pallas_call(body, **kw)
    n_in = len(kw["in_specs"])

    def with_deps(*refs):
        body(*refs[:n_in], *refs[n_in + len(deps):])

    kw["in_specs"] = list(kw["in_specs"]) + [pl.BlockSpec(memory_space=pl.ANY)] * len(deps)
    call = pl.pallas_call(with_deps, **kw)
    return lambda *operands: call(*operands, *deps)


def _params(*sem):
    return pltpu.CompilerParams(dimension_semantics=sem, vmem_limit_bytes=VMEM_LIMIT_BYTES)


def _tile(pref, dim):
    if dim <= pref:
        return dim
    for t in range(pref - pref % 128, 0, -128):
        if dim % t == 0:
            return t
    return dim


def _exchange(name, arrays, modes, deps=()):
    n = len(arrays)
    out_shape = []
    for a, mode in zip(arrays, modes):
        piece = a.shape if mode == "gather" else a.shape[1:]
        out_shape.append(jax.ShapeDtypeStruct((N_DEV,) + tuple(piece), a.dtype))

    def body(*refs):
        srcs, dsts = refs[:n], refs[n:2 * n]
        send_sems, recv_sems, local_sems = refs[2 * n:]
        x, y, c = lax.axis_index("x"), lax.axis_index("y"), lax.axis_index("c")
        me = 4 * x + 2 * y + c
        copies = []
        for i in range(n):
            gather = modes[i] == "gather"
            local = pltpu.make_async_copy(srcs[i] if gather else srcs[i].at[me], dsts[i].at[me], local_sems.at[i])
            local.start()
            copies.append(local)
            for k in range(1, N_DEV):
                kx, ky, kc = (k >> 2) & 1, (k >> 1) & 1, k & 1
                peer = (1 - x if kx else x, 1 - y if ky else y, 1 - c if kc else c)
                peer_idx = 4 * peer[0] + 2 * peer[1] + peer[2]
                remote = pltpu.make_async_remote_copy(
                    src_ref=srcs[i] if gather else srcs[i].at[peer_idx],
                    dst_ref=dsts[i].at[me],
                    send_sem=send_sems.at[i * (N_DEV - 1) + k - 1],
                    recv_sem=recv_sems.at[i * (N_DEV - 1) + k - 1],
                    device_id=peer, device_id_type=MESH)
                remote.start()
                copies.append(remote)
        for cp in copies:
            cp.wait()

    any_spec = pl.BlockSpec(memory_space=pl.ANY)
    return _pallas(
        body, deps, name=name, out_shape=out_shape,
        in_specs=[any_spec] * n, out_specs=[any_spec] * n,
        scratch_shapes=[pltpu.SemaphoreType.DMA((n * (N_DEV - 1),)),
                        pltpu.SemaphoreType.DMA((n * (N_DEV - 1),)),
                        pltpu.SemaphoreType.DMA((n,))],
    )(*arrays)


_HBM = pl.BlockSpec(memory_space=pltpu.HBM)
_SEM = pl.BlockSpec(memory_space=pltpu.SEMAPHORE)
_TOKEN = pl.BlockSpec(memory_space=pltpu.VMEM)
_EFFECT = pltpu.SideEffectType.DATAFLOW_SIDE_EFFECTING
N_CHIP = N_DEV // 2
_OTHER_CHIPS = (1, 2, 3)


def _place():
    x, y, c = lax.axis_index("x"), lax.axis_index("y"), lax.axis_index("c")
    return x, y, c, (x, y, 1 - c)


def _same_core_of(x, y, c, k):
    px = 1 - x if k & 2 else x
    py = 1 - y if k & 1 else y
    return (px, py, c), 2 * px + py


def _remote(src, dst, send_sem, recv_sem, device):
    return pltpu.make_async_remote_copy(src_ref=src, dst_ref=dst, send_sem=send_sem, recv_sem=recv_sem,
                                        device_id=device, device_id_type=MESH)


def _token_shape():
    return jax.ShapeDtypeStruct((8, 128), F32)


def _split_call(body, deps, name, operands, in_specs, out_shape, out_specs, aliases):
    return _pallas(body, deps, name=name, out_shape=out_shape, in_specs=in_specs, out_specs=out_specs,
                   input_output_aliases=aliases,
                   compiler_params=pltpu.CompilerParams(has_side_effects=_EFFECT))(*operands)


def _routes(x, y, c):
    first = (x + c - 2 * x * c, y + (1 - c) - 2 * y * (1 - c), c)
    second = (x + (1 - c) - 2 * x * (1 - c), y + c - 2 * y * c, c)
    return first, second, (1 - x, 1 - y, c)


def _index_of(device):
    return 4 * device[0] + 2 * device[1] + device[2]


def _gather_start(name, land, deps):
    def body(land_ref, send_sems, recv_first, recv_second, recv_d2d, land_thru, token):
        del land_thru
        x, y, c, sibling = _place()
        first, second, _ = _routes(x, y, c)
        mine = land_ref.at[4 * x + 2 * y + c]
        _remote(mine, mine, send_sems.at[0], recv_first.at[0], first).start()
        _remote(mine, mine, send_sems.at[1], recv_second.at[0], second).start()
        _remote(mine, mine, send_sems.at[2], recv_d2d.at[0], sibling).start()
        token[...] = jnp.zeros_like(token)

    one = pltpu.SemaphoreType.DMA((1,))
    return _split_call(
        body, deps, name, (pltpu.with_memory_space_constraint(land, pltpu.HBM),), (_HBM,),
        (pltpu.SemaphoreType.DMA((3,)), one, one, one, pltpu.HBM(land.shape, land.dtype), _token_shape()),
        (_SEM, _SEM, _SEM, _SEM, _HBM, _TOKEN), {0: 4})


def _gather_wait_local(name, started, land, after):
    recv_d2d = started[3]

    def body(land_ref, recv_d2d, after_ref, land_out):
        del after_ref, land_out
        x, y, c, sibling = _place()
        mine = land_ref.at[4 * x + 2 * y + c]
        _remote(mine, mine, recv_d2d.at[0], recv_d2d.at[0], sibling).wait_recv()

    return _split_call(
        body, (), name, (land, recv_d2d, after), (_HBM, _SEM, pl.BlockSpec(memory_space=pl.ANY)),
        (pltpu.HBM(land.shape, land.dtype),), (_HBM,), {0: 0})[0]


def _gather_hop(name, started, land, after):
    recv_first = started[1]

    def body(land_ref, recv_first, after_ref, send_hop, recv_hop, land_thru, token):
        del after_ref, land_thru
        x, y, c, _ = _place()
        first, second, _ = _routes(x, y, c)
        piece = land_ref.at[_index_of(first)]
        _remote(piece, piece, send_hop.at[0], recv_first.at[0], first).wait_recv()
        _remote(piece, piece, send_hop.at[0], recv_hop.at[0], second).start()
        token[...] = jnp.zeros_like(token)

    one = pltpu.SemaphoreType.DMA((1,))
    return _split_call(
        body, (), name, (land, recv_first, after), (_HBM, _SEM, pl.BlockSpec(memory_space=pl.ANY)),
        (one, one, pltpu.HBM(land.shape, land.dtype), _token_shape()), (_SEM, _SEM, _HBM, _TOKEN), {0: 2})


def _gather_relay(name, started, hopped, after):
    recv_second = started[2]
    _, recv_hop, land, _ = hopped

    def body(land_ref, recv_second, recv_hop, after_ref, send_fwd, recv_fwd, land_thru, token):
        del after_ref, land_thru
        token[...] = jnp.zeros_like(token)
        x, y, c, sibling = _place()
        mine = land_ref.at[4 * x + 2 * y + c]
        _remote(mine, mine, send_fwd.at[0], recv_second.at[0], sibling).wait_recv()
        _remote(mine, mine, send_fwd.at[0], recv_hop.at[0], sibling).wait_recv()
        for i, device in enumerate(_routes(x, y, c)):
            piece = land_ref.at[_index_of(device)]
            _remote(piece, piece, send_fwd.at[i], recv_fwd.at[i], sibling).start()

    return _split_call(
        body, (), name, (land, recv_second, recv_hop, after), (_HBM, _SEM, _SEM, pl.BlockSpec(memory_space=pl.ANY)),
        (pltpu.SemaphoreType.DMA((3,)), pltpu.SemaphoreType.DMA((3,)), pltpu.HBM(land.shape, land.dtype),
         _token_shape()),
        (_SEM, _SEM, _HBM, _TOKEN), {0: 2})


def _gather_wait(name, started, hopped, relayed, land, after, local_waited):
    send_sems, recv_d2d = started[0], started[3]
    send_hop = hopped[0]
    send_fwd, recv_fwd = relayed[0], relayed[1]

    def body(land_ref, send_sems, recv_d2d, send_hop, send_fwd, recv_fwd, after_ref, land_out):
        del after_ref, land_out
        x, y, c, sibling = _place()
        mine = land_ref.at[4 * x + 2 * y + c]
        for i in range(3):
            _remote(mine, mine, send_sems.at[i], recv_d2d.at[0], sibling).wait_send()
        _remote(mine, mine, send_hop.at[0], recv_d2d.at[0], sibling).wait_send()
        if not local_waited:
            _remote(mine, mine, send_sems.at[2], recv_d2d.at[0], sibling).wait_recv()
        for i in range(3):
            relay = _remote(mine, mine, send_fwd.at[i], recv_fwd.at[i], sibling)
            relay.wait_send()
            relay.wait_recv()

    return _split_call(
        body, (), name, (land, send_sems, recv_d2d, send_hop, send_fwd, recv_fwd, after),
        (_HBM, _SEM, _SEM, _SEM, _SEM, _SEM, pl.BlockSpec(memory_space=pl.ANY)),
        (pltpu.HBM(land.shape, land.dtype),), (_HBM,), {0: 0})[0]


def _push_start(name, srcs, modes, me, deps=()):
    n = len(srcs)
    lands = [_landing(src if mode == "gather" else lax.dynamic_index_in_dim(src, me, 0, keepdims=False), me)
             for src, mode in zip(srcs, modes)]

    def body(*refs):
        src_refs, land_refs, send_sems, recv_sems, token = refs[:n], refs[n:2 * n], refs[2 * n], refs[2 * n + 1], refs[-1]
        for cp in _push_copies(src_refs, land_refs, send_sems, recv_sems, modes):
            cp.start()
        token[...] = jnp.zeros_like(token)

    count = pltpu.SemaphoreType.DMA((n * (N_DEV - 1),))
    operands = [pltpu.with_memory_space_constraint(a, pltpu.HBM) for a in list(srcs) + lands]
    return _split_call(
        body, deps, name, operands, (_HBM,) * (2 * n),
        (count, count) + tuple(pltpu.HBM(a.shape, a.dtype) for a in operands) + (_token_shape(),),
        (_SEM, _SEM) + (_HBM,) * (2 * n) + (_TOKEN,), {i: 2 + i for i in range(2 * n)})


def _push_wait(name, started, modes, after):
    n = len(modes)
    send_sems, recv_sems = started[0], started[1]
    arrays = started[2:2 + 2 * n]

    def body(*refs):
        src_refs, land_refs, send_sems, recv_sems = refs[:n], refs[n:2 * n], refs[2 * n], refs[2 * n + 1]
        for cp in _push_copies(src_refs, land_refs, send_sems, recv_sems, modes):
            cp.wait_send()
            cp.wait_recv()

    return _split_call(
        body, (), name, tuple(arrays) + (send_sems, recv_sems, after),
        (_HBM,) * (2 * n) + (_SEM, _SEM, pl.BlockSpec(memory_space=pl.ANY)),
        tuple(pltpu.HBM(a.shape, a.dtype) for a in arrays), (_HBM,) * (2 * n), {i: i for i in range(2 * n)})[n:]


def _push_copies(src_refs, land_refs, send_sems, recv_sems, modes):
    x, y, c, _ = _place()
    me = 4 * x + 2 * y + c
    copies = []
    for i, mode in enumerate(modes):
        for k in range(1, N_DEV):
            peer = (1 - x if k & 4 else x, 1 - y if k & 2 else y, 1 - c if k & 1 else c)
            src = src_refs[i] if mode == "gather" else src_refs[i].at[_index_of(peer)]
            sem = i * (N_DEV - 1) + k - 1
            copies.append(_remote(src, land_refs[i].at[me], send_sems.at[sem], recv_sems.at[sem], peer))
    return copies


def _landing(own, me):
    land = lax.empty((N_DEV,) + own.shape, own.dtype)
    return lax.dynamic_update_slice(land, own[None], (me,) + (0,) * own.ndim)


def _pair_start(name, far, deps=()):
    pair = lax.empty(far.shape, far.dtype)

    def body(far_ref, pair_ref, send_sems, recv_sems, far_thru, pair_thru, token):
        del far_thru, pair_thru
        _remote(far_ref, pair_ref, send_sems.at[0], recv_sems.at[0], _place()[3]).start()
        token[...] = jnp.zeros_like(token)

    return _split_call(
        body, deps, name,
        (pltpu.with_memory_space_constraint(far, pltpu.HBM), pltpu.with_memory_space_constraint(pair, pltpu.HBM)),
        (_HBM, _HBM),
        (pltpu.SemaphoreType.DMA((1,)), pltpu.SemaphoreType.DMA((1,)),
         pltpu.HBM(far.shape, far.dtype), pltpu.HBM(pair.shape, pair.dtype), _token_shape()),
        (_SEM, _SEM, _HBM, _HBM, _TOKEN), {0: 2, 1: 3})


def _pair_wait(name, started, after):
    send_sems, recv_sems, far, pair, _ = started

    def body(far_ref, pair_ref, send_sems, recv_sems, after_ref, far_out, pair_out):
        del after_ref, far_out, pair_out
        cp = _remote(far_ref, pair_ref, send_sems.at[0], recv_sems.at[0], _place()[3])
        cp.wait_send()
        cp.wait_recv()

    return _split_call(
        body, (), name, (far, pair, send_sems, recv_sems, after),
        (_HBM, _HBM, _SEM, _SEM, pl.BlockSpec(memory_space=pl.ANY)),
        (pltpu.HBM(far.shape, far.dtype), pltpu.HBM(pair.shape, pair.dtype)), (_HBM, _HBM), {0: 0, 1: 1})[1]


def _chip_start(name, sums, deps=()):
    land = lax.empty((3,) + sums.shape[1:], sums.dtype)

    def body(s_ref, land_ref, send_sems, recv_sems, s_thru, land_thru, token):
        del s_thru, land_thru
        x, y, c, _ = _place()
        for k in _OTHER_CHIPS:
            peer, chip = _same_core_of(x, y, c, k)
            _remote(s_ref.at[chip], land_ref.at[k - 1], send_sems.at[k - 1], recv_sems.at[k - 1], peer).start()
        token[...] = jnp.zeros_like(token)

    return _split_call(
        body, deps, name,
        (pltpu.with_memory_space_constraint(sums, pltpu.HBM), pltpu.with_memory_space_constraint(land, pltpu.HBM)),
        (_HBM, _HBM),
        (pltpu.SemaphoreType.DMA((3,)), pltpu.SemaphoreType.DMA((3,)),
         pltpu.HBM(sums.shape, sums.dtype), pltpu.HBM(land.shape, land.dtype), _token_shape()),
        (_SEM, _SEM, _HBM, _HBM, _TOKEN), {0: 2, 1: 3})


def _chip_wait(name, started, after):
    send_sems, recv_sems, sums, land, _ = started

    def body(s_ref, land_ref, send_sems, recv_sems, after_ref, s_out, land_out):
        del after_ref, s_out, land_out
        x, y, c, _ = _place()
        for k in _OTHER_CHIPS:
            peer, chip = _same_core_of(x, y, c, k)
            cp = _remote(s_ref.at[chip], land_ref.at[k - 1], send_sems.at[k - 1], recv_sems.at[k - 1], peer)
            cp.wait_send()
            cp.wait_recv()

    return _split_call(
        body, (), name, (sums, land, send_sems, recv_sems, after),
        (_HBM, _HBM, _SEM, _SEM, pl.BlockSpec(memory_space=pl.ANY)),
        (pltpu.HBM(sums.shape, sums.dtype), pltpu.HBM(land.shape, land.dtype)), (_HBM, _HBM), {0: 0, 1: 1})


_DOT_DIMS = {"nn": (((1,), (0,)), ((), ())), "nt": (((1,), (1,)), ((), ())), "tn": (((0,), (0,)), ((), ()))}


def _matmul(name, mode, operands, in_specs, out_shape, out_specs, grid, acc_shape, epilogue, deps=(), carry=()):
    n_in, n_out, nk = len(operands), len(out_shape), grid[2]
    dims = _DOT_DIMS[mode]

    def body(*refs):
        a_ref, b_ref = refs[0], refs[1]
        extras, outs = refs[2:n_in], refs[n_in:n_in + n_out]
        b_val = b_ref[...]
        if b_val.ndim == 3:
            b_val = jnp.concatenate([b_val[g] for g in range(b_val.shape[0])], axis=1)
        part = lax.dot_general(a_ref[...], b_val, dims, preferred_element_type=F32)
        if nk == 1:
            epilogue(part, extras, outs)
            return
        acc = refs[-1]
        k = pl.program_id(2)

        @pl.when(k == 0)
        def _():
            acc[...] = part

        @pl.when(jnp.logical_and(k > 0, k < nk - 1))
        def _():
            acc[...] += part

        @pl.when(k == nk - 1)
        def _():
            epilogue(acc[...] + part, extras, outs)

    aliases = {n_in + len(deps) + i: i for i in range(len(carry))}
    return _pallas(body, tuple(deps) + tuple(carry), name=name, grid=grid, in_specs=in_specs, out_specs=out_specs,
                   out_shape=out_shape, input_output_aliases=aliases,
                   scratch_shapes=[pltpu.VMEM(acc_shape, F32)] if nk > 1 else [],
                   compiler_params=_params("parallel", "parallel", "arbitrary"))(*operands)


def _store(dtype):
    def epilogue(acc, extras, outs):
        outs[0][...] = acc.astype(dtype)
    return epilogue


def _residual_epilogue(acc, extras, outs):
    x_ref, gate_ref = extras
    outs[0][...] = acc.astype(outs[0].dtype)
    outs[1][...] = x_ref[...] + gate_ref[...] * acc


def _relu2_epilogue(acc, extras, outs):
    r = jnp.maximum(acc, 0.0)
    outs[0][...] = r.astype(outs[0].dtype)
    outs[1][...] = (r * r).astype(outs[1].dtype)


def _relu2_bwd_epilogue(acc, extras, outs):
    outs[0][...] = (acc * (2.0 * extras[0][...].astype(F32))).astype(outs[0].dtype)


def _no_extra_specs(tm, tn):
    return []


def _mm_nn(name, a, b, n_total, b_split, out_shape, epilogue, extras=(), extra_specs=_no_extra_specs, tm=MM_TM, tn=MM_TN, tk=MM_TK,
           deps=(), pieces=None, carry=()):
    m, kdim = a.shape
    tm, tk = _tile(tm, m), _tile(tk, kdim)
    n_blocks = None
    if b_split:
        piece = b.shape[2]
        tn = _tile(tn, piece)
        per = piece // tn
        if pieces is None:
            b_spec = pl.BlockSpec((None, tk, tn), lambda i, j, k: (j // per, k, j % per))
            out_spec = pl.BlockSpec((tm, tn), lambda i, j, k: (i, j))
        else:
            piece_of, count = pieces
            n_blocks = count * per

            def which(j):
                return piece_of(j // per)

            b_spec = pl.BlockSpec((None, tk, tn), lambda i, j, k: (which(j), k, j % per))
            out_spec = pl.BlockSpec((tm, tn), lambda i, j, k: (i, which(j) * per + j % per))
    else:
        tn = _tile(tn, n_total)
        b_spec = pl.BlockSpec((tk, tn), lambda i, j, k: (k, j))
        out_spec = pl.BlockSpec((tm, tn), lambda i, j, k: (i, j))
    if n_blocks is None:
        n_blocks = n_total // tn
    in_specs = [pl.BlockSpec((tm, tk), lambda i, j, k: (i, k)), b_spec] + list(extra_specs(tm, tn))
    return _matmul(name, "nn", (a, b) + tuple(extras), in_specs, out_shape, [out_spec] * len(out_shape),
                   (m // tm, n_blocks, kdim // tk), (tm, tn), epilogue, deps, carry)


def _mm_nt(name, a, b, n_total, b_split, out_shape, epilogue, extras=(), extra_specs=_no_extra_specs, tm=MM_TM, tn=MM_TN, tk=MM_TK,
           deps=()):
    m, kdim = a.shape
    tm, tn = _tile(tm, m), _tile(tn, n_total)
    if b_split and tk >= 2 * b.shape[2]:
        piece = b.shape[2]
        group = _tile(tk, kdim) // piece
        tk = group * piece
        b_spec = pl.BlockSpec((group, tn, piece), lambda i, j, k: (k, j, 0))
    elif b_split:
        piece = b.shape[2]
        tk = _tile(tk, piece)
        per = piece // tk
        b_spec = pl.BlockSpec((None, tn, tk), lambda i, j, k: (k // per, j, k % per))
    else:
        tk = _tile(tk, kdim)
        b_spec = pl.BlockSpec((tn, tk), lambda i, j, k: (j, k))
    in_specs = [pl.BlockSpec((tm, tk), lambda i, j, k: (i, k)), b_spec] + list(extra_specs(tm, tn))
    out_specs = [pl.BlockSpec((tm, tn), lambda i, j, k: (i, j)) for _ in out_shape]
    return _matmul(name, "nt", (a, b) + tuple(extras), in_specs, out_shape, out_specs,
                   (m // tm, n_total // tn, kdim // tk), (tm, tn), epilogue, deps)


def _add_pair_epilogue(acc, extras, outs):
    outs[0][...] = (acc + extras[0][...].astype(F32)).astype(outs[0].dtype)


def _mm_tn_half(name, a, b, col_pieces, near, pair=None, tm=MM_TM, tn=MM_TN, tk=MM_TK, deps=()):
    kdim, m = a.shape
    n_total = b.shape[1]
    tk = _tile(tk, kdim)

    def core():
        c = lax.axis_index("c")
        return c if near else 1 - c

    if col_pieces:
        piece = n_total // N_DEV
        tm, tn = _tile(tm, m), _tile(tn, piece)
        per = piece // tn
        grid = (m // tm, N_CHIP * per, kdim // tk)
        a_spec = pl.BlockSpec((tk, tm), lambda i, j, k: (k, i))
        b_spec = pl.BlockSpec((tk, tn), lambda i, j, k: (k, (2 * (j // per) + core()) * per + j % per))
        out_spec = pl.BlockSpec((None, tm, tn), lambda i, j, k: (j // per, i, j % per))
        out_shape = [jax.ShapeDtypeStruct((N_CHIP, m, piece), BF16)]
    else:
        piece = m // N_DEV
        tm, tn = _tile(tm, piece), _tile(tn, n_total)
        per = piece // tm
        grid = (N_CHIP * per, n_total // tn, kdim // tk)
        a_spec = pl.BlockSpec((tk, tm), lambda i, j, k: (k, (2 * (i // per) + core()) * per + i % per))
        b_spec = pl.BlockSpec((tk, tn), lambda i, j, k: (k, j))
        out_spec = pl.BlockSpec((None, tm, tn), lambda i, j, k: (i // per, i % per, j))
        out_shape = [jax.ShapeDtypeStruct((N_CHIP, piece, n_total), BF16)]
    operands, in_specs, epilogue = (a, b), [a_spec, b_spec], _store(BF16)
    if pair is not None:
        operands, in_specs, epilogue = (a, b, pair), [a_spec, b_spec, out_spec], _add_pair_epilogue
    return _matmul(name, "tn", operands, in_specs, out_shape, [out_spec], grid, (tm, tn), epilogue, deps)[0]


def _rms(xv):
    return lax.rsqrt(jnp.mean(xv * xv, axis=-1, keepdims=True) + EPS)


def _colsum(v):
    return jnp.sum(v, axis=0, keepdims=True)


def _norm_mod(name, x, g, scale, shift, tr=256, deps=()):
    s, d = x.shape
    tr = _tile(tr, s)

    def body(x_ref, g_ref, sc_ref, sh_ref, h_ref):
        xv = x_ref[...]
        h = (xv * _rms(xv)) * g_ref[...]
        h_ref[...] = (h * (1.0 + sc_ref[...]) + sh_ref[...]).astype(h_ref.dtype)

    row = pl.BlockSpec((tr, d), lambda i: (i, 0))
    vec = pl.BlockSpec((1, d), lambda i: (0, 0))
    return _pallas(body, deps, name=name, grid=(s // tr,), in_specs=[row, vec, vec, vec], out_specs=row,
                   out_shape=jax.ShapeDtypeStruct((s, d), BF16), compiler_params=_params("parallel"))(x, g, scale, shift)


def _loss_head(x3, target, gf, gate2, mlp, tr=128):
    s, d = x3.shape
    tr = _tile(tr, s)

    def body(x_ref, t_ref, gf_ref, gate_ref, mlp_ref, dx_ref, dbr_ref, dgf_ref, dgate_ref, loss_ref):
        @pl.when(pl.program_id(0) == 0)
        def _():
            dgf_ref[...] = jnp.zeros_like(dgf_ref)
            dgate_ref[...] = jnp.zeros_like(dgate_ref)
            loss_ref[...] = jnp.zeros_like(loss_ref)

        xv = x_ref[...]
        r = _rms(xv)
        xn = xv * r
        gfv = gf_ref[...]
        err = xn * gfv - t_ref[...]
        loss_ref[...] += 0.5 * _colsum(jnp.mean(err * err, axis=-1, keepdims=True))
        dy = err * (1.0 / d)
        dgf_ref[...] += _colsum(dy * xn)
        dxn = dy * gfv
        dx = r * (dxn - xn * jnp.mean(dxn * xn, axis=-1, keepdims=True))
        dx_ref[...] = dx
        dbr_ref[...] = (dx * gate_ref[...]).astype(dbr_ref.dtype)
        dgate_ref[...] += _colsum(dx * mlp_ref[...].astype(F32))

    row = pl.BlockSpec((tr, d), lambda i: (i, 0))
    vec = pl.BlockSpec((1, d), lambda i: (0, 0))
    return _pallas(
        body, name="loss_head", grid=(s // tr,), in_specs=[row, row, vec, vec, row],
        out_specs=[row, row, vec, vec, pl.BlockSpec((1, 128), lambda i: (0, 0))],
        out_shape=[jax.ShapeDtypeStruct((s, d), F32), jax.ShapeDtypeStruct((s, d), BF16),
                   jax.ShapeDtypeStruct((1, d), F32), jax.ShapeDtypeStruct((1, d), F32),
                   jax.ShapeDtypeStruct((1, 128), F32)],
        compiler_params=_params("arbitrary"))(x3, target, gf, gate2, mlp)


def _norm_mod_bwd(name, dh, xin, g, scale, dx_up, branch=None, gate=None, tr=128, deps=()):
    s, d = xin.shape
    tr = _tile(tr, s)
    with_gate = branch is not None

    def body(*refs):
        dh_ref, x_ref, g_ref, sc_ref, up_ref = refs[:5]
        if with_gate:
            br_ref, gate_ref = refs[5:7]
            dx_ref, dsh_ref, dsc_ref, dg_ref, dbr_ref, dgate_ref = refs[7:]
            sums = (dsh_ref, dsc_ref, dg_ref, dgate_ref)
        else:
            dx_ref, dsh_ref, dsc_ref, dg_ref = refs[5:]
            sums = (dsh_ref, dsc_ref, dg_ref)

        @pl.when(pl.program_id(0) == 0)
        def _():
            for ref in sums:
                ref[...] = jnp.zeros_like(ref)

        xv, dhv, gv = x_ref[...], dh_ref[...].astype(F32), g_ref[...]
        r = _rms(xv)
        xn = xv * r
        one_sc = 1.0 + sc_ref[...]
        dsh_ref[...] += _colsum(dhv)
        dsc_ref[...] += _colsum(dhv * (xn * gv))
        dg_ref[...] += _colsum(dhv * one_sc * xn)
        dxn = dhv * one_sc * gv
        dx = up_ref[...] + r * (dxn - xn * jnp.mean(dxn * xn, axis=-1, keepdims=True))
        dx_ref[...] = dx
        if with_gate:
            dbr_ref[...] = (dx * gate_ref[...]).astype(dbr_ref.dtype)
            dgate_ref[...] += _colsum(dx * br_ref[...].astype(F32))

    row = pl.BlockSpec((tr, d), lambda i: (i, 0))
    vec = pl.BlockSpec((1, d), lambda i: (0, 0))
    vshape = jax.ShapeDtypeStruct((1, d), F32)
    operands = [dh, xin, g, scale, dx_up]
    in_specs = [row, row, vec, vec, row]
    out_shape = [jax.ShapeDtypeStruct((s, d), F32), vshape, vshape, vshape]
    out_specs = [row, vec, vec, vec]
    if with_gate:
        operands += [branch, gate]
        in_specs += [row, vec]
        out_shape += [jax.ShapeDtypeStruct((s, d), BF16), vshape]
        out_specs += [row, vec]
    return _pallas(body, deps, name=name, grid=(s // tr,), in_specs=in_specs, out_specs=out_specs, out_shape=out_shape,
                   compiler_params=_params("arbitrary"))(*operands)


def _window_count(c0, rows, half, s):
    t = c0 + lax.broadcasted_iota(jnp.int32, (rows, 1), 0)
    return (jnp.minimum(t + half, s) - jnp.maximum(t - half, 0)).astype(F32)


def _zero_pads(pad, s):
    zeros = jnp.zeros((PAD_ROWS, pad.shape[1]), pad.dtype)
    pad[0:PAD_ROWS, :] = zeros
    pad[PAD_ROWS + s:PAD_ROWS + s + PAD_ROWS, :] = zeros


def _pool_fwd(proj, s, gd, cb, ch, deps=()):
    nsub = gd // cb

    def body(v_ref, o_ref, pad):
        g = pl.program_id(0)
        _zero_pads(pad, s)
        pad[PAD_ROWS:PAD_ROWS + s, :] = v_ref[...].astype(F32)
        for gi, window in enumerate(POOL_WINDOWS):
            half = window // 2

            @pl.when(g == gi)
            def _(half=half):
                for c0 in range(0, s, ch):
                    base = PAD_ROWS + c0
                    acc = pad[base - half:base - half + ch, :]
                    for j in range(-half + 1, half):
                        acc = acc + pad[base + j:base + j + ch, :]
                    out = acc / _window_count(c0, ch, half, s) - pad[base:base + ch, :]
                    o_ref[c0:c0 + ch, :] = out.astype(o_ref.dtype)

    spec = pl.BlockSpec((s, cb), lambda g, j: (0, g * nsub + j))
    return _pallas(body, deps, name="pool_fwd", grid=(N_POOL_GROUPS, nsub), in_specs=[spec], out_specs=spec,
                   out_shape=jax.ShapeDtypeStruct((s, N_POOL_GROUPS * gd), BF16),
                   scratch_shapes=[pltpu.VMEM((s + 2 * PAD_ROWS, cb), F32)],
                   compiler_params=_params("parallel", "parallel"))(proj)


def _pool_bwd(dpooled, dproj, s, gd, cb, ch):
    nsub = gd // cb

    def body(dp_ref, dproj_in, o_ref, pad):
        del dproj_in
        g = pl.program_id(0)
        _zero_pads(pad, s)
        for gi, window in enumerate(POOL_WINDOWS):
            half = window // 2

            @pl.when(g == gi)
            def _(half=half):
                for c0 in range(0, s, ch):
                    pad[PAD_ROWS + c0:PAD_ROWS + c0 + ch, :] = dp_ref[c0:c0 + ch, :] / _window_count(c0, ch, half, s)
                for c0 in range(0, s, ch):
                    base = PAD_ROWS + c0
                    acc = pad[base - half + 1:base - half + 1 + ch, :]
                    for j in range(-half + 2, half + 1):
                        acc = acc + pad[base + j:base + j + ch, :]
                    o_ref[c0:c0 + ch, :] = (acc - dp_ref[c0:c0 + ch, :]).astype(o_ref.dtype)

    spec = pl.BlockSpec((s, cb), lambda g, j: (0, g * nsub + j))
    return _pallas(body, name="pool_bwd", grid=(N_POOL_GROUPS, nsub),
                   in_specs=[spec, pl.BlockSpec(memory_space=pl.ANY)], out_specs=spec,
                   out_shape=jax.ShapeDtypeStruct(dproj.shape, dproj.dtype), input_output_aliases={1: 0},
                   scratch_shapes=[pltpu.VMEM((s + 2 * PAD_ROWS, cb), F32)],
                   compiler_params=_params("parallel", "parallel"))(dpooled, dproj)


def _poolmix_fwd(pooled, wmix, pool_scale, gnorm_g, d_model, tm=512):
    s = pooled.shape[0]
    gd = wmix.shape[1]
    tm = _tile(tm, s)

    def body(p_ref, w_ref, ps_ref, g_ref, apre_ref, mixed_ref):
        a_pre = jnp.dot(p_ref[...], w_ref[...], preferred_element_type=F32)
        apre_ref[...] = a_pre
        a_out = a_pre * ps_ref[...]
        mixed_ref[...] = ((a_out * _rms(a_out)) * g_ref[...]).astype(mixed_ref.dtype)

    blk = pl.BlockSpec((tm, gd), lambda g, i: (i, g))
    vec = pl.BlockSpec((1, gd), lambda g, i: (0, g))
    return _pallas(body, name="poolmix_fwd", grid=(N_POOL_GROUPS, s // tm),
                   in_specs=[blk, pl.BlockSpec((None, gd, gd), lambda g, i: (g, 0, 0)), vec, vec],
                   out_specs=[blk, blk],
                   out_shape=[jax.ShapeDtypeStruct((s, N_POOL_GROUPS * gd), F32), jax.ShapeDtypeStruct((s, d_model), BF16)],
                   compiler_params=_params("parallel", "parallel"))(pooled, wmix, pool_scale, gnorm_g)


def _poolmix_bwd(dmixed, a_pre, wmix, pool_scale, gnorm_g, tm=512):
    s = a_pre.shape[0]
    gd = wmix.shape[1]
    tm = _tile(tm, s)

    def body(dm_ref, apre_ref, w_ref, ps_ref, g_ref, dapre_ref, dpooled_ref, dps_ref, dg_ref):
        @pl.when(pl.program_id(1) == 0)
        def _():
            dps_ref[...] = jnp.zeros_like(dps_ref)
            dg_ref[...] = jnp.zeros_like(dg_ref)

        a_pre, dm, ps = apre_ref[...], dm_ref[...].astype(F32), ps_ref[...]
        a_out = a_pre * ps
        r = _rms(a_out)
        n = a_out * r
        dg_ref[...] += _colsum(dm * n)
        dn = dm * g_ref[...]
        da_out = r * (dn - n * jnp.mean(dn * n, axis=-1, keepdims=True))
        dps_ref[...] += _colsum(da_out * a_pre)
        da_pre = (da_out * ps).astype(BF16)
        dapre_ref[...] = da_pre
        dpooled_ref[...] = lax.dot_general(da_pre, w_ref[...], _DOT_DIMS["nt"], preferred_element_type=F32)

    blk = pl.BlockSpec((tm, gd), lambda g, i: (i, g))
    vec = pl.BlockSpec((1, gd), lambda g, i: (0, g))
    width = N_POOL_GROUPS * gd
    return _pallas(body, name="poolmix_bwd", grid=(N_POOL_GROUPS, s // tm),
                   in_specs=[blk, blk, pl.BlockSpec((None, gd, gd), lambda g, i: (g, 0, 0)), vec, vec],
                   out_specs=[blk, blk, vec, vec],
                   out_shape=[jax.ShapeDtypeStruct((s, width), BF16), jax.ShapeDtypeStruct((s, width), F32),
                              jax.ShapeDtypeStruct((1, width), F32), jax.ShapeDtypeStruct((1, width), F32)],
                   compiler_params=_params("parallel", "arbitrary"))(dmixed, a_pre, wmix, pool_scale, gnorm_g)


def _poolmix_wgrad(pooled, da_pre, gd, tk=1024):
    s = pooled.shape[0]
    tk = _tile(tk, s)
    nk = s // tk

    def body(p_ref, d_ref, o_ref, acc):
        k = pl.program_id(1)
        part = lax.dot_general(p_ref[...], d_ref[...], _DOT_DIMS["tn"], preferred_element_type=F32)

        @pl.when(k == 0)
        def _():
            acc[...] = part

        @pl.when(k > 0)
        def _():
            acc[...] += part

        @pl.when(k == nk - 1)
        def _():
            o_ref[...] = acc[...].astype(o_ref.dtype)

    blk = pl.BlockSpec((tk, gd), lambda g, k: (k, g))
    return _pallas(body, name="poolmix_wgrad", grid=(N_POOL_GROUPS, nk), in_specs=[blk, blk],
                   out_specs=pl.BlockSpec((None, gd, gd), lambda g, k: (g, 0, 0)),
                   out_shape=jax.ShapeDtypeStruct((N_POOL_GROUPS, gd, gd), BF16),
                   scratch_shapes=[pltpu.VMEM((gd, gd), F32)],
                   compiler_params=_params("parallel", "arbitrary"))(pooled, da_pre)


def _head_mean(v):
    parts = []
    for q in range(v.shape[1] // CONV_HEAD_DIM):
        m = jnp.mean(v[:, q * CONV_HEAD_DIM:(q + 1) * CONV_HEAD_DIM], axis=-1, keepdims=True)
        parts.append(jnp.broadcast_to(m, (v.shape[0], CONV_HEAD_DIM)))
    return parts[0] if len(parts) == 1 else jnp.concatenate(parts, axis=1)


def _conv_fwd(proj, mixed, conv_w, conv_b, gnorm_g, s, width, cb, ch, deps=()):
    nblk = width // cb

    def body(b_ref, c_ref, u_ref, w_ref, cb_ref, g_ref, mixed_in, o_ref, pad):
        del mixed_in
        _zero_pads(pad, s)
        pad[PAD_ROWS:PAD_ROWS + s, :] = c_ref[...].astype(F32) * u_ref[...].astype(F32)
        w = w_ref[...]
        for c0 in range(0, s, ch):
            base = PAD_ROWS + c0
            conv = (w[0:1] * pad[base - 1:base - 1 + ch, :] + w[1:2] * pad[base:base + ch, :]
                    + w[2:3] * pad[base + 1:base + 1 + ch, :] + cb_ref[...])
            bo = b_ref[c0:c0 + ch, :].astype(F32) * conv
            n = bo * lax.rsqrt(_head_mean(bo * bo) + EPS)
            o_ref[c0:c0 + ch, :] = (n * g_ref[...]).astype(o_ref.dtype)

    def part(p):
        return pl.BlockSpec((s, cb), lambda j: (0, p * nblk + j))

    vec = pl.BlockSpec((1, cb), lambda j: (0, j))
    return _pallas(body, deps, name="conv_fwd", grid=(nblk,),
                   in_specs=[part(1), part(2), part(3), pl.BlockSpec((3, cb), lambda j: (0, j)), vec, vec,
                             pl.BlockSpec(memory_space=pl.ANY)],
                   out_specs=part(1), out_shape=jax.ShapeDtypeStruct(mixed.shape, mixed.dtype),
                   input_output_aliases={6: 0},
                   scratch_shapes=[pltpu.VMEM((s + 2 * PAD_ROWS, cb), F32)],
                   compiler_params=_params("parallel"))(proj, proj, proj, conv_w, conv_b, gnorm_g, mixed)


def _conv_bwd(dmixed, proj, conv_w, conv_b, gnorm_g, s, width, cb, ch, deps=()):
    nblk = width // cb

    def body(dm_ref, b_ref, c_ref, u_ref, w_ref, cb_ref, g_ref, dproj_ref, dw_ref, dcb_ref, dg_ref,
             pad_cu, pad_dconv, db_buf, dc_buf, du_buf, sems):
        j = pl.program_id(0)
        _zero_pads(pad_cu, s)
        _zero_pads(pad_dconv, s)
        pad_cu[PAD_ROWS:PAD_ROWS + s, :] = c_ref[...].astype(F32) * u_ref[...].astype(F32)
        w, gv = w_ref[...], g_ref[...]
        zero = jnp.zeros((1, cb), F32)
        dw0, dw1, dw2, dcb, dg = zero, zero, zero, zero, zero
        for c0 in range(0, s, ch):
            base = PAD_ROWS + c0
            cu_prev, cu_here, cu_next = (pad_cu[base - 1:base - 1 + ch, :], pad_cu[base:base + ch, :],
                                         pad_cu[base + 1:base + 1 + ch, :])
            conv = w[0:1] * cu_prev + w[1:2] * cu_here + w[2:3] * cu_next + cb_ref[...]
            bg = b_ref[c0:c0 + ch, :].astype(F32)
            bo = bg * conv
            r = lax.rsqrt(_head_mean(bo * bo) + EPS)
            n = bo * r
            dm = dm_ref[c0:c0 + ch, :].astype(F32)
            dg = dg + _colsum(dm * n)
            dn = dm * gv
            dbo = r * (dn - n * _head_mean(dn * n))
            db_buf[c0:c0 + ch, :] = (dbo * conv).astype(BF16)
            dconv = dbo * bg
            pad_dconv[base:base + ch, :] = dconv
            dcb = dcb + _colsum(dconv)
            dw0 = dw0 + _colsum(dconv * cu_prev)
            dw1 = dw1 + _colsum(dconv * cu_here)
            dw2 = dw2 + _colsum(dconv * cu_next)
        dw_ref[0:1, :] = dw0
        dw_ref[1:2, :] = dw1
        dw_ref[2:3, :] = dw2
        dcb_ref[...] = dcb
        dg_ref[...] = dg
        for c0 in range(0, s, ch):
            base = PAD_ROWS + c0
            dcu = (w[0:1] * pad_dconv[base + 1:base + 1 + ch, :] + w[1:2] * pad_dconv[base:base + ch, :]
                   + w[2:3] * pad_dconv[base - 1:base - 1 + ch, :])
            dc_buf[c0:c0 + ch, :] = (dcu * u_ref[c0:c0 + ch, :].astype(F32)).astype(BF16)
            du_buf[c0:c0 + ch, :] = (dcu * c_ref[c0:c0 + ch, :].astype(F32)).astype(BF16)
        copies = []
        for p, buf in enumerate((db_buf, dc_buf, du_buf)):
            col = pl.multiple_of((p + 1) * width + j * cb, CONV_HEAD_DIM)
            copies.append(pltpu.make_async_copy(buf, dproj_ref.at[:, pl.ds(col, cb)], sems.at[p]))
            copies[-1].start()
        for cp in copies:
            cp.wait()

    def part(p):
        return pl.BlockSpec((s, cb), lambda j: (0, p * nblk + j))

    vec = pl.BlockSpec((1, cb), lambda j: (0, j))
    w_spec = pl.BlockSpec((3, cb), lambda j: (0, j))
    return _pallas(body, deps, name="conv_bwd", grid=(nblk,),
                   in_specs=[part(1), part(1), part(2), part(3), w_spec, vec, vec],
                   out_specs=[pl.BlockSpec(memory_space=pl.ANY), w_spec, vec, vec],
                   out_shape=[jax.ShapeDtypeStruct((s, 4 * width), BF16), jax.ShapeDtypeStruct((3, width), F32),
                              jax.ShapeDtypeStruct((1, width), F32), jax.ShapeDtypeStruct((1, width), F32)],
                   scratch_shapes=[pltpu.VMEM((s + 2 * PAD_ROWS, cb), F32), pltpu.VMEM((s + 2 * PAD_ROWS, cb), F32),
                                   pltpu.VMEM((s, cb), BF16), pltpu.VMEM((s, cb), BF16), pltpu.VMEM((s, cb), BF16),
                                   pltpu.SemaphoreType.DMA((3,))],
                   compiler_params=_params("arbitrary"))(dmixed, proj, proj, proj, conv_w, conv_b, gnorm_g)


def _adamw(w, g, m, v):
    m = ADAM_B1 * m + (1.0 - ADAM_B1) * g
    v = ADAM_B2 * v + (1.0 - ADAM_B2) * (g * g)
    m_hat = m / (1.0 - ADAM_B1 ** ADAM_STEP)
    v_hat = v / (1.0 - ADAM_B2 ** ADAM_STEP)
    delta = -ADAM_LR * (m_hat / (jnp.sqrt(v_hat) + ADAM_EPS) + ADAM_WD * w)
    return delta, m, v


def _ada_fwd(c_rows, w, b, tn=512):
    rows, d = c_rows.shape
    n = w.shape[1]
    tn = _tile(tn, n)

    def body(c_ref, w_ref, b_ref, o_ref):
        cv = c_ref[...]
        act = (cv * jax.nn.sigmoid(cv)).astype(BF16)
        o_ref[...] = jnp.dot(act, w_ref[...].astype(BF16), preferred_element_type=F32) + b_ref[...]

    return _pallas(body, name="ada_fwd", grid=(n // tn,),
                   in_specs=[pl.BlockSpec((rows, d), lambda j: (0, 0)), pl.BlockSpec((d, tn), lambda j: (0, j)),
                             pl.BlockSpec((1, tn), lambda j: (0, j))],
                   out_specs=pl.BlockSpec((rows, tn), lambda j: (0, j)),
                   out_shape=jax.ShapeDtypeStruct((rows, n), F32), compiler_params=_params("parallel"))(c_rows, w, b)


def _ada_bwd_adam(c_cols, dmod, w, m, v, tr=512, tn=1024):
    d, rows = c_cols.shape
    n = w.shape[1]
    tr, tn = _tile(tr, d), _tile(tn, n)

    def body(c_ref, dm_ref, w_ref, m_ref, v_ref, g_ref, dl_ref, nm_ref, nv_ref):
        cv = c_ref[...]
        act = (cv * jax.nn.sigmoid(cv)).astype(BF16)
        g = jnp.dot(act, dm_ref[...].astype(BF16), preferred_element_type=F32)
        g_ref[...] = g
        dl_ref[...], nm_ref[...], nv_ref[...] = _adamw(w_ref[...], g, m_ref[...], v_ref[...])

    blk = pl.BlockSpec((tr, tn), lambda i, j: (i, j))
    shape = jax.ShapeDtypeStruct((d, n), F32)
    return _pallas(body, name="ada_bwd_adam", grid=(d // tr, n // tn),
                   in_specs=[pl.BlockSpec((tr, rows), lambda i, j: (i, 0)), pl.BlockSpec((rows, tn), lambda i, j: (0, j)),
                             blk, blk, blk],
                   out_specs=[blk] * 4, out_shape=[shape] * 4,
                   compiler_params=_params("parallel", "parallel"))(c_cols, dmod, w, m, v)


def _reduce_adam(name, pieces, w, m, v, tr=256, tc=1024):
    r, c = w.shape
    tr, tc = _tile(tr, r), _tile(tc, c)

    def body(p_ref, w_ref, m_ref, v_ref, g_ref, dl_ref, nm_ref, nv_ref):
        g = p_ref[0].astype(F32)
        for j in range(1, N_DEV):
            g = g + p_ref[j].astype(F32)
        g_ref[...] = g
        dl_ref[...], nm_ref[...], nv_ref[...] = _adamw(w_ref[...], g, m_ref[...], v_ref[...])

    blk = pl.BlockSpec((tr, tc), lambda i, j: (i, j))
    shape = jax.ShapeDtypeStruct((r, c), F32)
    return _pallas(body, name=name, grid=(r // tr, c // tc),
                   in_specs=[pl.BlockSpec((N_DEV, tr, tc), lambda i, j: (0, i, j)), blk, blk, blk],
                   out_specs=[blk] * 4, out_shape=[shape] * 4,
                   compiler_params=_params("parallel", "parallel"))(pieces, w, m, v)


def _reduce_adam_chips(name, sums, land, w, m, v, tr=256, tc=1024):
    r, c = w.shape
    tr, tc = _tile(tr, r), _tile(tc, c)

    def body(s_ref, l_ref, w_ref, m_ref, v_ref, g_ref, dl_ref, nm_ref, nv_ref):
        g = s_ref[...].astype(F32)
        for k in range(3):
            g = g + l_ref[k].astype(F32)
        g_ref[...] = g
        dl_ref[...], nm_ref[...], nv_ref[...] = _adamw(w_ref[...], g, m_ref[...], v_ref[...])

    blk = pl.BlockSpec((tr, tc), lambda i, j: (i, j))
    shape = jax.ShapeDtypeStruct((r, c), F32)
    mine = pl.BlockSpec((None, tr, tc), lambda i, j: (2 * lax.axis_index("x") + lax.axis_index("y"), i, j))
    return _pallas(body, name=name, grid=(r // tr, c // tc),
                   in_specs=[mine, pl.BlockSpec((3, tr, tc), lambda i, j: (0, i, j)), blk, blk, blk],
                   out_specs=[blk] * 4, out_shape=[shape] * 4,
                   compiler_params=_params("parallel", "parallel"))(sums, land, w, m, v)


def _sum_devices(parts):
    n = parts.shape[1]

    def body(p_ref, o_ref):
        acc = p_ref[0:1, :]
        for j in range(1, N_DEV):
            acc = acc + p_ref[j:j + 1, :]
        o_ref[...] = acc

    return _pallas(body, name="sum_devices", out_shape=jax.ShapeDtypeStruct((1, n), F32),
                   compiler_params=pltpu.CompilerParams(vmem_limit_bytes=VMEM_LIMIT_BYTES))(parts)


def _adam_small(name, g, w, m, v):
    def body(g_ref, w_ref, m_ref, v_ref, dl_ref, nm_ref, nv_ref):
        dl_ref[...], nm_ref[...], nv_ref[...] = _adamw(w_ref[...], g_ref[...], m_ref[...], v_ref[...])

    shape = jax.ShapeDtypeStruct(w.shape, F32)
    return _pallas(body, name=name, out_shape=[shape] * 3,
                   compiler_params=pltpu.CompilerParams(vmem_limit_bytes=VMEM_LIMIT_BYTES))(g, w, m, v)


def kernel(x, c, w_ada, b_ada, norm1_g, w_in, pool_mix_w, pool_scale, conv_w, conv_b, gnorm_pool_g, gnorm_conv_g, w_out, norm2_g, w_mlp_in, w_mlp_out, final_g, loss_target, m_w_ada, m_b_ada, m_norm1_g, m_w_in, m_pool_mix_w, m_pool_scale, m_conv_w, m_conv_b, m_gnorm_pool_g, m_gnorm_conv_g, m_w_out, m_norm2_g, m_w_mlp_in, m_w_mlp_out, m_final_g, v_w_ada, v_b_ada, v_norm1_g, v_w_in, v_pool_mix_w, v_pool_scale, v_conv_w, v_conv_b, v_gnorm_pool_g, v_gnorm_conv_g, v_w_out, v_norm2_g, v_w_mlp_in, v_w_mlp_out, v_final_g):
    s, d = x.shape[1], x.shape[2]
    width = d // 2
    gd = width // N_POOL_GROUPS
    d_ff = w_mlp_in.shape[2] * N_DEV
    n_proj = w_in.shape[2] * N_DEV
    ada_cols = w_ada.shape[2]
    conv_cols = conv_w.shape[2]
    assert n_proj == 4 * width and ada_cols * N_DEV == N_MOD * d and d_ff % N_DEV == 0
    assert width % CONV_HEAD_DIM == 0 and s % 8 == 0
    seq_chunk = _tile(512, s)
    pool_cb = _tile(256, gd)
    conv_cb = CONV_HEAD_DIM

    me = 4 * lax.axis_index("x") + 2 * lax.axis_index("y") + lax.axis_index("c")
    x2d, target = x[0], loss_target[0]

    wmix_all, conv_w_all, c_all = _exchange(
        "gather_small_weights", [pool_mix_w[0].astype(BF16), conv_w[0], c], ["gather"] * 3)
    wmix_full = jnp.transpose(wmix_all, (1, 0, 2, 3)).reshape(N_POOL_GROUPS, gd, gd)
    conv_w_full = jnp.transpose(conv_w_all, (1, 0, 2)).reshape(3, width)
    c_rows = jnp.concatenate([c_all.reshape(N_DEV, d), jnp.zeros((N_DEV, d), F32)], axis=0)

    b_mine = lax.dynamic_slice(b_ada, (0, me * ada_cols), (1, ada_cols))
    mod_part = _ada_fwd(c_rows, w_ada[0], b_mine)
    (mod_all,) = _exchange("scatter_mod", [mod_part[:N_DEV].reshape(N_DEV, 1, ada_cols)], ["a2a"])
    mod = mod_all.reshape(1, N_MOD * d)

    started, hopped, relayed = {}, {}, {}

    def gather_start(wname, wgt, deps):
        land = _landing(wgt[0].astype(BF16), me)
        started[wname] = _gather_start("gather_" + wname + "_start", land, deps)
        return started[wname][5]

    def gather_hop(wname, land, after):
        hopped[wname] = _gather_hop("gather_" + wname + "_hop", started[wname], land, after)
        return hopped[wname][3]

    def gather_relay(wname, after):
        relayed[wname] = _gather_relay("gather_" + wname + "_relay", started[wname], hopped[wname], after)
        return relayed[wname][3]

    def gather_wait(wname, land, after, local_waited=False):
        return _gather_wait("gather_" + wname + "_wait", started[wname], hopped[wname], relayed[wname], land, after,
                            local_waited)

    def chip():
        return 2 * lax.axis_index("x") + lax.axis_index("y")

    def local_piece(t):
        return 2 * chip() + t

    def remote_piece(t):
        return (2 * chip() + 2 + t) % N_DEV

    def same_core_piece(t):
        return 2 * ((chip() + 1 + t) % N_CHIP) + lax.axis_index("c")

    def other_core_piece(t):
        return 2 * ((chip() + 1 + t) % N_CHIP) + 1 - lax.axis_index("c")

    tok_w_in = gather_start("w_in", w_in, (mod,))
    shift1, scale1, gate1, shift2, scale2, gate2 = [mod[:, i * d:(i + 1) * d] for i in range(N_MOD)]

    h1 = _norm_mod("norm1_fwd", x2d, norm1_g, scale1, shift1, deps=(tok_w_in,))
    proj_shape = [jax.ShapeDtypeStruct((s, n_proj), BF16)]
    w_in_local = _gather_wait_local("gather_w_in_local", started["w_in"], started["w_in"][4], h1)
    (proj,) = _mm_nn("in_proj_local", h1, w_in_local, n_proj, True, proj_shape, _store(BF16), pieces=(local_piece, 2))
    tok = gather_hop("w_in", w_in_local, proj)
    tok = gather_start("w_out", w_out, (tok,))
    tok = gather_relay("w_in", tok)
    tok = gather_hop("w_out", started["w_out"][4], tok)
    tok = gather_start("w_mlp_in", w_mlp_in, (tok,))
    w_in_all = gather_wait("w_in", relayed["w_in"][2], tok, True)
    (proj,) = _mm_nn("in_proj", h1, w_in_all, n_proj, True, proj_shape, _store(BF16), pieces=(remote_piece, 6),
                     carry=(proj,))
    pooled = _pool_fwd(proj, s, gd, pool_cb, seq_chunk)
    a_pre, mixed = _poolmix_fwd(pooled, wmix_full, pool_scale, gnorm_pool_g, d)
    tok = gather_hop("w_mlp_in", started["w_mlp_in"][4], a_pre)
    tok = gather_start("w_mlp_out", w_mlp_out, (tok,))
    tok = gather_relay("w_out", tok)
    mixed = _conv_fwd(proj, mixed, conv_w_full, conv_b, gnorm_conv_g, s, width, conv_cb, seq_chunk, deps=(tok,))

    def residual_specs(tm, tn):
        return [pl.BlockSpec((tm, tn), lambda i, j, k: (i, j)), pl.BlockSpec((1, tn), lambda i, j, k: (0, j))]

    sd_f32 = jax.ShapeDtypeStruct((s, d), F32)
    w_out_full = gather_wait("w_out", relayed["w_out"][2], mixed).reshape(d, d)
    sd_bf16 = jax.ShapeDtypeStruct((s, d), BF16)
    attn, x_mid = _mm_nn("out_proj", mixed, w_out_full, d, False, [sd_bf16, sd_f32], _residual_epilogue,
                         extras=(x2d, gate1), extra_specs=residual_specs)
    h2 = _norm_mod("norm2_fwd", x_mid, norm2_g, scale2, shift2)
    sf_bf16 = [jax.ShapeDtypeStruct((s, d_ff), BF16)] * 2
    w1_local = _gather_wait_local("gather_w_mlp_in_local", started["w_mlp_in"], hopped["w_mlp_in"][2], h2)
    relu, hid = _mm_nn("mlp_in_local", h2, w1_local, d_ff, True, sf_bf16, _relu2_epilogue, pieces=(local_piece, 2))
    hopped["w_mlp_in"] = hopped["w_mlp_in"][:2] + (w1_local,) + hopped["w_mlp_in"][3:]
    tok = gather_relay("w_mlp_in", hid)
    relu, hid = _mm_nn("mlp_in_same_core", h2, relayed["w_mlp_in"][2], d_ff, True, sf_bf16, _relu2_epilogue,
                       pieces=(same_core_piece, 3), carry=(relu, hid), deps=(tok,))
    w1_all = gather_wait("w_mlp_in", relayed["w_mlp_in"][2], hid, True)
    tok = gather_hop("w_mlp_out", started["w_mlp_out"][4], w1_all)
    relu, hid = _mm_nn("mlp_in_other_core", h2, w1_all, d_ff, True, sf_bf16, _relu2_epilogue,
                       pieces=(other_core_piece, 2), carry=(relu, hid), deps=(tok,))
    tok = gather_relay("w_mlp_out", hid)
    relu, hid = _mm_nn("mlp_in", h2, w1_all, d_ff, True, sf_bf16, _relu2_epilogue,
                       pieces=(lambda t: other_core_piece(t + 2), 1), carry=(relu, hid), deps=(tok,))
    w2_full = gather_wait("w_mlp_out", relayed["w_mlp_out"][2], hid).reshape(d_ff, d)
    mlp, x_last = _mm_nn("mlp_out", hid, w2_full, d, False, [sd_bf16, sd_f32], _residual_epilogue,
                         extras=(x_mid, gate2), extra_specs=residual_specs)

    dx_last, dmlp, d_final_g, dgate2, loss_row = _loss_head(x_last, target, final_g.reshape(1, d), gate2, mlp)

    def relu_specs(tm, tn):
        return [pl.BlockSpec((tm, tn), lambda i, j, k: (i, j))]

    def reduce_start(wname, a, b, col_pieces, deps=()):
        far = _mm_tn_half(wname + "_dw_far", a, b, col_pieces, near=False, deps=deps)
        return _pair_start("scatter_" + wname + "_pair_start", far)

    def reduce_chips(wname, a, b, col_pieces, pairs, after):
        pair = _pair_wait("scatter_" + wname + "_pair_wait", pairs, after)
        sums = _mm_tn_half(wname + "_dw_near", a, b, col_pieces, near=True, pair=pair)
        return _chip_start("scatter_" + wname + "_chip_start", sums)

    pairs_w2 = reduce_start("mlp_out", hid, dmlp, False)
    (dhpre,) = _mm_nt("mlp_out_dx", dmlp, w2_full, d_ff, False, sf_bf16[:1], _relu2_bwd_epilogue,
                      extras=(relu,), extra_specs=relu_specs, deps=(pairs_w2[4],))
    chips_w2 = reduce_chips("mlp_out", hid, dmlp, False, pairs_w2, dhpre)
    pairs_w1 = reduce_start("mlp_in", h2, dhpre, True, deps=(chips_w2[4],))
    (dh2,) = _mm_nt("mlp_in_dx", dhpre, w1_all, d, True, [sd_bf16], _store(BF16), tn=1024, tk=2048,
                    deps=(pairs_w1[4],))
    chips_w1 = reduce_chips("mlp_in", h2, dhpre, True, pairs_w1, dh2)
    dx_mid, dshift2, dscale2, d_norm2_g, dattn, dgate1 = _norm_mod_bwd(
        "norm2_bwd", dh2, x_mid, norm2_g, scale2, dx_last, branch=attn, gate=gate1, deps=(chips_w1[4],))

    pairs_w_out = reduce_start("out_proj", mixed, dattn, False)
    (dmixed,) = _mm_nt("out_proj_dx", dattn, w_out_full, d, False, [sd_bf16], _store(BF16), deps=(pairs_w_out[4],))
    chips_w_out = reduce_chips("out_proj", mixed, dattn, False, pairs_w_out, dmixed)
    dproj, d_conv_w, d_conv_b, d_gnorm_conv = _conv_bwd(dmixed, proj, conv_w_full, conv_b, gnorm_conv_g,
                                                        s, width, conv_cb, seq_chunk, deps=(chips_w_out[4],))
    da_pre, dpooled, d_pool_scale, d_gnorm_pool = _poolmix_bwd(dmixed, a_pre, wmix_full, pool_scale, gnorm_pool_g)
    g_wmix = _poolmix_wgrad(pooled, da_pre, gd)
    dproj = _pool_bwd(dpooled, dproj, s, gd, pool_cb, seq_chunk)

    pairs_w_in = reduce_start("in_proj", h1, dproj, True)
    (dh1,) = _mm_nt("in_proj_dx", dproj, w_in_all, d, True, [sd_bf16], _store(BF16), deps=(pairs_w_in[4],))
    chips_w_in = reduce_chips("in_proj", h1, dproj, True, pairs_w_in, dh1)
    grad_x, dshift1, dscale1, d_norm1_g = _norm_mod_bwd("norm1_bwd", dh1, x2d, norm1_g, scale1, dx_mid,
                                                        deps=(chips_w_in[4],))

    rows_mix = gd // N_DEV
    g_wmix_split = jnp.transpose(g_wmix.reshape(N_POOL_GROUPS, N_DEV, rows_mix, gd), (1, 0, 2, 3))
    g_wmix_split = g_wmix_split.reshape(N_DEV, N_POOL_GROUPS * rows_mix, gd)
    loss_pad = jnp.concatenate([loss_row[:, :1], jnp.zeros((1, 127), F32)], axis=1)
    dmod = jnp.concatenate([dshift1, dscale1, dgate1, dshift2, dscale2, dgate2], axis=1)
    small = jnp.concatenate([dmod, d_norm1_g, d_pool_scale, d_conv_b, d_gnorm_pool, d_gnorm_conv, d_norm2_g,
                             d_final_g, d_conv_w.reshape(1, 3 * width), loss_pad], axis=1)
    small_started = _push_start("exchange_small_grads_start", [g_wmix_split, small], ("a2a", "gather"), me)

    sums, landed = _chip_wait("scatter_w_mlp_out_chip_wait", chips_w2, small_started[-1])
    out_w2 = _reduce_adam_chips("adam_w_mlp_out", sums, landed, w_mlp_out[0], m_w_mlp_out[0], v_w_mlp_out[0])
    sums, landed = _chip_wait("scatter_w_mlp_in_chip_wait", chips_w1, out_w2[0])
    out_w1 = _reduce_adam_chips("adam_w_mlp_in", sums, landed, w_mlp_in[0], m_w_mlp_in[0], v_w_mlp_in[0])
    sums, landed = _chip_wait("scatter_w_out_chip_wait", chips_w_out, out_w1[0])
    out_w_out = _reduce_adam_chips("adam_w_out", sums, landed, w_out[0], m_w_out[0], v_w_out[0])
    sums, landed = _chip_wait("scatter_w_in_chip_wait", chips_w_in, out_w_out[0])
    out_w_in = _reduce_adam_chips("adam_w_in", sums, landed, w_in[0], m_w_in[0], v_w_in[0])

    p_wmix, small_all = _push_wait("exchange_small_grads_wait", small_started, ("a2a", "gather"), out_w_in[0])
    mix_shape = (N_POOL_GROUPS * rows_mix, gd)
    out_wmix = _reduce_adam("adam_pool_mix", p_wmix, pool_mix_w.reshape(mix_shape), m_pool_mix_w.reshape(mix_shape),
                            v_pool_mix_w.reshape(mix_shape))
    out_wmix = [a.reshape(pool_mix_w.shape) for a in out_wmix]
    small_all = small_all.reshape(N_DEV, small.shape[1])
    small_sum = _sum_devices(small_all)

    n_rep = (N_MOD + 1) * d + 4 * width + 2 * d
    loss = small_sum[0, n_rep + 3 * width]
    rep_names_w = [b_ada, norm1_g, pool_scale, conv_b, gnorm_pool_g, gnorm_conv_g, norm2_g, final_g.reshape(1, d)]
    rep_names_m = [m_b_ada, m_norm1_g, m_pool_scale, m_conv_b, m_gnorm_pool_g, m_gnorm_conv_g, m_norm2_g,
                   m_final_g.reshape(1, d)]
    rep_names_v = [v_b_ada, v_norm1_g, v_pool_scale, v_conv_b, v_gnorm_pool_g, v_gnorm_conv_g, v_norm2_g,
                   v_final_g.reshape(1, d)]
    rep_grad = small_sum[:, :n_rep]
    rep_delta, rep_m, rep_v = _adam_small("adam_replicated", rep_grad, jnp.concatenate(rep_names_w, axis=1),
                                          jnp.concatenate(rep_names_m, axis=1), jnp.concatenate(rep_names_v, axis=1))

    def split_rep(vec):
        out, off = [], 0
        for wgt in rep_names_w:
            n = wgt.shape[1]
            out.append(vec[:, off:off + n])
            off += n
        out[-1] = out[-1].reshape(d)
        return out

    conv_grad_full = small_sum[:, n_rep:n_rep + 3 * width].reshape(3, width)
    g_conv_w = lax.dynamic_slice(conv_grad_full, (0, me * conv_cols), (3, conv_cols))
    g_conv_w8 = jnp.concatenate([g_conv_w, jnp.zeros((5, conv_cols), F32)], axis=0)

    def pad8(a):
        return jnp.concatenate([a[0], jnp.zeros((5, conv_cols), F32)], axis=0)

    conv_delta, conv_m, conv_v = _adam_small("adam_conv_w", g_conv_w8, pad8(conv_w), pad8(m_conv_w), pad8(v_conv_w))

    dmod_all = small_all[:, :N_MOD * d]
    dmod_mine = lax.dynamic_slice(dmod_all, (0, me * ada_cols), (N_DEV, ada_cols))
    dmod_rows = jnp.concatenate([dmod_mine, jnp.zeros((N_DEV, ada_cols), F32)], axis=0)
    out_ada = _ada_bwd_adam(jnp.transpose(c_rows), dmod_rows, w_ada[0], m_w_ada[0], v_w_ada[0])

    rep_all = [split_rep(rep_grad), split_rep(rep_delta), split_rep(rep_m), split_rep(rep_v)]
    conv_all = [g_conv_w[None], conv_delta[None, :3], conv_m[None, :3], conv_v[None, :3]]
    outs = [loss, grad_x[None]]
    for kind in range(4):
        b_ada_o, norm1_o, pool_scale_o, conv_b_o, gpool_o, gconv_o, norm2_o, final_o = rep_all[kind]
        outs += [out_ada[kind][None], b_ada_o, norm1_o, out_w_in[kind][None], out_wmix[kind], pool_scale_o,
                 conv_all[kind], conv_b_o, gpool_o, gconv_o, out_w_out[kind][None], norm2_o, out_w1[kind][None],
                 out_w2[kind][None], final_o]
    return tuple(outs)
```

```python
import jax
import jax.numpy as jnp
from jax import lax
from jax.experimental import pallas as pl
from jax.experimental.pallas import tpu as pltpu

F32 = jnp.float32
BF16 = jnp.bfloat16
MESH = pl.DeviceIdType.MESH

N_DEV = 8
N_MOD = 6
EPS = 1e-6
POOL_WINDOWS = (2, 4, 8, 16)
N_POOL_GROUPS = len(POOL_WINDOWS)
CONV_HEAD_DIM = 128
PAD_ROWS = 16

ADAM_LR = 0.001
ADAM_B1 = 0.9
ADAM_B2 = 0.999
ADAM_EPS = 1e-08
ADAM_WD = 0.01
ADAM_STEP = 10

VMEM_LIMIT_BYTES = 56 * 1024 * 1024
MM_TM, MM_TN, MM_TK = 1024, 512, 4096


def _pallas(body, deps=(), **kw):
    if not deps:
        return pl.pallas_call(body, **kw)
    n_in = len(kw["in_specs"])

    def with_deps(*refs):
        body(*refs[:n_in], *refs[n_in + len(deps):])

    kw["in_specs"] = list(kw["in_specs"]) + [pl.BlockSpec(memory_space=pl.ANY)] * len(deps)
    call = pl.pallas_call(with_deps, **kw)
    return lambda *operands: call(*operands, *deps)


def _params(*sem):
    return pltpu.CompilerParams(dimension_semantics=sem, vmem_limit_bytes=VMEM_LIMIT_BYTES)


def _tile(pref, dim):
    if dim <= pref:
        return dim
    for t in range(pref - pref % 128, 0, -128):
        if dim % t == 0:
            return t
    return dim


def _exchange(name, arrays, modes, deps=()):
    n = len(arrays)
    out_shape = []
    for a, mode in zip(arrays, modes):
        piece = a.shape if mode == "gather" else a.shape[1:]
        out_shape.append(jax.ShapeDtypeStruct((N_DEV,) + tuple(piece), a.dtype))

    def body(*refs):
        srcs, dsts = refs[:n], refs[n:2 * n]
        send_sems, recv_sems, local_sems = refs[2 * n:]
        x, y, c = lax.axis_index("x"), lax.axis_index("y"), lax.axis_index("c")
        me = 4 * x + 2 * y + c
        copies = []
        for i in range(n):
            gather = modes[i] == "gather"
            local = pltpu.make_async_copy(srcs[i] if gather else srcs[i].at[me], dsts[i].at[me], local_sems.at[i])
            local.start()
            copies.append(local)
            for k in range(1, N_DEV):
                kx, ky, kc = (k >> 2) & 1, (k >> 1) & 1, k & 1
                peer = (1 - x if kx else x, 1 - y if ky else y, 1 - c if kc else c)
                peer_idx = 4 * peer[0] + 2 * peer[1] + peer[2]
                remote = pltpu.make_async_remote_copy(
                    src_ref=srcs[i] if gather else srcs[i].at[peer_idx],
                    dst_ref=dsts[i].at[me],
                    send_sem=send_sems.at[i * (N_DEV - 1) + k - 1],
                    recv_sem=recv_sems.at[i * (N_DEV - 1) + k - 1],
                    device_id=peer, device_id_type=MESH)
                remote.start()
                copies.append(remote)
        for cp in copies:
            cp.wait()

    any_spec = pl.BlockSpec(memory_space=pl.ANY)
    return _pallas(
        body, deps, name=name, out_shape=out_shape,
        in_specs=[any_spec] * n, out_specs=[any_spec] * n,
        scratch_shapes=[pltpu.SemaphoreType.DMA((n * (N_DEV - 1),)),
                        pltpu.SemaphoreType.DMA((n * (N_DEV - 1),)),
                        pltpu.SemaphoreType.DMA((n,))],
    )(*arrays)


_HBM = pl.BlockSpec(memory_space=pltpu.HBM)
_SEM = pl.BlockSpec(memory_space=pltpu.SEMAPHORE)
_TOKEN = pl.BlockSpec(memory_space=pltpu.VMEM)
_EFFECT = pltpu.SideEffectType.DATAFLOW_SIDE_EFFECTING
N_CHIP = N_DEV // 2
_OTHER_CHIPS = (1, 2, 3)


def _place():
    x, y, c = lax.axis_index("x"), lax.axis_index("y"), lax.axis_index("c")
    return x, y, c, (x, y, 1 - c)


def _same_core_of(x, y, c, k):
    px = 1 - x if k & 2 else x
    py = 1 - y if k & 1 else y
    return (px, py, c), 2 * px + py


def _remote(src, dst, send_sem, recv_sem, device):
    return pltpu.make_async_remote_copy(src_ref=src, dst_ref=dst, send_sem=send_sem, recv_sem=recv_sem,
                                        device_id=device, device_id_type=MESH)


def _token_shape():
    return jax.ShapeDtypeStruct((8, 128), F32)


def _split_call(body, deps, name, operands, in_specs, out_shape, out_specs, aliases):
    return _pallas(body, deps, name=name, out_shape=out_shape, in_specs=in_specs, out_specs=out_specs,
                   input_output_aliases=aliases,
                   compiler_params=pltpu.CompilerParams(has_side_effects=_EFFECT))(*operands)


def _routes(x, y, c):
    first = (x + c - 2 * x * c, y + (1 - c) - 2 * y * (1 - c), c)
    second = (x + (1 - c) - 2 * x * (1 - c), y + c - 2 * y * c, c)
    return first, second, (1 - x, 1 - y, c)


def _index_of(device):
    return 4 * device[0] + 2 * device[1] + device[2]


def _gather_start(name, land, deps):
    def body(land_ref, send_sems, recv_first, recv_second, recv_d2d, land_thru, token):
        del land_thru
        x, y, c, sibling = _place()
        first, second, _ = _routes(x, y, c)
        mine = land_ref.at[4 * x + 2 * y + c]
        _remote(mine, mine, send_sems.at[0], recv_first.at[0], first).start()
        _remote(mine, mine, send_sems.at[1], recv_second.at[0], second).start()
        _remote(mine, mine, send_sems.at[2], recv_d2d.at[0], sibling).start()
        token[...] = jnp.zeros_like(token)

    one = pltpu.SemaphoreType.DMA((1,))
    return _split_call(
        body, deps, name, (pltpu.with_memory_space_constraint(land, pltpu.HBM),), (_HBM,),
        (pltpu.SemaphoreType.DMA((3,)), one, one, one, pltpu.HBM(land.shape, land.dtype), _token_shape()),
        (_SEM, _SEM, _SEM, _SEM, _HBM, _TOKEN), {0: 4})


def _gather_wait_local(name, started, land, after):
    recv_d2d = started[3]

    def body(land_ref, recv_d2d, after_ref, land_out):
        del after_ref, land_out
        x, y, c, sibling = _place()
        mine = land_ref.at[4 * x + 2 * y + c]
        _remote(mine, mine, recv_d2d.at[0], recv_d2d.at[0], sibling).wait_recv()

    return _split_call(
        body, (), name, (land, recv_d2d, after), (_HBM, _SEM, pl.BlockSpec(memory_space=pl.ANY)),
        (pltpu.HBM(land.shape, land.dtype),), (_HBM,), {0: 0})[0]


def _gather_hop(name, started, land, after):
    recv_first = started[1]

    def body(land_ref, recv_first, after_ref, send_hop, recv_hop, land_thru, token):
        del after_ref, land_thru
        x, y, c, _ = _place()
        first, second, _ = _routes(x, y, c)
        piece = land_ref.at[_index_of(first)]
        _remote(piece, piece, send_hop.at[0], recv_first.at[0], first).wait_recv()
        _remote(piece, piece, send_hop.at[0], recv_hop.at[0], second).start()
        token[...] = jnp.zeros_like(token)

    one = pltpu.SemaphoreType.DMA((1,))
    return _split_call(
        body, (), name, (land, recv_first, after), (_HBM, _SEM, pl.BlockSpec(memory_space=pl.ANY)),
        (one, one, pltpu.HBM(land.shape, land.dtype), _token_shape()), (_SEM, _SEM, _HBM, _TOKEN), {0: 2})


def _gather_relay(name, started, hopped, after):
    recv_second = started[2]
    _, recv_hop, land, _ = hopped

    def body(land_ref, recv_second, recv_hop, after_ref, send_fwd, recv_fwd, land_thru, token):
        del after_ref, land_thru
        token[...] = jnp.zeros_like(token)
        x, y, c, sibling = _place()
        mine = land_ref.at[4 * x + 2 * y + c]
        _remote(mine, mine, send_fwd.at[0], recv_second.at[0], sibling).wait_recv()
        _remote(mine, mine, send_fwd.at[0], recv_hop.at[0], sibling).wait_recv()
        for i, device in enumerate(_routes(x, y, c)):
            piece = land_ref.at[_index_of(device)]
            _remote(piece, piece, send_fwd.at[i], recv_fwd.at[i], sibling).start()

    return _split_call(
        body, (), name, (land, recv_second, recv_hop, after), (_HBM, _SEM, _SEM, pl.BlockSpec(memory_space=pl.ANY)),
        (pltpu.SemaphoreType.DMA((3,)), pltpu.SemaphoreType.DMA((3,)), pltpu.HBM(land.shape, land.dtype),
         _token_shape()),
        (_SEM, _SEM, _HBM, _TOKEN), {0: 2})


def _gather_wait(name, started, hopped, relayed, land, after, local_waited):
    send_sems, recv_d2d = started[0], started[3]
    send_hop = hopped[0]
    send_fwd, recv_fwd = relayed[0], relayed[1]

    def body(land_ref, send_sems, recv_d2d, send_hop, send_fwd, recv_fwd, after_ref, land_out):
        del after_ref, land_out
        x, y, c, sibling = _place()
        mine = land_ref.at[4 * x + 2 * y + c]
        for i in range(3):
            _remote(mine, mine, send_sems.at[i], recv_d2d.at[0], sibling).wait_send()
        _remote(mine, mine, send_hop.at[0], recv_d2d.at[0], sibling).wait_send()
        if not local_waited:
            _remote(mine, mine, send_sems.at[2], recv_d2d.at[0], sibling).wait_recv()
        for i in range(3):
            relay = _remote(mine, mine, send_fwd.at[i], recv_fwd.at[i], sibling)
            relay.wait_send()
            relay.wait_recv()

    return _split_call(
        body, (), name, (land, send_sems, recv_d2d, send_hop, send_fwd, recv_fwd, after),
        (_HBM, _SEM, _SEM, _SEM, _SEM, _SEM, pl.BlockSpec(memory_space=pl.ANY)),
        (pltpu.HBM(land.shape, land.dtype),), (_HBM,), {0: 0})[0]


def _push_start(name, srcs, modes, me, deps=()):
    n = len(srcs)
    lands = [_landing(src if mode == "gather" else lax.dynamic_index_in_dim(src, me, 0, keepdims=False), me)
             for src, mode in zip(srcs, modes)]

    def body(*refs):
        src_refs, land_refs, send_sems, recv_sems, token = refs[:n], refs[n:2 * n], refs[2 * n], refs[2 * n + 1], refs[-1]
        for cp in _push_copies(src_refs, land_refs, send_sems, recv_sems, modes):
            cp.start()
        token[...] = jnp.zeros_like(token)

    count = pltpu.SemaphoreType.DMA((n * (N_DEV - 1),))
    operands = [pltpu.with_memory_space_constraint(a, pltpu.HBM) for a in list(srcs) + lands]
    return _split_call(
        body, deps, name, operands, (_HBM,) * (2 * n),
        (count, count) + tuple(pltpu.HBM(a.shape, a.dtype) for a in operands) + (_token_shape(),),
        (_SEM, _SEM) + (_HBM,) * (2 * n) + (_TOKEN,), {i: 2 + i for i in range(2 * n)})


def _push_wait(name, started, modes, after):
    n = len(modes)
    send_sems, recv_sems = started[0], started[1]
    arrays = started[2:2 + 2 * n]

    def body(*refs):
        src_refs, land_refs, send_sems, recv_sems = refs[:n], refs[n:2 * n], refs[2 * n], refs[2 * n + 1]
        for cp in _push_copies(src_refs, land_refs, send_sems, recv_sems, modes):
            cp.wait_send()
            cp.wait_recv()

    return _split_call(
        body, (), name, tuple(arrays) + (send_sems, recv_sems, after),
        (_HBM,) * (2 * n) + (_SEM, _SEM, pl.BlockSpec(memory_space=pl.ANY)),
        tuple(pltpu.HBM(a.shape, a.dtype) for a in arrays), (_HBM,) * (2 * n), {i: i for i in range(2 * n)})[n:]


def _push_copies(src_refs, land_refs, send_sems, recv_sems, modes):
    x, y, c, _ = _place()
    me = 4 * x + 2 * y + c
    copies = []
    for i, mode in enumerate(modes):
        for k in range(1, N_DEV):
            peer = (1 - x if k & 4 else x, 1 - y if k & 2 else y, 1 - c if k & 1 else c)
            src = src_refs[i] if mode == "gather" else src_refs[i].at[_index_of(peer)]
            sem = i * (N_DEV - 1) + k - 1
            copies.append(_remote(src, land_refs[i].at[me], send_sems.at[sem], recv_sems.at[sem], peer))
    return copies


def _landing(own, me):
    land = lax.empty((N_DEV,) + own.shape, own.dtype)
    return lax.dynamic_update_slice(land, own[None], (me,) + (0,) * own.ndim)


def _pair_start(name, far, deps=()):
    pair = lax.empty(far.shape, far.dtype)

    def body(far_ref, pair_ref, send_sems, recv_sems, far_thru, pair_thru, token):
        del far_thru, pair_thru
        _remote(far_ref, pair_ref, send_sems.at[0], recv_sems.at[0], _place()[3]).start()
        token[...] = jnp.zeros_like(token)

    return _split_call(
        body, deps, name,
        (pltpu.with_memory_space_constraint(far, pltpu.HBM), pltpu.with_memory_space_constraint(pair, pltpu.HBM)),
        (_HBM, _HBM),
        (pltpu.SemaphoreType.DMA((1,)), pltpu.SemaphoreType.DMA((1,)),
         pltpu.HBM(far.shape, far.dtype), pltpu.HBM(pair.shape, pair.dtype), _token_shape()),
        (_SEM, _SEM, _HBM, _HBM, _TOKEN), {0: 2, 1: 3})


def _pair_wait(name, started, after):
    send_sems, recv_sems, far, pair, _ = started

    def body(far_ref, pair_ref, send_sems, recv_sems, after_ref, far_out, pair_out):
        del after_ref, far_out, pair_out
        cp = _remote(far_ref, pair_ref, send_sems.at[0], recv_sems.at[0], _place()[3])
        cp.wait_send()
        cp.wait_recv()

    return _split_call(
        body, (), name, (far, pair, send_sems, recv_sems, after),
        (_HBM, _HBM, _SEM, _SEM, pl.BlockSpec(memory_space=pl.ANY)),
        (pltpu.HBM(far.shape, far.dtype), pltpu.HBM(pair.shape, pair.dtype)), (_HBM, _HBM), {0: 0, 1: 1})[1]


def _chip_start(name, sums, deps=()):
    land = lax.empty((3,) + sums.shape[1:], sums.dtype)

    def body(s_ref, land_ref, send_sems, recv_sems, s_thru, land_thru, token):
        del s_thru, land_thru
        x, y, c, _ = _place()
        for k in _OTHER_CHIPS:
            peer, chip = _same_core_of(x, y, c, k)
            _remote(s_ref.at[chip], land_ref.at[k - 1], send_sems.at[k - 1], recv_sems.at[k - 1], peer).start()
        token[...] = jnp.zeros_like(token)

    return _split_call(
        body, deps, name,
        (pltpu.with_memory_space_constraint(sums, pltpu.HBM), pltpu.with_memory_space_constraint(land, pltpu.HBM)),
        (_HBM, _HBM),
        (pltpu.SemaphoreType.DMA((3,)), pltpu.SemaphoreType.DMA((3,)),
         pltpu.HBM(sums.shape, sums.dtype), pltpu.HBM(land.shape, land.dtype), _token_shape()),
        (_SEM, _SEM, _HBM, _HBM, _TOKEN), {0: 2, 1: 3})


def _chip_wait(name, started, after):
    send_sems, recv_sems, sums, land, _ = started

    def body(s_ref, land_ref, send_sems, recv_sems, after_ref, s_out, land_out):
        del after_ref, s_out, land_out
        x, y, c, _ = _place()
        for k in _OTHER_CHIPS:
            peer, chip = _same_core_of(x, y, c, k)
            cp = _remote(s_ref.at[chip], land_ref.at[k - 1], send_sems.at[k - 1], recv_sems.at[k - 1], peer)
            cp.wait_send()
            cp.wait_recv()

    return _split_call(
        body, (), name, (sums, land, send_sems, recv_sems, after),
        (_HBM, _HBM, _SEM, _SEM, pl.BlockSpec(memory_space=pl.ANY)),
        (pltpu.HBM(sums.shape, sums.dtype), pltpu.HBM(land.shape, land.dtype)), (_HBM, _HBM), {0: 0, 1: 1})


_DOT_DIMS = {"nn": (((1,), (0,)), ((), ())), "nt": (((1,), (1,)), ((), ())), "tn": (((0,), (0,)), ((), ()))}


def _matmul(name, mode, operands, in_specs, out_shape, out_specs, grid, acc_shape, epilogue, deps=(), carry=()):
    n_in, n_out, nk = len(operands), len(out_shape), grid[2]
    dims = _DOT_DIMS[mode]

    def body(*refs):
        a_ref, b_ref = refs[0], refs[1]
        extras, outs = refs[2:n_in], refs[n_in:n_in + n_out]
        b_val = b_ref[...]
        if b_val.ndim == 3:
            b_val = jnp.concatenate([b_val[g] for g in range(b_val.shape[0])], axis=1)
        part = lax.dot_general(a_ref[...], b_val, dims, preferred_element_type=F32)
        if nk == 1:
            epilogue(part, extras, outs)
            return
        acc = refs[-1]
        k = pl.program_id(2)

        @pl.when(k == 0)
        def _():
            acc[...] = part

        @pl.when(jnp.logical_and(k > 0, k < nk - 1))
        def _():
            acc[...] += part

        @pl.when(k == nk - 1)
        def _():
            epilogue(acc[...] + part, extras, outs)

    aliases = {n_in + len(deps) + i: i for i in range(len(carry))}
    return _pallas(body, tuple(deps) + tuple(carry), name=name, grid=grid, in_specs=in_specs, out_specs=out_specs,
                   out_shape=out_shape, input_output_aliases=aliases,
                   scratch_shapes=[pltpu.VMEM(acc_shape, F32)] if nk > 1 else [],
                   compiler_params=_params("parallel", "parallel", "arbitrary"))(*operands)


def _store(dtype):
    def epilogue(acc, extras, outs):
        outs[0][...] = acc.astype(dtype)
    return epilogue


def _residual_epilogue(acc, extras, outs):
    x_ref, gate_ref = extras
    outs[0][...] = acc.astype(outs[0].dtype)
    outs[1][...] = x_ref[...] + gate_ref[...] * acc


def _relu2_epilogue(acc, extras, outs):
    r = jnp.maximum(acc, 0.0)
    outs[0][...] = r.astype(outs[0].dtype)
    outs[1][...] = (r * r).astype(outs[1].dtype)


def _relu2_bwd_epilogue(acc, extras, outs):
    outs[0][...] = (acc * (2.0 * extras[0][...].astype(F32))).astype(outs[0].dtype)


def _no_extra_specs(tm, tn):
    return []


def _mm_nn(name, a, b, n_total, b_split, out_shape, epilogue, extras=(), extra_specs=_no_extra_specs, tm=MM_TM, tn=MM_TN, tk=MM_TK,
           deps=(), pieces=None, carry=()):
    m, kdim = a.shape
    tm, tk = _tile(tm, m), _tile(tk, kdim)
    n_blocks = None
    if b_split:
        piece = b.shape[2]
        tn = _tile(tn, piece)
        per = piece // tn
        if pieces is None:
            b_spec = pl.BlockSpec((None, tk, tn), lambda i, j, k: (j // per, k, j % per))
            out_spec = pl.BlockSpec((tm, tn), lambda i, j, k: (i, j))
        else:
            piece_of, count = pieces
            n_blocks = count * per

            def which(j):
                return piece_of(j // per)

            b_spec = pl.BlockSpec((None, tk, tn), lambda i, j, k: (which(j), k, j % per))
            out_spec = pl.BlockSpec((tm, tn), lambda i, j, k: (i, which(j) * per + j % per))
    else:
        tn = _tile(tn, n_total)
        b_spec = pl.BlockSpec((tk, tn), lambda i, j, k: (k, j))
        out_spec = pl.BlockSpec((tm, tn), lambda i, j, k: (i, j))
    if n_blocks is None:
        n_blocks = n_total // tn
    in_specs = [pl.BlockSpec((tm, tk), lambda i, j, k: (i, k)), b_spec] + list(extra_specs(tm, tn))
    return _matmul(name, "nn", (a, b) + tuple(extras), in_specs, out_shape, [out_spec] * len(out_shape),
                   (m // tm, n_blocks, kdim // tk), (tm, tn), epilogue, deps, carry)


def _mm_nt(name, a, b, n_total, b_split, out_shape, epilogue, extras=(), extra_specs=_no_extra_specs, tm=MM_TM, tn=MM_TN, tk=MM_TK,
           deps=()):
    m, kdim = a.shape
    tm, tn = _tile(tm, m), _tile(tn, n_total)
    if b_split and tk >= 2 * b.shape[2]:
        piece = b.shape[2]
        group = _tile(tk, kdim) // piece
        tk = group * piece
        b_spec = pl.BlockSpec((group, tn, piece), lambda i, j, k: (k, j, 0))
    elif b_split:
        piece = b.shape[2]
        tk = _tile(tk, piece)
        per = piece // tk
        b_spec = pl.BlockSpec((None, tn, tk), lambda i, j, k: (k // per, j, k % per))
    else:
        tk = _tile(tk, kdim)
        b_spec = pl.BlockSpec((tn, tk), lambda i, j, k: (j, k))
    in_specs = [pl.BlockSpec((tm, tk), lambda i, j, k: (i, k)), b_spec] + list(extra_specs(tm, tn))
    out_specs = [pl.BlockSpec((tm, tn), lambda i, j, k: (i, j)) for _ in out_shape]
    return _matmul(name, "nt", (a, b) + tuple(extras), in_specs, out_shape, out_specs,
                   (m // tm, n_total // tn, kdim // tk), (tm, tn), epilogue, deps)


def _add_pair_epilogue(acc, extras, outs):
    outs[0][...] = (acc + extras[0][...].astype(F32)).astype(outs[0].dtype)


def _mm_tn_half(name, a, b, col_pieces, near, pair=None, tm=MM_TM, tn=MM_TN, tk=MM_TK, deps=()):
    kdim, m = a.shape
    n_total = b.shape[1]
    tk = _tile(tk, kdim)

    def core():
        c = lax.axis_index("c")
        return c if near else 1 - c

    if col_pieces:
        piece = n_total // N_DEV
        tm, tn = _tile(tm, m), _tile(tn, piece)
        per = piece // tn
        grid = (m // tm, N_CHIP * per, kdim // tk)
        a_spec = pl.BlockSpec((tk, tm), lambda i, j, k: (k, i))
        b_spec = pl.BlockSpec((tk, tn), lambda i, j, k: (k, (2 * (j // per) + core()) * per + j % per))
        out_spec = pl.BlockSpec((None, tm, tn), lambda i, j, k: (j // per, i, j % per))
        out_shape = [jax.ShapeDtypeStruct((N_CHIP, m, piece), BF16)]
    else:
        piece = m // N_DEV
        tm, tn = _tile(tm, piece), _tile(tn, n_total)
        per = piece // tm
        grid = (N_CHIP * per, n_total // tn, kdim // tk)
        a_spec = pl.BlockSpec((tk, tm), lambda i, j, k: (k, (2 * (i // per) + core()) * per + i % per))
        b_spec = pl.BlockSpec((tk, tn), lambda i, j, k: (k, j))
        out_spec = pl.BlockSpec((None, tm, tn), lambda i, j, k: (i // per, i % per, j))
        out_shape = [jax.ShapeDtypeStruct((N_CHIP, piece, n_total), BF16)]
    operands, in_specs, epilogue = (a, b), [a_spec, b_spec], _store(BF16)
    if pair is not None:
        operands, in_specs, epilogue = (a, b, pair), [a_spec, b_spec, out_spec], _add_pair_epilogue
    return _matmul(name, "tn", operands, in_specs, out_shape, [out_spec], grid, (tm, tn), epilogue, deps)[0]


def _rms(xv):
    return lax.rsqrt(jnp.mean(xv * xv, axis=-1, keepdims=True) + EPS)


def _colsum(v):
    return jnp.sum(v, axis=0, keepdims=True)


def _norm_mod(name, x, g, scale, shift, tr=256, deps=()):
    s, d = x.shape
    tr = _tile(tr, s)

    def body(x_ref, g_ref, sc_ref, sh_ref, h_ref):
        xv = x_ref[...]
        h = (xv * _rms(xv)) * g_ref[...]
        h_ref[...] = (h * (1.0 + sc_ref[...]) + sh_ref[...]).astype(h_ref.dtype)

    row = pl.BlockSpec((tr, d), lambda i: (i, 0))
    vec = pl.BlockSpec((1, d), lambda i: (0, 0))
    return _pallas(body, deps, name=name, grid=(s // tr,), in_specs=[row, vec, vec, vec], out_specs=row,
                   out_shape=jax.ShapeDtypeStruct((s, d), BF16), compiler_params=_params("parallel"))(x, g, scale, shift)


def _loss_head(x3, target, gf, gate2, mlp, tr=128):
    s, d = x3.shape
    tr = _tile(tr, s)

    def body(x_ref, t_ref, gf_ref, gate_ref, mlp_ref, dx_ref, dbr_ref, dgf_ref, dgate_ref, loss_ref):
        @pl.when(pl.program_id(0) == 0)
        def _():
            dgf_ref[...] = jnp.zeros_like(dgf_ref)
            dgate_ref[...] = jnp.zeros_like(dgate_ref)
            loss_ref[...] = jnp.zeros_like(loss_ref)

        xv = x_ref[...]
        r = _rms(xv)
        xn = xv * r
        gfv = gf_ref[...]
        err = xn * gfv - t_ref[...]
        loss_ref[...] += 0.5 * _colsum(jnp.mean(err * err, axis=-1, keepdims=True))
        dy = err * (1.0 / d)
        dgf_ref[...] += _colsum(dy * xn)
        dxn = dy * gfv
        dx = r * (dxn - xn * jnp.mean(dxn * xn, axis=-1, keepdims=True))
        dx_ref[...] = dx
        dbr_ref[...] = (dx * gate_ref[...]).astype(dbr_ref.dtype)
        dgate_ref[...] += _colsum(dx * mlp_ref[...].astype(F32))

    row = pl.BlockSpec((tr, d), lambda i: (i, 0))
    vec = pl.BlockSpec((1, d), lambda i: (0, 0))
    return _pallas(
        body, name="loss_head", grid=(s // tr,), in_specs=[row, row, vec, vec, row],
        out_specs=[row, row, vec, vec, pl.BlockSpec((1, 128), lambda i: (0, 0))],
        out_shape=[jax.ShapeDtypeStruct((s, d), F32), jax.ShapeDtypeStruct((s, d), BF16),
                   jax.ShapeDtypeStruct((1, d), F32), jax.ShapeDtypeStruct((1, d), F32),
                   jax.ShapeDtypeStruct((1, 128), F32)],
        compiler_params=_params("arbitrary"))(x3, target, gf, gate2, mlp)


def _norm_mod_bwd(name, dh, xin, g, scale, dx_up, branch=None, gate=None, tr=128, deps=()):
    s, d = xin.shape
    tr = _tile(tr, s)
    with_gate = branch is not None

    def body(*refs):
        dh_ref, x_ref, g_ref, sc_ref, up_ref = refs[:5]
        if with_gate:
            br_ref, gate_ref = refs[5:7]
            dx_ref, dsh_ref, dsc_ref, dg_ref, dbr_ref, dgate_ref = refs[7:]
            sums = (dsh_ref, dsc_ref, dg_ref, dgate_ref)
        else:
            dx_ref, dsh_ref, dsc_ref, dg_ref = refs[5:]
            sums = (dsh_ref, dsc_ref, dg_ref)

        @pl.when(pl.program_id(0) == 0)
        def _():
            for ref in sums:
                ref[...] = jnp.zeros_like(ref)

        xv, dhv, gv = x_ref[...], dh_ref[...].astype(F32), g_ref[...]
        r = _rms(xv)
        xn = xv * r
        one_sc = 1.0 + sc_ref[...]
        dsh_ref[...] += _colsum(dhv)
        dsc_ref[...] += _colsum(dhv * (xn * gv))
        dg_ref[...] += _colsum(dhv * one_sc * xn)
        dxn = dhv * one_sc * gv
        dx = up_ref[...] + r * (dxn - xn * jnp.mean(dxn * xn, axis=-1, keepdims=True))
        dx_ref[...] = dx
        if with_gate:
            dbr_ref[...] = (dx * gate_ref[...]).astype(dbr_ref.dtype)
            dgate_ref[...] += _colsum(dx * br_ref[...].astype(F32))

    row = pl.BlockSpec((tr, d), lambda i: (i, 0))
    vec = pl.BlockSpec((1, d), lambda i: (0, 0))
    vshape = jax.ShapeDtypeStruct((1, d), F32)
    operands = [dh, xin, g, scale, dx_up]
    in_specs = [row, row, vec, vec, row]
    out_shape = [jax.ShapeDtypeStruct((s, d), F32), vshape, vshape, vshape]
    out_specs = [row, vec, vec, vec]
    if with_gate:
        operands += [branch, gate]
        in_specs += [row, vec]
        out_shape += [jax.ShapeDtypeStruct((s, d), BF16), vshape]
        out_specs += [row, vec]
    return _pallas(body, deps, name=name, grid=(s // tr,), in_specs=in_specs, out_specs=out_specs, out_shape=out_shape,
                   compiler_params=_params("arbitrary"))(*operands)


def _window_count(c0, rows, half, s):
    t = c0 + lax.broadcasted_iota(jnp.int32, (rows, 1), 0)
    return (jnp.minimum(t + half, s) - jnp.maximum(t - half, 0)).astype(F32)


def _zero_pads(pad, s):
    zeros = jnp.zeros((PAD_ROWS, pad.shape[1]), pad.dtype)
    pad[0:PAD_ROWS, :] = zeros
    pad[PAD_ROWS + s:PAD_ROWS + s + PAD_ROWS, :] = zeros


def _pool_fwd(proj, s, gd, cb, ch, deps=()):
    nsub = gd // cb

    def body(v_ref, o_ref, pad):
        g = pl.program_id(0)
        _zero_pads(pad, s)
        pad[PAD_ROWS:PAD_ROWS + s, :] = v_ref[...].astype(F32)
        for gi, window in enumerate(POOL_WINDOWS):
            half = window // 2

            @pl.when(g == gi)
            def _(half=half):
                for c0 in range(0, s, ch):
                    base = PAD_ROWS + c0
                    acc = pad[base - half:base - half + ch, :]
                    for j in range(-half + 1, half):
                        acc = acc + pad[base + j:base + j + ch, :]
                    out = acc / _window_count(c0, ch, half, s) - pad[base:base + ch, :]
                    o_ref[c0:c0 + ch, :] = out.astype(o_ref.dtype)

    spec = pl.BlockSpec((s, cb), lambda g, j: (0, g * nsub + j))
    return _pallas(body, deps, name="pool_fwd", grid=(N_POOL_GROUPS, nsub), in_specs=[spec], out_specs=spec,
                   out_shape=jax.ShapeDtypeStruct((s, N_POOL_GROUPS * gd), BF16),
                   scratch_shapes=[pltpu.VMEM((s + 2 * PAD_ROWS, cb), F32)],
                   compiler_params=_params("parallel", "parallel"))(proj)


def _pool_bwd(dpooled, dproj, s, gd, cb, ch):
    nsub = gd // cb

    def body(dp_ref, dproj_in, o_ref, pad):
        del dproj_in
        g = pl.program_id(0)
        _zero_pads(pad, s)
        for gi, window in enumerate(POOL_WINDOWS):
            half = window // 2

            @pl.when(g == gi)
            def _(half=half):
                for c0 in range(0, s, ch):
                    pad[PAD_ROWS + c0:PAD_ROWS + c0 + ch, :] = dp_ref[c0:c0 + ch, :] / _window_count(c0, ch, half, s)
                for c0 in range(0, s, ch):
                    base = PAD_ROWS + c0
                    acc = pad[base - half + 1:base - half + 1 + ch, :]
                    for j in range(-half + 2, half + 1):
                        acc = acc + pad[base + j:base + j + ch, :]
                    o_ref[c0:c0 + ch, :] = (acc - dp_ref[c0:c0 + ch, :]).astype(o_ref.dtype)

    spec = pl.BlockSpec((s, cb), lambda g, j: (0, g * nsub + j))
    return _pallas(body, name="pool_bwd", grid=(N_POOL_GROUPS, nsub),
                   in_specs=[spec, pl.BlockSpec(memory_space=pl.ANY)], out_specs=spec,
                   out_shape=jax.ShapeDtypeStruct(dproj.shape, dproj.dtype), input_output_aliases={1: 0},
                   scratch_shapes=[pltpu.VMEM((s + 2 * PAD_ROWS, cb), F32)],
                   compiler_params=_params("parallel", "parallel"))(dpooled, dproj)


def _poolmix_fwd(pooled, wmix, pool_scale, gnorm_g, d_model, tm=512):
    s = pooled.shape[0]
    gd = wmix.shape[1]
    tm = _tile(tm, s)

    def body(p_ref, w_ref, ps_ref, g_ref, apre_ref, mixed_ref):
        a_pre = jnp.dot(p_ref[...], w_ref[...], preferred_element_type=F32)
        apre_ref[...] = a_pre
        a_out = a_pre * ps_ref[...]
        mixed_ref[...] = ((a_out * _rms(a_out)) * g_ref[...]).astype(mixed_ref.dtype)

    blk = pl.BlockSpec((tm, gd), lambda g, i: (i, g))
    vec = pl.BlockSpec((1, gd), lambda g, i: (0, g))
    return _pallas(body, name="poolmix_fwd", grid=(N_POOL_GROUPS, s // tm),
                   in_specs=[blk, pl.BlockSpec((None, gd, gd), lambda g, i: (g, 0, 0)), vec, vec],
                   out_specs=[blk, blk],
                   out_shape=[jax.ShapeDtypeStruct((s, N_POOL_GROUPS * gd), F32), jax.ShapeDtypeStruct((s, d_model), BF16)],
                   compiler_params=_params("parallel", "parallel"))(pooled, wmix, pool_scale, gnorm_g)


def _poolmix_bwd(dmixed, a_pre, wmix, pool_scale, gnorm_g, tm=512):
    s = a_pre.shape[0]
    gd = wmix.shape[1]
    tm = _tile(tm, s)

    def body(dm_ref, apre_ref, w_ref, ps_ref, g_ref, dapre_ref, dpooled_ref, dps_ref, dg_ref):
        @pl.when(pl.program_id(1) == 0)
        def _():
            dps_ref[...] = jnp.zeros_like(dps_ref)
            dg_ref[...] = jnp.zeros_like(dg_ref)

        a_pre, dm, ps = apre_ref[...], dm_ref[...].astype(F32), ps_ref[...]
        a_out = a_pre * ps
        r = _rms(a_out)
        n = a_out * r
        dg_ref[...] += _colsum(dm * n)
        dn = dm * g_ref[...]
        da_out = r * (dn - n * jnp.mean(dn * n, axis=-1, keepdims=True))
        dps_ref[...] += _colsum(da_out * a_pre)
        da_pre = (da_out * ps).astype(BF16)
        dapre_ref[...] = da_pre
        dpooled_ref[...] = lax.dot_general(da_pre, w_ref[...], _DOT_DIMS["nt"], preferred_element_type=F32)

    blk = pl.BlockSpec((tm, gd), lambda g, i: (i, g))
    vec = pl.BlockSpec((1, gd), lambda g, i: (0, g))
    width = N_POOL_GROUPS * gd
    return _pallas(body, name="poolmix_bwd", grid=(N_POOL_GROUPS, s // tm),
                   in_specs=[blk, blk, pl.BlockSpec((None, gd, gd), lambda g, i: (g, 0, 0)), vec, vec],
                   out_specs=[blk, blk, vec, vec],
                   out_shape=[jax.ShapeDtypeStruct((s, width), BF16), jax.ShapeDtypeStruct((s, width), F32),
                              jax.ShapeDtypeStruct((1, width), F32), jax.ShapeDtypeStruct((1, width), F32)],
                   compiler_params=_params("parallel", "arbitrary"))(dmixed, a_pre, wmix, pool_scale, gnorm_g)


def _poolmix_wgrad(pooled, da_pre, gd, tk=1024):
    s = pooled.shape[0]
    tk = _tile(tk, s)
    nk = s // tk

    def body(p_ref, d_ref, o_ref, acc):
        k = pl.program_id(1)
        part = lax.dot_general(p_ref[...], d_ref[...], _DOT_DIMS["tn"], preferred_element_type=F32)

        @pl.when(k == 0)
        def _():
            acc[...] = part

        @pl.when(k > 0)
        def _():
            acc[...] += part

        @pl.when(k == nk - 1)
        def _():
            o_ref[...] = acc[...].astype(o_ref.dtype)

    blk = pl.BlockSpec((tk, gd), lambda g, k: (k, g))
    return _pallas(body, name="poolmix_wgrad", grid=(N_POOL_GROUPS, nk), in_specs=[blk, blk],
                   out_specs=pl.BlockSpec((None, gd, gd), lambda g, k: (g, 0, 0)),
                   out_shape=jax.ShapeDtypeStruct((N_POOL_GROUPS, gd, gd), BF16),
                   scratch_shapes=[pltpu.VMEM((gd, gd), F32)],
                   compiler_params=_params("parallel", "arbitrary"))(pooled, da_pre)


def _head_mean(v):
    parts = []
    for q in range(v.shape[1] // CONV_HEAD_DIM):
        m = jnp.mean(v[:, q * CONV_HEAD_DIM:(q + 1) * CONV_HEAD_DIM], axis=-1, keepdims=True)
        parts.append(jnp.broadcast_to(m, (v.shape[0], CONV_HEAD_DIM)))
    return parts[0] if len(parts) == 1 else jnp.concatenate(parts, axis=1)


def _conv_fwd(proj, mixed, conv_w, conv_b, gnorm_g, s, width, cb, ch, deps=()):
    nblk = width // cb

    def body(b_ref, c_ref, u_ref, w_ref, cb_ref, g_ref, mixed_in, o_ref, pad):
        del mixed_in
        _zero_pads(pad, s)
        pad[PAD_ROWS:PAD_ROWS + s, :] = c_ref[...].astype(F32) * u_ref[...].astype(F32)
        w = w_ref[...]
        for c0 in range(0, s, ch):
            base = PAD_ROWS + c0
            conv = (w[0:1] * pad[base - 1:base - 1 + ch, :] + w[1:2] * pad[base:base + ch, :]
                    + w[2:3] * pad[base + 1:base + 1 + ch, :] + cb_ref[...])
            bo = b_ref[c0:c0 + ch, :].astype(F32) * conv
            n = bo * lax.rsqrt(_head_mean(bo * bo) + EPS)
            o_ref[c0:c0 + ch, :] = (n * g_ref[...]).astype(o_ref.dtype)

    def part(p):
        return pl.BlockSpec((s, cb), lambda j: (0, p * nblk + j))

    vec = pl.BlockSpec((1, cb), lambda j: (0, j))
    return _pallas(body, deps, name="conv_fwd", grid=(nblk,),
                   in_specs=[part(1), part(2), part(3), pl.BlockSpec((3, cb), lambda j: (0, j)), vec, vec,
                             pl.BlockSpec(memory_space=pl.ANY)],
                   out_specs=part(1), out_shape=jax.ShapeDtypeStruct(mixed.shape, mixed.dtype),
                   input_output_aliases={6: 0},
                   scratch_shapes=[pltpu.VMEM((s + 2 * PAD_ROWS, cb), F32)],
                   compiler_params=_params("parallel"))(proj, proj, proj, conv_w, conv_b, gnorm_g, mixed)


def _conv_bwd(dmixed, proj, conv_w, conv_b, gnorm_g, s, width, cb, ch, deps=()):
    nblk = width // cb

    def body(dm_ref, b_ref, c_ref, u_ref, w_ref, cb_ref, g_ref, dproj_ref, dw_ref, dcb_ref, dg_ref,
             pad_cu, pad_dconv, db_buf, dc_buf, du_buf, sems):
        j = pl.program_id(0)
        _zero_pads(pad_cu, s)
        _zero_pads(pad_dconv, s)
        pad_cu[PAD_ROWS:PAD_ROWS + s, :] = c_ref[...].astype(F32) * u_ref[...].astype(F32)
        w, gv = w_ref[...], g_ref[...]
        zero = jnp.zeros((1, cb), F32)
        dw0, dw1, dw2, dcb, dg = zero, zero, zero, zero, zero
        for c0 in range(0, s, ch):
            base = PAD_ROWS + c0
            cu_prev, cu_here, cu_next = (pad_cu[base - 1:base - 1 + ch, :], pad_cu[base:base + ch, :],
                                         pad_cu[base + 1:base + 1 + ch, :])
            conv = w[0:1] * cu_prev + w[1:2] * cu_here + w[2:3] * cu_next + cb_ref[...]
            bg = b_ref[c0:c0 + ch, :].astype(F32)
            bo = bg * conv
            r = lax.rsqrt(_head_mean(bo * bo) + EPS)
            n = bo * r
            dm = dm_ref[c0:c0 + ch, :].astype(F32)
            dg = dg + _colsum(dm * n)
            dn = dm * gv
            dbo = r * (dn - n * _head_mean(dn * n))
            db_buf[c0:c0 + ch, :] = (dbo * conv).astype(BF16)
            dconv = dbo * bg
            pad_dconv[base:base + ch, :] = dconv
            dcb = dcb + _colsum(dconv)
            dw0 = dw0 + _colsum(dconv * cu_prev)
            dw1 = dw1 + _colsum(dconv * cu_here)
            dw2 = dw2 + _colsum(dconv * cu_next)
        dw_ref[0:1, :] = dw0
        dw_ref[1:2, :] = dw1
        dw_ref[2:3, :] = dw2
        dcb_ref[...] = dcb
        dg_ref[...] = dg
        for c0 in range(0, s, ch):
            base = PAD_ROWS + c0
            dcu = (w[0:1] * pad_dconv[base + 1:base + 1 + ch, :] + w[1:2] * pad_dconv[base:base + ch, :]
                   + w[2:3] * pad_dconv[base - 1:base - 1 + ch, :])
            dc_buf[c0:c0 + ch, :] = (dcu * u_ref[c0:c0 + ch, :].astype(F32)).astype(BF16)
            du_buf[c0:c0 + ch, :] = (dcu * c_ref[c0:c0 + ch, :].astype(F32)).astype(BF16)
        copies = []
        for p, buf in enumerate((db_buf, dc_buf, du_buf)):
            col = pl.multiple_of((p + 1) * width + j * cb, CONV_HEAD_DIM)
            copies.append(pltpu.make_async_copy(buf, dproj_ref.at[:, pl.ds(col, cb)], sems.at[p]))
            copies[-1].start()
        for cp in copies:
            cp.wait()

    def part(p):
        return pl.BlockSpec((s, cb), lambda j: (0, p * nblk + j))

    vec = pl.BlockSpec((1, cb), lambda j: (0, j))
    w_spec = pl.BlockSpec((3, cb), lambda j: (0, j))
    return _pallas(body, deps, name="conv_bwd", grid=(nblk,),
                   in_specs=[part(1), part(1), part(2), part(3), w_spec, vec, vec],
                   out_specs=[pl.BlockSpec(memory_space=pl.ANY), w_spec, vec, vec],
                   out_shape=[jax.ShapeDtypeStruct((s, 4 * width), BF16), jax.ShapeDtypeStruct((3, width), F32),
                              jax.ShapeDtypeStruct((1, width), F32), jax.ShapeDtypeStruct((1, width), F32)],
                   scratch_shapes=[pltpu.VMEM((s + 2 * PAD_ROWS, cb), F32), pltpu.VMEM((s + 2 * PAD_ROWS, cb), F32),
                                   pltpu.VMEM((s, cb), BF16), pltpu.VMEM((s, cb), BF16), pltpu.VMEM((s, cb), BF16),
                                   pltpu.SemaphoreType.DMA((3,))],
                   compiler_params=_params("arbitrary"))(dmixed, proj, proj, proj, conv_w, conv_b, gnorm_g)


def _adamw(w, g, m, v):
    m = ADAM_B1 * m + (1.0 - ADAM_B1) * g
    v = ADAM_B2 * v + (1.0 - ADAM_B2) * (g * g)
    m_hat = m / (1.0 - ADAM_B1 ** ADAM_STEP)
    v_hat = v / (1.0 - ADAM_B2 ** ADAM_STEP)
    delta = -ADAM_LR * (m_hat / (jnp.sqrt(v_hat) + ADAM_EPS) + ADAM_WD * w)
    return delta, m, v


def _ada_fwd(c_rows, w, b, tn=512):
    rows, d = c_rows.shape
    n = w.shape[1]
    tn = _tile(tn, n)

    def body(c_ref, w_ref, b_ref, o_ref):
        cv = c_ref[...]
        act = (cv * jax.nn.sigmoid(cv)).astype(BF16)
        o_ref[...] = jnp.dot(act, w_ref[...].astype(BF16), preferred_element_type=F32) + b_ref[...]

    return _pallas(body, name="ada_fwd", grid=(n // tn,),
                   in_specs=[pl.BlockSpec((rows, d), lambda j: (0, 0)), pl.BlockSpec((d, tn), lambda j: (0, j)),
                             pl.BlockSpec((1, tn), lambda j: (0, j))],
                   out_specs=pl.BlockSpec((rows, tn), lambda j: (0, j)),
                   out_shape=jax.ShapeDtypeStruct((rows, n), F32), compiler_params=_params("parallel"))(c_rows, w, b)


def _ada_bwd_adam(c_cols, dmod, w, m, v, tr=512, tn=1024):
    d, rows = c_cols.shape
    n = w.shape[1]
    tr, tn = _tile(tr, d), _tile(tn, n)

    def body(c_ref, dm_ref, w_ref, m_ref, v_ref, g_ref, dl_ref, nm_ref, nv_ref):
        cv = c_ref[...]
        act = (cv * jax.nn.sigmoid(cv)).astype(BF16)
        g = jnp.dot(act, dm_ref[...].astype(BF16), preferred_element_type=F32)
        g_ref[...] = g
        dl_ref[...], nm_ref[...], nv_ref[...] = _adamw(w_ref[...], g, m_ref[...], v_ref[...])

    blk = pl.BlockSpec((tr, tn), lambda i, j: (i, j))
    shape = jax.ShapeDtypeStruct((d, n), F32)
    return _pallas(body, name="ada_bwd_adam", grid=(d // tr, n // tn),
                   in_specs=[pl.BlockSpec((tr, rows), lambda i, j: (i, 0)), pl.BlockSpec((rows, tn), lambda i, j: (0, j)),
                             blk, blk, blk],
                   out_specs=[blk] * 4, out_shape=[shape] * 4,
                   compiler_params=_params("parallel", "parallel"))(c_cols, dmod, w, m, v)


def _reduce_adam(name, pieces, w, m, v, tr=256, tc=1024):
    r, c = w.shape
    tr, tc = _tile(tr, r), _tile(tc, c)

    def body(p_ref, w_ref, m_ref, v_ref, g_ref, dl_ref, nm_ref, nv_ref):
        g = p_ref[0].astype(F32)
        for j in range(1, N_DEV):
            g = g + p_ref[j].astype(F32)
        g_ref[...] = g
        dl_ref[...], nm_ref[...], nv_ref[...] = _adamw(w_ref[...], g, m_ref[...], v_ref[...])

    blk = pl.BlockSpec((tr, tc), lambda i, j: (i, j))
    shape = jax.ShapeDtypeStruct((r, c), F32)
    return _pallas(body, name=name, grid=(r // tr, c // tc),
                   in_specs=[pl.BlockSpec((N_DEV, tr, tc), lambda i, j: (0, i, j)), blk, blk, blk],
                   out_specs=[blk] * 4, out_shape=[shape] * 4,
                   compiler_params=_params("parallel", "parallel"))(pieces, w, m, v)


def _reduce_adam_chips(name, sums, land, w, m, v, tr=256, tc=1024):
    r, c = w.shape
    tr, tc = _tile(tr, r), _tile(tc, c)

    def body(s_ref, l_ref, w_ref, m_ref, v_ref, g_ref, dl_ref, nm_ref, nv_ref):
        g = s_ref[...].astype(F32)
        for k in range(3):
            g = g + l_ref[k].astype(F32)
        g_ref[...] = g
        dl_ref[...], nm_ref[...], nv_ref[...] = _adamw(w_ref[...], g, m_ref[...], v_ref[...])

    blk = pl.BlockSpec((tr, tc), lambda i, j: (i, j))
    shape = jax.ShapeDtypeStruct((r, c), F32)
    mine = pl.BlockSpec((None, tr, tc), lambda i, j: (2 * lax.axis_index("x") + lax.axis_index("y"), i, j))
    return _pallas(body, name=name, grid=(r // tr, c // tc),
                   in_specs=[mine, pl.BlockSpec((3, tr, tc), lambda i, j: (0, i, j)), blk, blk, blk],
                   out_specs=[blk] * 4, out_shape=[shape] * 4,
                   compiler_params=_params("parallel", "parallel"))(sums, land, w, m, v)


def _sum_devices(parts):
    n = parts.shape[1]

    def body(p_ref, o_ref):
        acc = p_ref[0:1, :]
        for j in range(1, N_DEV):
            acc = acc + p_ref[j:j + 1, :]
        o_ref[...] = acc

    return _pallas(body, name="sum_devices", out_shape=jax.ShapeDtypeStruct((1, n), F32),
                   compiler_params=pltpu.CompilerParams(vmem_limit_bytes=VMEM_LIMIT_BYTES))(parts)


def _adam_small(name, g, w, m, v):
    def body(g_ref, w_ref, m_ref, v_ref, dl_ref, nm_ref, nv_ref):
        dl_ref[...], nm_ref[...], nv_ref[...] = _adamw(w_ref[...], g_ref[...], m_ref[...], v_ref[...])

    shape = jax.ShapeDtypeStruct(w.shape, F32)
    return _pallas(body, name=name, out_shape=[shape] * 3,
                   compiler_params=pltpu.CompilerParams(vmem_limit_bytes=VMEM_LIMIT_BYTES))(g, w, m, v)


def kernel(x, c, w_ada, b_ada, norm1_g, w_in, pool_mix_w, pool_scale, conv_w, conv_b, gnorm_pool_g, gnorm_conv_g, w_out, norm2_g, w_mlp_in, w_mlp_out, final_g, loss_target, m_w_ada, m_b_ada, m_norm1_g, m_w_in, m_pool_mix_w, m_pool_scale, m_conv_w, m_conv_b, m_gnorm_pool_g, m_gnorm_conv_g, m_w_out, m_norm2_g, m_w_mlp_in, m_w_mlp_out, m_final_g, v_w_ada, v_b_ada, v_norm1_g, v_w_in, v_pool_mix_w, v_pool_scale, v_conv_w, v_conv_b, v_gnorm_pool_g, v_gnorm_conv_g, v_w_out, v_norm2_g, v_w_mlp_in, v_w_mlp_out, v_final_g):
    s, d = x.shape[1], x.shape[2]
    width = d // 2
    gd = width // N_POOL_GROUPS
    d_ff = w_mlp_in.shape[2] * N_DEV
    n_proj = w_in.shape[2] * N_DEV
    ada_cols = w_ada.shape[2]
    conv_cols = conv_w.shape[2]
    assert n_proj == 4 * width and ada_cols * N_DEV == N_MOD * d and d_ff % N_DEV == 0
    assert width % CONV_HEAD_DIM == 0 and s % 8 == 0
    seq_chunk = _tile(512, s)
    pool_cb = _tile(256, gd)
    conv_cb = CONV_HEAD_DIM

    me = 4 * lax.axis_index("x") + 2 * lax.axis_index("y") + lax.axis_index("c")
    x2d, target = x[0], loss_target[0]

    conv_w_all, c_all = _exchange("gather_small_weights", [conv_w[0], c], ["gather"] * 2)
    conv_w_full = jnp.transpose(conv_w_all, (1, 0, 2)).reshape(3, width)
    c_rows = jnp.concatenate([c_all.reshape(N_DEV, d), jnp.zeros((N_DEV, d), F32)], axis=0)

    b_mine = lax.dynamic_slice(b_ada, (0, me * ada_cols), (1, ada_cols))
    mod_part = _ada_fwd(c_rows, w_ada[0], b_mine)
    (mod_all,) = _exchange("scatter_mod", [mod_part[:N_DEV].reshape(N_DEV, 1, ada_cols)], ["a2a"])
    mod = mod_all.reshape(1, N_MOD * d)

    started, hopped, relayed = {}, {}, {}

    def gather_start(wname, wgt, deps):
        land = _landing(wgt[0].astype(BF16), me)
        started[wname] = _gather_start("gather_" + wname + "_start", land, deps)
        return started[wname][5]

    def gather_hop(wname, land, after):
        hopped[wname] = _gather_hop("gather_" + wname + "_hop", started[wname], land, after)
        return hopped[wname][3]

    def gather_relay(wname, after):
        relayed[wname] = _gather_relay("gather_" + wname + "_relay", started[wname], hopped[wname], after)
        return relayed[wname][3]

    def gather_wait(wname, land, after, local_waited=False):
        return _gather_wait("gather_" + wname + "_wait", started[wname], hopped[wname], relayed[wname], land, after,
                            local_waited)

    def chip():
        return 2 * lax.axis_index("x") + lax.axis_index("y")

    def local_piece(t):
        return 2 * chip() + t

    def same_core_piece(t):
        return 2 * ((chip() + 1 + t) % N_CHIP) + lax.axis_index("c")

    def other_core_piece(t):
        return 2 * ((chip() + 1 + t) % N_CHIP) + 1 - lax.axis_index("c")

    def routed_piece(t):
        first, second, diagonal = _routes(lax.axis_index("x"), lax.axis_index("y"), lax.axis_index("c"))
        return jnp.where(t == 0, _index_of(first), jnp.where(t == 1, _index_of(second), _index_of(diagonal)))

    tok_w_in = gather_start("w_in", w_in, (mod,))
    shift1, scale1, gate1, shift2, scale2, gate2 = [mod[:, i * d:(i + 1) * d] for i in range(N_MOD)]

    h1 = _norm_mod("norm1_fwd", x2d, norm1_g, scale1, shift1, deps=(tok_w_in,))
    proj_shape = [jax.ShapeDtypeStruct((s, n_proj), BF16)]
    w_in_local = _gather_wait_local("gather_w_in_local", started["w_in"], started["w_in"][4], h1)
    (proj,) = _mm_nn("in_proj_local", h1, w_in_local, n_proj, True, proj_shape, _store(BF16), pieces=(local_piece, 2))
    tok = gather_hop("w_in", w_in_local, proj)
    tok = gather_start("w_out", w_out, (tok,))
    mix_started = _push_start("gather_pool_mix_start", [pool_mix_w[0].astype(BF16)], ("gather",), me, deps=(tok,))
    (proj,) = _mm_nn("in_proj_first", h1, hopped["w_in"][2], n_proj, True, proj_shape, _store(BF16),
                     pieces=(routed_piece, 1), carry=(proj,), deps=(mix_started[-1],))
    tok = gather_relay("w_in", proj)
    tok = gather_hop("w_out", started["w_out"][4], tok)
    tok = gather_start("w_mlp_in", w_mlp_in, (tok,))
    (proj,) = _mm_nn("in_proj_same_core", h1, relayed["w_in"][2], n_proj, True, proj_shape, _store(BF16),
                     pieces=(lambda t: routed_piece(t + 1), 2), carry=(proj,), deps=(tok,))
    w_in_all = gather_wait("w_in", relayed["w_in"][2], proj, True)
    (proj,) = _mm_nn("in_proj", h1, w_in_all, n_proj, True, proj_shape, _store(BF16), pieces=(other_core_piece, 3),
                     carry=(proj,))
    pooled = _pool_fwd(proj, s, gd, pool_cb, seq_chunk)
    (wmix_all,) = _push_wait("gather_pool_mix_wait", mix_started, ("gather",), pooled)
    wmix_full = jnp.transpose(wmix_all, (1, 0, 2, 3)).reshape(N_POOL_GROUPS, gd, gd)
    a_pre, mixed = _poolmix_fwd(pooled, wmix_full, pool_scale, gnorm_pool_g, d)
    tok = gather_hop("w_mlp_in", started["w_mlp_in"][4], a_pre)
    tok = gather_start("w_mlp_out", w_mlp_out, (tok,))
    tok = gather_relay("w_out", tok)
    mixed = _conv_fwd(proj, mixed, conv_w_full, conv_b, gnorm_conv_g, s, width, conv_cb, seq_chunk, deps=(tok,))

    def residual_specs(tm, tn):
        return [pl.BlockSpec((tm, tn), lambda i, j, k: (i, j)), pl.BlockSpec((1, tn), lambda i, j, k: (0, j))]

    sd_f32 = jax.ShapeDtypeStruct((s, d), F32)
    w_out_full = gather_wait("w_out", relayed["w_out"][2], mixed).reshape(d, d)
    sd_bf16 = jax.ShapeDtypeStruct((s, d), BF16)
    attn, x_mid = _mm_nn("out_proj", mixed, w_out_full, d, False, [sd_bf16, sd_f32], _residual_epilogue,
                         extras=(x2d, gate1), extra_specs=residual_specs)
    h2 = _norm_mod("norm2_fwd", x_mid, norm2_g, scale2, shift2)
    sf_bf16 = [jax.ShapeDtypeStruct((s, d_ff), BF16)] * 2
    w1_local = _gather_wait_local("gather_w_mlp_in_local", started["w_mlp_in"], hopped["w_mlp_in"][2], h2)
    relu, hid = _mm_nn("mlp_in_local", h2, w1_local, d_ff, True, sf_bf16, _relu2_epilogue, pieces=(local_piece, 2))
    hopped["w_mlp_in"] = hopped["w_mlp_in"][:2] + (w1_local,) + hopped["w_mlp_in"][3:]
    tok = gather_relay("w_mlp_in", hid)
    relu, hid = _mm_nn("mlp_in_same_core", h2, relayed["w_mlp_in"][2], d_ff, True, sf_bf16, _relu2_epilogue,
                       pieces=(same_core_piece, 3), carry=(relu, hid), deps=(tok,))
    w1_all = gather_wait("w_mlp_in", relayed["w_mlp_in"][2], hid, True)
    tok = gather_hop("w_mlp_out", started["w_mlp_out"][4], w1_all)
    relu, hid = _mm_nn("mlp_in_other_core", h2, w1_all, d_ff, True, sf_bf16, _relu2_epilogue,
                       pieces=(other_core_piece, 2), carry=(relu, hid), deps=(tok,))
    tok = gather_relay("w_mlp_out", hid)
    relu, hid = _mm_nn("mlp_in", h2, w1_all, d_ff, True, sf_bf16, _relu2_epilogue,
                       pieces=(lambda t: other_core_piece(t + 2), 1), carry=(relu, hid), deps=(tok,))
    w2_full = gather_wait("w_mlp_out", relayed["w_mlp_out"][2], hid).reshape(d_ff, d)
    mlp, x_last = _mm_nn("mlp_out", hid, w2_full, d, False, [sd_bf16, sd_f32], _residual_epilogue,
                         extras=(x_mid, gate2), extra_specs=residual_specs)

    dx_last, dmlp, d_final_g, dgate2, loss_row = _loss_head(x_last, target, final_g.reshape(1, d), gate2, mlp)

    def relu_specs(tm, tn):
        return [pl.BlockSpec((tm, tn), lambda i, j, k: (i, j))]

    def reduce_start(wname, a, b, col_pieces, deps=()):
        far = _mm_tn_half(wname + "_dw_far", a, b, col_pieces, near=False, deps=deps)
        return _pair_start("scatter_" + wname + "_pair_start", far)

    def reduce_chips(wname, a, b, col_pieces, pairs, after):
        pair = _pair_wait("scatter_" + wname + "_pair_wait", pairs, after)
        sums = _mm_tn_half(wname + "_dw_near", a, b, col_pieces, near=True, pair=pair)
        return _chip_start("scatter_" + wname + "_chip_start", sums)

    pairs_w2 = reduce_start("mlp_out", hid, dmlp, False)
    (dhpre,) = _mm_nt("mlp_out_dx", dmlp, w2_full, d_ff, False, sf_bf16[:1], _relu2_bwd_epilogue,
                      extras=(relu,), extra_specs=relu_specs, deps=(pairs_w2[4],))
    chips_w2 = reduce_chips("mlp_out", hid, dmlp, False, pairs_w2, dhpre)
    pairs_w1 = reduce_start("mlp_in", h2, dhpre, True, deps=(chips_w2[4],))
    (dh2,) = _mm_nt("mlp_in_dx", dhpre, w1_all, d, True, [sd_bf16], _store(BF16), tn=1024, tk=2048,
                    deps=(pairs_w1[4],))
    chips_w1 = reduce_chips("mlp_in", h2, dhpre, True, pairs_w1, dh2)
    dx_mid, dshift2, dscale2, d_norm2_g, dattn, dgate1 = _norm_mod_bwd(
        "norm2_bwd", dh2, x_mid, norm2_g, scale2, dx_last, branch=attn, gate=gate1, deps=(chips_w1[4],))

    pairs_w_out = reduce_start("out_proj", mixed, dattn, False)
    (dmixed,) = _mm_nt("out_proj_dx", dattn, w_out_full, d, False, [sd_bf16], _store(BF16), deps=(pairs_w_out[4],))
    chips_w_out = reduce_chips("out_proj", mixed, dattn, False, pairs_w_out, dmixed)
    dproj, d_conv_w, d_conv_b, d_gnorm_conv = _conv_bwd(dmixed, proj, conv_w_full, conv_b, gnorm_conv_g,
                                                        s, width, conv_cb, seq_chunk, deps=(chips_w_out[4],))
    da_pre, dpooled, d_pool_scale, d_gnorm_pool = _poolmix_bwd(dmixed, a_pre, wmix_full, pool_scale, gnorm_pool_g)
    g_wmix = _poolmix_wgrad(pooled, da_pre, gd)
    dproj = _pool_bwd(dpooled, dproj, s, gd, pool_cb, seq_chunk)

    pairs_w_in = reduce_start("in_proj", h1, dproj, True)
    (dh1,) = _mm_nt("in_proj_dx", dproj, w_in_all, d, True, [sd_bf16], _store(BF16), deps=(pairs_w_in[4],))
    chips_w_in = reduce_chips("in_proj", h1, dproj, True, pairs_w_in, dh1)
    grad_x, dshift1, dscale1, d_norm1_g = _norm_mod_bwd("norm1_bwd", dh1, x2d, norm1_g, scale1, dx_mid,
                                                        deps=(chips_w_in[4],))

    rows_mix = gd // N_DEV
    g_wmix_split = jnp.transpose(g_wmix.reshape(N_POOL_GROUPS, N_DEV, rows_mix, gd), (1, 0, 2, 3))
    g_wmix_split = g_wmix_split.reshape(N_DEV, N_POOL_GROUPS * rows_mix, gd)
    loss_pad = jnp.concatenate([loss_row[:, :1], jnp.zeros((1, 127), F32)], axis=1)
    dmod = jnp.concatenate([dshift1, dscale1, dgate1, dshift2, dscale2, dgate2], axis=1)
    small = jnp.concatenate([dmod, d_norm1_g, d_pool_scale, d_conv_b, d_gnorm_pool, d_gnorm_conv, d_norm2_g,
                             d_final_g, d_conv_w.reshape(1, 3 * width), loss_pad], axis=1)
    small_started = _push_start("exchange_small_grads_start", [g_wmix_split, small], ("a2a", "gather"), me)

    sums, landed = _chip_wait("scatter_w_mlp_out_chip_wait", chips_w2, small_started[-1])
    out_w2 = _reduce_adam_chips("adam_w_mlp_out", sums, landed, w_mlp_out[0], m_w_mlp_out[0], v_w_mlp_out[0])
    sums, landed = _chip_wait("scatter_w_mlp_in_chip_wait", chips_w1, out_w2[0])
    out_w1 = _reduce_adam_chips("adam_w_mlp_in", sums, landed, w_mlp_in[0], m_w_mlp_in[0], v_w_mlp_in[0])
    sums, landed = _chip_wait("scatter_w_out_chip_wait", chips_w_out, out_w1[0])
    out_w_out = _reduce_adam_chips("adam_w_out", sums, landed, w_out[0], m_w_out[0], v_w_out[0])
    sums, landed = _chip_wait("scatter_w_in_chip_wait", chips_w_in, out_w_out[0])
    out_w_in = _reduce_adam_chips("adam_w_in", sums, landed, w_in[0], m_w_in[0], v_w_in[0])

    p_wmix, small_all = _push_wait("exchange_small_grads_wait", small_started, ("a2a", "gather"), out_w_in[0])
    mix_shape = (N_POOL_GROUPS * rows_mix, gd)
    out_wmix = _reduce_adam("adam_pool_mix", p_wmix, pool_mix_w.reshape(mix_shape), m_pool_mix_w.reshape(mix_shape),
                            v_pool_mix_w.reshape(mix_shape))
    out_wmix = [a.reshape(pool_mix_w.shape) for a in out_wmix]
    small_all = small_all.reshape(N_DEV, small.shape[1])
    small_sum = _sum_devices(small_all)

    n_rep = (N_MOD + 1) * d + 4 * width + 2 * d
    loss = small_sum[0, n_rep + 3 * width]
    rep_names_w = [b_ada, norm1_g, pool_scale, conv_b, gnorm_pool_g, gnorm_conv_g, norm2_g, final_g.reshape(1, d)]
    rep_names_m = [m_b_ada, m_norm1_g, m_pool_scale, m_conv_b, m_gnorm_pool_g, m_gnorm_conv_g, m_norm2_g,
                   m_final_g.reshape(1, d)]
    rep_names_v = [v_b_ada, v_norm1_g, v_pool_scale, v_conv_b, v_gnorm_pool_g, v_gnorm_conv_g, v_norm2_g,
                   v_final_g.reshape(1, d)]
    rep_grad = small_sum[:, :n_rep]
    rep_delta, rep_m, rep_v = _adam_small("adam_replicated", rep_grad, jnp.concatenate(rep_names_w, axis=1),
                                          jnp.concatenate(rep_names_m, axis=1), jnp.concatenate(rep_names_v, axis=1))

    def split_rep(vec):
        out, off = [], 0
        for wgt in rep_names_w:
            n = wgt.shape[1]
            out.append(vec[:, off:off + n])
            off += n
        out[-1] = out[-1].reshape(d)
        return out

    conv_grad_full = small_sum[:, n_rep:n_rep + 3 * width].reshape(3, width)
    g_conv_w = lax.dynamic_slice(conv_grad_full, (0, me * conv_cols), (3, conv_cols))
    g_conv_w8 = jnp.concatenate([g_conv_w, jnp.zeros((5, conv_cols), F32)], axis=0)

    def pad8(a):
        return jnp.concatenate([a[0], jnp.zeros((5, conv_cols), F32)], axis=0)

    conv_delta, conv_m, conv_v = _adam_small("adam_conv_w", g_conv_w8, pad8(conv_w), pad8(m_conv_w), pad8(v_conv_w))

    dmod_all = small_all[:, :N_MOD * d]
    dmod_mine = lax.dynamic_slice(dmod_all, (0, me * ada_cols), (N_DEV, ada_cols))
    dmod_rows = jnp.concatenate([dmod_mine, jnp.zeros((N_DEV, ada_cols), F32)], axis=0)
    out_ada = _ada_bwd_adam(jnp.transpose(c_rows), dmod_rows, w_ada[0], m_w_ada[0], v_w_ada[0])

    rep_all = [split_rep(rep_grad), split_rep(rep_delta), split_rep(rep_m), split_rep(rep_v)]
    conv_all = [g_conv_w[None], conv_delta[None, :3], conv_m[None, :3], conv_v[None, :3]]
    outs = [loss, grad_x[None]]
    for kind in range(4):
        b_ada_o, norm1_o, pool_scale_o, conv_b_o, gpool_o, gconv_o, norm2_o, final_o = rep_all[kind]
        outs += [out_ada[kind][None], b_ada_o, norm1_o, out_w_in[kind][None], out_wmix[kind], pool_scale_o,
                 conv_all[kind], conv_b_o, gpool_o, gconv_o, out_w_out[kind][None], norm2_o, out_w1[kind][None],
                 out_w2[kind][None], final_o]
    return tuple(outs)
```

```python
import jax
import jax.numpy as jnp
from jax import lax
from jax.experimental import pallas as pl
from jax.experimental.pallas import tpu as pltpu

F32 = jnp.float32
BF16 = jnp.bfloat16
MESH = pl.DeviceIdType.MESH

N_DEV = 8
N_MOD = 6
EPS = 1e-6
POOL_WINDOWS = (2, 4, 8, 16)
N_POOL_GROUPS = len(POOL_WINDOWS)
CONV_HEAD_DIM = 128
PAD_ROWS = 16

ADAM_LR = 0.001
ADAM_B1 = 0.9
ADAM_B2 = 0.999
ADAM_EPS = 1e-08
ADAM_WD = 0.01
ADAM_STEP = 10

VMEM_LIMIT_BYTES = 56 * 1024 * 1024
MM_TM, MM_TN, MM_TK = 1024, 512, 4096


def _pallas(body, deps=(), **kw):
    if not deps:
        return pl.pallas_call(body, **kw)
    n_in = len(kw["in_specs"])

    def with_deps(*refs):
        body(*refs[:n_in], *refs[n_in + len(deps):])

    kw["in_specs"] = list(kw["in_specs"]) + [pl.BlockSpec(memory_space=pl.ANY)] * len(deps)
    call = pl.pallas_call(with_deps, **kw)
    return lambda *operands: call(*operands, *deps)


def _params(*sem):
    return pltpu.CompilerParams(dimension_semantics=sem, vmem_limit_bytes=VMEM_LIMIT_BYTES)


def _tile(pref, dim):
    if dim <= pref:
        return dim
    for t in range(pref - pref % 128, 0, -128):
        if dim % t == 0:
            return t
    return dim


def _exchange(name, arrays, modes, deps=()):
    n = len(arrays)
    out_shape = []
    for a, mode in zip(arrays, modes):
        piece = a.shape if mode == "gather" else a.shape[1:]
        out_shape.append(jax.ShapeDtypeStruct((N_DEV,) + tuple(piece), a.dtype))

    def body(*refs):
        srcs, dsts = refs[:n], refs[n:2 * n]
        send_sems, recv_sems, local_sems = refs[2 * n:]
        x, y, c = lax.axis_index("x"), lax.axis_index("y"), lax.axis_index("c")
        me = 4 * x + 2 * y + c
        copies = []
        for i in range(n):
            gather = modes[i] == "gather"
            local = pltpu.make_async_copy(srcs[i] if gather else srcs[i].at[me], dsts[i].at[me], local_sems.at[i])
            local.start()
            copies.append(local)
            for k in range(1, N_DEV):
                kx, ky, kc = (k >> 2) & 1, (k >> 1) & 1, k & 1
                peer = (1 - x if kx else x, 1 - y if ky else y, 1 - c if kc else c)
                peer_idx = 4 * peer[0] + 2 * peer[1] + peer[2]
                remote = pltpu.make_async_remote_copy(
                    src_ref=srcs[i] if gather else srcs[i].at[peer_idx],
                    dst_ref=dsts[i].at[me],
                    send_sem=send_sems.at[i * (N_DEV - 1) + k - 1],
                    recv_sem=recv_sems.at[i * (N_DEV - 1) + k - 1],
                    device_id=peer, device_id_type=MESH)
                remote.start()
                copies.append(remote)
        for cp in copies:
            cp.wait()

    any_spec = pl.BlockSpec(memory_space=pl.ANY)
    return _pallas(
        body, deps, name=name, out_shape=out_shape,
        in_specs=[any_spec] * n, out_specs=[any_spec] * n,
        scratch_shapes=[pltpu.SemaphoreType.DMA((n * (N_DEV - 1),)),
                        pltpu.SemaphoreType.DMA((n * (N_DEV - 1),)),
                        pltpu.SemaphoreType.DMA((n,))],
    )(*arrays)


_HBM = pl.BlockSpec(memory_space=pltpu.HBM)
_SEM = pl.BlockSpec(memory_space=pltpu.SEMAPHORE)
_TOKEN = pl.BlockSpec(memory_space=pltpu.VMEM)
_EFFECT = pltpu.SideEffectType.DATAFLOW_SIDE_EFFECTING
N_CHIP = N_DEV // 2
_OTHER_CHIPS = (1, 2, 3)


def _place():
    x, y, c = lax.axis_index("x"), lax.axis_index("y"), lax.axis_index("c")
    return x, y, c, (x, y, 1 - c)


def _same_core_of(x, y, c, k):
    px = 1 - x if k & 2 else x
    py = 1 - y if k & 1 else y
    return (px, py, c), 2 * px + py


def _remote(src, dst, send_sem, recv_sem, device):
    return pltpu.make_async_remote_copy(src_ref=src, dst_ref=dst, send_sem=send_sem, recv_sem=recv_sem,
                                        device_id=device, device_id_type=MESH)


def _token_shape():
    return jax.ShapeDtypeStruct((8, 128), F32)


def _split_call(body, deps, name, operands, in_specs, out_shape, out_specs, aliases):
    return _pallas(body, deps, name=name, out_shape=out_shape, in_specs=in_specs, out_specs=out_specs,
                   input_output_aliases=aliases,
                   compiler_params=pltpu.CompilerParams(has_side_effects=_EFFECT))(*operands)


def _routes(x, y, c):
    first = (x + c - 2 * x * c, y + (1 - c) - 2 * y * (1 - c), c)
    second = (x + (1 - c) - 2 * x * (1 - c), y + c - 2 * y * c, c)
    return first, second, (1 - x, 1 - y, c)


def _index_of(device):
    return 4 * device[0] + 2 * device[1] + device[2]


def _gather_start(name, land, deps):
    def body(land_ref, send_sems, recv_first, recv_second, recv_d2d, land_thru, token):
        del land_thru
        x, y, c, sibling = _place()
        first, second, _ = _routes(x, y, c)
        mine = land_ref.at[4 * x + 2 * y + c]
        _remote(mine, mine, send_sems.at[0], recv_first.at[0], first).start()
        _remote(mine, mine, send_sems.at[1], recv_second.at[0], second).start()
        _remote(mine, mine, send_sems.at[2], recv_d2d.at[0], sibling).start()
        token[...] = jnp.zeros_like(token)

    one = pltpu.SemaphoreType.DMA((1,))
    return _split_call(
        body, deps, name, (pltpu.with_memory_space_constraint(land, pltpu.HBM),), (_HBM,),
        (pltpu.SemaphoreType.DMA((3,)), one, one, one, pltpu.HBM(land.shape, land.dtype), _token_shape()),
        (_SEM, _SEM, _SEM, _SEM, _HBM, _TOKEN), {0: 4})


def _gather_wait_local(name, started, land, after):
    recv_d2d = started[3]

    def body(land_ref, recv_d2d, after_ref, land_out):
        del after_ref, land_out
        x, y, c, sibling = _place()
        mine = land_ref.at[4 * x + 2 * y + c]
        _remote(mine, mine, recv_d2d.at[0], recv_d2d.at[0], sibling).wait_recv()

    return _split_call(
        body, (), name, (land, recv_d2d, after), (_HBM, _SEM, pl.BlockSpec(memory_space=pl.ANY)),
        (pltpu.HBM(land.shape, land.dtype),), (_HBM,), {0: 0})[0]


def _gather_hop(name, started, land, after):
    recv_first = started[1]

    def body(land_ref, recv_first, after_ref, send_hop, recv_hop, land_thru, token):
        del after_ref, land_thru
        x, y, c, _ = _place()
        first, second, _ = _routes(x, y, c)
        piece = land_ref.at[_index_of(first)]
        _remote(piece, piece, send_hop.at[0], recv_first.at[0], first).wait_recv()
        _remote(piece, piece, send_hop.at[0], recv_hop.at[0], second).start()
        token[...] = jnp.zeros_like(token)

    one = pltpu.SemaphoreType.DMA((1,))
    return _split_call(
        body, (), name, (land, recv_first, after), (_HBM, _SEM, pl.BlockSpec(memory_space=pl.ANY)),
        (one, one, pltpu.HBM(land.shape, land.dtype), _token_shape()), (_SEM, _SEM, _HBM, _TOKEN), {0: 2})


def _gather_relay(name, started, hopped, after):
    recv_second = started[2]
    _, recv_hop, land, _ = hopped

    def body(land_ref, recv_second, recv_hop, after_ref, send_fwd, recv_fwd, land_thru, token):
        del after_ref, land_thru
        token[...] = jnp.zeros_like(token)
        x, y, c, sibling = _place()
        mine = land_ref.at[4 * x + 2 * y + c]
        _remote(mine, mine, send_fwd.at[0], recv_second.at[0], sibling).wait_recv()
        _remote(mine, mine, send_fwd.at[0], recv_hop.at[0], sibling).wait_recv()
        for i, device in enumerate(_routes(x, y, c)):
            piece = land_ref.at[_index_of(device)]
            _remote(piece, piece, send_fwd.at[i], recv_fwd.at[i], sibling).start()

    return _split_call(
        body, (), name, (land, recv_second, recv_hop, after), (_HBM, _SEM, _SEM, pl.BlockSpec(memory_space=pl.ANY)),
        (pltpu.SemaphoreType.DMA((3,)), pltpu.SemaphoreType.DMA((3,)), pltpu.HBM(land.shape, land.dtype),
         _token_shape()),
        (_SEM, _SEM, _HBM, _TOKEN), {0: 2})


def _gather_wait(name, started, hopped, relayed, land, after, local_waited):
    send_sems, recv_d2d = started[0], started[3]
    send_hop = hopped[0]
    send_fwd, recv_fwd = relayed[0], relayed[1]

    def body(land_ref, send_sems, recv_d2d, send_hop, send_fwd, recv_fwd, after_ref, land_out):
        del after_ref, land_out
        x, y, c, sibling = _place()
        mine = land_ref.at[4 * x + 2 * y + c]
        for i in range(3):
            _remote(mine, mine, send_sems.at[i], recv_d2d.at[0], sibling).wait_send()
        _remote(mine, mine, send_hop.at[0], recv_d2d.at[0], sibling).wait_send()
        if not local_waited:
            _remote(mine, mine, send_sems.at[2], recv_d2d.at[0], sibling).wait_recv()
        for i in range(3):
            relay = _remote(mine, mine, send_fwd.at[i], recv_fwd.at[i], sibling)
            relay.wait_send()
            relay.wait_recv()

    return _split_call(
        body, (), name, (land, send_sems, recv_d2d, send_hop, send_fwd, recv_fwd, after),
        (_HBM, _SEM, _SEM, _SEM, _SEM, _SEM, pl.BlockSpec(memory_space=pl.ANY)),
        (pltpu.HBM(land.shape, land.dtype),), (_HBM,), {0: 0})[0]


def _push_start(name, srcs, modes, me, deps=()):
    n = len(srcs)
    lands = [_landing(src if mode == "gather" else lax.dynamic_index_in_dim(src, me, 0, keepdims=False), me)
             for src, mode in zip(srcs, modes)]

    def body(*refs):
        src_refs, land_refs, send_sems, recv_sems, token = refs[:n], refs[n:2 * n], refs[2 * n], refs[2 * n + 1], refs[-1]
        for cp in _push_copies(src_refs, land_refs, send_sems, recv_sems, modes):
            cp.start()
        token[...] = jnp.zeros_like(token)

    count = pltpu.SemaphoreType.DMA((n * (N_DEV - 1),))
    operands = [pltpu.with_memory_space_constraint(a, pltpu.HBM) for a in list(srcs) + lands]
    return _split_call(
        body, deps, name, operands, (_HBM,) * (2 * n),
        (count, count) + tuple(pltpu.HBM(a.shape, a.dtype) for a in operands) + (_token_shape(),),
        (_SEM, _SEM) + (_HBM,) * (2 * n) + (_TOKEN,), {i: 2 + i for i in range(2 * n)})


def _push_wait(name, started, modes, after):
    n = len(modes)
    send_sems, recv_sems = started[0], started[1]
    arrays = started[2:2 + 2 * n]

    def body(*refs):
        src_refs, land_refs, send_sems, recv_sems = refs[:n], refs[n:2 * n], refs[2 * n], refs[2 * n + 1]
        for cp in _push_copies(src_refs, land_refs, send_sems, recv_sems, modes):
            cp.wait_send()
            cp.wait_recv()

    return _split_call(
        body, (), name, tuple(arrays) + (send_sems, recv_sems, after),
        (_HBM,) * (2 * n) + (_SEM, _SEM, pl.BlockSpec(memory_space=pl.ANY)),
        tuple(pltpu.HBM(a.shape, a.dtype) for a in arrays), (_HBM,) * (2 * n), {i: i for i in range(2 * n)})[n:]


def _push_copies(src_refs, land_refs, send_sems, recv_sems, modes):
    x, y, c, _ = _place()
    me = 4 * x + 2 * y + c
    copies = []
    for i, mode in enumerate(modes):
        for k in range(1, N_DEV):
            peer = (1 - x if k & 4 else x, 1 - y if k & 2 else y, 1 - c if k & 1 else c)
            src = src_refs[i] if mode == "gather" else src_refs[i].at[_index_of(peer)]
            sem = i * (N_DEV - 1) + k - 1
            copies.append(_remote(src, land_refs[i].at[me], send_sems.at[sem], recv_sems.at[sem], peer))
    return copies


def _landing(own, me):
    land = lax.empty((N_DEV,) + own.shape, own.dtype)
    return lax.dynamic_update_slice(land, own[None], (me,) + (0,) * own.ndim)


def _pair_start(name, far, deps=()):
    pair = lax.empty(far.shape, far.dtype)

    def body(far_ref, pair_ref, send_sems, recv_sems, far_thru, pair_thru, token):
        del far_thru, pair_thru
        _remote(far_ref, pair_ref, send_sems.at[0], recv_sems.at[0], _place()[3]).start()
        token[...] = jnp.zeros_like(token)

    return _split_call(
        body, deps, name,
        (pltpu.with_memory_space_constraint(far, pltpu.HBM), pltpu.with_memory_space_constraint(pair, pltpu.HBM)),
        (_HBM, _HBM),
        (pltpu.SemaphoreType.DMA((1,)), pltpu.SemaphoreType.DMA((1,)),
         pltpu.HBM(far.shape, far.dtype), pltpu.HBM(pair.shape, pair.dtype), _token_shape()),
        (_SEM, _SEM, _HBM, _HBM, _TOKEN), {0: 2, 1: 3})


def _pair_wait(name, started, after):
    send_sems, recv_sems, far, pair, _ = started

    def body(far_ref, pair_ref, send_sems, recv_sems, after_ref, far_out, pair_out):
        del after_ref, far_out, pair_out
        cp = _remote(far_ref, pair_ref, send_sems.at[0], recv_sems.at[0], _place()[3])
        cp.wait_send()
        cp.wait_recv()

    return _split_call(
        body, (), name, (far, pair, send_sems, recv_sems, after),
        (_HBM, _HBM, _SEM, _SEM, pl.BlockSpec(memory_space=pl.ANY)),
        (pltpu.HBM(far.shape, far.dtype), pltpu.HBM(pair.shape, pair.dtype)), (_HBM, _HBM), {0: 0, 1: 1})[1]


def _chip_start(name, sums, deps=()):
    land = lax.empty((3,) + sums.shape[1:], sums.dtype)

    def body(s_ref, land_ref, send_sems, recv_sems, s_thru, land_thru, token):
        del s_thru, land_thru
        x, y, c, _ = _place()
        for k in _OTHER_CHIPS:
            peer, chip = _same_core_of(x, y, c, k)
            _remote(s_ref.at[chip], land_ref.at[k - 1], send_sems.at[k - 1], recv_sems.at[k - 1], peer).start()
        token[...] = jnp.zeros_like(token)

    return _split_call(
        body, deps, name,
        (pltpu.with_memory_space_constraint(sums, pltpu.HBM), pltpu.with_memory_space_constraint(land, pltpu.HBM)),
        (_HBM, _HBM),
        (pltpu.SemaphoreType.DMA((3,)), pltpu.SemaphoreType.DMA((3,)),
         pltpu.HBM(sums.shape, sums.dtype), pltpu.HBM(land.shape, land.dtype), _token_shape()),
        (_SEM, _SEM, _HBM, _HBM, _TOKEN), {0: 2, 1: 3})


def _chip_wait(name, started, after):
    send_sems, recv_sems, sums, land, _ = started

    def body(s_ref, land_ref, send_sems, recv_sems, after_ref, s_out, land_out):
        del after_ref, s_out, land_out
        x, y, c, _ = _place()
        for k in _OTHER_CHIPS:
            peer, chip = _same_core_of(x, y, c, k)
            cp = _remote(s_ref.at[chip], land_ref.at[k - 1], send_sems.at[k - 1], recv_sems.at[k - 1], peer)
            cp.wait_send()
            cp.wait_recv()

    return _split_call(
        body, (), name, (sums, land, send_sems, recv_sems, after),
        (_HBM, _HBM, _SEM, _SEM, pl.BlockSpec(memory_space=pl.ANY)),
        (pltpu.HBM(sums.shape, sums.dtype), pltpu.HBM(land.shape, land.dtype)), (_HBM, _HBM), {0: 0, 1: 1})


_DOT_DIMS = {"nn": (((1,), (0,)), ((), ())), "nt": (((1,), (1,)), ((), ())), "tn": (((0,), (0,)), ((), ()))}


def _matmul(name, mode, operands, in_specs, out_shape, out_specs, grid, acc_shape, epilogue, deps=(), carry=()):
    n_in, n_out, nk = len(operands), len(out_shape), grid[2]
    dims = _DOT_DIMS[mode]

    def body(*refs):
        a_ref, b_ref = refs[0], refs[1]
        extras, outs = refs[2:n_in], refs[n_in:n_in + n_out]
        b_val = b_ref[...]
        if b_val.ndim == 3:
            b_val = jnp.concatenate([b_val[g] for g in range(b_val.shape[0])], axis=1)
        part = lax.dot_general(a_ref[...], b_val, dims, preferred_element_type=F32)
        if nk == 1:
            epilogue(part, extras, outs)
            return
        acc = refs[-1]
        k = pl.program_id(2)

        @pl.when(k == 0)
        def _():
            acc[...] = part

        @pl.when(jnp.logical_and(k > 0, k < nk - 1))
        def _():
            acc[...] += part

        @pl.when(k == nk - 1)
        def _():
            epilogue(acc[...] + part, extras, outs)

    aliases = {n_in + len(deps) + i: i for i in range(len(carry))}
    return _pallas(body, tuple(deps) + tuple(carry), name=name, grid=grid, in_specs=in_specs, out_specs=out_specs,
                   out_shape=out_shape, input_output_aliases=aliases,
                   scratch_shapes=[pltpu.VMEM(acc_shape, F32)] if nk > 1 else [],
                   compiler_params=_params("parallel", "parallel", "arbitrary"))(*operands)


def _store(dtype):
    def epilogue(acc, extras, outs):
        outs[0][...] = acc.astype(dtype)
    return epilogue


def _residual_epilogue(acc, extras, outs):
    x_ref, gate_ref = extras
    outs[0][...] = acc.astype(outs[0].dtype)
    outs[1][...] = x_ref[...] + gate_ref[...] * acc


def _relu2_epilogue(acc, extras, outs):
    r = jnp.maximum(acc, 0.0)
    outs[0][...] = r.astype(outs[0].dtype)
    outs[1][...] = (r * r).astype(outs[1].dtype)


def _relu2_bwd_epilogue(acc, extras, outs):
    outs[0][...] = (acc * (2.0 * extras[0][...].astype(F32))).astype(outs[0].dtype)


def _no_extra_specs(tm, tn):
    return []


def _mm_nn(name, a, b, n_total, b_split, out_shape, epilogue, extras=(), extra_specs=_no_extra_specs, tm=MM_TM, tn=MM_TN, tk=MM_TK,
           deps=(), pieces=None, carry=()):
    m, kdim = a.shape
    tm, tk = _tile(tm, m), _tile(tk, kdim)
    n_blocks = None
    if b_split:
        piece = b.shape[2]
        tn = _tile(tn, piece)
        per = piece // tn
        if pieces is None:
            b_spec = pl.BlockSpec((None, tk, tn), lambda i, j, k: (j // per, k, j % per))
            out_spec = pl.BlockSpec((tm, tn), lambda i, j, k: (i, j))
        else:
            piece_of, count = pieces
            n_blocks = count * per

            def which(j):
                return piece_of(j // per)

            b_spec = pl.BlockSpec((None, tk, tn), lambda i, j, k: (which(j), k, j % per))
            out_spec = pl.BlockSpec((tm, tn), lambda i, j, k: (i, which(j) * per + j % per))
    else:
        tn = _tile(tn, n_total)
        b_spec = pl.BlockSpec((tk, tn), lambda i, j, k: (k, j))
        out_spec = pl.BlockSpec((tm, tn), lambda i, j, k: (i, j))
    if n_blocks is None:
        n_blocks = n_total // tn
    in_specs = [pl.BlockSpec((tm, tk), lambda i, j, k: (i, k)), b_spec] + list(extra_specs(tm, tn))
    return _matmul(name, "nn", (a, b) + tuple(extras), in_specs, out_shape, [out_spec] * len(out_shape),
                   (m // tm, n_blocks, kdim // tk), (tm, tn), epilogue, deps, carry)


def _mm_nt(name, a, b, n_total, b_split, out_shape, epilogue, extras=(), extra_specs=_no_extra_specs, tm=MM_TM, tn=MM_TN, tk=MM_TK,
           deps=()):
    m, kdim = a.shape
    tm, tn = _tile(tm, m), _tile(tn, n_total)
    if b_split and tk >= 2 * b.shape[2]:
        piece = b.shape[2]
        group = _tile(tk, kdim) // piece
        tk = group * piece
        b_spec = pl.BlockSpec((group, tn, piece), lambda i, j, k: (k, j, 0))
    elif b_split:
        piece = b.shape[2]
        tk = _tile(tk, piece)
        per = piece // tk
        b_spec = pl.BlockSpec((None, tn, tk), lambda i, j, k: (k // per, j, k % per))
    else:
        tk = _tile(tk, kdim)
        b_spec = pl.BlockSpec((tn, tk), lambda i, j, k: (j, k))
    in_specs = [pl.BlockSpec((tm, tk), lambda i, j, k: (i, k)), b_spec] + list(extra_specs(tm, tn))
    out_specs = [pl.BlockSpec((tm, tn), lambda i, j, k: (i, j)) for _ in out_shape]
    return _matmul(name, "nt", (a, b) + tuple(extras), in_specs, out_shape, out_specs,
                   (m // tm, n_total // tn, kdim // tk), (tm, tn), epilogue, deps)


def _add_pair_epilogue(acc, extras, outs):
    outs[0][...] = (acc + extras[0][...].astype(F32)).astype(outs[0].dtype)


def _mm_tn_half(name, a, b, col_pieces, near, pair=None, tm=MM_TM, tn=MM_TN, tk=MM_TK, deps=()):
    kdim, m = a.shape
    n_total = b.shape[1]
    tk = _tile(tk, kdim)

    def core():
        c = lax.axis_index("c")
        return c if near else 1 - c

    if col_pieces:
        piece = n_total // N_DEV
        tm, tn = _tile(tm, m), _tile(tn, piece)
        per = piece // tn
        grid = (m // tm, N_CHIP * per, kdim // tk)
        a_spec = pl.BlockSpec((tk, tm), lambda i, j, k: (k, i))
        b_spec = pl.BlockSpec((tk, tn), lambda i, j, k: (k, (2 * (j // per) + core()) * per + j % per))
        out_spec = pl.BlockSpec((None, tm, tn), lambda i, j, k: (j // per, i, j % per))
        out_shape = [jax.ShapeDtypeStruct((N_CHIP, m, piece), BF16)]
    else:
        piece = m // N_DEV
        tm, tn = _tile(tm, piece), _tile(tn, n_total)
        per = piece // tm
        grid = (N_CHIP * per, n_total // tn, kdim // tk)
        a_spec = pl.BlockSpec((tk, tm), lambda i, j, k: (k, (2 * (i // per) + core()) * per + i % per))
        b_spec = pl.BlockSpec((tk, tn), lambda i, j, k: (k, j))
        out_spec = pl.BlockSpec((None, tm, tn), lambda i, j, k: (i // per, i % per, j))
        out_shape = [jax.ShapeDtypeStruct((N_CHIP, piece, n_total), BF16)]
    operands, in_specs, epilogue = (a, b), [a_spec, b_spec], _store(BF16)
    if pair is not None:
        operands, in_specs, epilogue = (a, b, pair), [a_spec, b_spec, out_spec], _add_pair_epilogue
    return _matmul(name, "tn", operands, in_specs, out_shape, [out_spec], grid, (tm, tn), epilogue, deps)[0]


def _rms(xv):
    return lax.rsqrt(jnp.mean(xv * xv, axis=-1, keepdims=True) + EPS)


def _colsum(v):
    return jnp.sum(v, axis=0, keepdims=True)


def _norm_mod(name, x, g, scale, shift, tr=256, deps=()):
    s, d = x.shape
    tr = _tile(tr, s)

    def body(x_ref, g_ref, sc_ref, sh_ref, h_ref):
        xv = x_ref[...]
        h = (xv * _rms(xv)) * g_ref[...]
        h_ref[...] = (h * (1.0 + sc_ref[...]) + sh_ref[...]).astype(h_ref.dtype)

    row = pl.BlockSpec((tr, d), lambda i: (i, 0))
    vec = pl.BlockSpec((1, d), lambda i: (0, 0))
    return _pallas(body, deps, name=name, grid=(s // tr,), in_specs=[row, vec, vec, vec], out_specs=row,
                   out_shape=jax.ShapeDtypeStruct((s, d), BF16), compiler_params=_params("parallel"))(x, g, scale, shift)


def _loss_head(x3, target, gf, gate2, mlp, tr=128):
    s, d = x3.shape
    tr = _tile(tr, s)

    def body(x_ref, t_ref, gf_ref, gate_ref, mlp_ref, dx_ref, dbr_ref, dgf_ref, dgate_ref, loss_ref):
        @pl.when(pl.program_id(0) == 0)
        def _():
            dgf_ref[...] = jnp.zeros_like(dgf_ref)
            dgate_ref[...] = jnp.zeros_like(dgate_ref)
            loss_ref[...] = jnp.zeros_like(loss_ref)

        xv = x_ref[...]
        r = _rms(xv)
        xn = xv * r
        gfv = gf_ref[...]
        err = xn * gfv - t_ref[...]
        loss_ref[...] += 0.5 * _colsum(jnp.mean(err * err, axis=-1, keepdims=True))
        dy = err * (1.0 / d)
        dgf_ref[...] += _colsum(dy * xn)
        dxn = dy * gfv
        dx = r * (dxn - xn * jnp.mean(dxn * xn, axis=-1, keepdims=True))
        dx_ref[...] = dx
        dbr_ref[...] = (dx * gate_ref[...]).astype(dbr_ref.dtype)
        dgate_ref[...] += _colsum(dx * mlp_ref[...].astype(F32))

    row = pl.BlockSpec((tr, d), lambda i: (i, 0))
    vec = pl.BlockSpec((1, d), lambda i: (0, 0))
    return _pallas(
        body, name="loss_head", grid=(s // tr,), in_specs=[row, row, vec, vec, row],
        out_specs=[row, row, vec, vec, pl.BlockSpec((1, 128), lambda i: (0, 0))],
        out_shape=[jax.ShapeDtypeStruct((s, d), F32), jax.ShapeDtypeStruct((s, d), BF16),
                   jax.ShapeDtypeStruct((1, d), F32), jax.ShapeDtypeStruct((1, d), F32),
                   jax.ShapeDtypeStruct((1, 128), F32)],
        compiler_params=_params("arbitrary"))(x3, target, gf, gate2, mlp)


def _norm_mod_bwd(name, dh, xin, g, scale, dx_up, branch=None, gate=None, tr=128, deps=()):
    s, d = xin.shape
    tr = _tile(tr, s)
    with_gate = branch is not None

    def body(*refs):
        dh_ref, x_ref, g_ref, sc_ref, up_ref = refs[:5]
        if with_gate:
            br_ref, gate_ref = refs[5:7]
            dx_ref, dsh_ref, dsc_ref, dg_ref, dbr_ref, dgate_ref = refs[7:]
            sums = (dsh_ref, dsc_ref, dg_ref, dgate_ref)
        else:
            dx_ref, dsh_ref, dsc_ref, dg_ref = refs[5:]
            sums = (dsh_ref, dsc_ref, dg_ref)

        @pl.when(pl.program_id(0) == 0)
        def _():
            for ref in sums:
                ref[...] = jnp.zeros_like(ref)

        xv, dhv, gv = x_ref[...], dh_ref[...].astype(F32), g_ref[...]
        r = _rms(xv)
        xn = xv * r
        one_sc = 1.0 + sc_ref[...]
        dsh_ref[...] += _colsum(dhv)
        dsc_ref[...] += _colsum(dhv * (xn * gv))
        dg_ref[...] += _colsum(dhv * one_sc * xn)
        dxn = dhv * one_sc * gv
        dx = up_ref[...] + r * (dxn - xn * jnp.mean(dxn * xn, axis=-1, keepdims=True))
        dx_ref[...] = dx
        if with_gate:
            dbr_ref[...] = (dx * gate_ref[...]).astype(dbr_ref.dtype)
            dgate_ref[...] += _colsum(dx * br_ref[...].astype(F32))

    row = pl.BlockSpec((tr, d), lambda i: (i, 0))
    vec = pl.BlockSpec((1, d), lambda i: (0, 0))
    vshape = jax.ShapeDtypeStruct((1, d), F32)
    operands = [dh, xin, g, scale, dx_up]
    in_specs = [row, row, vec, vec, row]
    out_shape = [jax.ShapeDtypeStruct((s, d), F32), vshape, vshape, vshape]
    out_specs = [row, vec, vec, vec]
    if with_gate:
        operands += [branch, gate]
        in_specs += [row, vec]
        out_shape += [jax.ShapeDtypeStruct((s, d), BF16), vshape]
        out_specs += [row, vec]
    return _pallas(body, deps, name=name, grid=(s // tr,), in_specs=in_specs, out_specs=out_specs, out_shape=out_shape,
                   compiler_params=_params("arbitrary"))(*operands)


def _window_count(c0, rows, half, s):
    t = c0 + lax.broadcasted_iota(jnp.int32, (rows, 1), 0)
    return (jnp.minimum(t + half, s) - jnp.maximum(t - half, 0)).astype(F32)


def _zero_pads(pad, s):
    zeros = jnp.zeros((PAD_ROWS, pad.shape[1]), pad.dtype)
    pad[0:PAD_ROWS, :] = zeros
    pad[PAD_ROWS + s:PAD_ROWS + s + PAD_ROWS, :] = zeros


def _pool_fwd(proj, s, gd, cb, ch, deps=()):
    nsub = gd // cb

    def body(v_ref, o_ref, pad):
        g = pl.program_id(0)
        _zero_pads(pad, s)
        pad[PAD_ROWS:PAD_ROWS + s, :] = v_ref[...].astype(F32)
        for gi, window in enumerate(POOL_WINDOWS):
            half = window // 2

            @pl.when(g == gi)
            def _(half=half):
                for c0 in range(0, s, ch):
                    base = PAD_ROWS + c0
                    acc = pad[base - half:base - half + ch, :]
                    for j in range(-half + 1, half):
                        acc = acc + pad[base + j:base + j + ch, :]
                    out = acc / _window_count(c0, ch, half, s) - pad[base:base + ch, :]
                    o_ref[c0:c0 + ch, :] = out.astype(o_ref.dtype)

    spec = pl.BlockSpec((s, cb), lambda g, j: (0, g * nsub + j))
    return _pallas(body, deps, name="pool_fwd", grid=(N_POOL_GROUPS, nsub), in_specs=[spec], out_specs=spec,
                   out_shape=jax.ShapeDtypeStruct((s, N_POOL_GROUPS * gd), BF16),
                   scratch_shapes=[pltpu.VMEM((s + 2 * PAD_ROWS, cb), F32)],
                   compiler_params=_params("parallel", "parallel"))(proj)


def _pool_bwd(dpooled, dproj, s, gd, cb, ch):
    nsub = gd // cb

    def body(dp_ref, dproj_in, o_ref, pad):
        del dproj_in
        g = pl.program_id(0)
        _zero_pads(pad, s)
        for gi, window in enumerate(POOL_WINDOWS):
            half = window // 2

            @pl.when(g == gi)
            def _(half=half):
                for c0 in range(0, s, ch):
                    pad[PAD_ROWS + c0:PAD_ROWS + c0 + ch, :] = dp_ref[c0:c0 + ch, :] / _window_count(c0, ch, half, s)
                for c0 in range(0, s, ch):
                    base = PAD_ROWS + c0
                    acc = pad[base - half + 1:base - half + 1 + ch, :]
                    for j in range(-half + 2, half + 1):
                        acc = acc + pad[base + j:base + j + ch, :]
                    o_ref[c0:c0 + ch, :] = (acc - dp_ref[c0:c0 + ch, :]).astype(o_ref.dtype)

    spec = pl.BlockSpec((s, cb), lambda g, j: (0, g * nsub + j))
    return _pallas(body, name="pool_bwd", grid=(N_POOL_GROUPS, nsub),
                   in_specs=[spec, pl.BlockSpec(memory_space=pl.ANY)], out_specs=spec,
                   out_shape=jax.ShapeDtypeStruct(dproj.shape, dproj.dtype), input_output_aliases={1: 0},
                   scratch_shapes=[pltpu.VMEM((s + 2 * PAD_ROWS, cb), F32)],
                   compiler_params=_params("parallel", "parallel"))(dpooled, dproj)


def _poolmix_fwd(pooled, wmix, pool_scale, gnorm_g, d_model, tm=512):
    s = pooled.shape[0]
    gd = wmix.shape[1]
    tm = _tile(tm, s)

    def body(p_ref, w_ref, ps_ref, g_ref, apre_ref, mixed_ref):
        a_pre = jnp.dot(p_ref[...], w_ref[...], preferred_element_type=F32)
        apre_ref[...] = a_pre
        a_out = a_pre * ps_ref[...]
        mixed_ref[...] = ((a_out * _rms(a_out)) * g_ref[...]).astype(mixed_ref.dtype)

    blk = pl.BlockSpec((tm, gd), lambda g, i: (i, g))
    vec = pl.BlockSpec((1, gd), lambda g, i: (0, g))
    return _pallas(body, name="poolmix_fwd", grid=(N_POOL_GROUPS, s // tm),
                   in_specs=[blk, pl.BlockSpec((None, gd, gd), lambda g, i: (g, 0, 0)), vec, vec],
                   out_specs=[blk, blk],
                   out_shape=[jax.ShapeDtypeStruct((s, N_POOL_GROUPS * gd), F32), jax.ShapeDtypeStruct((s, d_model), BF16)],
                   compiler_params=_params("parallel", "parallel"))(pooled, wmix, pool_scale, gnorm_g)


def _poolmix_bwd(dmixed, a_pre, wmix, pool_scale, gnorm_g, tm=512):
    s = a_pre.shape[0]
    gd = wmix.shape[1]
    tm = _tile(tm, s)

    def body(dm_ref, apre_ref, w_ref, ps_ref, g_ref, dapre_ref, dpooled_ref, dps_ref, dg_ref):
        @pl.when(pl.program_id(1) == 0)
        def _():
            dps_ref[...] = jnp.zeros_like(dps_ref)
            dg_ref[...] = jnp.zeros_like(dg_ref)

        a_pre, dm, ps = apre_ref[...], dm_ref[...].astype(F32), ps_ref[...]
        a_out = a_pre * ps
        r = _rms(a_out)
        n = a_out * r
        dg_ref[...] += _colsum(dm * n)
        dn = dm * g_ref[...]
        da_out = r * (dn - n * jnp.mean(dn * n, axis=-1, keepdims=True))
        dps_ref[...] += _colsum(da_out * a_pre)
        da_pre = (da_out * ps).astype(BF16)
        dapre_ref[...] = da_pre
        dpooled_ref[...] = lax.dot_general(da_pre, w_ref[...], _DOT_DIMS["nt"], preferred_element_type=F32)

    blk = pl.BlockSpec((tm, gd), lambda g, i: (i, g))
    vec = pl.BlockSpec((1, gd), lambda g, i: (0, g))
    width = N_POOL_GROUPS * gd
    return _pallas(body, name="poolmix_bwd", grid=(N_POOL_GROUPS, s // tm),
                   in_specs=[blk, blk, pl.BlockSpec((None, gd, gd), lambda g, i: (g, 0, 0)), vec, vec],
                   out_specs=[blk, blk, vec, vec],
                   out_shape=[jax.ShapeDtypeStruct((s, width), BF16), jax.ShapeDtypeStruct((s, width), F32),
                              jax.ShapeDtypeStruct((1, width), F32), jax.ShapeDtypeStruct((1, width), F32)],
                   compiler_params=_params("parallel", "arbitrary"))(dmixed, a_pre, wmix, pool_scale, gnorm_g)


def _poolmix_wgrad(pooled, da_pre, gd, tk=1024):
    s = pooled.shape[0]
    tk = _tile(tk, s)
    nk = s // tk

    def body(p_ref, d_ref, o_ref, acc):
        k = pl.program_id(1)
        part = lax.dot_general(p_ref[...], d_ref[...], _DOT_DIMS["tn"], preferred_element_type=F32)

        @pl.when(k == 0)
        def _():
            acc[...] = part

        @pl.when(k > 0)
        def _():
            acc[...] += part

        @pl.when(k == nk - 1)
        def _():
            o_ref[...] = acc[...].astype(o_ref.dtype)

    blk = pl.BlockSpec((tk, gd), lambda g, k: (k, g))
    return _pallas(body, name="poolmix_wgrad", grid=(N_POOL_GROUPS, nk), in_specs=[blk, blk],
                   out_specs=pl.BlockSpec((None, gd, gd), lambda g, k: (g, 0, 0)),
                   out_shape=jax.ShapeDtypeStruct((N_POOL_GROUPS, gd, gd), BF16),
                   scratch_shapes=[pltpu.VMEM((gd, gd), F32)],
                   compiler_params=_params("parallel", "arbitrary"))(pooled, da_pre)


def _head_mean(v):
    parts = []
    for q in range(v.shape[1] // CONV_HEAD_DIM):
        m = jnp.mean(v[:, q * CONV_HEAD_DIM:(q + 1) * CONV_HEAD_DIM], axis=-1, keepdims=True)
        parts.append(jnp.broadcast_to(m, (v.shape[0], CONV_HEAD_DIM)))
    return parts[0] if len(parts) == 1 else jnp.concatenate(parts, axis=1)


def _conv_fwd(proj, mixed, conv_w, conv_b, gnorm_g, s, width, cb, ch, deps=()):
    nblk = width // cb

    def body(b_ref, c_ref, u_ref, w_ref, cb_ref, g_ref, mixed_in, o_ref, pad):
        del mixed_in
        _zero_pads(pad, s)
        pad[PAD_ROWS:PAD_ROWS + s, :] = c_ref[...].astype(F32) * u_ref[...].astype(F32)
        w = w_ref[...]
        for c0 in range(0, s, ch):
            base = PAD_ROWS + c0
            conv = (w[0:1] * pad[base - 1:base - 1 + ch, :] + w[1:2] * pad[base:base + ch, :]
                    + w[2:3] * pad[base + 1:base + 1 + ch, :] + cb_ref[...])
            bo = b_ref[c0:c0 + ch, :].astype(F32) * conv
            n = bo * lax.rsqrt(_head_mean(bo * bo) + EPS)
            o_ref[c0:c0 + ch, :] = (n * g_ref[...]).astype(o_ref.dtype)

    def part(p):
        return pl.BlockSpec((s, cb), lambda j: (0, p * nblk + j))

    vec = pl.BlockSpec((1, cb), lambda j: (0, j))
    return _pallas(body, deps, name="conv_fwd", grid=(nblk,),
                   in_specs=[part(1), part(2), part(3), pl.BlockSpec((3, cb), lambda j: (0, j)), vec, vec,
                             pl.BlockSpec(memory_space=pl.ANY)],
                   out_specs=part(1), out_shape=jax.ShapeDtypeStruct(mixed.shape, mixed.dtype),
                   input_output_aliases={6: 0},
                   scratch_shapes=[pltpu.VMEM((s + 2 * PAD_ROWS, cb), F32)],
                   compiler_params=_params("parallel"))(proj, proj, proj, conv_w, conv_b, gnorm_g, mixed)


def _conv_bwd(dmixed, proj, conv_w, conv_b, gnorm_g, s, width, cb, ch, deps=()):
    nblk = width // cb
    assert nblk >= 2

    def body(dm_ref, b_ref, c_ref, u_ref, w_ref, cb_ref, g_ref, dproj_ref, dw_ref, dcb_ref, dg_ref,
             pad_cu, pad_dconv, out_bufs, sems):
        j = pl.program_id(0)
        slot = j % 2

        def out_copies(step, slot_of_step):
            copies = []
            for p in range(3):
                col = pl.multiple_of((p + 1) * width + step * cb, CONV_HEAD_DIM)
                copies.append(pltpu.make_async_copy(out_bufs.at[slot_of_step, p], dproj_ref.at[:, pl.ds(col, cb)],
                                                    sems.at[slot_of_step * 3 + p]))
            return copies

        @pl.when(j >= 2)
        def _():
            for cp in out_copies(j - 2, slot):
                cp.wait()

        _zero_pads(pad_cu, s)
        _zero_pads(pad_dconv, s)
        pad_cu[PAD_ROWS:PAD_ROWS + s, :] = c_ref[...].astype(F32) * u_ref[...].astype(F32)
        w, gv = w_ref[...], g_ref[...]
        zero = jnp.zeros((1, cb), F32)
        dw0, dw1, dw2, dcb, dg = zero, zero, zero, zero, zero
        for c0 in range(0, s, ch):
            base = PAD_ROWS + c0
            cu_prev, cu_here, cu_next = (pad_cu[base - 1:base - 1 + ch, :], pad_cu[base:base + ch, :],
                                         pad_cu[base + 1:base + 1 + ch, :])
            conv = w[0:1] * cu_prev + w[1:2] * cu_here + w[2:3] * cu_next + cb_ref[...]
            bg = b_ref[c0:c0 + ch, :].astype(F32)
            bo = bg * conv
            r = lax.rsqrt(_head_mean(bo * bo) + EPS)
            n = bo * r
            dm = dm_ref[c0:c0 + ch, :].astype(F32)
            dg = dg + _colsum(dm * n)
            dn = dm * gv
            dbo = r * (dn - n * _head_mean(dn * n))
            out_bufs[slot, 0, c0:c0 + ch, :] = (dbo * conv).astype(BF16)
            dconv = dbo * bg
            pad_dconv[base:base + ch, :] = dconv
            dcb = dcb + _colsum(dconv)
            dw0 = dw0 + _colsum(dconv * cu_prev)
            dw1 = dw1 + _colsum(dconv * cu_here)
            dw2 = dw2 + _colsum(dconv * cu_next)
        dw_ref[0:1, :] = dw0
        dw_ref[1:2, :] = dw1
        dw_ref[2:3, :] = dw2
        dcb_ref[...] = dcb
        dg_ref[...] = dg
        for c0 in range(0, s, ch):
            base = PAD_ROWS + c0
            dcu = (w[0:1] * pad_dconv[base + 1:base + 1 + ch, :] + w[1:2] * pad_dconv[base:base + ch, :]
                   + w[2:3] * pad_dconv[base - 1:base - 1 + ch, :])
            out_bufs[slot, 1, c0:c0 + ch, :] = (dcu * u_ref[c0:c0 + ch, :].astype(F32)).astype(BF16)
            out_bufs[slot, 2, c0:c0 + ch, :] = (dcu * c_ref[c0:c0 + ch, :].astype(F32)).astype(BF16)
        for cp in out_copies(j, slot):
            cp.start()

        @pl.when(j == nblk - 1)
        def _():
            for cp in out_copies(j, slot) + out_copies(j - 1, 1 - slot):
                cp.wait()

    def part(p):
        return pl.BlockSpec((s, cb), lambda j: (0, p * nblk + j))

    vec = pl.BlockSpec((1, cb), lambda j: (0, j))
    w_spec = pl.BlockSpec((3, cb), lambda j: (0, j))
    return _pallas(body, deps, name="conv_bwd", grid=(nblk,),
                   in_specs=[part(1), part(1), part(2), part(3), w_spec, vec, vec],
                   out_specs=[pl.BlockSpec(memory_space=pl.ANY), w_spec, vec, vec],
                   out_shape=[jax.ShapeDtypeStruct((s, 4 * width), BF16), jax.ShapeDtypeStruct((3, width), F32),
                              jax.ShapeDtypeStruct((1, width), F32), jax.ShapeDtypeStruct((1, width), F32)],
                   scratch_shapes=[pltpu.VMEM((s + 2 * PAD_ROWS, cb), F32), pltpu.VMEM((s + 2 * PAD_ROWS, cb), F32),
                                   pltpu.VMEM((2, 3, s, cb), BF16), pltpu.SemaphoreType.DMA((6,))],
                   compiler_params=_params("arbitrary"))(dmixed, proj, proj, proj, conv_w, conv_b, gnorm_g)


def _adamw(w, g, m, v):
    m = ADAM_B1 * m + (1.0 - ADAM_B1) * g
    v = ADAM_B2 * v + (1.0 - ADAM_B2) * (g * g)
    m_hat = m / (1.0 - ADAM_B1 ** ADAM_STEP)
    v_hat = v / (1.0 - ADAM_B2 ** ADAM_STEP)
    delta = -ADAM_LR * (m_hat / (jnp.sqrt(v_hat) + ADAM_EPS) + ADAM_WD * w)
    return delta, m, v


def _ada_fwd(c_rows, w, b, tn=512):
    rows, d = c_rows.shape
    n = w.shape[1]
    tn = _tile(tn, n)

    def body(c_ref, w_ref, b_ref, o_ref):
        cv = c_ref[...]
        act = (cv * jax.nn.sigmoid(cv)).astype(BF16)
        o_ref[...] = jnp.dot(act, w_ref[...].astype(BF16), preferred_element_type=F32) + b_ref[...]

    return _pallas(body, name="ada_fwd", grid=(n // tn,),
                   in_specs=[pl.BlockSpec((rows, d), lambda j: (0, 0)), pl.BlockSpec((d, tn), lambda j: (0, j)),
                             pl.BlockSpec((1, tn), lambda j: (0, j))],
                   out_specs=pl.BlockSpec((rows, tn), lambda j: (0, j)),
                   out_shape=jax.ShapeDtypeStruct((rows, n), F32), compiler_params=_params("parallel"))(c_rows, w, b)


def _ada_bwd_adam(c_cols, dmod, w, m, v, tr=512, tn=1024):
    d, rows = c_cols.shape
    n = w.shape[1]
    tr, tn = _tile(tr, d), _tile(tn, n)

    def body(c_ref, dm_ref, w_ref, m_ref, v_ref, g_ref, dl_ref, nm_ref, nv_ref):
        cv = c_ref[...]
        act = (cv * jax.nn.sigmoid(cv)).astype(BF16)
        g = jnp.dot(act, dm_ref[...].astype(BF16), preferred_element_type=F32)
        g_ref[...] = g
        dl_ref[...], nm_ref[...], nv_ref[...] = _adamw(w_ref[...], g, m_ref[...], v_ref[...])

    blk = pl.BlockSpec((tr, tn), lambda i, j: (i, j))
    shape = jax.ShapeDtypeStruct((d, n), F32)
    return _pallas(body, name="ada_bwd_adam", grid=(d // tr, n // tn),
                   in_specs=[pl.BlockSpec((tr, rows), lambda i, j: (i, 0)), pl.BlockSpec((rows, tn), lambda i, j: (0, j)),
                             blk, blk, blk],
                   out_specs=[blk] * 4, out_shape=[shape] * 4,
                   compiler_params=_params("parallel", "parallel"))(c_cols, dmod, w, m, v)


def _reduce_adam(name, pieces, w, m, v, tr=256, tc=1024):
    r, c = w.shape
    tr, tc = _tile(tr, r), _tile(tc, c)

    def body(p_ref, w_ref, m_ref, v_ref, g_ref, dl_ref, nm_ref, nv_ref):
        g = p_ref[0].astype(F32)
        for j in range(1, N_DEV):
            g = g + p_ref[j].astype(F32)
        g_ref[...] = g
        dl_ref[...], nm_ref[...], nv_ref[...] = _adamw(w_ref[...], g, m_ref[...], v_ref[...])

    blk = pl.BlockSpec((tr, tc), lambda i, j: (i, j))
    shape = jax.ShapeDtypeStruct((r, c), F32)
    return _pallas(body, name=name, grid=(r // tr, c // tc),
                   in_specs=[pl.BlockSpec((N_DEV, tr, tc), lambda i, j: (0, i, j)), blk, blk, blk],
                   out_specs=[blk] * 4, out_shape=[shape] * 4,
                   compiler_params=_params("parallel", "parallel"))(pieces, w, m, v)


def _reduce_adam_chips(name, sums, land, w, m, v, tr=256, tc=1024):
    r, c = w.shape
    tr, tc = _tile(tr, r), _tile(tc, c)

    def body(s_ref, l_ref, w_ref, m_ref, v_ref, g_ref, dl_ref, nm_ref, nv_ref):
        g = s_ref[...].astype(F32)
        for k in range(3):
            g = g + l_ref[k].astype(F32)
        g_ref[...] = g
        dl_ref[...], nm_ref[...], nv_ref[...] = _adamw(w_ref[...], g, m_ref[...], v_ref[...])

    blk = pl.BlockSpec((tr, tc), lambda i, j: (i, j))
    shape = jax.ShapeDtypeStruct((r, c), F32)
    mine = pl.BlockSpec((None, tr, tc), lambda i, j: (2 * lax.axis_index("x") + lax.axis_index("y"), i, j))
    return _pallas(body, name=name, grid=(r // tr, c // tc),
                   in_specs=[mine, pl.BlockSpec((3, tr, tc), lambda i, j: (0, i, j)), blk, blk, blk],
                   out_specs=[blk] * 4, out_shape=[shape] * 4,
                   compiler_params=_params("parallel", "parallel"))(sums, land, w, m, v)


def _sum_devices(parts):
    n = parts.shape[1]

    def body(p_ref, o_ref):
        acc = p_ref[0:1, :]
        for j in range(1, N_DEV):
            acc = acc + p_ref[j:j + 1, :]
        o_ref[...] = acc

    return _pallas(body, name="sum_devices", out_shape=jax.ShapeDtypeStruct((1, n), F32),
                   compiler_params=pltpu.CompilerParams(vmem_limit_bytes=VMEM_LIMIT_BYTES))(parts)


def _adam_small(name, g, w, m, v):
    def body(g_ref, w_ref, m_ref, v_ref, dl_ref, nm_ref, nv_ref):
        dl_ref[...], nm_ref[...], nv_ref[...] = _adamw(w_ref[...], g_ref[...], m_ref[...], v_ref[...])

    shape = jax.ShapeDtypeStruct(w.shape, F32)
    return _pallas(body, name=name, out_shape=[shape] * 3,
                   compiler_params=pltpu.CompilerParams(vmem_limit_bytes=VMEM_LIMIT_BYTES))(g, w, m, v)


def kernel(x, c, w_ada, b_ada, norm1_g, w_in, pool_mix_w, pool_scale, conv_w, conv_b, gnorm_pool_g, gnorm_conv_g, w_out, norm2_g, w_mlp_in, w_mlp_out, final_g, loss_target, m_w_ada, m_b_ada, m_norm1_g, m_w_in, m_pool_mix_w, m_pool_scale, m_conv_w, m_conv_b, m_gnorm_pool_g, m_gnorm_conv_g, m_w_out, m_norm2_g, m_w_mlp_in, m_w_mlp_out, m_final_g, v_w_ada, v_b_ada, v_norm1_g, v_w_in, v_pool_mix_w, v_pool_scale, v_conv_w, v_conv_b, v_gnorm_pool_g, v_gnorm_conv_g, v_w_out, v_norm2_g, v_w_mlp_in, v_w_mlp_out, v_final_g):
    s, d = x.shape[1], x.shape[2]
    width = d // 2
    gd = width // N_POOL_GROUPS
    d_ff = w_mlp_in.shape[2] * N_DEV
    n_proj = w_in.shape[2] * N_DEV
    ada_cols = w_ada.shape[2]
    conv_cols = conv_w.shape[2]
    assert n_proj == 4 * width and ada_cols * N_DEV == N_MOD * d and d_ff % N_DEV == 0
    assert width % CONV_HEAD_DIM == 0 and s % 8 == 0
    seq_chunk = _tile(512, s)
    pool_cb = _tile(256, gd)
    conv_cb = CONV_HEAD_DIM

    me = 4 * lax.axis_index("x") + 2 * lax.axis_index("y") + lax.axis_index("c")
    x2d, target = x[0], loss_target[0]

    conv_w_all, c_all = _exchange("gather_small_weights", [conv_w[0], c], ["gather"] * 2)
    conv_w_full = jnp.transpose(conv_w_all, (1, 0, 2)).reshape(3, width)
    c_rows = jnp.concatenate([c_all.reshape(N_DEV, d), jnp.zeros((N_DEV, d), F32)], axis=0)

    b_mine = lax.dynamic_slice(b_ada, (0, me * ada_cols), (1, ada_cols))
    mod_part = _ada_fwd(c_rows, w_ada[0], b_mine)
    (mod_all,) = _exchange("scatter_mod", [mod_part[:N_DEV].reshape(N_DEV, 1, ada_cols)], ["a2a"])
    mod = mod_all.reshape(1, N_MOD * d)

    started, hopped, relayed = {}, {}, {}

    def gather_start(wname, wgt, deps):
        land = _landing(wgt[0].astype(BF16), me)
        started[wname] = _gather_start("gather_" + wname + "_start", land, deps)
        return started[wname][5]

    def gather_hop(wname, land, after):
        hopped[wname] = _gather_hop("gather_" + wname + "_hop", started[wname], land, after)
        return hopped[wname][3]

    def gather_relay(wname, after):
        relayed[wname] = _gather_relay("gather_" + wname + "_relay", started[wname], hopped[wname], after)
        return relayed[wname][3]

    def gather_wait(wname, land, after, local_waited=False):
        return _gather_wait("gather_" + wname + "_wait", started[wname], hopped[wname], relayed[wname], land, after,
                            local_waited)

    def chip():
        return 2 * lax.axis_index("x") + lax.axis_index("y")

    def local_piece(t):
        return 2 * chip() + t

    def same_core_piece(t):
        return 2 * ((chip() + 1 + t) % N_CHIP) + lax.axis_index("c")

    def other_core_piece(t):
        return 2 * ((chip() + 1 + t) % N_CHIP) + 1 - lax.axis_index("c")

    def routed_piece(t):
        first, second, diagonal = _routes(lax.axis_index("x"), lax.axis_index("y"), lax.axis_index("c"))
        return jnp.where(t == 0, _index_of(first), jnp.where(t == 1, _index_of(second), _index_of(diagonal)))

    tok_w_in = gather_start("w_in", w_in, (mod,))
    shift1, scale1, gate1, shift2, scale2, gate2 = [mod[:, i * d:(i + 1) * d] for i in range(N_MOD)]

    h1 = _norm_mod("norm1_fwd", x2d, norm1_g, scale1, shift1, deps=(tok_w_in,))
    proj_shape = [jax.ShapeDtypeStruct((s, n_proj), BF16)]
    w_in_local = _gather_wait_local("gather_w_in_local", started["w_in"], started["w_in"][4], h1)
    (proj,) = _mm_nn("in_proj_local", h1, w_in_local, n_proj, True, proj_shape, _store(BF16), pieces=(local_piece, 2))
    tok = gather_hop("w_in", w_in_local, proj)
    tok = gather_start("w_out", w_out, (tok,))
    mix_started = _push_start("gather_pool_mix_start", [pool_mix_w[0].astype(BF16)], ("gather",), me, deps=(tok,))
    (proj,) = _mm_nn("in_proj_first", h1, hopped["w_in"][2], n_proj, True, proj_shape, _store(BF16),
                     pieces=(routed_piece, 1), carry=(proj,), deps=(mix_started[-1],))
    tok = gather_relay("w_in", proj)
    tok = gather_hop("w_out", started["w_out"][4], tok)
    tok = gather_start("w_mlp_in", w_mlp_in, (tok,))
    (proj,) = _mm_nn("in_proj_same_core", h1, relayed["w_in"][2], n_proj, True, proj_shape, _store(BF16),
                     pieces=(lambda t: routed_piece(t + 1), 2), carry=(proj,), deps=(tok,))
    w_in_all = gather_wait("w_in", relayed["w_in"][2], proj, True)
    (proj,) = _mm_nn("in_proj", h1, w_in_all, n_proj, True, proj_shape, _store(BF16), pieces=(other_core_piece, 3),
                     carry=(proj,))
    pooled = _pool_fwd(proj, s, gd, pool_cb, seq_chunk)
    (wmix_all,) = _push_wait("gather_pool_mix_wait", mix_started, ("gather",), pooled)
    wmix_full = jnp.transpose(wmix_all, (1, 0, 2, 3)).reshape(N_POOL_GROUPS, gd, gd)
    a_pre, mixed = _poolmix_fwd(pooled, wmix_full, pool_scale, gnorm_pool_g, d)
    tok = gather_hop("w_mlp_in", started["w_mlp_in"][4], a_pre)
    tok = gather_start("w_mlp_out", w_mlp_out, (tok,))
    tok = gather_relay("w_out", tok)
    mixed = _conv_fwd(proj, mixed, conv_w_full, conv_b, gnorm_conv_g, s, width, conv_cb, seq_chunk, deps=(tok,))

    def residual_specs(tm, tn):
        return [pl.BlockSpec((tm, tn), lambda i, j, k: (i, j)), pl.BlockSpec((1, tn), lambda i, j, k: (0, j))]

    sd_f32 = jax.ShapeDtypeStruct((s, d), F32)
    w_out_full = gather_wait("w_out", relayed["w_out"][2], mixed).reshape(d, d)
    sd_bf16 = jax.ShapeDtypeStruct((s, d), BF16)
    attn, x_mid = _mm_nn("out_proj", mixed, w_out_full, d, False, [sd_bf16, sd_f32], _residual_epilogue,
                         extras=(x2d, gate1), extra_specs=residual_specs)
    h2 = _norm_mod("norm2_fwd", x_mid, norm2_g, scale2, shift2)
    sf_bf16 = [jax.ShapeDtypeStruct((s, d_ff), BF16)] * 2
    w1_local = _gather_wait_local("gather_w_mlp_in_local", started["w_mlp_in"], hopped["w_mlp_in"][2], h2)
    relu, hid = _mm_nn("mlp_in_local", h2, w1_local, d_ff, True, sf_bf16, _relu2_epilogue, pieces=(local_piece, 2))
    hopped["w_mlp_in"] = hopped["w_mlp_in"][:2] + (w1_local,) + hopped["w_mlp_in"][3:]
    tok = gather_relay("w_mlp_in", hid)
    relu, hid = _mm_nn("mlp_in_same_core", h2, relayed["w_mlp_in"][2], d_ff, True, sf_bf16, _relu2_epilogue,
                       pieces=(same_core_piece, 3), carry=(relu, hid), deps=(tok,))
    w1_all = gather_wait("w_mlp_in", relayed["w_mlp_in"][2], hid, True)
    tok = gather_hop("w_mlp_out", started["w_mlp_out"][4], w1_all)
    relu, hid = _mm_nn("mlp_in_other_core", h2, w1_all, d_ff, True, sf_bf16, _relu2_epilogue,
                       pieces=(other_core_piece, 2), carry=(relu, hid), deps=(tok,))
    tok = gather_relay("w_mlp_out", hid)
    relu, hid = _mm_nn("mlp_in", h2, w1_all, d_ff, True, sf_bf16, _relu2_epilogue,
                       pieces=(lambda t: other_core_piece(t + 2), 1), carry=(relu, hid), deps=(tok,))
    w2_full = gather_wait("w_mlp_out", relayed["w_mlp_out"][2], hid).reshape(d_ff, d)
    mlp, x_last = _mm_nn("mlp_out", hid, w2_full, d, False, [sd_bf16, sd_f32], _residual_epilogue,
                         extras=(x_mid, gate2), extra_specs=residual_specs)

    dx_last, dmlp, d_final_g, dgate2, loss_row = _loss_head(x_last, target, final_g.reshape(1, d), gate2, mlp)

    def relu_specs(tm, tn):
        return [pl.BlockSpec((tm, tn), lambda i, j, k: (i, j))]

    def reduce_start(wname, a, b, col_pieces, deps=()):
        far = _mm_tn_half(wname + "_dw_far", a, b, col_pieces, near=False, deps=deps)
        return _pair_start("scatter_" + wname + "_pair_start", far)

    def reduce_chips(wname, a, b, col_pieces, pairs, after):
        pair = _pair_wait("scatter_" + wname + "_pair_wait", pairs, after)
        sums = _mm_tn_half(wname + "_dw_near", a, b, col_pieces, near=True, pair=pair)
        return _chip_start("scatter_" + wname + "_chip_start", sums)

    pairs_w2 = reduce_start("mlp_out", hid, dmlp, False)
    (dhpre,) = _mm_nt("mlp_out_dx", dmlp, w2_full, d_ff, False, sf_bf16[:1], _relu2_bwd_epilogue,
                      extras=(relu,), extra_specs=relu_specs, deps=(pairs_w2[4],))
    chips_w2 = reduce_chips("mlp_out", hid, dmlp, False, pairs_w2, dhpre)
    pairs_w1 = reduce_start("mlp_in", h2, dhpre, True, deps=(chips_w2[4],))
    (dh2,) = _mm_nt("mlp_in_dx", dhpre, w1_all, d, True, [sd_bf16], _store(BF16), tn=1024, tk=2048,
                    deps=(pairs_w1[4],))
    chips_w1 = reduce_chips("mlp_in", h2, dhpre, True, pairs_w1, dh2)
    dx_mid, dshift2, dscale2, d_norm2_g, dattn, dgate1 = _norm_mod_bwd(
        "norm2_bwd", dh2, x_mid, norm2_g, scale2, dx_last, branch=attn, gate=gate1, deps=(chips_w1[4],))

    pairs_w_out = reduce_start("out_proj", mixed, dattn, False)
    (dmixed,) = _mm_nt("out_proj_dx", dattn, w_out_full, d, False, [sd_bf16], _store(BF16), deps=(pairs_w_out[4],))
    chips_w_out = reduce_chips("out_proj", mixed, dattn, False, pairs_w_out, dmixed)
    dproj, d_conv_w, d_conv_b, d_gnorm_conv = _conv_bwd(dmixed, proj, conv_w_full, conv_b, gnorm_conv_g,
                                                        s, width, conv_cb, seq_chunk, deps=(chips_w_out[4],))
    da_pre, dpooled, d_pool_scale, d_gnorm_pool = _poolmix_bwd(dmixed, a_pre, wmix_full, pool_scale, gnorm_pool_g)
    g_wmix = _poolmix_wgrad(pooled, da_pre, gd)
    dproj = _pool_bwd(dpooled, dproj, s, gd, pool_cb, seq_chunk)

    pairs_w_in = reduce_start("in_proj", h1, dproj, True)
    (dh1,) = _mm_nt("in_proj_dx", dproj, w_in_all, d, True, [sd_bf16], _store(BF16), deps=(pairs_w_in[4],))
    chips_w_in = reduce_chips("in_proj", h1, dproj, True, pairs_w_in, dh1)
    grad_x, dshift1, dscale1, d_norm1_g = _norm_mod_bwd("norm1_bwd", dh1, x2d, norm1_g, scale1, dx_mid,
                                                        deps=(chips_w_in[4],))

    rows_mix = gd // N_DEV
    g_wmix_split = jnp.transpose(g_wmix.reshape(N_POOL_GROUPS, N_DEV, rows_mix, gd), (1, 0, 2, 3))
    g_wmix_split = g_wmix_split.reshape(N_DEV, N_POOL_GROUPS * rows_mix, gd)
    loss_pad = jnp.concatenate([loss_row[:, :1], jnp.zeros((1, 127), F32)], axis=1)
    dmod = jnp.concatenate([dshift1, dscale1, dgate1, dshift2, dscale2, dgate2], axis=1)
    small = jnp.concatenate([dmod, d_norm1_g, d_pool_scale, d_conv_b, d_gnorm_pool, d_gnorm_conv, d_norm2_g,
                             d_final_g, d_conv_w.reshape(1, 3 * width), loss_pad], axis=1)
    small_started = _push_start("exchange_small_grads_start", [g_wmix_split, small], ("a2a", "gather"), me)

    sums, landed = _chip_wait("scatter_w_mlp_out_chip_wait", chips_w2, small_started[-1])
    out_w2 = _reduce_adam_chips("adam_w_mlp_out", sums, landed, w_mlp_out[0], m_w_mlp_out[0], v_w_mlp_out[0])
    sums, landed = _chip_wait("scatter_w_mlp_in_chip_wait", chips_w1, out_w2[0])
    out_w1 = _reduce_adam_chips("adam_w_mlp_in", sums, landed, w_mlp_in[0], m_w_mlp_in[0], v_w_mlp_in[0])
    sums, landed = _chip_wait("scatter_w_out_chip_wait", chips_w_out, out_w1[0])
    out_w_out = _reduce_adam_chips("adam_w_out", sums, landed, w_out[0], m_w_out[0], v_w_out[0])
    sums, landed = _chip_wait("scatter_w_in_chip_wait", chips_w_in, out_w_out[0])
    out_w_in = _reduce_adam_chips("adam_w_in", sums, landed, w_in[0], m_w_in[0], v_w_in[0])

    p_wmix, small_all = _push_wait("exchange_small_grads_wait", small_started, ("a2a", "gather"), out_w_in[0])
    mix_shape = (N_POOL_GROUPS * rows_mix, gd)
    out_wmix = _reduce_adam("adam_pool_mix", p_wmix, pool_mix_w.reshape(mix_shape), m_pool_mix_w.reshape(mix_shape),
                            v_pool_mix_w.reshape(mix_shape))
    out_wmix = [a.reshape(pool_mix_w.shape) for a in out_wmix]
    small_all = small_all.reshape(N_DEV, small.shape[1])
    small_sum = _sum_devices(small_all)

    n_rep = (N_MOD + 1) * d + 4 * width + 2 * d
    loss = small_sum[0, n_rep + 3 * width]
    rep_names_w = [b_ada, norm1_g, pool_scale, conv_b, gnorm_pool_g, gnorm_conv_g, norm2_g, final_g.reshape(1, d)]
    rep_names_m = [m_b_ada, m_norm1_g, m_pool_scale, m_conv_b, m_gnorm_pool_g, m_gnorm_conv_g, m_norm2_g,
                   m_final_g.reshape(1, d)]
    rep_names_v = [v_b_ada, v_norm1_g, v_pool_scale, v_conv_b, v_gnorm_pool_g, v_gnorm_conv_g, v_norm2_g,
                   v_final_g.reshape(1, d)]
    rep_grad = small_sum[:, :n_rep]
    rep_delta, rep_m, rep_v = _adam_small("adam_replicated", rep_grad, jnp.concatenate(rep_names_w, axis=1),
                                          jnp.concatenate(rep_names_m, axis=1), jnp.concatenate(rep_names_v, axis=1))

    def split_rep(vec):
        out, off = [], 0
        for wgt in rep_names_w:
            n = wgt.shape[1]
            out.append(vec[:, off:off + n])
            off += n
        out[-1] = out[-1].reshape(d)
        return out

    conv_grad_full = small_sum[:, n_rep:n_rep + 3 * width].reshape(3, width)
    g_conv_w = lax.dynamic_slice(conv_grad_full, (0, me * conv_cols), (3, conv_cols))
    g_conv_w8 = jnp.concatenate([g_conv_w, jnp.zeros((5, conv_cols), F32)], axis=0)

    def pad8(a):
        return jnp.concatenate([a[0], jnp.zeros((5, conv_cols), F32)], axis=0)

    conv_delta, conv_m, conv_v = _adam_small("adam_conv_w", g_conv_w8, pad8(conv_w), pad8(m_conv_w), pad8(v_conv_w))

    dmod_all = small_all[:, :N_MOD * d]
    dmod_mine = lax.dynamic_slice(dmod_all, (0, me * ada_cols), (N_DEV, ada_cols))
    dmod_rows = jnp.concatenate([dmod_mine, jnp.zeros((N_DEV, ada_cols), F32)], axis=0)
    out_ada = _ada_bwd_adam(jnp.transpose(c_rows), dmod_rows, w_ada[0], m_w_ada[0], v_w_ada[0])

    rep_all = [split_rep(rep_grad), split_rep(rep_delta), split_rep(rep_m), split_rep(rep_v)]
    conv_all = [g_conv_w[None], conv_delta[None, :3], conv_m[None, :3], conv_v[None, :3]]
    outs = [loss, grad_x[None]]
    for kind in range(4):
        b_ada_o, norm1_o, pool_scale_o, conv_b_o, gpool_o, gconv_o, norm2_o, final_o = rep_all[kind]
        outs += [out_ada[kind][None], b_ada_o, norm1_o, out_w_in[kind][None], out_wmix[kind], pool_scale_o,
                 conv_all[kind], conv_b_o, gpool_o, gconv_o, out_w_out[kind][None], norm2_o, out_w1[kind][None],
                 out_w2[kind][None], final_o]
    return tuple(outs)
```

```python
import jax
import jax.numpy as jnp
from jax import lax
from jax.experimental import pallas as pl
from jax.experimental.pallas import tpu as pltpu

F32 = jnp.float32
BF16 = jnp.bfloat16
MESH = pl.DeviceIdType.MESH

N_DEV = 8
N_MOD = 6
EPS = 1e-6
POOL_WINDOWS = (2, 4, 8, 16)
N_POOL_GROUPS = len(POOL_WINDOWS)
CONV_HEAD_DIM = 128
PAD_ROWS = 16

ADAM_LR = 0.001
ADAM_B1 = 0.9
ADAM_B2 = 0.999
ADAM_EPS = 1e-08
ADAM_WD = 0.01
ADAM_STEP = 10

VMEM_LIMIT_BYTES = 56 * 1024 * 1024
MM_TM, MM_TN, MM_TK = 1024, 512, 4096


def _pallas(body, deps=(), **kw):
    if not deps:
        return pl.pallas_call(body, **kw)
    n_in = len(kw["in_specs"])

    def with_deps(*refs):
        body(*refs[:n_in], *refs[n_in + len(deps):])

    kw["in_specs"] = list(kw["in_specs"]) + [pl.BlockSpec(memory_space=pl.ANY)] * len(deps)
    call = pl.pallas_call(with_deps, **kw)
    return lambda *operands: call(*operands, *deps)


def _params(*sem):
    return pltpu.CompilerParams(dimension_semantics=sem, vmem_limit_bytes=VMEM_LIMIT_BYTES)


def _tile(pref, dim):
    if dim <= pref:
        return dim
    for t in range(pref - pref % 128, 0, -128):
        if dim % t == 0:
            return t
    return dim


def _exchange(name, arrays, modes, deps=()):
    n = len(arrays)
    out_shape = []
    for a, mode in zip(arrays, modes):
        piece = a.shape if mode == "gather" else a.shape[1:]
        out_shape.append(jax.ShapeDtypeStruct((N_DEV,) + tuple(piece), a.dtype))

    def body(*refs):
        srcs, dsts = refs[:n], refs[n:2 * n]
        send_sems, recv_sems, local_sems = refs[2 * n:]
        x, y, c = lax.axis_index("x"), lax.axis_index("y"), lax.axis_index("c")
        me = 4 * x + 2 * y + c
        copies = []
        for i in range(n):
            gather = modes[i] == "gather"
            local = pltpu.make_async_copy(srcs[i] if gather else srcs[i].at[me], dsts[i].at[me], local_sems.at[i])
            local.start()
            copies.append(local)
            for k in range(1, N_DEV):
                kx, ky, kc = (k >> 2) & 1, (k >> 1) & 1, k & 1
                peer = (1 - x if kx else x, 1 - y if ky else y, 1 - c if kc else c)
                peer_idx = 4 * peer[0] + 2 * peer[1] + peer[2]
                remote = pltpu.make_async_remote_copy(
                    src_ref=srcs[i] if gather else srcs[i].at[peer_idx],
                    dst_ref=dsts[i].at[me],
                    send_sem=send_sems.at[i * (N_DEV - 1) + k - 1],
                    recv_sem=recv_sems.at[i * (N_DEV - 1) + k - 1],
                    device_id=peer, device_id_type=MESH)
                remote.start()
                copies.append(remote)
        for cp in copies:
            cp.wait()

    any_spec = pl.BlockSpec(memory_space=pl.ANY)
    return _pallas(
        body, deps, name=name, out_shape=out_shape,
        in_specs=[any_spec] * n, out_specs=[any_spec] * n,
        scratch_shapes=[pltpu.SemaphoreType.DMA((n * (N_DEV - 1),)),
                        pltpu.SemaphoreType.DMA((n * (N_DEV - 1),)),
                        pltpu.SemaphoreType.DMA((n,))],
    )(*arrays)


_HBM = pl.BlockSpec(memory_space=pltpu.HBM)
_SEM = pl.BlockSpec(memory_space=pltpu.SEMAPHORE)
_TOKEN = pl.BlockSpec(memory_space=pltpu.VMEM)
_EFFECT = pltpu.SideEffectType.DATAFLOW_SIDE_EFFECTING
N_CHIP = N_DEV // 2
_OTHER_CHIPS = (1, 2, 3)


def _place():
    x, y, c = lax.axis_index("x"), lax.axis_index("y"), lax.axis_index("c")
    return x, y, c, (x, y, 1 - c)


def _same_core_of(x, y, c, k):
    px = 1 - x if k & 2 else x
    py = 1 - y if k & 1 else y
    return (px, py, c), 2 * px + py


def _remote(src, dst, send_sem, recv_sem, device):
    return pltpu.make_async_remote_copy(src_ref=src, dst_ref=dst, send_sem=send_sem, recv_sem=recv_sem,
                                        device_id=device, device_id_type=MESH)


def _token_shape():
    return jax.ShapeDtypeStruct((8, 128), F32)


def _split_call(body, deps, name, operands, in_specs, out_shape, out_specs, aliases):
    return _pallas(body, deps, name=name, out_shape=out_shape, in_specs=in_specs, out_specs=out_specs,
                   input_output_aliases=aliases,
                   compiler_params=pltpu.CompilerParams(has_side_effects=_EFFECT))(*operands)


def _routes(x, y, c):
    first = (x + c - 2 * x * c, y + (1 - c) - 2 * y * (1 - c), c)
    second = (x + (1 - c) - 2 * x * (1 - c), y + c - 2 * y * c, c)
    return first, second, (1 - x, 1 - y, c)


def _index_of(device):
    return 4 * device[0] + 2 * device[1] + device[2]


def _gather_start(name, land, deps):
    def body(land_ref, send_sems, recv_first, recv_second, recv_d2d, land_thru, token):
        del land_thru
        x, y, c, sibling = _place()
        first, second, _ = _routes(x, y, c)
        mine = land_ref.at[4 * x + 2 * y + c]
        _remote(mine, mine, send_sems.at[0], recv_first.at[0], first).start()
        _remote(mine, mine, send_sems.at[1], recv_second.at[0], second).start()
        _remote(mine, mine, send_sems.at[2], recv_d2d.at[0], sibling).start()
        token[...] = jnp.zeros_like(token)

    one = pltpu.SemaphoreType.DMA((1,))
    return _split_call(
        body, deps, name, (pltpu.with_memory_space_constraint(land, pltpu.HBM),), (_HBM,),
        (pltpu.SemaphoreType.DMA((3,)), one, one, one, pltpu.HBM(land.shape, land.dtype), _token_shape()),
        (_SEM, _SEM, _SEM, _SEM, _HBM, _TOKEN), {0: 4})


def _gather_wait_local(name, started, land, after):
    recv_d2d = started[3]

    def body(land_ref, recv_d2d, after_ref, land_out):
        del after_ref, land_out
        x, y, c, sibling = _place()
        mine = land_ref.at[4 * x + 2 * y + c]
        _remote(mine, mine, recv_d2d.at[0], recv_d2d.at[0], sibling).wait_recv()

    return _split_call(
        body, (), name, (land, recv_d2d, after), (_HBM, _SEM, pl.BlockSpec(memory_space=pl.ANY)),
        (pltpu.HBM(land.shape, land.dtype),), (_HBM,), {0: 0})[0]


def _gather_hop(name, started, land, after):
    recv_first = started[1]

    def body(land_ref, recv_first, after_ref, send_hop, recv_hop, land_thru, token):
        del after_ref, land_thru
        x, y, c, _ = _place()
        first, second, _ = _routes(x, y, c)
        piece = land_ref.at[_index_of(first)]
        _remote(piece, piece, send_hop.at[0], recv_first.at[0], first).wait_recv()
        _remote(piece, piece, send_hop.at[0], recv_hop.at[0], second).start()
        token[...] = jnp.zeros_like(token)

    one = pltpu.SemaphoreType.DMA((1,))
    return _split_call(
        body, (), name, (land, recv_first, after), (_HBM, _SEM, pl.BlockSpec(memory_space=pl.ANY)),
        (one, one, pltpu.HBM(land.shape, land.dtype), _token_shape()), (_SEM, _SEM, _HBM, _TOKEN), {0: 2})


def _gather_relay(name, started, hopped, after):
    recv_second = started[2]
    _, recv_hop, land, _ = hopped

    def body(land_ref, recv_second, recv_hop, after_ref, send_fwd, recv_fwd, land_thru, token):
        del after_ref, land_thru
        token[...] = jnp.zeros_like(token)
        x, y, c, sibling = _place()
        mine = land_ref.at[4 * x + 2 * y + c]
        _remote(mine, mine, send_fwd.at[0], recv_second.at[0], sibling).wait_recv()
        _remote(mine, mine, send_fwd.at[0], recv_hop.at[0], sibling).wait_recv()
        for i, device in enumerate(_routes(x, y, c)):
            piece = land_ref.at[_index_of(device)]
            _remote(piece, piece, send_fwd.at[i], recv_fwd.at[i], sibling).start()

    return _split_call(
        body, (), name, (land, recv_second, recv_hop, after), (_HBM, _SEM, _SEM, pl.BlockSpec(memory_space=pl.ANY)),
        (pltpu.SemaphoreType.DMA((3,)), pltpu.SemaphoreType.DMA((3,)), pltpu.HBM(land.shape, land.dtype),
         _token_shape()),
        (_SEM, _SEM, _HBM, _TOKEN), {0: 2})


def _gather_wait(name, started, hopped, relayed, land, after, local_waited):
    send_sems, recv_d2d = started[0], started[3]
    send_hop = hopped[0]
    send_fwd, recv_fwd = relayed[0], relayed[1]

    def body(land_ref, send_sems, recv_d2d, send_hop, send_fwd, recv_fwd, after_ref, land_out):
        del after_ref, land_out
        x, y, c, sibling = _place()
        mine = land_ref.at[4 * x + 2 * y + c]
        for i in range(3):
            _remote(mine, mine, send_sems.at[i], recv_d2d.at[0], sibling).wait_send()
        _remote(mine, mine, send_hop.at[0], recv_d2d.at[0], sibling).wait_send()
        if not local_waited:
            _remote(mine, mine, send_sems.at[2], recv_d2d.at[0], sibling).wait_recv()
        for i in range(3):
            relay = _remote(mine, mine, send_fwd.at[i], recv_fwd.at[i], sibling)
            relay.wait_send()
            relay.wait_recv()

    return _split_call(
        body, (), name, (land, send_sems, recv_d2d, send_hop, send_fwd, recv_fwd, after),
        (_HBM, _SEM, _SEM, _SEM, _SEM, _SEM, pl.BlockSpec(memory_space=pl.ANY)),
        (pltpu.HBM(land.shape, land.dtype),), (_HBM,), {0: 0})[0]


def _push_start(name, srcs, modes, me, deps=()):
    n = len(srcs)
    lands = [_landing(src if mode == "gather" else lax.dynamic_index_in_dim(src, me, 0, keepdims=False), me)
             for src, mode in zip(srcs, modes)]

    def body(*refs):
        src_refs, land_refs, send_sems, recv_sems, token = refs[:n], refs[n:2 * n], refs[2 * n], refs[2 * n + 1], refs[-1]
        for cp in _push_copies(src_refs, land_refs, send_sems, recv_sems, modes):
            cp.start()
        token[...] = jnp.zeros_like(token)

    count = pltpu.SemaphoreType.DMA((n * (N_DEV - 1),))
    operands = [pltpu.with_memory_space_constraint(a, pltpu.HBM) for a in list(srcs) + lands]
    return _split_call(
        body, deps, name, operands, (_HBM,) * (2 * n),
        (count, count) + tuple(pltpu.HBM(a.shape, a.dtype) for a in operands) + (_token_shape(),),
        (_SEM, _SEM) + (_HBM,) * (2 * n) + (_TOKEN,), {i: 2 + i for i in range(2 * n)})


def _push_wait(name, started, modes, after):
    n = len(modes)
    send_sems, recv_sems = started[0], started[1]
    arrays = started[2:2 + 2 * n]

    def body(*refs):
        src_refs, land_refs, send_sems, recv_sems = refs[:n], refs[n:2 * n], refs[2 * n], refs[2 * n + 1]
        for cp in _push_copies(src_refs, land_refs, send_sems, recv_sems, modes):
            cp.wait_send()
            cp.wait_recv()

    return _split_call(
        body, (), name, tuple(arrays) + (send_sems, recv_sems, after),
        (_HBM,) * (2 * n) + (_SEM, _SEM, pl.BlockSpec(memory_space=pl.ANY)),
        tuple(pltpu.HBM(a.shape, a.dtype) for a in arrays), (_HBM,) * (2 * n), {i: i for i in range(2 * n)})[n:]


def _push_copies(src_refs, land_refs, send_sems, recv_sems, modes):
    x, y, c, _ = _place()
    me = 4 * x + 2 * y + c
    copies = []
    for i, mode in enumerate(modes):
        for k in range(1, N_DEV):
            peer = (1 - x if k & 4 else x, 1 - y if k & 2 else y, 1 - c if k & 1 else c)
            src = src_refs[i] if mode == "gather" else src_refs[i].at[_index_of(peer)]
            sem = i * (N_DEV - 1) + k - 1
            copies.append(_remote(src, land_refs[i].at[me], send_sems.at[sem], recv_sems.at[sem], peer))
    return copies


def _landing(own, me):
    land = lax.empty((N_DEV,) + own.shape, own.dtype)
    return lax.dynamic_update_slice(land, own[None], (me,) + (0,) * own.ndim)


def _pair_start(name, far, deps=()):
    pair = lax.empty(far.shape, far.dtype)

    def body(far_ref, pair_ref, send_sems, recv_sems, far_thru, pair_thru, token):
        del far_thru, pair_thru
        _remote(far_ref, pair_ref, send_sems.at[0], recv_sems.at[0], _place()[3]).start()
        token[...] = jnp.zeros_like(token)

    return _split_call(
        body, deps, name,
        (pltpu.with_memory_space_constraint(far, pltpu.HBM), pltpu.with_memory_space_constraint(pair, pltpu.HBM)),
        (_HBM, _HBM),
        (pltpu.SemaphoreType.DMA((1,)), pltpu.SemaphoreType.DMA((1,)),
         pltpu.HBM(far.shape, far.dtype), pltpu.HBM(pair.shape, pair.dtype), _token_shape()),
        (_SEM, _SEM, _HBM, _HBM, _TOKEN), {0: 2, 1: 3})


def _pair_wait(name, started, after):
    send_sems, recv_sems, far, pair, _ = started

    def body(far_ref, pair_ref, send_sems, recv_sems, after_ref, far_out, pair_out):
        del after_ref, far_out, pair_out
        cp = _remote(far_ref, pair_ref, send_sems.at[0], recv_sems.at[0], _place()[3])
        cp.wait_send()
        cp.wait_recv()

    return _split_call(
        body, (), name, (far, pair, send_sems, recv_sems, after),
        (_HBM, _HBM, _SEM, _SEM, pl.BlockSpec(memory_space=pl.ANY)),
        (pltpu.HBM(far.shape, far.dtype), pltpu.HBM(pair.shape, pair.dtype)), (_HBM, _HBM), {0: 0, 1: 1})[1]


def _chip_start(name, sums, deps=()):
    land = lax.empty((3,) + sums.shape[1:], sums.dtype)

    def body(s_ref, land_ref, send_sems, recv_sems, s_thru, land_thru, token):
        del s_thru, land_thru
        x, y, c, _ = _place()
        for k in _OTHER_CHIPS:
            peer, chip = _same_core_of(x, y, c, k)
            _remote(s_ref.at[chip], land_ref.at[k - 1], send_sems.at[k - 1], recv_sems.at[k - 1], peer).start()
        token[...] = jnp.zeros_like(token)

    return _split_call(
        body, deps, name,
        (pltpu.with_memory_space_constraint(sums, pltpu.HBM), pltpu.with_memory_space_constraint(land, pltpu.HBM)),
        (_HBM, _HBM),
        (pltpu.SemaphoreType.DMA((3,)), pltpu.SemaphoreType.DMA((3,)),
         pltpu.HBM(sums.shape, sums.dtype), pltpu.HBM(land.shape, land.dtype), _token_shape()),
        (_SEM, _SEM, _HBM, _HBM, _TOKEN), {0: 2, 1: 3})


def _chip_wait(name, started, after):
    send_sems, recv_sems, sums, land, _ = started

    def body(s_ref, land_ref, send_sems, recv_sems, after_ref, s_out, land_out):
        del after_ref, s_out, land_out
        x, y, c, _ = _place()
        for k in _OTHER_CHIPS:
            peer, chip = _same_core_of(x, y, c, k)
            cp = _remote(s_ref.at[chip], land_ref.at[k - 1], send_sems.at[k - 1], recv_sems.at[k - 1], peer)
            cp.wait_send()
            cp.wait_recv()

    return _split_call(
        body, (), name, (sums, land, send_sems, recv_sems, after),
        (_HBM, _HBM, _SEM, _SEM, pl.BlockSpec(memory_space=pl.ANY)),
        (pltpu.HBM(sums.shape, sums.dtype), pltpu.HBM(land.shape, land.dtype)), (_HBM, _HBM), {0: 0, 1: 1})


_DOT_DIMS = {"nn": (((1,), (0,)), ((), ())), "nt": (((1,), (1,)), ((), ())), "tn": (((0,), (0,)), ((), ()))}


def _matmul(name, mode, operands, in_specs, out_shape, out_specs, grid, acc_shape, epilogue, deps=(), carry=()):
    n_in, n_out, nk = len(operands), len(out_shape), grid[2]
    dims = _DOT_DIMS[mode]

    def body(*refs):
        a_ref, b_ref = refs[0], refs[1]
        extras, outs = refs[2:n_in], refs[n_in:n_in + n_out]
        b_val = b_ref[...]
        if b_val.ndim == 3:
            b_val = jnp.concatenate([b_val[g] for g in range(b_val.shape[0])], axis=1)
        part = lax.dot_general(a_ref[...], b_val, dims, preferred_element_type=F32)
        if nk == 1:
            epilogue(part, extras, outs)
            return
        acc = refs[-1]
        k = pl.program_id(2)

        @pl.when(k == 0)
        def _():
            acc[...] = part

        @pl.when(jnp.logical_and(k > 0, k < nk - 1))
        def _():
            acc[...] += part

        @pl.when(k == nk - 1)
        def _():
            epilogue(acc[...] + part, extras, outs)

    aliases = {n_in + len(deps) + i: i for i in range(len(carry))}
    return _pallas(body, tuple(deps) + tuple(carry), name=name, grid=grid, in_specs=in_specs, out_specs=out_specs,
                   out_shape=out_shape, input_output_aliases=aliases,
                   scratch_shapes=[pltpu.VMEM(acc_shape, F32)] if nk > 1 else [],
                   compiler_params=_params("parallel", "parallel", "arbitrary"))(*operands)


def _store(dtype):
    def epilogue(acc, extras, outs):
        outs[0][...] = acc.astype(dtype)
    return epilogue


def _residual_epilogue(acc, extras, outs):
    x_ref, gate_ref = extras
    outs[0][...] = acc.astype(outs[0].dtype)
    outs[1][...] = x_ref[...] + gate_ref[...] * acc


def _relu2_epilogue(acc, extras, outs):
    r = jnp.maximum(acc, 0.0)
    outs[0][...] = r.astype(outs[0].dtype)
    outs[1][...] = (r * r).astype(outs[1].dtype)


def _relu2_bwd_epilogue(acc, extras, outs):
    outs[0][...] = (acc * (2.0 * extras[0][...].astype(F32))).astype(outs[0].dtype)


def _no_extra_specs(tm, tn):
    return []


def _mm_nn(name, a, b, n_total, b_split, out_shape, epilogue, extras=(), extra_specs=_no_extra_specs, tm=MM_TM, tn=MM_TN, tk=MM_TK,
           deps=(), pieces=None, carry=()):
    m, kdim = a.shape
    tm, tk = _tile(tm, m), _tile(tk, kdim)
    n_blocks = None
    if b_split:
        piece = b.shape[2]
        tn = _tile(tn, piece)
        per = piece // tn
        if pieces is None:
            b_spec = pl.BlockSpec((None, tk, tn), lambda i, j, k: (j // per, k, j % per))
            out_spec = pl.BlockSpec((tm, tn), lambda i, j, k: (i, j))
        else:
            piece_of, count = pieces
            n_blocks = count * per

            def which(j):
                return piece_of(j // per)

            b_spec = pl.BlockSpec((None, tk, tn), lambda i, j, k: (which(j), k, j % per))
            out_spec = pl.BlockSpec((tm, tn), lambda i, j, k: (i, which(j) * per + j % per))
    else:
        tn = _tile(tn, n_total)
        b_spec = pl.BlockSpec((tk, tn), lambda i, j, k: (k, j))
        out_spec = pl.BlockSpec((tm, tn), lambda i, j, k: (i, j))
    if n_blocks is None:
        n_blocks = n_total // tn
    in_specs = [pl.BlockSpec((tm, tk), lambda i, j, k: (i, k)), b_spec] + list(extra_specs(tm, tn))
    return _matmul(name, "nn", (a, b) + tuple(extras), in_specs, out_shape, [out_spec] * len(out_shape),
                   (m // tm, n_blocks, kdim // tk), (tm, tn), epilogue, deps, carry)


def _mm_nt(name, a, b, n_total, b_split, out_shape, epilogue, extras=(), extra_specs=_no_extra_specs, tm=MM_TM, tn=MM_TN, tk=MM_TK,
           deps=()):
    m, kdim = a.shape
    tm, tn = _tile(tm, m), _tile(tn, n_total)
    if b_split and tk >= 2 * b.shape[2]:
        piece = b.shape[2]
        group = _tile(tk, kdim) // piece
        tk = group * piece
        b_spec = pl.BlockSpec((group, tn, piece), lambda i, j, k: (k, j, 0))
    elif b_split:
        piece = b.shape[2]
        tk = _tile(tk, piece)
        per = piece // tk
        b_spec = pl.BlockSpec((None, tn, tk), lambda i, j, k: (k // per, j, k % per))
    else:
        tk = _tile(tk, kdim)
        b_spec = pl.BlockSpec((tn, tk), lambda i, j, k: (j, k))
    in_specs = [pl.BlockSpec((tm, tk), lambda i, j, k: (i, k)), b_spec] + list(extra_specs(tm, tn))
    out_specs = [pl.BlockSpec((tm, tn), lambda i, j, k: (i, j)) for _ in out_shape]
    return _matmul(name, "nt", (a, b) + tuple(extras), in_specs, out_shape, out_specs,
                   (m // tm, n_total // tn, kdim // tk), (tm, tn), epilogue, deps)


def _add_pair_epilogue(acc, extras, outs):
    outs[0][...] = (acc + extras[0][...].astype(F32)).astype(outs[0].dtype)


def _mm_tn_half(name, a, b, col_pieces, near, pair=None, tm=MM_TM, tn=MM_TN, tk=MM_TK, deps=()):
    kdim, m = a.shape
    n_total = b.shape[1]
    tk = _tile(tk, kdim)

    def core():
        c = lax.axis_index("c")
        return c if near else 1 - c

    if col_pieces:
        piece = n_total // N_DEV
        tm, tn = _tile(tm, m), _tile(tn, piece)
        per = piece // tn
        grid = (m // tm, N_CHIP * per, kdim // tk)
        a_spec = pl.BlockSpec((tk, tm), lambda i, j, k: (k, i))
        b_spec = pl.BlockSpec((tk, tn), lambda i, j, k: (k, (2 * (j // per) + core()) * per + j % per))
        out_spec = pl.BlockSpec((None, tm, tn), lambda i, j, k: (j // per, i, j % per))
        out_shape = [jax.ShapeDtypeStruct((N_CHIP, m, piece), BF16)]
    else:
        piece = m // N_DEV
        tm, tn = _tile(tm, piece), _tile(tn, n_total)
        per = piece // tm
        grid = (N_CHIP * per, n_total // tn, kdim // tk)
        a_spec = pl.BlockSpec((tk, tm), lambda i, j, k: (k, (2 * (i // per) + core()) * per + i % per))
        b_spec = pl.BlockSpec((tk, tn), lambda i, j, k: (k, j))
        out_spec = pl.BlockSpec((None, tm, tn), lambda i, j, k: (i // per, i % per, j))
        out_shape = [jax.ShapeDtypeStruct((N_CHIP, piece, n_total), BF16)]
    operands, in_specs, epilogue = (a, b), [a_spec, b_spec], _store(BF16)
    if pair is not None:
        operands, in_specs, epilogue = (a, b, pair), [a_spec, b_spec, out_spec], _add_pair_epilogue
    return _matmul(name, "tn", operands, in_specs, out_shape, [out_spec], grid, (tm, tn), epilogue, deps)[0]


def _rms(xv):
    return lax.rsqrt(jnp.mean(xv * xv, axis=-1, keepdims=True) + EPS)


def _colsum(v):
    return jnp.sum(v, axis=0, keepdims=True)


ROW_STRIP = 16


def _fold8(v):
    return v[0:8] + v[8:16]


def _norm_mod(name, x, g, scale, shift, tr=256, deps=()):
    s, d = x.shape
    tr = _tile(tr, s)

    def body(x_ref, g_ref, sc_ref, sh_ref, h_ref):
        xv = x_ref[...]
        h = (xv * _rms(xv)) * g_ref[...]
        h_ref[...] = (h * (1.0 + sc_ref[...]) + sh_ref[...]).astype(h_ref.dtype)

    row = pl.BlockSpec((tr, d), lambda i: (i, 0))
    vec = pl.BlockSpec((1, d), lambda i: (0, 0))
    return _pallas(body, deps, name=name, grid=(s // tr,), in_specs=[row, vec, vec, vec], out_specs=row,
                   out_shape=jax.ShapeDtypeStruct((s, d), BF16), compiler_params=_params("parallel"))(x, g, scale, shift)


def _loss_head(x3, target, gf, gate2, mlp, tr=128):
    s, d = x3.shape
    tr = _tile(tr, s)

    assert tr % ROW_STRIP == 0

    def body(x_ref, t_ref, gf_ref, gate_ref, mlp_ref, dx_ref, dbr_ref, dgf_ref, dgate_ref, loss_ref, acc):
        @pl.when(pl.program_id(0) == 0)
        def _():
            dgf_ref[...] = jnp.zeros_like(dgf_ref)
            dgate_ref[...] = jnp.zeros_like(dgate_ref)
            loss_ref[...] = jnp.zeros_like(loss_ref)

        acc[...] = jnp.zeros_like(acc)
        gfv, gatev = gf_ref[...], gate_ref[...]
        row_loss = jnp.zeros((ROW_STRIP, 1), F32)
        for r0 in range(0, tr, ROW_STRIP):
            rows = slice(r0, r0 + ROW_STRIP)
            xv = x_ref[rows, :]
            r = _rms(xv)
            xn = xv * r
            err = xn * gfv - t_ref[rows, :]
            row_loss = row_loss + jnp.mean(err * err, axis=-1, keepdims=True)
            dy = err * (1.0 / d)
            acc[0] += _fold8(dy * xn)
            dxn = dy * gfv
            dx = r * (dxn - xn * jnp.mean(dxn * xn, axis=-1, keepdims=True))
            dx_ref[rows, :] = dx
            dbr_ref[rows, :] = (dx * gatev).astype(dbr_ref.dtype)
            acc[1] += _fold8(dx * mlp_ref[rows, :].astype(F32))
        loss_ref[...] += 0.5 * _colsum(row_loss)
        dgf_ref[...] += _colsum(acc[0])
        dgate_ref[...] += _colsum(acc[1])

    row = pl.BlockSpec((tr, d), lambda i: (i, 0))
    vec = pl.BlockSpec((1, d), lambda i: (0, 0))
    return _pallas(
        body, name="loss_head", grid=(s // tr,), in_specs=[row, row, vec, vec, row],
        scratch_shapes=[pltpu.VMEM((2, 8, d), F32)],
        out_specs=[row, row, vec, vec, pl.BlockSpec((1, 128), lambda i: (0, 0))],
        out_shape=[jax.ShapeDtypeStruct((s, d), F32), jax.ShapeDtypeStruct((s, d), BF16),
                   jax.ShapeDtypeStruct((1, d), F32), jax.ShapeDtypeStruct((1, d), F32),
                   jax.ShapeDtypeStruct((1, 128), F32)],
        compiler_params=_params("arbitrary"))(x3, target, gf, gate2, mlp)


def _norm_mod_bwd(name, dh, xin, g, scale, dx_up, branch=None, gate=None, tr=128, deps=()):
    s, d = xin.shape
    tr = _tile(tr, s)
    with_gate = branch is not None

    def body(*refs):
        dh_ref, x_ref, g_ref, sc_ref, up_ref = refs[:5]
        if with_gate:
            br_ref, gate_ref = refs[5:7]
            dx_ref, dsh_ref, dsc_ref, dg_ref, dbr_ref, dgate_ref, acc = refs[7:]
            sums = (dsh_ref, dsc_ref, dg_ref, dgate_ref)
        else:
            dx_ref, dsh_ref, dsc_ref, dg_ref, acc = refs[5:]
            sums = (dsh_ref, dsc_ref, dg_ref)

        @pl.when(pl.program_id(0) == 0)
        def _():
            for ref in sums:
                ref[...] = jnp.zeros_like(ref)

        acc[...] = jnp.zeros_like(acc)
        gv = g_ref[...]
        one_sc = 1.0 + sc_ref[...]
        for r0 in range(0, tr, ROW_STRIP):
            rows = slice(r0, r0 + ROW_STRIP)
            xv, dhv = x_ref[rows, :], dh_ref[rows, :].astype(F32)
            r = _rms(xv)
            xn = xv * r
            acc[0] += _fold8(dhv)
            acc[1] += _fold8(dhv * (xn * gv))
            acc[2] += _fold8(dhv * one_sc * xn)
            dxn = dhv * one_sc * gv
            dx = up_ref[rows, :] + r * (dxn - xn * jnp.mean(dxn * xn, axis=-1, keepdims=True))
            dx_ref[rows, :] = dx
            if with_gate:
                dbr_ref[rows, :] = (dx * gate_ref[...]).astype(dbr_ref.dtype)
                acc[3] += _fold8(dx * br_ref[rows, :].astype(F32))
        for i, ref in enumerate(sums):
            ref[...] += _colsum(acc[i])

    row = pl.BlockSpec((tr, d), lambda i: (i, 0))
    vec = pl.BlockSpec((1, d), lambda i: (0, 0))
    vshape = jax.ShapeDtypeStruct((1, d), F32)
    operands = [dh, xin, g, scale, dx_up]
    in_specs = [row, row, vec, vec, row]
    out_shape = [jax.ShapeDtypeStruct((s, d), F32), vshape, vshape, vshape]
    out_specs = [row, vec, vec, vec]
    if with_gate:
        operands += [branch, gate]
        in_specs += [row, vec]
        out_shape += [jax.ShapeDtypeStruct((s, d), BF16), vshape]
        out_specs += [row, vec]
    assert tr % ROW_STRIP == 0
    return _pallas(body, deps, name=name, grid=(s // tr,), in_specs=in_specs, out_specs=out_specs, out_shape=out_shape,
                   scratch_shapes=[pltpu.VMEM((4 if with_gate else 3, 8, d), F32)],
                   compiler_params=_params("arbitrary"))(*operands)


def _window_count(c0, rows, half, s):
    t = c0 + lax.broadcasted_iota(jnp.int32, (rows, 1), 0)
    return (jnp.minimum(t + half, s) - jnp.maximum(t - half, 0)).astype(F32)


def _zero_pads(pad, s):
    zeros = jnp.zeros((PAD_ROWS, pad.shape[1]), pad.dtype)
    pad[0:PAD_ROWS, :] = zeros
    pad[PAD_ROWS + s:PAD_ROWS + s + PAD_ROWS, :] = zeros


def _pool_fwd(proj, s, gd, cb, ch, deps=()):
    nsub = gd // cb

    def body(v_ref, o_ref, pad):
        g = pl.program_id(0)
        _zero_pads(pad, s)
        pad[PAD_ROWS:PAD_ROWS + s, :] = v_ref[...].astype(F32)
        for gi, window in enumerate(POOL_WINDOWS):
            half = window // 2

            @pl.when(g == gi)
            def _(half=half):
                for c0 in range(0, s, ch):
                    base = PAD_ROWS + c0
                    acc = pad[base - half:base - half + ch, :]
                    for j in range(-half + 1, half):
                        acc = acc + pad[base + j:base + j + ch, :]
                    out = acc / _window_count(c0, ch, half, s) - pad[base:base + ch, :]
                    o_ref[c0:c0 + ch, :] = out.astype(o_ref.dtype)

    spec = pl.BlockSpec((s, cb), lambda g, j: (0, g * nsub + j))
    return _pallas(body, deps, name="pool_fwd", grid=(N_POOL_GROUPS, nsub), in_specs=[spec], out_specs=spec,
                   out_shape=jax.ShapeDtypeStruct((s, N_POOL_GROUPS * gd), BF16),
                   scratch_shapes=[pltpu.VMEM((s + 2 * PAD_ROWS, cb), F32)],
                   compiler_params=_params("parallel", "parallel"))(proj)


def _pool_bwd(dpooled, dproj, s, gd, cb, ch):
    nsub = gd // cb

    def body(dp_ref, dproj_in, o_ref, pad):
        del dproj_in
        g = pl.program_id(0)
        _zero_pads(pad, s)
        for gi, window in enumerate(POOL_WINDOWS):
            half = window // 2

            @pl.when(g == gi)
            def _(half=half):
                for c0 in range(0, s, ch):
                    pad[PAD_ROWS + c0:PAD_ROWS + c0 + ch, :] = dp_ref[c0:c0 + ch, :] / _window_count(c0, ch, half, s)
                for c0 in range(0, s, ch):
                    base = PAD_ROWS + c0
                    acc = pad[base - half + 1:base - half + 1 + ch, :]
                    for j in range(-half + 2, half + 1):
                        acc = acc + pad[base + j:base + j + ch, :]
                    o_ref[c0:c0 + ch, :] = (acc - dp_ref[c0:c0 + ch, :]).astype(o_ref.dtype)

    spec = pl.BlockSpec((s, cb), lambda g, j: (0, g * nsub + j))
    return _pallas(body, name="pool_bwd", grid=(N_POOL_GROUPS, nsub),
                   in_specs=[spec, pl.BlockSpec(memory_space=pl.ANY)], out_specs=spec,
                   out_shape=jax.ShapeDtypeStruct(dproj.shape, dproj.dtype), input_output_aliases={1: 0},
                   scratch_shapes=[pltpu.VMEM((s + 2 * PAD_ROWS, cb), F32)],
                   compiler_params=_params("parallel", "parallel"))(dpooled, dproj)


def _poolmix_fwd(pooled, wmix, pool_scale, gnorm_g, d_model, tm=512):
    s = pooled.shape[0]
    gd = wmix.shape[1]
    tm = _tile(tm, s)

    def body(p_ref, w_ref, ps_ref, g_ref, apre_ref, mixed_ref):
        a_pre = jnp.dot(p_ref[...], w_ref[...], preferred_element_type=F32)
        apre_ref[...] = a_pre
        a_out = a_pre * ps_ref[...]
        mixed_ref[...] = ((a_out * _rms(a_out)) * g_ref[...]).astype(mixed_ref.dtype)

    blk = pl.BlockSpec((tm, gd), lambda g, i: (i, g))
    vec = pl.BlockSpec((1, gd), lambda g, i: (0, g))
    return _pallas(body, name="poolmix_fwd", grid=(N_POOL_GROUPS, s // tm),
                   in_specs=[blk, pl.BlockSpec((None, gd, gd), lambda g, i: (g, 0, 0)), vec, vec],
                   out_specs=[blk, blk],
                   out_shape=[jax.ShapeDtypeStruct((s, N_POOL_GROUPS * gd), F32), jax.ShapeDtypeStruct((s, d_model), BF16)],
                   compiler_params=_params("parallel", "parallel"))(pooled, wmix, pool_scale, gnorm_g)


def _poolmix_bwd(dmixed, a_pre, wmix, pool_scale, gnorm_g, tm=512):
    s = a_pre.shape[0]
    gd = wmix.shape[1]
    tm = _tile(tm, s)

    def body(dm_ref, apre_ref, w_ref, ps_ref, g_ref, dapre_ref, dpooled_ref, dps_ref, dg_ref):
        @pl.when(pl.program_id(1) == 0)
        def _():
            dps_ref[...] = jnp.zeros_like(dps_ref)
            dg_ref[...] = jnp.zeros_like(dg_ref)

        a_pre, dm, ps = apre_ref[...], dm_ref[...].astype(F32), ps_ref[...]
        a_out = a_pre * ps
        r = _rms(a_out)
        n = a_out * r
        dg_ref[...] += _colsum(dm * n)
        dn = dm * g_ref[...]
        da_out = r * (dn - n * jnp.mean(dn * n, axis=-1, keepdims=True))
        dps_ref[...] += _colsum(da_out * a_pre)
        da_pre = (da_out * ps).astype(BF16)
        dapre_ref[...] = da_pre
        dpooled_ref[...] = lax.dot_general(da_pre, w_ref[...], _DOT_DIMS["nt"], preferred_element_type=F32)

    blk = pl.BlockSpec((tm, gd), lambda g, i: (i, g))
    vec = pl.BlockSpec((1, gd), lambda g, i: (0, g))
    width = N_POOL_GROUPS * gd
    return _pallas(body, name="poolmix_bwd", grid=(N_POOL_GROUPS, s // tm),
                   in_specs=[blk, blk, pl.BlockSpec((None, gd, gd), lambda g, i: (g, 0, 0)), vec, vec],
                   out_specs=[blk, blk, vec, vec],
                   out_shape=[jax.ShapeDtypeStruct((s, width), BF16), jax.ShapeDtypeStruct((s, width), F32),
                              jax.ShapeDtypeStruct((1, width), F32), jax.ShapeDtypeStruct((1, width), F32)],
                   compiler_params=_params("parallel", "arbitrary"))(dmixed, a_pre, wmix, pool_scale, gnorm_g)


def _poolmix_wgrad(pooled, da_pre, gd, tk=1024):
    s = pooled.shape[0]
    tk = _tile(tk, s)
    nk = s // tk

    def body(p_ref, d_ref, o_ref, acc):
        k = pl.program_id(1)
        part = lax.dot_general(p_ref[...], d_ref[...], _DOT_DIMS["tn"], preferred_element_type=F32)

        @pl.when(k == 0)
        def _():
            acc[...] = part

        @pl.when(k > 0)
        def _():
            acc[...] += part

        @pl.when(k == nk - 1)
        def _():
            o_ref[...] = acc[...].astype(o_ref.dtype)

    blk = pl.BlockSpec((tk, gd), lambda g, k: (k, g))
    return _pallas(body, name="poolmix_wgrad", grid=(N_POOL_GROUPS, nk), in_specs=[blk, blk],
                   out_specs=pl.BlockSpec((None, gd, gd), lambda g, k: (g, 0, 0)),
                   out_shape=jax.ShapeDtypeStruct((N_POOL_GROUPS, gd, gd), BF16),
                   scratch_shapes=[pltpu.VMEM((gd, gd), F32)],
                   compiler_params=_params("parallel", "arbitrary"))(pooled, da_pre)


def _head_mean(v):
    parts = []
    for q in range(v.shape[1] // CONV_HEAD_DIM):
        m = jnp.mean(v[:, q * CONV_HEAD_DIM:(q + 1) * CONV_HEAD_DIM], axis=-1, keepdims=True)
        parts.append(jnp.broadcast_to(m, (v.shape[0], CONV_HEAD_DIM)))
    return parts[0] if len(parts) == 1 else jnp.concatenate(parts, axis=1)


def _conv_fwd(proj, mixed, conv_w, conv_b, gnorm_g, s, width, cb, ch, deps=()):
    nblk = width // cb

    def body(b_ref, c_ref, u_ref, w_ref, cb_ref, g_ref, mixed_in, o_ref, pad):
        del mixed_in
        _zero_pads(pad, s)
        pad[PAD_ROWS:PAD_ROWS + s, :] = c_ref[...].astype(F32) * u_ref[...].astype(F32)
        w = w_ref[...]
        for c0 in range(0, s, ch):
            base = PAD_ROWS + c0
            conv = (w[0:1] * pad[base - 1:base - 1 + ch, :] + w[1:2] * pad[base:base + ch, :]
                    + w[2:3] * pad[base + 1:base + 1 + ch, :] + cb_ref[...])
            bo = b_ref[c0:c0 + ch, :].astype(F32) * conv
            n = bo * lax.rsqrt(_head_mean(bo * bo) + EPS)
            o_ref[c0:c0 + ch, :] = (n * g_ref[...]).astype(o_ref.dtype)

    def part(p):
        return pl.BlockSpec((s, cb), lambda j: (0, p * nblk + j))

    vec = pl.BlockSpec((1, cb), lambda j: (0, j))
    return _pallas(body, deps, name="conv_fwd", grid=(nblk,),
                   in_specs=[part(1), part(2), part(3), pl.BlockSpec((3, cb), lambda j: (0, j)), vec, vec,
                             pl.BlockSpec(memory_space=pl.ANY)],
                   out_specs=part(1), out_shape=jax.ShapeDtypeStruct(mixed.shape, mixed.dtype),
                   input_output_aliases={6: 0},
                   scratch_shapes=[pltpu.VMEM((s + 2 * PAD_ROWS, cb), F32)],
                   compiler_params=_params("parallel"))(proj, proj, proj, conv_w, conv_b, gnorm_g, mixed)


def _conv_bwd(dmixed, proj, conv_w, conv_b, gnorm_g, s, width, cb, ch, deps=()):
    nblk = width // cb
    assert nblk >= 2

    def body(dm_ref, b_ref, c_ref, u_ref, w_ref, cb_ref, g_ref, dproj_ref, dw_ref, dcb_ref, dg_ref,
             pad_cu, pad_dconv, out_bufs, sems):
        j = pl.program_id(0)
        slot = j % 2

        def out_copies(step, slot_of_step):
            copies = []
            for p in range(3):
                col = pl.multiple_of((p + 1) * width + step * cb, CONV_HEAD_DIM)
                copies.append(pltpu.make_async_copy(out_bufs.at[slot_of_step, p], dproj_ref.at[:, pl.ds(col, cb)],
                                                    sems.at[slot_of_step * 3 + p]))
            return copies

        @pl.when(j >= 2)
        def _():
            for cp in out_copies(j - 2, slot):
                cp.wait()

        _zero_pads(pad_cu, s)
        _zero_pads(pad_dconv, s)
        pad_cu[PAD_ROWS:PAD_ROWS + s, :] = c_ref[...].astype(F32) * u_ref[...].astype(F32)
        w, gv = w_ref[...], g_ref[...]
        zero = jnp.zeros((1, cb), F32)
        dw0, dw1, dw2, dcb, dg = zero, zero, zero, zero, zero
        for c0 in range(0, s, ch):
            base = PAD_ROWS + c0
            cu_prev, cu_here, cu_next = (pad_cu[base - 1:base - 1 + ch, :], pad_cu[base:base + ch, :],
                                         pad_cu[base + 1:base + 1 + ch, :])
            conv = w[0:1] * cu_prev + w[1:2] * cu_here + w[2:3] * cu_next + cb_ref[...]
            bg = b_ref[c0:c0 + ch, :].astype(F32)
            bo = bg * conv
            r = lax.rsqrt(_head_mean(bo * bo) + EPS)
            n = bo * r
            dm = dm_ref[c0:c0 + ch, :].astype(F32)
            dg = dg + _colsum(dm * n)
            dn = dm * gv
            dbo = r * (dn - n * _head_mean(dn * n))
            out_bufs[slot, 0, c0:c0 + ch, :] = (dbo * conv).astype(BF16)
            dconv = dbo * bg
            pad_dconv[base:base + ch, :] = dconv
            dcb = dcb + _colsum(dconv)
            dw0 = dw0 + _colsum(dconv * cu_prev)
            dw1 = dw1 + _colsum(dconv * cu_here)
            dw2 = dw2 + _colsum(dconv * cu_next)
        dw_ref[0:1, :] = dw0
        dw_ref[1:2, :] = dw1
        dw_ref[2:3, :] = dw2
        dcb_ref[...] = dcb
        dg_ref[...] = dg
        for c0 in range(0, s, ch):
            base = PAD_ROWS + c0
            dcu = (w[0:1] * pad_dconv[base + 1:base + 1 + ch, :] + w[1:2] * pad_dconv[base:base + ch, :]
                   + w[2:3] * pad_dconv[base - 1:base - 1 + ch, :])
            out_bufs[slot, 1, c0:c0 + ch, :] = (dcu * u_ref[c0:c0 + ch, :].astype(F32)).astype(BF16)
            out_bufs[slot, 2, c0:c0 + ch, :] = (dcu * c_ref[c0:c0 + ch, :].astype(F32)).astype(BF16)
        for cp in out_copies(j, slot):
            cp.start()

        @pl.when(j == nblk - 1)
        def _():
            for cp in out_copies(j, slot) + out_copies(j - 1, 1 - slot):
                cp.wait()

    def part(p):
        return pl.BlockSpec((s, cb), lambda j: (0, p * nblk + j))

    vec = pl.BlockSpec((1, cb), lambda j: (0, j))
    w_spec = pl.BlockSpec((3, cb), lambda j: (0, j))
    return _pallas(body, deps, name="conv_bwd", grid=(nblk,),
                   in_specs=[part(1), part(1), part(2), part(3), w_spec, vec, vec],
                   out_specs=[pl.BlockSpec(memory_space=pl.ANY), w_spec, vec, vec],
                   out_shape=[jax.ShapeDtypeStruct((s, 4 * width), BF16), jax.ShapeDtypeStruct((3, width), F32),
                              jax.ShapeDtypeStruct((1, width), F32), jax.ShapeDtypeStruct((1, width), F32)],
                   scratch_shapes=[pltpu.VMEM((s + 2 * PAD_ROWS, cb), F32), pltpu.VMEM((s + 2 * PAD_ROWS, cb), F32),
                                   pltpu.VMEM((2, 3, s, cb), BF16), pltpu.SemaphoreType.DMA((6,))],
                   compiler_params=_params("arbitrary"))(dmixed, proj, proj, proj, conv_w, conv_b, gnorm_g)


def _adamw(w, g, m, v):
    m = ADAM_B1 * m + (1.0 - ADAM_B1) * g
    v = ADAM_B2 * v + (1.0 - ADAM_B2) * (g * g)
    m_hat = m / (1.0 - ADAM_B1 ** ADAM_STEP)
    v_hat = v / (1.0 - ADAM_B2 ** ADAM_STEP)
    delta = -ADAM_LR * (m_hat / (jnp.sqrt(v_hat) + ADAM_EPS) + ADAM_WD * w)
    return delta, m, v


def _ada_fwd(c_rows, w, b, tn=512):
    rows, d = c_rows.shape
    n = w.shape[1]
    tn = _tile(tn, n)

    def body(c_ref, w_ref, b_ref, o_ref):
        cv = c_ref[...]
        act = (cv * jax.nn.sigmoid(cv)).astype(BF16)
        o_ref[...] = jnp.dot(act, w_ref[...].astype(BF16), preferred_element_type=F32) + b_ref[...]

    return _pallas(body, name="ada_fwd", grid=(n // tn,),
                   in_specs=[pl.BlockSpec((rows, d), lambda j: (0, 0)), pl.BlockSpec((d, tn), lambda j: (0, j)),
                             pl.BlockSpec((1, tn), lambda j: (0, j))],
                   out_specs=pl.BlockSpec((rows, tn), lambda j: (0, j)),
                   out_shape=jax.ShapeDtypeStruct((rows, n), F32), compiler_params=_params("parallel"))(c_rows, w, b)


def _ada_bwd_adam(c_cols, dmod, w, m, v, tr=512, tn=1024):
    d, rows = c_cols.shape
    n = w.shape[1]
    tr, tn = _tile(tr, d), _tile(tn, n)

    def body(c_ref, dm_ref, w_ref, m_ref, v_ref, g_ref, dl_ref, nm_ref, nv_ref):
        cv = c_ref[...]
        act = (cv * jax.nn.sigmoid(cv)).astype(BF16)
        g = jnp.dot(act, dm_ref[...].astype(BF16), preferred_element_type=F32)
        g_ref[...] = g
        dl_ref[...], nm_ref[...], nv_ref[...] = _adamw(w_ref[...], g, m_ref[...], v_ref[...])

    blk = pl.BlockSpec((tr, tn), lambda i, j: (i, j))
    shape = jax.ShapeDtypeStruct((d, n), F32)
    return _pallas(body, name="ada_bwd_adam", grid=(d // tr, n // tn),
                   in_specs=[pl.BlockSpec((tr, rows), lambda i, j: (i, 0)), pl.BlockSpec((rows, tn), lambda i, j: (0, j)),
                             blk, blk, blk],
                   out_specs=[blk] * 4, out_shape=[shape] * 4,
                   compiler_params=_params("parallel", "parallel"))(c_cols, dmod, w, m, v)


def _reduce_adam(name, pieces, w, m, v, tr=256, tc=1024):
    r, c = w.shape
    tr, tc = _tile(tr, r), _tile(tc, c)

    def body(p_ref, w_ref, m_ref, v_ref, g_ref, dl_ref, nm_ref, nv_ref):
        g = p_ref[0].astype(F32)
        for j in range(1, N_DEV):
            g = g + p_ref[j].astype(F32)
        g_ref[...] = g
        dl_ref[...], nm_ref[...], nv_ref[...] = _adamw(w_ref[...], g, m_ref[...], v_ref[...])

    blk = pl.BlockSpec((tr, tc), lambda i, j: (i, j))
    shape = jax.ShapeDtypeStruct((r, c), F32)
    return _pallas(body, name=name, grid=(r // tr, c // tc),
                   in_specs=[pl.BlockSpec((N_DEV, tr, tc), lambda i, j: (0, i, j)), blk, blk, blk],
                   out_specs=[blk] * 4, out_shape=[shape] * 4,
                   compiler_params=_params("parallel", "parallel"))(pieces, w, m, v)


def _reduce_adam_chips(name, sums, land, w, m, v, tr=256, tc=1024):
    r, c = w.shape
    tr, tc = _tile(tr, r), _tile(tc, c)

    def body(s_ref, l_ref, w_ref, m_ref, v_ref, g_ref, dl_ref, nm_ref, nv_ref):
        g = s_ref[...].astype(F32)
        for k in range(3):
            g = g + l_ref[k].astype(F32)
        g_ref[...] = g
        dl_ref[...], nm_ref[...], nv_ref[...] = _adamw(w_ref[...], g, m_ref[...], v_ref[...])

    blk = pl.BlockSpec((tr, tc), lambda i, j: (i, j))
    shape = jax.ShapeDtypeStruct((r, c), F32)
    mine = pl.BlockSpec((None, tr, tc), lambda i, j: (2 * lax.axis_index("x") + lax.axis_index("y"), i, j))
    return _pallas(body, name=name, grid=(r // tr, c // tc),
                   in_specs=[mine, pl.BlockSpec((3, tr, tc), lambda i, j: (0, i, j)), blk, blk, blk],
                   out_specs=[blk] * 4, out_shape=[shape] * 4,
                   compiler_params=_params("parallel", "parallel"))(sums, land, w, m, v)


def _sum_devices(parts):
    n = parts.shape[1]

    def body(p_ref, o_ref):
        acc = p_ref[0:1, :]
        for j in range(1, N_DEV):
            acc = acc + p_ref[j:j + 1, :]
        o_ref[...] = acc

    return _pallas(body, name="sum_devices", out_shape=jax.ShapeDtypeStruct((1, n), F32),
                   compiler_params=pltpu.CompilerParams(vmem_limit_bytes=VMEM_LIMIT_BYTES))(parts)


def _adam_small(name, g, w, m, v):
    def body(g_ref, w_ref, m_ref, v_ref, dl_ref, nm_ref, nv_ref):
        dl_ref[...], nm_ref[...], nv_ref[...] = _adamw(w_ref[...], g_ref[...], m_ref[...], v_ref[...])

    shape = jax.ShapeDtypeStruct(w.shape, F32)
    return _pallas(body, name=name, out_shape=[shape] * 3,
                   compiler_params=pltpu.CompilerParams(vmem_limit_bytes=VMEM_LIMIT_BYTES))(g, w, m, v)


def kernel(x, c, w_ada, b_ada, norm1_g, w_in, pool_mix_w, pool_scale, conv_w, conv_b, gnorm_pool_g, gnorm_conv_g, w_out, norm2_g, w_mlp_in, w_mlp_out, final_g, loss_target, m_w_ada, m_b_ada, m_norm1_g, m_w_in, m_pool_mix_w, m_pool_scale, m_conv_w, m_conv_b, m_gnorm_pool_g, m_gnorm_conv_g, m_w_out, m_norm2_g, m_w_mlp_in, m_w_mlp_out, m_final_g, v_w_ada, v_b_ada, v_norm1_g, v_w_in, v_pool_mix_w, v_pool_scale, v_conv_w, v_conv_b, v_gnorm_pool_g, v_gnorm_conv_g, v_w_out, v_norm2_g, v_w_mlp_in, v_w_mlp_out, v_final_g):
    s, d = x.shape[1], x.shape[2]
    width = d // 2
    gd = width // N_POOL_GROUPS
    d_ff = w_mlp_in.shape[2] * N_DEV
    n_proj = w_in.shape[2] * N_DEV
    ada_cols = w_ada.shape[2]
    conv_cols = conv_w.shape[2]
    assert n_proj == 4 * width and ada_cols * N_DEV == N_MOD * d and d_ff % N_DEV == 0
    assert width % CONV_HEAD_DIM == 0 and s % 8 == 0
    seq_chunk = _tile(512, s)
    pool_cb = _tile(256, gd)
    conv_cb = CONV_HEAD_DIM

    me = 4 * lax.axis_index("x") + 2 * lax.axis_index("y") + lax.axis_index("c")
    x2d, target = x[0], loss_target[0]

    conv_w_all, c_all = _exchange("gather_small_weights", [conv_w[0], c], ["gather"] * 2)
    conv_w_full = jnp.transpose(conv_w_all, (1, 0, 2)).reshape(3, width)
    c_rows = jnp.concatenate([c_all.reshape(N_DEV, d), jnp.zeros((N_DEV, d), F32)], axis=0)

    b_mine = lax.dynamic_slice(b_ada, (0, me * ada_cols), (1, ada_cols))
    mod_part = _ada_fwd(c_rows, w_ada[0], b_mine)
    (mod_all,) = _exchange("scatter_mod", [mod_part[:N_DEV].reshape(N_DEV, 1, ada_cols)], ["a2a"])
    mod = mod_all.reshape(1, N_MOD * d)

    started, hopped, relayed = {}, {}, {}

    def gather_start(wname, wgt, deps):
        land = _landing(wgt[0].astype(BF16), me)
        started[wname] = _gather_start("gather_" + wname + "_start", land, deps)
        return started[wname][5]

    def gather_hop(wname, land, after):
        hopped[wname] = _gather_hop("gather_" + wname + "_hop", started[wname], land, after)
        return hopped[wname][3]

    def gather_relay(wname, after):
        relayed[wname] = _gather_relay("gather_" + wname + "_relay", started[wname], hopped[wname], after)
        return relayed[wname][3]

    def gather_wait(wname, land, after, local_waited=False):
        return _gather_wait("gather_" + wname + "_wait", started[wname], hopped[wname], relayed[wname], land, after,
                            local_waited)

    def chip():
        return 2 * lax.axis_index("x") + lax.axis_index("y")

    def local_piece(t):
        return 2 * chip() + t

    def same_core_piece(t):
        return 2 * ((chip() + 1 + t) % N_CHIP) + lax.axis_index("c")

    def other_core_piece(t):
        return 2 * ((chip() + 1 + t) % N_CHIP) + 1 - lax.axis_index("c")

    def routed_piece(t):
        first, second, diagonal = _routes(lax.axis_index("x"), lax.axis_index("y"), lax.axis_index("c"))
        return jnp.where(t == 0, _index_of(first), jnp.where(t == 1, _index_of(second), _index_of(diagonal)))

    tok_w_in = gather_start("w_in", w_in, (mod,))
    shift1, scale1, gate1, shift2, scale2, gate2 = [mod[:, i * d:(i + 1) * d] for i in range(N_MOD)]

    h1 = _norm_mod("norm1_fwd", x2d, norm1_g, scale1, shift1, deps=(tok_w_in,))
    proj_shape = [jax.ShapeDtypeStruct((s, n_proj), BF16)]
    w_in_local = _gather_wait_local("gather_w_in_local", started["w_in"], started["w_in"][4], h1)
    (proj,) = _mm_nn("in_proj_local", h1, w_in_local, n_proj, True, proj_shape, _store(BF16), pieces=(local_piece, 2))
    tok = gather_hop("w_in", w_in_local, proj)
    tok = gather_start("w_out", w_out, (tok,))
    mix_started = _push_start("gather_pool_mix_start", [pool_mix_w[0].astype(BF16)], ("gather",), me, deps=(tok,))
    (proj,) = _mm_nn("in_proj_first", h1, hopped["w_in"][2], n_proj, True, proj_shape, _store(BF16),
                     pieces=(routed_piece, 1), carry=(proj,), deps=(mix_started[-1],))
    tok = gather_relay("w_in", proj)
    tok = gather_hop("w_out", started["w_out"][4], tok)
    tok = gather_start("w_mlp_in", w_mlp_in, (tok,))
    (proj,) = _mm_nn("in_proj_same_core", h1, relayed["w_in"][2], n_proj, True, proj_shape, _store(BF16),
                     pieces=(lambda t: routed_piece(t + 1), 2), carry=(proj,), deps=(tok,))
    w_in_all = gather_wait("w_in", relayed["w_in"][2], proj, True)
    (proj,) = _mm_nn("in_proj", h1, w_in_all, n_proj, True, proj_shape, _store(BF16), pieces=(other_core_piece, 3),
                     carry=(proj,))
    pooled = _pool_fwd(proj, s, gd, pool_cb, seq_chunk)
    (wmix_all,) = _push_wait("gather_pool_mix_wait", mix_started, ("gather",), pooled)
    wmix_full = jnp.transpose(wmix_all, (1, 0, 2, 3)).reshape(N_POOL_GROUPS, gd, gd)
    a_pre, mixed = _poolmix_fwd(pooled, wmix_full, pool_scale, gnorm_pool_g, d)
    tok = gather_hop("w_mlp_in", started["w_mlp_in"][4], a_pre)
    tok = gather_start("w_mlp_out", w_mlp_out, (tok,))
    tok = gather_relay("w_out", tok)
    mixed = _conv_fwd(proj, mixed, conv_w_full, conv_b, gnorm_conv_g, s, width, conv_cb, seq_chunk, deps=(tok,))

    def residual_specs(tm, tn):
        return [pl.BlockSpec((tm, tn), lambda i, j, k: (i, j)), pl.BlockSpec((1, tn), lambda i, j, k: (0, j))]

    sd_f32 = jax.ShapeDtypeStruct((s, d), F32)
    w_out_full = gather_wait("w_out", relayed["w_out"][2], mixed).reshape(d, d)
    sd_bf16 = jax.ShapeDtypeStruct((s, d), BF16)
    attn, x_mid = _mm_nn("out_proj", mixed, w_out_full, d, False, [sd_bf16, sd_f32], _residual_epilogue,
                         extras=(x2d, gate1), extra_specs=residual_specs)
    h2 = _norm_mod("norm2_fwd", x_mid, norm2_g, scale2, shift2)
    sf_bf16 = [jax.ShapeDtypeStruct((s, d_ff), BF16)] * 2
    w1_local = _gather_wait_local("gather_w_mlp_in_local", started["w_mlp_in"], hopped["w_mlp_in"][2], h2)
    relu, hid = _mm_nn("mlp_in_local", h2, w1_local, d_ff, True, sf_bf16, _relu2_epilogue, pieces=(local_piece, 2))
    hopped["w_mlp_in"] = hopped["w_mlp_in"][:2] + (w1_local,) + hopped["w_mlp_in"][3:]
    tok = gather_relay("w_mlp_in", hid)
    relu, hid = _mm_nn("mlp_in_same_core", h2, relayed["w_mlp_in"][2], d_ff, True, sf_bf16, _relu2_epilogue,
                       pieces=(same_core_piece, 3), carry=(relu, hid), deps=(tok,))
    w1_all = gather_wait("w_mlp_in", relayed["w_mlp_in"][2], hid, True)
    tok = gather_hop("w_mlp_out", started["w_mlp_out"][4], w1_all)
    relu, hid = _mm_nn("mlp_in_other_core", h2, w1_all, d_ff, True, sf_bf16, _relu2_epilogue,
                       pieces=(other_core_piece, 2), carry=(relu, hid), deps=(tok,))
    tok = gather_relay("w_mlp_out", hid)
    relu, hid = _mm_nn("mlp_in", h2, w1_all, d_ff, True, sf_bf16, _relu2_epilogue,
                       pieces=(lambda t: other_core_piece(t + 2), 1), carry=(relu, hid), deps=(tok,))
    w2_full = gather_wait("w_mlp_out", relayed["w_mlp_out"][2], hid).reshape(d_ff, d)
    mlp, x_last = _mm_nn("mlp_out", hid, w2_full, d, False, [sd_bf16, sd_f32], _residual_epilogue,
                         extras=(x_mid, gate2), extra_specs=residual_specs)

    dx_last, dmlp, d_final_g, dgate2, loss_row = _loss_head(x_last, target, final_g.reshape(1, d), gate2, mlp)

    def relu_specs(tm, tn):
        return [pl.BlockSpec((tm, tn), lambda i, j, k: (i, j))]

    def reduce_start(wname, a, b, col_pieces, deps=()):
        far = _mm_tn_half(wname + "_dw_far", a, b, col_pieces, near=False, deps=deps)
        return _pair_start("scatter_" + wname + "_pair_start", far)

    def reduce_chips(wname, a, b, col_pieces, pairs, after):
        pair = _pair_wait("scatter_" + wname + "_pair_wait", pairs, after)
        sums = _mm_tn_half(wname + "_dw_near", a, b, col_pieces, near=True, pair=pair)
        return _chip_start("scatter_" + wname + "_chip_start", sums)

    pairs_w2 = reduce_start("mlp_out", hid, dmlp, False)
    (dhpre,) = _mm_nt("mlp_out_dx", dmlp, w2_full, d_ff, False, sf_bf16[:1], _relu2_bwd_epilogue,
                      extras=(relu,), extra_specs=relu_specs, deps=(pairs_w2[4],))
    chips_w2 = reduce_chips("mlp_out", hid, dmlp, False, pairs_w2, dhpre)
    pairs_w1 = reduce_start("mlp_in", h2, dhpre, True, deps=(chips_w2[4],))
    (dh2,) = _mm_nt("mlp_in_dx", dhpre, w1_all, d, True, [sd_bf16], _store(BF16), tn=1024, tk=2048,
                    deps=(pairs_w1[4],))
    chips_w1 = reduce_chips("mlp_in", h2, dhpre, True, pairs_w1, dh2)
    dx_mid, dshift2, dscale2, d_norm2_g, dattn, dgate1 = _norm_mod_bwd(
        "norm2_bwd", dh2, x_mid, norm2_g, scale2, dx_last, branch=attn, gate=gate1, deps=(chips_w1[4],))

    pairs_w_out = reduce_start("out_proj", mixed, dattn, False)
    (dmixed,) = _mm_nt("out_proj_dx", dattn, w_out_full, d, False, [sd_bf16], _store(BF16), deps=(pairs_w_out[4],))
    chips_w_out = reduce_chips("out_proj", mixed, dattn, False, pairs_w_out, dmixed)
    dproj, d_conv_w, d_conv_b, d_gnorm_conv = _conv_bwd(dmixed, proj, conv_w_full, conv_b, gnorm_conv_g,
                                                        s, width, conv_cb, seq_chunk, deps=(chips_w_out[4],))
    da_pre, dpooled, d_pool_scale, d_gnorm_pool = _poolmix_bwd(dmixed, a_pre, wmix_full, pool_scale, gnorm_pool_g)
    g_wmix = _poolmix_wgrad(pooled, da_pre, gd)
    dproj = _pool_bwd(dpooled, dproj, s, gd, pool_cb, seq_chunk)

    pairs_w_in = reduce_start("in_proj", h1, dproj, True)
    (dh1,) = _mm_nt("in_proj_dx", dproj, w_in_all, d, True, [sd_bf16], _store(BF16), deps=(pairs_w_in[4],))
    chips_w_in = reduce_chips("in_proj", h1, dproj, True, pairs_w_in, dh1)
    grad_x, dshift1, dscale1, d_norm1_g = _norm_mod_bwd("norm1_bwd", dh1, x2d, norm1_g, scale1, dx_mid,
                                                        deps=(chips_w_in[4],))

    rows_mix = gd // N_DEV
    g_wmix_split = jnp.transpose(g_wmix.reshape(N_POOL_GROUPS, N_DEV, rows_mix, gd), (1, 0, 2, 3))
    g_wmix_split = g_wmix_split.reshape(N_DEV, N_POOL_GROUPS * rows_mix, gd)
    loss_pad = jnp.concatenate([loss_row[:, :1], jnp.zeros((1, 127), F32)], axis=1)
    dmod = jnp.concatenate([dshift1, dscale1, dgate1, dshift2, dscale2, dgate2], axis=1)
    small = jnp.concatenate([dmod, d_norm1_g, d_pool_scale, d_conv_b, d_gnorm_pool, d_gnorm_conv, d_norm2_g,
                             d_final_g, d_conv_w.reshape(1, 3 * width), loss_pad], axis=1)
    small_started = _push_start("exchange_small_grads_start", [g_wmix_split, small], ("a2a", "gather"), me)

    sums, landed = _chip_wait("scatter_w_mlp_out_chip_wait", chips_w2, small_started[-1])
    out_w2 = _reduce_adam_chips("adam_w_mlp_out", sums, landed, w_mlp_out[0], m_w_mlp_out[0], v_w_mlp_out[0])
    sums, landed = _chip_wait("scatter_w_mlp_in_chip_wait", chips_w1, out_w2[0])
    out_w1 = _reduce_adam_chips("adam_w_mlp_in", sums, landed, w_mlp_in[0], m_w_mlp_in[0], v_w_mlp_in[0])
    sums, landed = _chip_wait("scatter_w_out_chip_wait", chips_w_out, out_w1[0])
    out_w_out = _reduce_adam_chips("adam_w_out", sums, landed, w_out[0], m_w_out[0], v_w_out[0])
    sums, landed = _chip_wait("scatter_w_in_chip_wait", chips_w_in, out_w_out[0])
    out_w_in = _reduce_adam_chips("adam_w_in", sums, landed, w_in[0], m_w_in[0], v_w_in[0])

    p_wmix, small_all = _push_wait("exchange_small_grads_wait", small_started, ("a2a", "gather"), out_w_in[0])
    mix_shape = (N_POOL_GROUPS * rows_mix, gd)
    out_wmix = _reduce_adam("adam_pool_mix", p_wmix, pool_mix_w.reshape(mix_shape), m_pool_mix_w.reshape(mix_shape),
                            v_pool_mix_w.reshape(mix_shape))
    out_wmix = [a.reshape(pool_mix_w.shape) for a in out_wmix]
    small_all = small_all.reshape(N_DEV, small.shape[1])
    small_sum = _sum_devices(small_all)

    n_rep = (N_MOD + 1) * d + 4 * width + 2 * d
    loss = small_sum[0, n_rep + 3 * width]
    rep_names_w = [b_ada, norm1_g, pool_scale, conv_b, gnorm_pool_g, gnorm_conv_g, norm2_g, final_g.reshape(1, d)]
    rep_names_m = [m_b_ada, m_norm1_g, m_pool_scale, m_conv_b, m_gnorm_pool_g, m_gnorm_conv_g, m_norm2_g,
                   m_final_g.reshape(1, d)]
    rep_names_v = [v_b_ada, v_norm1_g, v_pool_scale, v_conv_b, v_gnorm_pool_g, v_gnorm_conv_g, v_norm2_g,
                   v_final_g.reshape(1, d)]
    rep_grad = small_sum[:, :n_rep]
    rep_delta, rep_m, rep_v = _adam_small("adam_replicated", rep_grad, jnp.concatenate(rep_names_w, axis=1),
                                          jnp.concatenate(rep_names_m, axis=1), jnp.concatenate(rep_names_v, axis=1))

    def split_rep(vec):
        out, off = [], 0
        for wgt in rep_names_w:
            n = wgt.shape[1]
            out.append(vec[:, off:off + n])
            off += n
        out[-1] = out[-1].reshape(d)
        return out

    conv_grad_full = small_sum[:, n_rep:n_rep + 3 * width].reshape(3, width)
    g_conv_w = lax.dynamic_slice(conv_grad_full, (0, me * conv_cols), (3, conv_cols))
    g_conv_w8 = jnp.concatenate([g_conv_w, jnp.zeros((5, conv_cols), F32)], axis=0)

    def pad8(a):
        return jnp.concatenate([a[0], jnp.zeros((5, conv_cols), F32)], axis=0)

    conv_delta, conv_m, conv_v = _adam_small("adam_conv_w", g_conv_w8, pad8(conv_w), pad8(m_conv_w), pad8(v_conv_w))

    dmod_all = small_all[:, :N_MOD * d]
    dmod_mine = lax.dynamic_slice(dmod_all, (0, me * ada_cols), (N_DEV, ada_cols))
    dmod_rows = jnp.concatenate([dmod_mine, jnp.zeros((N_DEV, ada_cols), F32)], axis=0)
    out_ada = _ada_bwd_adam(jnp.transpose(c_rows), dmod_rows, w_ada[0], m_w_ada[0], v_w_ada[0])

    rep_all = [split_rep(rep_grad), split_rep(rep_delta), split_rep(rep_m), split_rep(rep_v)]
    conv_all = [g_conv_w[None], conv_delta[None, :3], conv_m[None, :3], conv_v[None, :3]]
    outs = [loss, grad_x[None]]
    for kind in range(4):
        b_ada_o, norm1_o, pool_scale_o, conv_b_o, gpool_o, gconv_o, norm2_o, final_o = rep_all[kind]
        outs += [out_ada[kind][None], b_ada_o, norm1_o, out_w_in[kind][None], out_wmix[kind], pool_scale_o,
                 conv_all[kind], conv_b_o, gpool_o, gconv_o, out_w_out[kind][None], norm2_o, out_w1[kind][None],
                 out_w2[kind][None], final_o]
    return tuple(outs)
```

```python
import jax
import jax.numpy as jnp
from jax import lax
from jax.experimental import pallas as pl
from jax.experimental.pallas import tpu as pltpu

F32 = jnp.float32
BF16 = jnp.bfloat16
MESH = pl.DeviceIdType.MESH

N_DEV = 8
N_MOD = 6
EPS = 1e-6
POOL_WINDOWS = (2, 4, 8, 16)
N_POOL_GROUPS = len(POOL_WINDOWS)
CONV_HEAD_DIM = 128
PAD_ROWS = 16

ADAM_LR = 0.001
ADAM_B1 = 0.9
ADAM_B2 = 0.999
ADAM_EPS = 1e-08
ADAM_WD = 0.01
ADAM_STEP = 10

VMEM_LIMIT_BYTES = 56 * 1024 * 1024
MM_TM, MM_TN, MM_TK = 1024, 512, 4096


def _pallas(body, deps=(), **kw):
    if not deps:
        return pl.pallas_call(body, **kw)
    n_in = len(kw["in_specs"])

    def with_deps(*refs):
        body(*refs[:n_in], *refs[n_in + len(deps):])

    kw["in_specs"] = list(kw["in_specs"]) + [pl.BlockSpec(memory_space=pl.ANY)] * len(deps)
    call = pl.pallas_call(with_deps, **kw)
    return lambda *operands: call(*operands, *deps)


def _params(*sem):
    return pltpu.CompilerParams(dimension_semantics=sem, vmem_limit_bytes=VMEM_LIMIT_BYTES)


def _tile(pref, dim):
    if dim <= pref:
        return dim
    for t in range(pref - pref % 128, 0, -128):
        if dim % t == 0:
            return t
    return dim


def _exchange(name, arrays, modes, deps=()):
    n = len(arrays)
    out_shape = []
    for a, mode in zip(arrays, modes):
        piece = a.shape if mode == "gather" else a.shape[1:]
        out_shape.append(jax.ShapeDtypeStruct((N_DEV,) + tuple(piece), a.dtype))

    def body(*refs):
        srcs, dsts = refs[:n], refs[n:2 * n]
        send_sems, recv_sems, local_sems = refs[2 * n:]
        x, y, c = lax.axis_index("x"), lax.axis_index("y"), lax.axis_index("c")
        me = 4 * x + 2 * y + c
        copies = []
        for i in range(n):
            gather = modes[i] == "gather"
            local = pltpu.make_async_copy(srcs[i] if gather else srcs[i].at[me], dsts[i].at[me], local_sems.at[i])
            local.start()
            copies.append(local)
            for k in range(1, N_DEV):
                kx, ky, kc = (k >> 2) & 1, (k >> 1) & 1, k & 1
                peer = (1 - x if kx else x, 1 - y if ky else y, 1 - c if kc else c)
                peer_idx = 4 * peer[0] + 2 * peer[1] + peer[2]
                remote = pltpu.make_async_remote_copy(
                    src_ref=srcs[i] if gather else srcs[i].at[peer_idx],
                    dst_ref=dsts[i].at[me],
                    send_sem=send_sems.at[i * (N_DEV - 1) + k - 1],
                    recv_sem=recv_sems.at[i * (N_DEV - 1) + k - 1],
                    device_id=peer, device_id_type=MESH)
                remote.start()
                copies.append(remote)
        for cp in copies:
            cp.wait()

    any_spec = pl.BlockSpec(memory_space=pl.ANY)
    return _pallas(
        body, deps, name=name, out_shape=out_shape,
        in_specs=[any_spec] * n, out_specs=[any_spec] * n,
        scratch_shapes=[pltpu.SemaphoreType.DMA((n * (N_DEV - 1),)),
                        pltpu.SemaphoreType.DMA((n * (N_DEV - 1),)),
                        pltpu.SemaphoreType.DMA((n,))],
    )(*arrays)


_HBM = pl.BlockSpec(memory_space=pltpu.HBM)
_SEM = pl.BlockSpec(memory_space=pltpu.SEMAPHORE)
_TOKEN = pl.BlockSpec(memory_space=pltpu.VMEM)
_EFFECT = pltpu.SideEffectType.DATAFLOW_SIDE_EFFECTING
N_CHIP = N_DEV // 2
_OTHER_CHIPS = (1, 2, 3)


def _place():
    x, y, c = lax.axis_index("x"), lax.axis_index("y"), lax.axis_index("c")
    return x, y, c, (x, y, 1 - c)


def _same_core_of(x, y, c, k):
    px = 1 - x if k & 2 else x
    py = 1 - y if k & 1 else y
    return (px, py, c), 2 * px + py


def _remote(src, dst, send_sem, recv_sem, device):
    return pltpu.make_async_remote_copy(src_ref=src, dst_ref=dst, send_sem=send_sem, recv_sem=recv_sem,
                                        device_id=device, device_id_type=MESH)


def _token_shape():
    return jax.ShapeDtypeStruct((8, 128), F32)


def _split_call(body, deps, name, operands, in_specs, out_shape, out_specs, aliases):
    return _pallas(body, deps, name=name, out_shape=out_shape, in_specs=in_specs, out_specs=out_specs,
                   input_output_aliases=aliases,
                   compiler_params=pltpu.CompilerParams(has_side_effects=_EFFECT))(*operands)


def _routes(x, y, c):
    first = (x + c - 2 * x * c, y + (1 - c) - 2 * y * (1 - c), c)
    second = (x + (1 - c) - 2 * x * (1 - c), y + c - 2 * y * c, c)
    return first, second, (1 - x, 1 - y, c)


def _index_of(device):
    return 4 * device[0] + 2 * device[1] + device[2]


def _gather_start(name, land, deps):
    def body(land_ref, send_sems, recv_first, recv_second, recv_d2d, land_thru, token):
        del land_thru
        x, y, c, sibling = _place()
        first, second, _ = _routes(x, y, c)
        mine = land_ref.at[4 * x + 2 * y + c]
        _remote(mine, mine, send_sems.at[0], recv_first.at[0], first).start()
        _remote(mine, mine, send_sems.at[1], recv_second.at[0], second).start()
        _remote(mine, mine, send_sems.at[2], recv_d2d.at[0], sibling).start()
        token[...] = jnp.zeros_like(token)

    one = pltpu.SemaphoreType.DMA((1,))
    return _split_call(
        body, deps, name, (pltpu.with_memory_space_constraint(land, pltpu.HBM),), (_HBM,),
        (pltpu.SemaphoreType.DMA((3,)), one, one, one, pltpu.HBM(land.shape, land.dtype), _token_shape()),
        (_SEM, _SEM, _SEM, _SEM, _HBM, _TOKEN), {0: 4})


def _gather_wait_local(name, started, land, after):
    recv_d2d = started[3]

    def body(land_ref, recv_d2d, after_ref, land_out):
        del after_ref, land_out
        x, y, c, sibling = _place()
        mine = land_ref.at[4 * x + 2 * y + c]
        _remote(mine, mine, recv_d2d.at[0], recv_d2d.at[0], sibling).wait_recv()

    return _split_call(
        body, (), name, (land, recv_d2d, after), (_HBM, _SEM, pl.BlockSpec(memory_space=pl.ANY)),
        (pltpu.HBM(land.shape, land.dtype),), (_HBM,), {0: 0})[0]


def _gather_hop(name, started, land, after):
    recv_first = started[1]

    def body(land_ref, recv_first, after_ref, send_hop, recv_hop, land_thru, token):
        del after_ref, land_thru
        x, y, c, _ = _place()
        first, second, _ = _routes(x, y, c)
        piece = land_ref.at[_index_of(first)]
        _remote(piece, piece, send_hop.at[0], recv_first.at[0], first).wait_recv()
        _remote(piece, piece, send_hop.at[0], recv_hop.at[0], second).start()
        token[...] = jnp.zeros_like(token)

    one = pltpu.SemaphoreType.DMA((1,))
    return _split_call(
        body, (), name, (land, recv_first, after), (_HBM, _SEM, pl.BlockSpec(memory_space=pl.ANY)),
        (one, one, pltpu.HBM(land.shape, land.dtype), _token_shape()), (_SEM, _SEM, _HBM, _TOKEN), {0: 2})


def _gather_relay(name, started, hopped, after):
    recv_second = started[2]
    _, recv_hop, land, _ = hopped

    def body(land_ref, recv_second, recv_hop, after_ref, send_fwd, recv_fwd, land_thru, token):
        del after_ref, land_thru
        token[...] = jnp.zeros_like(token)
        x, y, c, sibling = _place()
        mine = land_ref.at[4 * x + 2 * y + c]
        _remote(mine, mine, send_fwd.at[0], recv_second.at[0], sibling).wait_recv()
        _remote(mine, mine, send_fwd.at[0], recv_hop.at[0], sibling).wait_recv()
        for i, device in enumerate(_routes(x, y, c)):
            piece = land_ref.at[_index_of(device)]
            _remote(piece, piece, send_fwd.at[i], recv_fwd.at[i], sibling).start()

    return _split_call(
        body, (), name, (land, recv_second, recv_hop, after), (_HBM, _SEM, _SEM, pl.BlockSpec(memory_space=pl.ANY)),
        (pltpu.SemaphoreType.DMA((3,)), pltpu.SemaphoreType.DMA((3,)), pltpu.HBM(land.shape, land.dtype),
         _token_shape()),
        (_SEM, _SEM, _HBM, _TOKEN), {0: 2})


def _gather_wait(name, started, hopped, relayed, land, after, local_waited):
    send_sems, recv_d2d = started[0], started[3]
    send_hop = hopped[0]
    send_fwd, recv_fwd = relayed[0], relayed[1]

    def body(land_ref, send_sems, recv_d2d, send_hop, send_fwd, recv_fwd, after_ref, land_out):
        del after_ref, land_out
        x, y, c, sibling = _place()
        mine = land_ref.at[4 * x + 2 * y + c]
        for i in range(3):
            _remote(mine, mine, send_sems.at[i], recv_d2d.at[0], sibling).wait_send()
        _remote(mine, mine, send_hop.at[0], recv_d2d.at[0], sibling).wait_send()
        if not local_waited:
            _remote(mine, mine, send_sems.at[2], recv_d2d.at[0], sibling).wait_recv()
        for i in range(3):
            relay = _remote(mine, mine, send_fwd.at[i], recv_fwd.at[i], sibling)
            relay.wait_send()
            relay.wait_recv()

    return _split_call(
        body, (), name, (land, send_sems, recv_d2d, send_hop, send_fwd, recv_fwd, after),
        (_HBM, _SEM, _SEM, _SEM, _SEM, _SEM, pl.BlockSpec(memory_space=pl.ANY)),
        (pltpu.HBM(land.shape, land.dtype),), (_HBM,), {0: 0})[0]


def _push_start(name, srcs, modes, me, deps=()):
    n = len(srcs)
    lands = [_landing(src if mode == "gather" else lax.dynamic_index_in_dim(src, me, 0, keepdims=False), me)
             for src, mode in zip(srcs, modes)]

    def body(*refs):
        src_refs, land_refs, send_sems, recv_sems, token = refs[:n], refs[n:2 * n], refs[2 * n], refs[2 * n + 1], refs[-1]
        for cp in _push_copies(src_refs, land_refs, send_sems, recv_sems, modes):
            cp.start()
        token[...] = jnp.zeros_like(token)

    count = pltpu.SemaphoreType.DMA((n * (N_DEV - 1),))
    operands = [pltpu.with_memory_space_constraint(a, pltpu.HBM) for a in list(srcs) + lands]
    return _split_call(
        body, deps, name, operands, (_HBM,) * (2 * n),
        (count, count) + tuple(pltpu.HBM(a.shape, a.dtype) for a in operands) + (_token_shape(),),
        (_SEM, _SEM) + (_HBM,) * (2 * n) + (_TOKEN,), {i: 2 + i for i in range(2 * n)})


def _push_wait(name, started, modes, after):
    n = len(modes)
    send_sems, recv_sems = started[0], started[1]
    arrays = started[2:2 + 2 * n]

    def body(*refs):
        src_refs, land_refs, send_sems, recv_sems = refs[:n], refs[n:2 * n], refs[2 * n], refs[2 * n + 1]
        for cp in _push_copies(src_refs, land_refs, send_sems, recv_sems, modes):
            cp.wait_send()
            cp.wait_recv()

    return _split_call(
        body, (), name, tuple(arrays) + (send_sems, recv_sems, after),
        (_HBM,) * (2 * n) + (_SEM, _SEM, pl.BlockSpec(memory_space=pl.ANY)),
        tuple(pltpu.HBM(a.shape, a.dtype) for a in arrays), (_HBM,) * (2 * n), {i: i for i in range(2 * n)})[n:]


def _push_copies(src_refs, land_refs, send_sems, recv_sems, modes):
    x, y, c, _ = _place()
    me = 4 * x + 2 * y + c
    copies = []
    for i, mode in enumerate(modes):
        for k in range(1, N_DEV):
            peer = (1 - x if k & 4 else x, 1 - y if k & 2 else y, 1 - c if k & 1 else c)
            src = src_refs[i] if mode == "gather" else src_refs[i].at[_index_of(peer)]
            sem = i * (N_DEV - 1) + k - 1
            copies.append(_remote(src, land_refs[i].at[me], send_sems.at[sem], recv_sems.at[sem], peer))
    return copies


def _landing(own, me):
    land = lax.empty((N_DEV,) + own.shape, own.dtype)
    return lax.dynamic_update_slice(land, own[None], (me,) + (0,) * own.ndim)


def _pair_start(name, far, deps=()):
    pair = lax.empty(far.shape, far.dtype)

    def body(far_ref, pair_ref, send_sems, recv_sems, far_thru, pair_thru, token):
        del far_thru, pair_thru
        _remote(far_ref, pair_ref, send_sems.at[0], recv_sems.at[0], _place()[3]).start()
        token[...] = jnp.zeros_like(token)

    return _split_call(
        body, deps, name,
        (pltpu.with_memory_space_constraint(far, pltpu.HBM), pltpu.with_memory_space_constraint(pair, pltpu.HBM)),
        (_HBM, _HBM),
        (pltpu.SemaphoreType.DMA((1,)), pltpu.SemaphoreType.DMA((1,)),
         pltpu.HBM(far.shape, far.dtype), pltpu.HBM(pair.shape, pair.dtype), _token_shape()),
        (_SEM, _SEM, _HBM, _HBM, _TOKEN), {0: 2, 1: 3})


def _pair_wait(name, started, after):
    send_sems, recv_sems, far, pair, _ = started

    def body(far_ref, pair_ref, send_sems, recv_sems, after_ref, far_out, pair_out):
        del after_ref, far_out, pair_out
        cp = _remote(far_ref, pair_ref, send_sems.at[0], recv_sems.at[0], _place()[3])
        cp.wait_send()
        cp.wait_recv()

    return _split_call(
        body, (), name, (far, pair, send_sems, recv_sems, after),
        (_HBM, _HBM, _SEM, _SEM, pl.BlockSpec(memory_space=pl.ANY)),
        (pltpu.HBM(far.shape, far.dtype), pltpu.HBM(pair.shape, pair.dtype)), (_HBM, _HBM), {0: 0, 1: 1})[1]


def _chip_start(name, sums, deps=()):
    land = lax.empty((3,) + sums.shape[1:], sums.dtype)

    def body(s_ref, land_ref, send_sems, recv_sems, s_thru, land_thru, token):
        del s_thru, land_thru
        x, y, c, _ = _place()
        for k in _OTHER_CHIPS:
            peer, chip = _same_core_of(x, y, c, k)
            _remote(s_ref.at[chip], land_ref.at[k - 1], send_sems.at[k - 1], recv_sems.at[k - 1], peer).start()
        token[...] = jnp.zeros_like(token)

    return _split_call(
        body, deps, name,
        (pltpu.with_memory_space_constraint(sums, pltpu.HBM), pltpu.with_memory_space_constraint(land, pltpu.HBM)),
        (_HBM, _HBM),
        (pltpu.SemaphoreType.DMA((3,)), pltpu.SemaphoreType.DMA((3,)),
         pltpu.HBM(sums.shape, sums.dtype), pltpu.HBM(land.shape, land.dtype), _token_shape()),
        (_SEM, _SEM, _HBM, _HBM, _TOKEN), {0: 2, 1: 3})


def _chip_wait(name, started, after):
    send_sems, recv_sems, sums, land, _ = started

    def body(s_ref, land_ref, send_sems, recv_sems, after_ref, s_out, land_out):
        del after_ref, s_out, land_out
        x, y, c, _ = _place()
        for k in _OTHER_CHIPS:
            peer, chip = _same_core_of(x, y, c, k)
            cp = _remote(s_ref.at[chip], land_ref.at[k - 1], send_sems.at[k - 1], recv_sems.at[k - 1], peer)
            cp.wait_send()
            cp.wait_recv()

    return _split_call(
        body, (), name, (sums, land, send_sems, recv_sems, after),
        (_HBM, _HBM, _SEM, _SEM, pl.BlockSpec(memory_space=pl.ANY)),
        (pltpu.HBM(sums.shape, sums.dtype), pltpu.HBM(land.shape, land.dtype)), (_HBM, _HBM), {0: 0, 1: 1})


_DOT_DIMS = {"nn": (((1,), (0,)), ((), ())), "nt": (((1,), (1,)), ((), ())), "tn": (((0,), (0,)), ((), ()))}


def _matmul(name, mode, operands, in_specs, out_shape, out_specs, grid, acc_shape, epilogue, deps=(), carry=()):
    n_in, n_out, nk = len(operands), len(out_shape), grid[2]
    dims = _DOT_DIMS[mode]

    def body(*refs):
        a_ref, b_ref = refs[0], refs[1]
        extras, outs = refs[2:n_in], refs[n_in:n_in + n_out]
        b_val = b_ref[...]
        if b_val.ndim == 3:
            b_val = jnp.concatenate([b_val[g] for g in range(b_val.shape[0])], axis=1)
        part = lax.dot_general(a_ref[...], b_val, dims, preferred_element_type=F32)
        if nk == 1:
            epilogue(part, extras, outs)
            return
        acc = refs[-1]
        k = pl.program_id(2)

        @pl.when(k == 0)
        def _():
            acc[...] = part

        @pl.when(jnp.logical_and(k > 0, k < nk - 1))
        def _():
            acc[...] += part

        @pl.when(k == nk - 1)
        def _():
            epilogue(acc[...] + part, extras, outs)

    aliases = {n_in + len(deps) + i: i for i in range(len(carry))}
    return _pallas(body, tuple(deps) + tuple(carry), name=name, grid=grid, in_specs=in_specs, out_specs=out_specs,
                   out_shape=out_shape, input_output_aliases=aliases,
                   scratch_shapes=[pltpu.VMEM(acc_shape, F32)] if nk > 1 else [],
                   compiler_params=_params("parallel", "parallel", "arbitrary"))(*operands)


def _store(dtype):
    def epilogue(acc, extras, outs):
        outs[0][...] = acc.astype(dtype)
    return epilogue


def _residual_epilogue(acc, extras, outs):
    x_ref, gate_ref = extras
    outs[0][...] = acc.astype(outs[0].dtype)
    outs[1][...] = x_ref[...] + gate_ref[...] * acc


def _relu2_epilogue(acc, extras, outs):
    r = jnp.maximum(acc, 0.0)
    outs[0][...] = r.astype(outs[0].dtype)
    outs[1][...] = (r * r).astype(outs[1].dtype)


def _relu2_bwd_epilogue(acc, extras, outs):
    outs[0][...] = (acc * (2.0 * extras[0][...].astype(F32))).astype(outs[0].dtype)


def _no_extra_specs(tm, tn):
    return []


def _mm_nn(name, a, b, n_total, b_split, out_shape, epilogue, extras=(), extra_specs=_no_extra_specs, tm=MM_TM, tn=MM_TN, tk=MM_TK,
           deps=(), pieces=None, carry=()):
    m, kdim = a.shape
    tm, tk = _tile(tm, m), _tile(tk, kdim)
    n_blocks = None
    if b_split:
        piece = b.shape[2]
        tn = _tile(tn, piece)
        per = piece // tn
        if pieces is None:
            b_spec = pl.BlockSpec((None, tk, tn), lambda i, j, k: (j // per, k, j % per))
            out_spec = pl.BlockSpec((tm, tn), lambda i, j, k: (i, j))
        else:
            piece_of, count = pieces
            n_blocks = count * per

            def which(j):
                return piece_of(j // per)

            b_spec = pl.BlockSpec((None, tk, tn), lambda i, j, k: (which(j), k, j % per))
            out_spec = pl.BlockSpec((tm, tn), lambda i, j, k: (i, which(j) * per + j % per))
    else:
        tn = _tile(tn, n_total)
        b_spec = pl.BlockSpec((tk, tn), lambda i, j, k: (k, j))
        out_spec = pl.BlockSpec((tm, tn), lambda i, j, k: (i, j))
    if n_blocks is None:
        n_blocks = n_total // tn
    in_specs = [pl.BlockSpec((tm, tk), lambda i, j, k: (i, k)), b_spec] + list(extra_specs(tm, tn))
    return _matmul(name, "nn", (a, b) + tuple(extras), in_specs, out_shape, [out_spec] * len(out_shape),
                   (m // tm, n_blocks, kdim // tk), (tm, tn), epilogue, deps, carry)


def _mm_nt(name, a, b, n_total, b_split, out_shape, epilogue, extras=(), extra_specs=_no_extra_specs, tm=MM_TM, tn=MM_TN, tk=MM_TK,
           deps=()):
    m, kdim = a.shape
    tm, tn = _tile(tm, m), _tile(tn, n_total)
    if b_split and tk >= 2 * b.shape[2]:
        piece = b.shape[2]
        group = _tile(tk, kdim) // piece
        tk = group * piece
        b_spec = pl.BlockSpec((group, tn, piece), lambda i, j, k: (k, j, 0))
    elif b_split:
        piece = b.shape[2]
        tk = _tile(tk, piece)
        per = piece // tk
        b_spec = pl.BlockSpec((None, tn, tk), lambda i, j, k: (k // per, j, k % per))
    else:
        tk = _tile(tk, kdim)
        b_spec = pl.BlockSpec((tn, tk), lambda i, j, k: (j, k))
    in_specs = [pl.BlockSpec((tm, tk), lambda i, j, k: (i, k)), b_spec] + list(extra_specs(tm, tn))
    out_specs = [pl.BlockSpec((tm, tn), lambda i, j, k: (i, j)) for _ in out_shape]
    return _matmul(name, "nt", (a, b) + tuple(extras), in_specs, out_shape, out_specs,
                   (m // tm, n_total // tn, kdim // tk), (tm, tn), epilogue, deps)


def _add_pair_epilogue(acc, extras, outs):
    outs[0][...] = (acc + extras[0][...].astype(F32)).astype(outs[0].dtype)


def _mm_tn_half(name, a, b, col_pieces, near, pair=None, tm=MM_TM, tn=MM_TN, tk=MM_TK, deps=()):
    kdim, m = a.shape
    n_total = b.shape[1]
    tk = _tile(tk, kdim)

    def core():
        c = lax.axis_index("c")
        return c if near else 1 - c

    if col_pieces:
        piece = n_total // N_DEV
        tm, tn = _tile(tm, m), _tile(tn, piece)
        per = piece // tn
        grid = (m // tm, N_CHIP * per, kdim // tk)
        a_spec = pl.BlockSpec((tk, tm), lambda i, j, k: (k, i))
        b_spec = pl.BlockSpec((tk, tn), lambda i, j, k: (k, (2 * (j // per) + core()) * per + j % per))
        out_spec = pl.BlockSpec((None, tm, tn), lambda i, j, k: (j // per, i, j % per))
        out_shape = [jax.ShapeDtypeStruct((N_CHIP, m, piece), BF16)]
    else:
        piece = m // N_DEV
        tm, tn = _tile(tm, piece), _tile(tn, n_total)
        per = piece // tm
        grid = (N_CHIP * per, n_total // tn, kdim // tk)
        a_spec = pl.BlockSpec((tk, tm), lambda i, j, k: (k, (2 * (i // per) + core()) * per + i % per))
        b_spec = pl.BlockSpec((tk, tn), lambda i, j, k: (k, j))
        out_spec = pl.BlockSpec((None, tm, tn), lambda i, j, k: (i // per, i % per, j))
        out_shape = [jax.ShapeDtypeStruct((N_CHIP, piece, n_total), BF16)]
    operands, in_specs, epilogue = (a, b), [a_spec, b_spec], _store(BF16)
    if pair is not None:
        operands, in_specs, epilogue = (a, b, pair), [a_spec, b_spec, out_spec], _add_pair_epilogue
    return _matmul(name, "tn", operands, in_specs, out_shape, [out_spec], grid, (tm, tn), epilogue, deps)[0]


def _rms(xv):
    return lax.rsqrt(jnp.mean(xv * xv, axis=-1, keepdims=True) + EPS)


def _colsum(v):
    return jnp.sum(v, axis=0, keepdims=True)


def _norm_mod(name, x, g, scale, shift, tr=256, deps=()):
    s, d = x.shape
    tr = _tile(tr, s)

    def body(x_ref, g_ref, sc_ref, sh_ref, h_ref):
        xv = x_ref[...]
        h = (xv * _rms(xv)) * g_ref[...]
        h_ref[...] = (h * (1.0 + sc_ref[...]) + sh_ref[...]).astype(h_ref.dtype)

    row = pl.BlockSpec((tr, d), lambda i: (i, 0))
    vec = pl.BlockSpec((1, d), lambda i: (0, 0))
    return _pallas(body, deps, name=name, grid=(s // tr,), in_specs=[row, vec, vec, vec], out_specs=row,
                   out_shape=jax.ShapeDtypeStruct((s, d), BF16), compiler_params=_params("parallel"))(x, g, scale, shift)


def _loss_head(x3, target, gf, gate2, mlp, tr=128):
    s, d = x3.shape
    tr = _tile(tr, s)

    def body(x_ref, t_ref, gf_ref, gate_ref, mlp_ref, dx_ref, dbr_ref, dgf_ref, dgate_ref, loss_ref):
        @pl.when(pl.program_id(0) == 0)
        def _():
            dgf_ref[...] = jnp.zeros_like(dgf_ref)
            dgate_ref[...] = jnp.zeros_like(dgate_ref)
            loss_ref[...] = jnp.zeros_like(loss_ref)

        xv = x_ref[...]
        r = _rms(xv)
        xn = xv * r
        gfv = gf_ref[...]
        err = xn * gfv - t_ref[...]
        loss_ref[...] += 0.5 * _colsum(jnp.mean(err * err, axis=-1, keepdims=True))
        dy = err * (1.0 / d)
        dgf_ref[...] += _colsum(dy * xn)
        dxn = dy * gfv
        dx = r * (dxn - xn * jnp.mean(dxn * xn, axis=-1, keepdims=True))
        dx_ref[...] = dx
        dbr_ref[...] = (dx * gate_ref[...]).astype(dbr_ref.dtype)
        dgate_ref[...] += _colsum(dx * mlp_ref[...].astype(F32))

    row = pl.BlockSpec((tr, d), lambda i: (i, 0))
    vec = pl.BlockSpec((1, d), lambda i: (0, 0))
    return _pallas(
        body, name="loss_head", grid=(s // tr,), in_specs=[row, row, vec, vec, row],
        out_specs=[row, row, vec, vec, pl.BlockSpec((1, 128), lambda i: (0, 0))],
        out_shape=[jax.ShapeDtypeStruct((s, d), F32), jax.ShapeDtypeStruct((s, d), BF16),
                   jax.ShapeDtypeStruct((1, d), F32), jax.ShapeDtypeStruct((1, d), F32),
                   jax.ShapeDtypeStruct((1, 128), F32)],
        compiler_params=_params("arbitrary"))(x3, target, gf, gate2, mlp)


def _norm_mod_bwd(name, dh, xin, g, scale, dx_up, branch=None, gate=None, tr=128, deps=()):
    s, d = xin.shape
    tr = _tile(tr, s)
    with_gate = branch is not None

    def body(*refs):
        dh_ref, x_ref, g_ref, sc_ref, up_ref = refs[:5]
        if with_gate:
            br_ref, gate_ref = refs[5:7]
            dx_ref, dsh_ref, dsc_ref, dg_ref, dbr_ref, dgate_ref = refs[7:]
            sums = (dsh_ref, dsc_ref, dg_ref, dgate_ref)
        else:
            dx_ref, dsh_ref, dsc_ref, dg_ref = refs[5:]
            sums = (dsh_ref, dsc_ref, dg_ref)

        @pl.when(pl.program_id(0) == 0)
        def _():
            for ref in sums:
                ref[...] = jnp.zeros_like(ref)

        xv, dhv, gv = x_ref[...], dh_ref[...].astype(F32), g_ref[...]
        r = _rms(xv)
        xn = xv * r
        one_sc = 1.0 + sc_ref[...]
        dsh_ref[...] += _colsum(dhv)
        dsc_ref[...] += _colsum(dhv * (xn * gv))
        dg_ref[...] += _colsum(dhv * one_sc * xn)
        dxn = dhv * one_sc * gv
        dx = up_ref[...] + r * (dxn - xn * jnp.mean(dxn * xn, axis=-1, keepdims=True))
        dx_ref[...] = dx
        if with_gate:
            dbr_ref[...] = (dx * gate_ref[...]).astype(dbr_ref.dtype)
            dgate_ref[...] += _colsum(dx * br_ref[...].astype(F32))

    row = pl.BlockSpec((tr, d), lambda i: (i, 0))
    vec = pl.BlockSpec((1, d), lambda i: (0, 0))
    vshape = jax.ShapeDtypeStruct((1, d), F32)
    operands = [dh, xin, g, scale, dx_up]
    in_specs = [row, row, vec, vec, row]
    out_shape = [jax.ShapeDtypeStruct((s, d), F32), vshape, vshape, vshape]
    out_specs = [row, vec, vec, vec]
    if with_gate:
        operands += [branch, gate]
        in_specs += [row, vec]
        out_shape += [jax.ShapeDtypeStruct((s, d), BF16), vshape]
        out_specs += [row, vec]
    return _pallas(body, deps, name=name, grid=(s // tr,), in_specs=in_specs, out_specs=out_specs, out_shape=out_shape,
                   compiler_params=_params("arbitrary"))(*operands)


def _window_count(c0, rows, half, s):
    t = c0 + lax.broadcasted_iota(jnp.int32, (rows, 1), 0)
    return (jnp.minimum(t + half, s) - jnp.maximum(t - half, 0)).astype(F32)


def _zero_pads(pad, s):
    zeros = jnp.zeros((PAD_ROWS, pad.shape[1]), pad.dtype)
    pad[0:PAD_ROWS, :] = zeros
    pad[PAD_ROWS + s:PAD_ROWS + s + PAD_ROWS, :] = zeros


def _pool_fwd(proj, s, gd, cb, ch, deps=()):
    nsub = gd // cb

    def body(v_ref, o_ref, pad):
        g = pl.program_id(0)
        _zero_pads(pad, s)
        pad[PAD_ROWS:PAD_ROWS + s, :] = v_ref[...].astype(F32)
        for gi, window in enumerate(POOL_WINDOWS):
            half = window // 2

            @pl.when(g == gi)
            def _(half=half):
                for c0 in range(0, s, ch):
                    base = PAD_ROWS + c0
                    acc = pad[base - half:base - half + ch, :]
                    for j in range(-half + 1, half):
                        acc = acc + pad[base + j:base + j + ch, :]
                    out = acc / _window_count(c0, ch, half, s) - pad[base:base + ch, :]
                    o_ref[c0:c0 + ch, :] = out.astype(o_ref.dtype)

    spec = pl.BlockSpec((s, cb), lambda g, j: (0, g * nsub + j))
    return _pallas(body, deps, name="pool_fwd", grid=(N_POOL_GROUPS, nsub), in_specs=[spec], out_specs=spec,
                   out_shape=jax.ShapeDtypeStruct((s, N_POOL_GROUPS * gd), BF16),
                   scratch_shapes=[pltpu.VMEM((s + 2 * PAD_ROWS, cb), F32)],
                   compiler_params=_params("parallel", "parallel"))(proj)


def _pool_bwd(dpooled, dproj, s, gd, cb, ch):
    nsub = gd // cb

    def body(dp_ref, dproj_in, o_ref, pad):
        del dproj_in
        g = pl.program_id(0)
        _zero_pads(pad, s)
        for gi, window in enumerate(POOL_WINDOWS):
            half = window // 2

            @pl.when(g == gi)
            def _(half=half):
                for c0 in range(0, s, ch):
                    pad[PAD_ROWS + c0:PAD_ROWS + c0 + ch, :] = dp_ref[c0:c0 + ch, :] / _window_count(c0, ch, half, s)
                for c0 in range(0, s, ch):
                    base = PAD_ROWS + c0
                    acc = pad[base - half + 1:base - half + 1 + ch, :]
                    for j in range(-half + 2, half + 1):
                        acc = acc + pad[base + j:base + j + ch, :]
                    o_ref[c0:c0 + ch, :] = (acc - dp_ref[c0:c0 + ch, :]).astype(o_ref.dtype)

    spec = pl.BlockSpec((s, cb), lambda g, j: (0, g * nsub + j))
    return _pallas(body, name="pool_bwd", grid=(N_POOL_GROUPS, nsub),
                   in_specs=[spec, pl.BlockSpec(memory_space=pl.ANY)], out_specs=spec,
                   out_shape=jax.ShapeDtypeStruct(dproj.shape, dproj.dtype), input_output_aliases={1: 0},
                   scratch_shapes=[pltpu.VMEM((s + 2 * PAD_ROWS, cb), F32)],
                   compiler_params=_params("parallel", "parallel"))(dpooled, dproj)


def _poolmix_fwd(pooled, wmix, pool_scale, gnorm_g, d_model, tm=512):
    s = pooled.shape[0]
    gd = wmix.shape[1]
    tm = _tile(tm, s)

    def body(p_ref, w_ref, ps_ref, g_ref, apre_ref, mixed_ref):
        a_pre = jnp.dot(p_ref[...], w_ref[...], preferred_element_type=F32)
        apre_ref[...] = a_pre
        a_out = a_pre * ps_ref[...]
        mixed_ref[...] = ((a_out * _rms(a_out)) * g_ref[...]).astype(mixed_ref.dtype)

    blk = pl.BlockSpec((tm, gd), lambda g, i: (i, g))
    vec = pl.BlockSpec((1, gd), lambda g, i: (0, g))
    return _pallas(body, name="poolmix_fwd", grid=(N_POOL_GROUPS, s // tm),
                   in_specs=[blk, pl.BlockSpec((None, gd, gd), lambda g, i: (g, 0, 0)), vec, vec],
                   out_specs=[blk, blk],
                   out_shape=[jax.ShapeDtypeStruct((s, N_POOL_GROUPS * gd), F32), jax.ShapeDtypeStruct((s, d_model), BF16)],
                   compiler_params=_params("parallel", "parallel"))(pooled, wmix, pool_scale, gnorm_g)


def _poolmix_bwd(dmixed, a_pre, wmix, pool_scale, gnorm_g, tm=512):
    s = a_pre.shape[0]
    gd = wmix.shape[1]
    tm = _tile(tm, s)

    def body(dm_ref, apre_ref, w_ref, ps_ref, g_ref, dapre_ref, dpooled_ref, dps_ref, dg_ref):
        @pl.when(pl.program_id(1) == 0)
        def _():
            dps_ref[...] = jnp.zeros_like(dps_ref)
            dg_ref[...] = jnp.zeros_like(dg_ref)

        a_pre, dm, ps = apre_ref[...], dm_ref[...].astype(F32), ps_ref[...]
        a_out = a_pre * ps
        r = _rms(a_out)
        n = a_out * r
        dg_ref[...] += _colsum(dm * n)
        dn = dm * g_ref[...]
        da_out = r * (dn - n * jnp.mean(dn * n, axis=-1, keepdims=True))
        dps_ref[...] += _colsum(da_out * a_pre)
        da_pre = (da_out * ps).astype(BF16)
        dapre_ref[...] = da_pre
        dpooled_ref[...] = lax.dot_general(da_pre, w_ref[...], _DOT_DIMS["nt"], preferred_element_type=F32)

    blk = pl.BlockSpec((tm, gd), lambda g, i: (i, g))
    vec = pl.BlockSpec((1, gd), lambda g, i: (0, g))
    width = N_POOL_GROUPS * gd
    return _pallas(body, name="poolmix_bwd", grid=(N_POOL_GROUPS, s // tm),
                   in_specs=[blk, blk, pl.BlockSpec((None, gd, gd), lambda g, i: (g, 0, 0)), vec, vec],
                   out_specs=[blk, blk, vec, vec],
                   out_shape=[jax.ShapeDtypeStruct((s, width), BF16), jax.ShapeDtypeStruct((s, width), F32),
                              jax.ShapeDtypeStruct((1, width), F32), jax.ShapeDtypeStruct((1, width), F32)],
                   compiler_params=_params("parallel", "arbitrary"))(dmixed, a_pre, wmix, pool_scale, gnorm_g)


def _poolmix_wgrad(pooled, da_pre, gd, tk=1024):
    s = pooled.shape[0]
    tk = _tile(tk, s)
    nk = s // tk

    def body(p_ref, d_ref, o_ref, acc):
        k = pl.program_id(1)
        part = lax.dot_general(p_ref[...], d_ref[...], _DOT_DIMS["tn"], preferred_element_type=F32)

        @pl.when(k == 0)
        def _():
            acc[...] = part

        @pl.when(k > 0)
        def _():
            acc[...] += part

        @pl.when(k == nk - 1)
        def _():
            o_ref[...] = acc[...].astype(o_ref.dtype)

    blk = pl.BlockSpec((tk, gd), lambda g, k: (k, g))
    return _pallas(body, name="poolmix_wgrad", grid=(N_POOL_GROUPS, nk), in_specs=[blk, blk],
                   out_specs=pl.BlockSpec((None, gd, gd), lambda g, k: (g, 0, 0)),
                   out_shape=jax.ShapeDtypeStruct((N_POOL_GROUPS, gd, gd), BF16),
                   scratch_shapes=[pltpu.VMEM((gd, gd), F32)],
                   compiler_params=_params("parallel", "arbitrary"))(pooled, da_pre)


def _head_mean(v):
    parts = []
    for q in range(v.shape[1] // CONV_HEAD_DIM):
        m = jnp.mean(v[:, q * CONV_HEAD_DIM:(q + 1) * CONV_HEAD_DIM], axis=-1, keepdims=True)
        parts.append(jnp.broadcast_to(m, (v.shape[0], CONV_HEAD_DIM)))
    return parts[0] if len(parts) == 1 else jnp.concatenate(parts, axis=1)


def _conv_fwd(proj, mixed, conv_w, conv_b, gnorm_g, s, width, cb, ch, deps=()):
    nblk = width // cb

    def body(b_ref, c_ref, u_ref, w_ref, cb_ref, g_ref, mixed_in, o_ref, pad):
        del mixed_in
        _zero_pads(pad, s)
        pad[PAD_ROWS:PAD_ROWS + s, :] = c_ref[...].astype(F32) * u_ref[...].astype(F32)
        w = w_ref[...]
        for c0 in range(0, s, ch):
            base = PAD_ROWS + c0
            conv = (w[0:1] * pad[base - 1:base - 1 + ch, :] + w[1:2] * pad[base:base + ch, :]
                    + w[2:3] * pad[base + 1:base + 1 + ch, :] + cb_ref[...])
            bo = b_ref[c0:c0 + ch, :].astype(F32) * conv
            n = bo * lax.rsqrt(_head_mean(bo * bo) + EPS)
            o_ref[c0:c0 + ch, :] = (n * g_ref[...]).astype(o_ref.dtype)

    def part(p):
        return pl.BlockSpec((s, cb), lambda j: (0, p * nblk + j))

    vec = pl.BlockSpec((1, cb), lambda j: (0, j))
    return _pallas(body, deps, name="conv_fwd", grid=(nblk,),
                   in_specs=[part(1), part(2), part(3), pl.BlockSpec((3, cb), lambda j: (0, j)), vec, vec,
                             pl.BlockSpec(memory_space=pl.ANY)],
                   out_specs=part(1), out_shape=jax.ShapeDtypeStruct(mixed.shape, mixed.dtype),
                   input_output_aliases={6: 0},
                   scratch_shapes=[pltpu.VMEM((s + 2 * PAD_ROWS, cb), F32)],
                   compiler_params=_params("parallel"))(proj, proj, proj, conv_w, conv_b, gnorm_g, mixed)


def _conv_bwd(dmixed, proj, conv_w, conv_b, gnorm_g, s, width, cb, ch, deps=()):
    nblk = width // cb
    assert nblk >= 2

    def body(dm_ref, b_ref, c_ref, u_ref, w_ref, cb_ref, g_ref, dproj_ref, dw_ref, dcb_ref, dg_ref,
             pad_cu, pad_dconv, out_bufs, sems):
        j = pl.program_id(0)
        slot = j % 2

        def out_copies(step, slot_of_step):
            copies = []
            for p in range(3):
                col = pl.multiple_of((p + 1) * width + step * cb, CONV_HEAD_DIM)
                copies.append(pltpu.make_async_copy(out_bufs.at[slot_of_step, p], dproj_ref.at[:, pl.ds(col, cb)],
                                                    sems.at[slot_of_step * 3 + p]))
            return copies

        @pl.when(j >= 2)
        def _():
            for cp in out_copies(j - 2, slot):
                cp.wait()

        _zero_pads(pad_cu, s)
        _zero_pads(pad_dconv, s)
        pad_cu[PAD_ROWS:PAD_ROWS + s, :] = c_ref[...].astype(F32) * u_ref[...].astype(F32)
        w, gv = w_ref[...], g_ref[...]
        zero = jnp.zeros((1, cb), F32)
        dw0, dw1, dw2, dcb, dg = zero, zero, zero, zero, zero
        for c0 in range(0, s, ch):
            base = PAD_ROWS + c0
            cu_prev, cu_here, cu_next = (pad_cu[base - 1:base - 1 + ch, :], pad_cu[base:base + ch, :],
                                         pad_cu[base + 1:base + 1 + ch, :])
            conv = w[0:1] * cu_prev + w[1:2] * cu_here + w[2:3] * cu_next + cb_ref[...]
            bg = b_ref[c0:c0 + ch, :].astype(F32)
            bo = bg * conv
            r = lax.rsqrt(_head_mean(bo * bo) + EPS)
            n = bo * r
            dm = dm_ref[c0:c0 + ch, :].astype(F32)
            dg = dg + _colsum(dm * n)
            dn = dm * gv
            dbo = r * (dn - n * _head_mean(dn * n))
            out_bufs[slot, 0, c0:c0 + ch, :] = (dbo * conv).astype(BF16)
            dconv = dbo * bg
            pad_dconv[base:base + ch, :] = dconv
            dcb = dcb + _colsum(dconv)
            dw0 = dw0 + _colsum(dconv * cu_prev)
            dw1 = dw1 + _colsum(dconv * cu_here)
            dw2 = dw2 + _colsum(dconv * cu_next)
        dw_ref[0:1, :] = dw0
        dw_ref[1:2, :] = dw1
        dw_ref[2:3, :] = dw2
        dcb_ref[...] = dcb
        dg_ref[...] = dg
        for c0 in range(0, s, ch):
            base = PAD_ROWS + c0
            dcu = (w[0:1] * pad_dconv[base + 1:base + 1 + ch, :] + w[1:2] * pad_dconv[base:base + ch, :]
                   + w[2:3] * pad_dconv[base - 1:base - 1 + ch, :])
            out_bufs[slot, 1, c0:c0 + ch, :] = (dcu * u_ref[c0:c0 + ch, :].astype(F32)).astype(BF16)
            out_bufs[slot, 2, c0:c0 + ch, :] = (dcu * c_ref[c0:c0 + ch, :].astype(F32)).astype(BF16)
        for cp in out_copies(j, slot):
            cp.start()

        @pl.when(j == nblk - 1)
        def _():
            for cp in out_copies(j, slot) + out_copies(j - 1, 1 - slot):
                cp.wait()

    def part(p):
        return pl.BlockSpec((s, cb), lambda j: (0, p * nblk + j))

    vec = pl.BlockSpec((1, cb), lambda j: (0, j))
    w_spec = pl.BlockSpec((3, cb), lambda j: (0, j))
    return _pallas(body, deps, name="conv_bwd", grid=(nblk,),
                   in_specs=[part(1), part(1), part(2), part(3), w_spec, vec, vec],
                   out_specs=[pl.BlockSpec(memory_space=pl.ANY), w_spec, vec, vec],
                   out_shape=[jax.ShapeDtypeStruct((s, 4 * width), BF16), jax.ShapeDtypeStruct((3, width), F32),
                              jax.ShapeDtypeStruct((1, width), F32), jax.ShapeDtypeStruct((1, width), F32)],
                   scratch_shapes=[pltpu.VMEM((s + 2 * PAD_ROWS, cb), F32), pltpu.VMEM((s + 2 * PAD_ROWS, cb), F32),
                                   pltpu.VMEM((2, 3, s, cb), BF16), pltpu.SemaphoreType.DMA((6,))],
                   compiler_params=_params("arbitrary"))(dmixed, proj, proj, proj, conv_w, conv_b, gnorm_g)


def _adamw(w, g, m, v):
    m = ADAM_B1 * m + (1.0 - ADAM_B1) * g
    v = ADAM_B2 * v + (1.0 - ADAM_B2) * (g * g)
    m_hat = m / (1.0 - ADAM_B1 ** ADAM_STEP)
    v_hat = v / (1.0 - ADAM_B2 ** ADAM_STEP)
    delta = -ADAM_LR * (m_hat / (jnp.sqrt(v_hat) + ADAM_EPS) + ADAM_WD * w)
    return delta, m, v


def _ada_fwd(c_rows, w, b, tn=512):
    rows, d = c_rows.shape
    n = w.shape[1]
    tn = _tile(tn, n)

    def body(c_ref, w_ref, b_ref, o_ref):
        cv = c_ref[...]
        act = (cv * jax.nn.sigmoid(cv)).astype(BF16)
        o_ref[...] = jnp.dot(act, w_ref[...].astype(BF16), preferred_element_type=F32) + b_ref[...]

    return _pallas(body, name="ada_fwd", grid=(n // tn,),
                   in_specs=[pl.BlockSpec((rows, d), lambda j: (0, 0)), pl.BlockSpec((d, tn), lambda j: (0, j)),
                             pl.BlockSpec((1, tn), lambda j: (0, j))],
                   out_specs=pl.BlockSpec((rows, tn), lambda j: (0, j)),
                   out_shape=jax.ShapeDtypeStruct((rows, n), F32), compiler_params=_params("parallel"))(c_rows, w, b)


def _ada_bwd_adam(c_cols, dmod, w, m, v, tr=512, tn=1024):
    d, rows = c_cols.shape
    n = w.shape[1]
    tr, tn = _tile(tr, d), _tile(tn, n)

    def body(c_ref, dm_ref, w_ref, m_ref, v_ref, g_ref, dl_ref, nm_ref, nv_ref):
        cv = c_ref[...]
        act = (cv * jax.nn.sigmoid(cv)).astype(BF16)
        g = jnp.dot(act, dm_ref[...].astype(BF16), preferred_element_type=F32)
        g_ref[...] = g
        dl_ref[...], nm_ref[...], nv_ref[...] = _adamw(w_ref[...], g, m_ref[...], v_ref[...])

    blk = pl.BlockSpec((tr, tn), lambda i, j: (i, j))
    shape = jax.ShapeDtypeStruct((d, n), F32)
    return _pallas(body, name="ada_bwd_adam", grid=(d // tr, n // tn),
                   in_specs=[pl.BlockSpec((tr, rows), lambda i, j: (i, 0)), pl.BlockSpec((rows, tn), lambda i, j: (0, j)),
                             blk, blk, blk],
                   out_specs=[blk] * 4, out_shape=[shape] * 4,
                   compiler_params=_params("parallel", "parallel"))(c_cols, dmod, w, m, v)


def _reduce_adam(name, pieces, w, m, v, tr=256, tc=1024):
    r, c = w.shape
    tr, tc = _tile(tr, r), _tile(tc, c)

    def body(p_ref, w_ref, m_ref, v_ref, g_ref, dl_ref, nm_ref, nv_ref):
        g = p_ref[0].astype(F32)
        for j in range(1, N_DEV):
            g = g + p_ref[j].astype(F32)
        g_ref[...] = g
        dl_ref[...], nm_ref[...], nv_ref[...] = _adamw(w_ref[...], g, m_ref[...], v_ref[...])

    blk = pl.BlockSpec((tr, tc), lambda i, j: (i, j))
    shape = jax.ShapeDtypeStruct((r, c), F32)
    return _pallas(body, name=name, grid=(r // tr, c // tc),
                   in_specs=[pl.BlockSpec((N_DEV, tr, tc), lambda i, j: (0, i, j)), blk, blk, blk],
                   out_specs=[blk] * 4, out_shape=[shape] * 4,
                   compiler_params=_params("parallel", "parallel"))(pieces, w, m, v)


def _reduce_adam_chips(name, sums, land, w, m, v, tr=256, tc=1024):
    r, c = w.shape
    tr, tc = _tile(tr, r), _tile(tc, c)

    def body(s_ref, l_ref, w_ref, m_ref, v_ref, g_ref, dl_ref, nm_ref, nv_ref):
        g = s_ref[...].astype(F32)
        for k in range(3):
            g = g + l_ref[k].astype(F32)
        g_ref[...] = g
        dl_ref[...], nm_ref[...], nv_ref[...] = _adamw(w_ref[...], g, m_ref[...], v_ref[...])

    blk = pl.BlockSpec((tr, tc), lambda i, j: (i, j))
    shape = jax.ShapeDtypeStruct((r, c), F32)
    mine = pl.BlockSpec((None, tr, tc), lambda i, j: (2 * lax.axis_index("x") + lax.axis_index("y"), i, j))
    return _pallas(body, name=name, grid=(r // tr, c // tc),
                   in_specs=[mine, pl.BlockSpec((3, tr, tc), lambda i, j: (0, i, j)), blk, blk, blk],
                   out_specs=[blk] * 4, out_shape=[shape] * 4,
                   compiler_params=_params("parallel", "parallel"))(sums, land, w, m, v)


def _sum_devices(parts):
    n = parts.shape[1]

    def body(p_ref, o_ref):
        acc = p_ref[0:1, :]
        for j in range(1, N_DEV):
            acc = acc + p_ref[j:j + 1, :]
        o_ref[...] = acc

    return _pallas(body, name="sum_devices", out_shape=jax.ShapeDtypeStruct((1, n), F32),
                   compiler_params=pltpu.CompilerParams(vmem_limit_bytes=VMEM_LIMIT_BYTES))(parts)


def _adam_small(name, g, w, m, v):
    def body(g_ref, w_ref, m_ref, v_ref, dl_ref, nm_ref, nv_ref):
        dl_ref[...], nm_ref[...], nv_ref[...] = _adamw(w_ref[...], g_ref[...], m_ref[...], v_ref[...])

    shape = jax.ShapeDtypeStruct(w.shape, F32)
    return _pallas(body, name=name, out_shape=[shape] * 3,
                   compiler_params=pltpu.CompilerParams(vmem_limit_bytes=VMEM_LIMIT_BYTES))(g, w, m, v)


def kernel(x, c, w_ada, b_ada, norm1_g, w_in, pool_mix_w, pool_scale, conv_w, conv_b, gnorm_pool_g, gnorm_conv_g, w_out, norm2_g, w_mlp_in, w_mlp_out, final_g, loss_target, m_w_ada, m_b_ada, m_norm1_g, m_w_in, m_pool_mix_w, m_pool_scale, m_conv_w, m_conv_b, m_gnorm_pool_g, m_gnorm_conv_g, m_w_out, m_norm2_g, m_w_mlp_in, m_w_mlp_out, m_final_g, v_w_ada, v_b_ada, v_norm1_g, v_w_in, v_pool_mix_w, v_pool_scale, v_conv_w, v_conv_b, v_gnorm_pool_g, v_gnorm_conv_g, v_w_out, v_norm2_g, v_w_mlp_in, v_w_mlp_out, v_final_g):
    s, d = x.shape[1], x.shape[2]
    width = d // 2
    gd = width // N_POOL_GROUPS
    d_ff = w_mlp_in.shape[2] * N_DEV
    n_proj = w_in.shape[2] * N_DEV
    ada_cols = w_ada.shape[2]
    conv_cols = conv_w.shape[2]
    assert n_proj == 4 * width and ada_cols * N_DEV == N_MOD * d and d_ff % N_DEV == 0
    assert width % CONV_HEAD_DIM == 0 and s % 8 == 0
    seq_chunk = _tile(512, s)
    pool_cb = _tile(256, gd)
    conv_cb = CONV_HEAD_DIM

    me = 4 * lax.axis_index("x") + 2 * lax.axis_index("y") + lax.axis_index("c")
    x2d, target = x[0], loss_target[0]

    conv_w_all, c_all = _exchange("gather_small_weights", [conv_w[0], c], ["gather"] * 2)
    conv_w_full = jnp.transpose(conv_w_all, (1, 0, 2)).reshape(3, width)
    c_rows = jnp.concatenate([c_all.reshape(N_DEV, d), jnp.zeros((N_DEV, d), F32)], axis=0)

    b_mine = lax.dynamic_slice(b_ada, (0, me * ada_cols), (1, ada_cols))
    mod_part = _ada_fwd(c_rows, w_ada[0], b_mine)
    (mod_all,) = _exchange("scatter_mod", [mod_part[:N_DEV].reshape(N_DEV, 1, ada_cols)], ["a2a"])
    mod = mod_all.reshape(1, N_MOD * d)

    started, hopped, relayed = {}, {}, {}

    def gather_start(wname, wgt, deps):
        land = _landing(wgt[0].astype(BF16), me)
        started[wname] = _gather_start("gather_" + wname + "_start", land, deps)
        return started[wname][5]

    def gather_hop(wname, land, after):
        hopped[wname] = _gather_hop("gather_" + wname + "_hop", started[wname], land, after)
        return hopped[wname][3]

    def gather_relay(wname, after):
        relayed[wname] = _gather_relay("gather_" + wname + "_relay", started[wname], hopped[wname], after)
        return relayed[wname][3]

    def gather_wait(wname, land, after, local_waited=False):
        return _gather_wait("gather_" + wname + "_wait", started[wname], hopped[wname], relayed[wname], land, after,
                            local_waited)

    def chip():
        return 2 * lax.axis_index("x") + lax.axis_index("y")

    def local_piece(t):
        return 2 * chip() + t

    def same_core_piece(t):
        return 2 * ((chip() + 1 + t) % N_CHIP) + lax.axis_index("c")

    def other_core_piece(t):
        return 2 * ((chip() + 1 + t) % N_CHIP) + 1 - lax.axis_index("c")

    def routed_piece(t):
        first, second, diagonal = _routes(lax.axis_index("x"), lax.axis_index("y"), lax.axis_index("c"))
        return jnp.where(t == 0, _index_of(first), jnp.where(t == 1, _index_of(second), _index_of(diagonal)))

    tok_w_in = gather_start("w_in", w_in, (mod,))
    shift1, scale1, gate1, shift2, scale2, gate2 = [mod[:, i * d:(i + 1) * d] for i in range(N_MOD)]

    h1 = _norm_mod("norm1_fwd", x2d, norm1_g, scale1, shift1, deps=(tok_w_in,))
    proj_shape = [jax.ShapeDtypeStruct((s, n_proj), BF16)]
    w_in_local = _gather_wait_local("gather_w_in_local", started["w_in"], started["w_in"][4], h1)
    (proj,) = _mm_nn("in_proj_local", h1, w_in_local, n_proj, True, proj_shape, _store(BF16), pieces=(local_piece, 2))
    tok = gather_hop("w_in", w_in_local, proj)
    tok = gather_start("w_out", w_out, (tok,))
    mix_started = _push_start("gather_pool_mix_start", [pool_mix_w[0].astype(BF16)], ("gather",), me, deps=(tok,))
    (proj,) = _mm_nn("in_proj_first", h1, hopped["w_in"][2], n_proj, True, proj_shape, _store(BF16),
                     pieces=(routed_piece, 1), carry=(proj,), deps=(mix_started[-1],))
    tok = gather_relay("w_in", proj)
    tok = gather_hop("w_out", started["w_out"][4], tok)
    tok = gather_start("w_mlp_in", w_mlp_in, (tok,))
    (proj,) = _mm_nn("in_proj_same_core", h1, relayed["w_in"][2], n_proj, True, proj_shape, _store(BF16),
                     pieces=(lambda t: routed_piece(t + 1), 2), carry=(proj,), deps=(tok,))
    w_in_all = gather_wait("w_in", relayed["w_in"][2], proj, True)
    (proj,) = _mm_nn("in_proj", h1, w_in_all, n_proj, True, proj_shape, _store(BF16), pieces=(other_core_piece, 3),
                     carry=(proj,))
    pooled = _pool_fwd(proj, s, gd, pool_cb, seq_chunk)
    (wmix_all,) = _push_wait("gather_pool_mix_wait", mix_started, ("gather",), pooled)
    wmix_full = jnp.transpose(wmix_all, (1, 0, 2, 3)).reshape(N_POOL_GROUPS, gd, gd)
    a_pre, mixed = _poolmix_fwd(pooled, wmix_full, pool_scale, gnorm_pool_g, d)
    tok = gather_relay("w_out", a_pre)
    mixed = _conv_fwd(proj, mixed, conv_w_full, conv_b, gnorm_conv_g, s, width, conv_cb, seq_chunk, deps=(tok,))
    tok = gather_hop("w_mlp_in", started["w_mlp_in"][4], mixed)
    tok = gather_start("w_mlp_out", w_mlp_out, (tok,))

    def residual_specs(tm, tn):
        return [pl.BlockSpec((tm, tn), lambda i, j, k: (i, j)), pl.BlockSpec((1, tn), lambda i, j, k: (0, j))]

    sd_f32 = jax.ShapeDtypeStruct((s, d), F32)
    w_out_full = gather_wait("w_out", relayed["w_out"][2], tok).reshape(d, d)
    sd_bf16 = jax.ShapeDtypeStruct((s, d), BF16)
    attn, x_mid = _mm_nn("out_proj", mixed, w_out_full, d, False, [sd_bf16, sd_f32], _residual_epilogue,
                         extras=(x2d, gate1), extra_specs=residual_specs)
    h2 = _norm_mod("norm2_fwd", x_mid, norm2_g, scale2, shift2)
    sf_bf16 = [jax.ShapeDtypeStruct((s, d_ff), BF16)] * 2
    w1_local = _gather_wait_local("gather_w_mlp_in_local", started["w_mlp_in"], hopped["w_mlp_in"][2], h2)
    relu, hid = _mm_nn("mlp_in_local", h2, w1_local, d_ff, True, sf_bf16, _relu2_epilogue, pieces=(local_piece, 2))
    hopped["w_mlp_in"] = hopped["w_mlp_in"][:2] + (w1_local,) + hopped["w_mlp_in"][3:]
    tok = gather_relay("w_mlp_in", hid)
    relu, hid = _mm_nn("mlp_in_same_core", h2, relayed["w_mlp_in"][2], d_ff, True, sf_bf16, _relu2_epilogue,
                       pieces=(same_core_piece, 3), carry=(relu, hid), deps=(tok,))
    w1_all = gather_wait("w_mlp_in", relayed["w_mlp_in"][2], hid, True)
    tok = gather_hop("w_mlp_out", started["w_mlp_out"][4], w1_all)
    relu, hid = _mm_nn("mlp_in_other_core", h2, w1_all, d_ff, True, sf_bf16, _relu2_epilogue,
                       pieces=(other_core_piece, 2), carry=(relu, hid), deps=(tok,))
    tok = gather_relay("w_mlp_out", hid)
    relu, hid = _mm_nn("mlp_in", h2, w1_all, d_ff, True, sf_bf16, _relu2_epilogue,
                       pieces=(lambda t: other_core_piece(t + 2), 1), carry=(relu, hid), deps=(tok,))
    w2_full = gather_wait("w_mlp_out", relayed["w_mlp_out"][2], hid).reshape(d_ff, d)
    mlp, x_last = _mm_nn("mlp_out", hid, w2_full, d, False, [sd_bf16, sd_f32], _residual_epilogue,
                         extras=(x_mid, gate2), extra_specs=residual_specs)

    dx_last, dmlp, d_final_g, dgate2, loss_row = _loss_head(x_last, target, final_g.reshape(1, d), gate2, mlp)

    def relu_specs(tm, tn):
        return [pl.BlockSpec((tm, tn), lambda i, j, k: (i, j))]

    def reduce_start(wname, a, b, col_pieces, deps=()):
        far = _mm_tn_half(wname + "_dw_far", a, b, col_pieces, near=False, deps=deps)
        return _pair_start("scatter_" + wname + "_pair_start", far)

    def reduce_chips(wname, a, b, col_pieces, pairs, after):
        pair = _pair_wait("scatter_" + wname + "_pair_wait", pairs, after)
        sums = _mm_tn_half(wname + "_dw_near", a, b, col_pieces, near=True, pair=pair)
        return _chip_start("scatter_" + wname + "_chip_start", sums)

    pairs_w2 = reduce_start("mlp_out", hid, dmlp, False)
    (dhpre,) = _mm_nt("mlp_out_dx", dmlp, w2_full, d_ff, False, sf_bf16[:1], _relu2_bwd_epilogue,
                      extras=(relu,), extra_specs=relu_specs, deps=(pairs_w2[4],))
    chips_w2 = reduce_chips("mlp_out", hid, dmlp, False, pairs_w2, dhpre)
    pairs_w1 = reduce_start("mlp_in", h2, dhpre, True, deps=(chips_w2[4],))
    (dh2,) = _mm_nt("mlp_in_dx", dhpre, w1_all, d, True, [sd_bf16], _store(BF16), tn=1024, tk=2048,
                    deps=(pairs_w1[4],))
    chips_w1 = reduce_chips("mlp_in", h2, dhpre, True, pairs_w1, dh2)
    dx_mid, dshift2, dscale2, d_norm2_g, dattn, dgate1 = _norm_mod_bwd(
        "norm2_bwd", dh2, x_mid, norm2_g, scale2, dx_last, branch=attn, gate=gate1, deps=(chips_w1[4],))

    pairs_w_out = reduce_start("out_proj", mixed, dattn, False)
    (dmixed,) = _mm_nt("out_proj_dx", dattn, w_out_full, d, False, [sd_bf16], _store(BF16), deps=(pairs_w_out[4],))
    chips_w_out = reduce_chips("out_proj", mixed, dattn, False, pairs_w_out, dmixed)
    dproj, d_conv_w, d_conv_b, d_gnorm_conv = _conv_bwd(dmixed, proj, conv_w_full, conv_b, gnorm_conv_g,
                                                        s, width, conv_cb, seq_chunk, deps=(chips_w_out[4],))
    da_pre, dpooled, d_pool_scale, d_gnorm_pool = _poolmix_bwd(dmixed, a_pre, wmix_full, pool_scale, gnorm_pool_g)
    g_wmix = _poolmix_wgrad(pooled, da_pre, gd)
    dproj = _pool_bwd(dpooled, dproj, s, gd, pool_cb, seq_chunk)

    pairs_w_in = reduce_start("in_proj", h1, dproj, True)
    (dh1,) = _mm_nt("in_proj_dx", dproj, w_in_all, d, True, [sd_bf16], _store(BF16), deps=(pairs_w_in[4],))
    chips_w_in = reduce_chips("in_proj", h1, dproj, True, pairs_w_in, dh1)
    grad_x, dshift1, dscale1, d_norm1_g = _norm_mod_bwd("norm1_bwd", dh1, x2d, norm1_g, scale1, dx_mid,
                                                        deps=(chips_w_in[4],))

    rows_mix = gd // N_DEV
    g_wmix_split = jnp.transpose(g_wmix.reshape(N_POOL_GROUPS, N_DEV, rows_mix, gd), (1, 0, 2, 3))
    g_wmix_split = g_wmix_split.reshape(N_DEV, N_POOL_GROUPS * rows_mix, gd)
    loss_pad = jnp.concatenate([loss_row[:, :1], jnp.zeros((1, 127), F32)], axis=1)
    dmod = jnp.concatenate([dshift1, dscale1, dgate1, dshift2, dscale2, dgate2], axis=1)
    small = jnp.concatenate([dmod, d_norm1_g, d_pool_scale, d_conv_b, d_gnorm_pool, d_gnorm_conv, d_norm2_g,
                             d_final_g, d_conv_w.reshape(1, 3 * width), loss_pad], axis=1)
    small_started = _push_start("exchange_small_grads_start", [g_wmix_split, small], ("a2a", "gather"), me)

    sums, landed = _chip_wait("scatter_w_mlp_out_chip_wait", chips_w2, small_started[-1])
    out_w2 = _reduce_adam_chips("adam_w_mlp_out", sums, landed, w_mlp_out[0], m_w_mlp_out[0], v_w_mlp_out[0])
    sums, landed = _chip_wait("scatter_w_mlp_in_chip_wait", chips_w1, out_w2[0])
    out_w1 = _reduce_adam_chips("adam_w_mlp_in", sums, landed, w_mlp_in[0], m_w_mlp_in[0], v_w_mlp_in[0])
    sums, landed = _chip_wait("scatter_w_out_chip_wait", chips_w_out, out_w1[0])
    out_w_out = _reduce_adam_chips("adam_w_out", sums, landed, w_out[0], m_w_out[0], v_w_out[0])
    sums, landed = _chip_wait("scatter_w_in_chip_wait", chips_w_in, out_w_out[0])
    out_w_in = _reduce_adam_chips("adam_w_in", sums, landed, w_in[0], m_w_in[0], v_w_in[0])

    p_wmix, small_all = _push_wait("exchange_small_grads_wait", small_started, ("a2a", "gather"), out_w_in[0])
    mix_shape = (N_POOL_GROUPS * rows_mix, gd)
    out_wmix = _reduce_adam("adam_pool_mix", p_wmix, pool_mix_w.reshape(mix_shape), m_pool_mix_w.reshape(mix_shape),
                            v_pool_mix_w.reshape(mix_shape))
    out_wmix = [a.reshape(pool_mix_w.shape) for a in out_wmix]
    small_all = small_all.reshape(N_DEV, small.shape[1])
    small_sum = _sum_devices(small_all)

    n_rep = (N_MOD + 1) * d + 4 * width + 2 * d
    loss = small_sum[0, n_rep + 3 * width]
    rep_names_w = [b_ada, norm1_g, pool_scale, conv_b, gnorm_pool_g, gnorm_conv_g, norm2_g, final_g.reshape(1, d)]
    rep_names_m = [m_b_ada, m_norm1_g, m_pool_scale, m_conv_b, m_gnorm_pool_g, m_gnorm_conv_g, m_norm2_g,
                   m_final_g.reshape(1, d)]
    rep_names_v = [v_b_ada, v_norm1_g, v_pool_scale, v_conv_b, v_gnorm_pool_g, v_gnorm_conv_g, v_norm2_g,
                   v_final_g.reshape(1, d)]
    rep_grad = small_sum[:, :n_rep]
    rep_delta, rep_m, rep_v = _adam_small("adam_replicated", rep_grad, jnp.concatenate(rep_names_w, axis=1),
                                          jnp.concatenate(rep_names_m, axis=1), jnp.concatenate(rep_names_v, axis=1))

    def split_rep(vec):
        out, off = [], 0
        for wgt in rep_names_w:
            n = wgt.shape[1]
            out.append(vec[:, off:off + n])
            off += n
        out[-1] = out[-1].reshape(d)
        return out

    conv_grad_full = small_sum[:, n_rep:n_rep + 3 * width].reshape(3, width)
    g_conv_w = lax.dynamic_slice(conv_grad_full, (0, me * conv_cols), (3, conv_cols))
    g_conv_w8 = jnp.concatenate([g_conv_w, jnp.zeros((5, conv_cols), F32)], axis=0)

    def pad8(a):
        return jnp.concatenate([a[0], jnp.zeros((5, conv_cols), F32)], axis=0)

    conv_delta, conv_m, conv_v = _adam_small("adam_conv_w", g_conv_w8, pad8(conv_w), pad8(m_conv_w), pad8(v_conv_w))

    dmod_all = small_all[:, :N_MOD * d]
    dmod_mine = lax.dynamic_slice(dmod_all, (0, me * ada_cols), (N_DEV, ada_cols))
    dmod_rows = jnp.concatenate([dmod_mine, jnp.zeros((N_DEV, ada_cols), F32)], axis=0)
    out_ada = _ada_bwd_adam(jnp.transpose(c_rows), dmod_rows, w_ada[0], m_w_ada[0], v_w_ada[0])

    rep_all = [split_rep(rep_grad), split_rep(rep_delta), split_rep(rep_m), split_rep(rep_v)]
    conv_all = [g_conv_w[None], conv_delta[None, :3], conv_m[None, :3], conv_v[None, :3]]
    outs = [loss, grad_x[None]]
    for kind in range(4):
        b_ada_o, norm1_o, pool_scale_o, conv_b_o, gpool_o, gconv_o, norm2_o, final_o = rep_all[kind]
        outs += [out_ada[kind][None], b_ada_o, norm1_o, out_w_in[kind][None], out_wmix[kind], pool_scale_o,
                 conv_all[kind], conv_b_o, gpool_o, gconv_o, out_w_out[kind][None], norm2_o, out_w1[kind][None],
                 out_w2[kind][None], final_o]
    return tuple(outs)
```
